```python
import jax, jax.numpy as jnp
from jax import lax
import numpy as np

D_MODEL = 2048
BATCH = 8
SEQ = 4096
DEPTH = 2

N_MIXERS = 2
N_MLA_LAYERS = (DEPTH + 1) // 2
N_CONV_LAYERS = DEPTH // 2
MLA_HEADS = 16
QK_NOPE_DIM = 128
QK_ROPE_DIM = 64
V_HEAD_DIM = 128
Q_LORA_RANK = 512
KV_LORA_RANK = 512
MLA_LATENT_DIM = Q_LORA_RANK + KV_LORA_RANK + QK_ROPE_DIM
ROPE_THETA = 10000.0
CONV_WIDTH = 3
D_FF = 4 * D_MODEL
Q_BLOCK = 128
NORM_EPS = 1e-6
N_MOD = 6

kernel_name = "hybrid_mla_shortconv_adaln_sandwich"


def rmsnorm(x, g):
    x32 = x.astype(jnp.float32)
    y = x32 * lax.rsqrt(jnp.mean(x32 * x32, axis=-1, keepdims=True) + NORM_EPS)
    return (y * g.astype(jnp.float32)).astype(x.dtype)


def rope_cos_sin(positions):
    inv_freq = ROPE_THETA ** (-jnp.arange(0, QK_ROPE_DIM, 2, dtype=jnp.float32) / QK_ROPE_DIM)
    ang = positions.astype(jnp.float32)[..., None] * inv_freq
    return jnp.cos(ang), jnp.sin(ang)


def apply_rope(t, cos, sin):
    t32 = t.astype(jnp.float32)
    half = QK_ROPE_DIM // 2
    t1, t2 = t32[..., :half], t32[..., half:]
    out = jnp.concatenate([t1 * cos - t2 * sin, t2 * cos + t1 * sin], axis=-1)
    return out.astype(t.dtype)


def mla_mixer(h, positions, w_in, g_q, g_kv, w_uq, w_ukv, w_o):
    B, S, _ = h.shape
    lat = h @ w_in
    c_q = rmsnorm(lat[..., :Q_LORA_RANK], g_q)
    c_kv = rmsnorm(lat[..., Q_LORA_RANK:Q_LORA_RANK + KV_LORA_RANK], g_kv)
    k_rope = lat[..., Q_LORA_RANK + KV_LORA_RANK:]
    cos, sin = rope_cos_sin(positions)
    k_rope = apply_rope(k_rope, cos, sin)
    q = jnp.einsum('bsr,rhd->bshd', c_q, w_uq)
    q_nope = q[..., :QK_NOPE_DIM]
    q_rope = apply_rope(q[..., QK_NOPE_DIM:], cos[:, :, None, :], sin[:, :, None, :])
    kv = jnp.einsum('bsr,rhd->bshd', c_kv, w_ukv)
    k_nope, v = kv[..., :QK_NOPE_DIM], kv[..., QK_NOPE_DIM:]

    n_blk = S // Q_BLOCK
    scale = (QK_NOPE_DIM + QK_ROPE_DIM) ** -0.5
    qn_blocks = q_nope.reshape(B, n_blk, Q_BLOCK, MLA_HEADS, QK_NOPE_DIM).transpose(1, 0, 2, 3, 4)
    qr_blocks = q_rope.reshape(B, n_blk, Q_BLOCK, MLA_HEADS, QK_ROPE_DIM).transpose(1, 0, 2, 3, 4)
    starts = jnp.arange(n_blk, dtype=jnp.int32) * Q_BLOCK
    k_idx = jnp.arange(S, dtype=jnp.int32)

    def attend(args):
        qn, qr, start = args
        s = (jnp.einsum('bqhd,bkhd->bhqk', qn, k_nope, preferred_element_type=jnp.float32)
             + jnp.einsum('bqhd,bkd->bhqk', qr, k_rope, preferred_element_type=jnp.float32)) * scale
        q_idx = start + jnp.arange(Q_BLOCK, dtype=jnp.int32)
        causal = k_idx[None, :] <= q_idx[:, None]
        s = jnp.where(causal, s, jnp.finfo(jnp.float32).min)
        p = jax.nn.softmax(s, axis=-1)
        return jnp.einsum('bhqk,bkhd->bqhd', p.astype(v.dtype), v)

    o = lax.map(attend, (qn_blocks, qr_blocks, starts))
    o = o.transpose(1, 0, 2, 3, 4).reshape(B, S, MLA_HEADS * V_HEAD_DIM)
    return o @ w_o


def short_conv_mixer(h, w_in, conv_w, w_out):
    proj = h @ w_in
    b_gate = proj[..., :D_MODEL]
    c_gate = proj[..., D_MODEL:2 * D_MODEL]
    u = proj[..., 2 * D_MODEL:]
    z = c_gate * u
    z = lax.conv_general_dilated(z, conv_w[:, None, :].astype(z.dtype), window_strides=(1,),
                                 padding=[(CONV_WIDTH - 1, 0)],
                                 dimension_numbers=('NWC', 'WIO', 'NWC'),
                                 feature_group_count=D_MODEL)
    return (b_gate * z) @ w_out


def sq_relu_mlp(h, w_up, w_down):
    a = jax.nn.relu(h @ w_up)
    return (a * a) @ w_down


def _fwd_setup_inputs(seed: int = 0) -> dict:
    key = jax.random.key(seed)
    ks = jax.random.split(key, 20)
    f32 = jnp.float32

    def nrm(k, shape, fan_in, mult=1.0):
        return jax.random.normal(k, shape, f32) * (mult * fan_in ** -0.5)

    x = jax.random.normal(ks[0], (BATCH, SEQ, D_MODEL), f32)
    c = jax.random.normal(ks[1], (BATCH, D_MODEL), f32)
    offsets = jax.random.randint(ks[2], (BATCH, 1), 0, 1024, dtype=jnp.int32)
    positions = jnp.arange(SEQ, dtype=jnp.int32)[None, :] + offsets
    w_mod = nrm(ks[3], (DEPTH, D_MODEL, N_MOD * D_MODEL), D_MODEL, 0.5)
    b_mod = 0.02 * jax.random.normal(ks[4], (DEPTH, N_MOD * D_MODEL), f32)
    norm_g = 1.0 + 0.05 * jax.random.normal(ks[5], (DEPTH, 4, D_MODEL), f32)
    mla_w_in = nrm(ks[6], (N_MLA_LAYERS, D_MODEL, MLA_LATENT_DIM), D_MODEL)
    mla_g_q = 1.0 + 0.05 * jax.random.normal(ks[7], (N_MLA_LAYERS, Q_LORA_RANK), f32)
    mla_g_kv = 1.0 + 0.05 * jax.random.normal(ks[8], (N_MLA_LAYERS, KV_LORA_RANK), f32)
    mla_w_uq = nrm(ks[9], (N_MLA_LAYERS, Q_LORA_RANK, MLA_HEADS, QK_NOPE_DIM + QK_ROPE_DIM), Q_LORA_RANK)
    mla_w_ukv = nrm(ks[10], (N_MLA_LAYERS, KV_LORA_RANK, MLA_HEADS, QK_NOPE_DIM + V_HEAD_DIM), KV_LORA_RANK)
    mla_w_o = nrm(ks[11], (N_MLA_LAYERS, MLA_HEADS * V_HEAD_DIM, D_MODEL), MLA_HEADS * V_HEAD_DIM)
    conv_w_in = nrm(ks[12], (N_CONV_LAYERS, D_MODEL, 3 * D_MODEL), D_MODEL)
    conv_w = nrm(ks[13], (N_CONV_LAYERS, CONV_WIDTH, D_MODEL), CONV_WIDTH)
    conv_w_out = nrm(ks[14], (N_CONV_LAYERS, D_MODEL, D_MODEL), D_MODEL)
    mlp_w_up = nrm(ks[15], (DEPTH, D_MODEL, D_FF), D_MODEL)
    mlp_w_down = nrm(ks[16], (DEPTH, D_FF, D_MODEL), D_FF)
    return {"x": x, "c": c, "positions": positions, "w_mod": w_mod, "b_mod": b_mod,
            "norm_g": norm_g, "mla_w_in": mla_w_in, "mla_g_q": mla_g_q, "mla_g_kv": mla_g_kv,
            "mla_w_uq": mla_w_uq, "mla_w_ukv": mla_w_ukv, "mla_w_o": mla_w_o,
            "conv_w_in": conv_w_in, "conv_w": conv_w, "conv_w_out": conv_w_out,
            "mlp_w_up": mlp_w_up, "mlp_w_down": mlp_w_down}


def _fwd_reference(x, c, positions, w_mod, b_mod, norm_g, mla_w_in, mla_g_q, mla_g_kv,
              mla_w_uq, mla_w_ukv, mla_w_o, conv_w_in, conv_w, conv_w_out,
              mlp_w_up, mlp_w_down):
    cond = jax.nn.silu(c)
    for i in range(DEPTH):
        mod = (cond @ w_mod[i] + b_mod[i])[:, None, :]
        sh1, sc1, g1, sh2, sc2, g2 = jnp.split(mod, N_MOD, axis=-1)
        h = rmsnorm(x, norm_g[i, 0]) * (1.0 + sc1) + sh1
        j = i // N_MIXERS
        if i % N_MIXERS == 0:
            y = mla_mixer(h, positions, mla_w_in[j], mla_g_q[j], mla_g_kv[j],
                          mla_w_uq[j], mla_w_ukv[j], mla_w_o[j])
        else:
            y = short_conv_mixer(h, conv_w_in[j], conv_w[j], conv_w_out[j])
        x = x + g1 * rmsnorm(y, norm_g[i, 1])
        h = rmsnorm(x, norm_g[i, 2]) * (1.0 + sc2) + sh2
        y = sq_relu_mlp(h, mlp_w_up[i], mlp_w_down[i])
        x = x + g2 * rmsnorm(y, norm_g[i, 3])
    return x


import jax as _jax
import jax.numpy as _jnp

TWIN_FORMAT = 'train_step'
FWD_PARAMS = ['x', 'c', 'positions', 'w_mod', 'b_mod', 'norm_g', 'mla_w_in', 'mla_g_q', 'mla_g_kv', 'mla_w_uq', 'mla_w_ukv', 'mla_w_o', 'conv_w_in', 'conv_w', 'conv_w_out', 'mlp_w_up', 'mlp_w_down']
TWIN_WEIGHTS = ['w_mod', 'b_mod', 'norm_g', 'mla_w_in', 'mla_g_q', 'mla_g_kv', 'mla_w_uq', 'mla_w_ukv', 'mla_w_o', 'conv_w_in', 'conv_w', 'conv_w_out', 'mlp_w_up', 'mlp_w_down']
TWIN_DIFF_INPUT = 'x'
TWIN_INPUTS = ['x', 'c', 'positions', 'w_mod', 'b_mod', 'norm_g', 'mla_w_in', 'mla_g_q', 'mla_g_kv', 'mla_w_uq', 'mla_w_ukv', 'mla_w_o', 'conv_w_in', 'conv_w', 'conv_w_out', 'mlp_w_up', 'mlp_w_down', 'loss_target', 'm_w_mod', 'm_b_mod', 'm_norm_g', 'm_mla_w_in', 'm_mla_g_q', 'm_mla_g_kv', 'm_mla_w_uq', 'm_mla_w_ukv', 'm_mla_w_o', 'm_conv_w_in', 'm_conv_w', 'm_conv_w_out', 'm_mlp_w_up', 'm_mlp_w_down', 'v_w_mod', 'v_b_mod', 'v_norm_g', 'v_mla_w_in', 'v_mla_g_q', 'v_mla_g_kv', 'v_mla_w_uq', 'v_mla_w_ukv', 'v_mla_w_o', 'v_conv_w_in', 'v_conv_w', 'v_conv_w_out', 'v_mlp_w_up', 'v_mlp_w_down']
TWIN_OUTPUTS = ['loss', 'grad_x', 'grad_w_mod', 'grad_b_mod', 'grad_norm_g', 'grad_mla_w_in', 'grad_mla_g_q', 'grad_mla_g_kv', 'grad_mla_w_uq', 'grad_mla_w_ukv', 'grad_mla_w_o', 'grad_conv_w_in', 'grad_conv_w', 'grad_conv_w_out', 'grad_mlp_w_up', 'grad_mlp_w_down', 'delta_w_mod', 'delta_b_mod', 'delta_norm_g', 'delta_mla_w_in', 'delta_mla_g_q', 'delta_mla_g_kv', 'delta_mla_w_uq', 'delta_mla_w_ukv', 'delta_mla_w_o', 'delta_conv_w_in', 'delta_conv_w', 'delta_conv_w_out', 'delta_mlp_w_up', 'delta_mlp_w_down', 'new_m_w_mod', 'new_m_b_mod', 'new_m_norm_g', 'new_m_mla_w_in', 'new_m_mla_g_q', 'new_m_mla_g_kv', 'new_m_mla_w_uq', 'new_m_mla_w_ukv', 'new_m_mla_w_o', 'new_m_conv_w_in', 'new_m_conv_w', 'new_m_conv_w_out', 'new_m_mlp_w_up', 'new_m_mlp_w_down', 'new_v_w_mod', 'new_v_b_mod', 'new_v_norm_g', 'new_v_mla_w_in', 'new_v_mla_g_q', 'new_v_mla_g_kv', 'new_v_mla_w_uq', 'new_v_mla_w_ukv', 'new_v_mla_w_o', 'new_v_conv_w_in', 'new_v_conv_w', 'new_v_conv_w_out', 'new_v_mlp_w_up', 'new_v_mlp_w_down']
TWIN_LEAF_KINDS = {'loss': 'loss', 'grad_x': 'grad_x', 'grad_w_mod': 'grad_w', 'grad_b_mod': 'grad_w', 'grad_norm_g': 'grad_w', 'grad_mla_w_in': 'grad_w', 'grad_mla_g_q': 'grad_w', 'grad_mla_g_kv': 'grad_w', 'grad_mla_w_uq': 'grad_w', 'grad_mla_w_ukv': 'grad_w', 'grad_mla_w_o': 'grad_w', 'grad_conv_w_in': 'grad_w', 'grad_conv_w': 'grad_w', 'grad_conv_w_out': 'grad_w', 'grad_mlp_w_up': 'grad_w', 'grad_mlp_w_down': 'grad_w', 'delta_w_mod': 'delta_w', 'delta_b_mod': 'delta_w', 'delta_norm_g': 'delta_w', 'delta_mla_w_in': 'delta_w', 'delta_mla_g_q': 'delta_w', 'delta_mla_g_kv': 'delta_w', 'delta_mla_w_uq': 'delta_w', 'delta_mla_w_ukv': 'delta_w', 'delta_mla_w_o': 'delta_w', 'delta_conv_w_in': 'delta_w', 'delta_conv_w': 'delta_w', 'delta_conv_w_out': 'delta_w', 'delta_mlp_w_up': 'delta_w', 'delta_mlp_w_down': 'delta_w', 'new_m_w_mod': 'new_m', 'new_m_b_mod': 'new_m', 'new_m_norm_g': 'new_m', 'new_m_mla_w_in': 'new_m', 'new_m_mla_g_q': 'new_m', 'new_m_mla_g_kv': 'new_m', 'new_m_mla_w_uq': 'new_m', 'new_m_mla_w_ukv': 'new_m', 'new_m_mla_w_o': 'new_m', 'new_m_conv_w_in': 'new_m', 'new_m_conv_w': 'new_m', 'new_m_conv_w_out': 'new_m', 'new_m_mlp_w_up': 'new_m', 'new_m_mlp_w_down': 'new_m', 'new_v_w_mod': 'new_v', 'new_v_b_mod': 'new_v', 'new_v_norm_g': 'new_v', 'new_v_mla_w_in': 'new_v', 'new_v_mla_g_q': 'new_v', 'new_v_mla_g_kv': 'new_v', 'new_v_mla_w_uq': 'new_v', 'new_v_mla_w_ukv': 'new_v', 'new_v_mla_w_o': 'new_v', 'new_v_conv_w_in': 'new_v', 'new_v_conv_w': 'new_v', 'new_v_conv_w_out': 'new_v', 'new_v_mlp_w_up': 'new_v', 'new_v_mlp_w_down': 'new_v'}


def _forward(args):
    return _fwd_reference(*[args[k] for k in FWD_PARAMS])


def _output_shape():
    def fwd():
        inp = _fwd_setup_inputs(0)
        return _fwd_reference(*[inp[k] for k in FWD_PARAMS])
    out = _jax.eval_shape(fwd)
    return out.shape, out.dtype

N_MICROBATCH = 1
ADAM_LR = 0.001
ADAM_B1 = 0.9
ADAM_B2 = 0.999
ADAM_EPS = 1e-08
ADAM_WD = 0.01
ADAM_STEP = 10
PER_EXAMPLE_BATCH_AXIS = {'x': 0, 'c': 0, 'positions': 0, 'loss_target': 0}
SHARED_INPUTS = []
_WEIGHT_DTYPES = {'w_mod': _jnp.float32, 'b_mod': _jnp.float32, 'norm_g': _jnp.float32, 'mla_w_in': _jnp.float32, 'mla_g_q': _jnp.float32, 'mla_g_kv': _jnp.float32, 'mla_w_uq': _jnp.float32, 'mla_w_ukv': _jnp.float32, 'mla_w_o': _jnp.float32, 'conv_w_in': _jnp.float32, 'conv_w': _jnp.float32, 'conv_w_out': _jnp.float32, 'mlp_w_up': _jnp.float32, 'mlp_w_down': _jnp.float32}
MOMENT_SCALE = {'w_mod': 9.288057e-01, 'b_mod': 1.797779e+00, 'norm_g': 1.293938e+00, 'mla_w_in': 1.073553e+00, 'mla_g_q': 5.592133e-02, 'mla_g_kv': 1.506352e+00, 'mla_w_uq': 2.273119e-02, 'mla_w_ukv': 5.254461e-01, 'mla_w_o': 7.422078e-01, 'conv_w_in': 5.136418e-02, 'conv_w': 5.199090e-02, 'conv_w_out': 5.433986e-02, 'mlp_w_up': 8.914186e-02, 'mlp_w_down': 3.263780e-01}


def _to_microbatches(a, axis):
    t = _jnp.moveaxis(a, axis, 0)
    t = t.reshape((N_MICROBATCH, t.shape[0] // N_MICROBATCH) + t.shape[1:])
    return _jnp.moveaxis(t, 1, axis + 1)


def setup_inputs(seed: int = 0) -> dict:
    inp = _fwd_setup_inputs(seed)
    key = _jax.random.fold_in(_jax.random.key(seed), 7919)
    shape, _ = _output_shape()
    out = dict(inp)
    out["loss_target"] = _jax.random.normal(_jax.random.fold_in(key, 0), shape, _jnp.float32)
    for i, name in enumerate(TWIN_WEIGHTS):
        w = inp[name].astype(_jnp.float32)
        if MOMENT_SCALE is None:
            s = _jnp.sqrt(_jnp.mean(_jnp.square(w)) + 1e-30)
        else:
            s = MOMENT_SCALE[name]
        km, kv = _jax.random.split(_jax.random.fold_in(key, i + 1))
        out[name] = w
        out["m_" + name] = s * _jax.random.normal(km, w.shape, _jnp.float32)
        out["v_" + name] = (s * s) * _jax.random.uniform(kv, w.shape, _jnp.float32, 0.5, 1.5)
    if N_MICROBATCH > 1:
        for name, axis in PER_EXAMPLE_BATCH_AXIS.items():
            out[name] = _to_microbatches(out[name], axis)
    return {'x': out['x'], 'c': out['c'], 'positions': out['positions'], 'w_mod': out['w_mod'], 'b_mod': out['b_mod'], 'norm_g': out['norm_g'], 'mla_w_in': out['mla_w_in'], 'mla_g_q': out['mla_g_q'], 'mla_g_kv': out['mla_g_kv'], 'mla_w_uq': out['mla_w_uq'], 'mla_w_ukv': out['mla_w_ukv'], 'mla_w_o': out['mla_w_o'], 'conv_w_in': out['conv_w_in'], 'conv_w': out['conv_w'], 'conv_w_out': out['conv_w_out'], 'mlp_w_up': out['mlp_w_up'], 'mlp_w_down': out['mlp_w_down'], 'loss_target': out['loss_target'], 'm_w_mod': out['m_w_mod'], 'm_b_mod': out['m_b_mod'], 'm_norm_g': out['m_norm_g'], 'm_mla_w_in': out['m_mla_w_in'], 'm_mla_g_q': out['m_mla_g_q'], 'm_mla_g_kv': out['m_mla_g_kv'], 'm_mla_w_uq': out['m_mla_w_uq'], 'm_mla_w_ukv': out['m_mla_w_ukv'], 'm_mla_w_o': out['m_mla_w_o'], 'm_conv_w_in': out['m_conv_w_in'], 'm_conv_w': out['m_conv_w'], 'm_conv_w_out': out['m_conv_w_out'], 'm_mlp_w_up': out['m_mlp_w_up'], 'm_mlp_w_down': out['m_mlp_w_down'], 'v_w_mod': out['v_w_mod'], 'v_b_mod': out['v_b_mod'], 'v_norm_g': out['v_norm_g'], 'v_mla_w_in': out['v_mla_w_in'], 'v_mla_g_q': out['v_mla_g_q'], 'v_mla_g_kv': out['v_mla_g_kv'], 'v_mla_w_uq': out['v_mla_w_uq'], 'v_mla_w_ukv': out['v_mla_w_ukv'], 'v_mla_w_o': out['v_mla_w_o'], 'v_conv_w_in': out['v_conv_w_in'], 'v_conv_w': out['v_conv_w'], 'v_conv_w_out': out['v_conv_w_out'], 'v_mlp_w_up': out['v_mlp_w_up'], 'v_mlp_w_down': out['v_mlp_w_down']}


def _loss(weights, diff, rest, loss_target):
    with _jax.named_scope("forward"):
        args = {**rest, TWIN_DIFF_INPUT: diff, **{k: w.astype(_WEIGHT_DTYPES[k]) for k, w in weights.items()}}
        y = _forward(args)
    with _jax.named_scope("loss_head"):
        err = _jnp.square(y.astype(_jnp.float32) - loss_target)
        return 0.5 * _jnp.sum(_jnp.mean(err, axis=-1)) if err.ndim else 0.5 * err


def _adamw(w, g, m, v):
    m = ADAM_B1 * m + (1.0 - ADAM_B1) * g
    v = ADAM_B2 * v + (1.0 - ADAM_B2) * _jnp.square(g)
    m_hat = m / (1.0 - ADAM_B1 ** ADAM_STEP)
    v_hat = v / (1.0 - ADAM_B2 ** ADAM_STEP)
    delta = -ADAM_LR * (m_hat / (_jnp.sqrt(v_hat) + ADAM_EPS) + ADAM_WD * w)
    return delta, m, v


def reference(x, c, positions, w_mod, b_mod, norm_g, mla_w_in, mla_g_q, mla_g_kv, mla_w_uq, mla_w_ukv, mla_w_o, conv_w_in, conv_w, conv_w_out, mlp_w_up, mlp_w_down, loss_target, m_w_mod, m_b_mod, m_norm_g, m_mla_w_in, m_mla_g_q, m_mla_g_kv, m_mla_w_uq, m_mla_w_ukv, m_mla_w_o, m_conv_w_in, m_conv_w, m_conv_w_out, m_mlp_w_up, m_mlp_w_down, v_w_mod, v_b_mod, v_norm_g, v_mla_w_in, v_mla_g_q, v_mla_g_kv, v_mla_w_uq, v_mla_w_ukv, v_mla_w_o, v_conv_w_in, v_conv_w, v_conv_w_out, v_mlp_w_up, v_mlp_w_down):
    given = dict(x=x, c=c, positions=positions, w_mod=w_mod, b_mod=b_mod, norm_g=norm_g, mla_w_in=mla_w_in, mla_g_q=mla_g_q, mla_g_kv=mla_g_kv, mla_w_uq=mla_w_uq, mla_w_ukv=mla_w_ukv, mla_w_o=mla_w_o, conv_w_in=conv_w_in, conv_w=conv_w, conv_w_out=conv_w_out, mlp_w_up=mlp_w_up, mlp_w_down=mlp_w_down, loss_target=loss_target, m_w_mod=m_w_mod, m_b_mod=m_b_mod, m_norm_g=m_norm_g, m_mla_w_in=m_mla_w_in, m_mla_g_q=m_mla_g_q, m_mla_g_kv=m_mla_g_kv, m_mla_w_uq=m_mla_w_uq, m_mla_w_ukv=m_mla_w_ukv, m_mla_w_o=m_mla_w_o, m_conv_w_in=m_conv_w_in, m_conv_w=m_conv_w, m_conv_w_out=m_conv_w_out, m_mlp_w_up=m_mlp_w_up, m_mlp_w_down=m_mlp_w_down, v_w_mod=v_w_mod, v_b_mod=v_b_mod, v_norm_g=v_norm_g, v_mla_w_in=v_mla_w_in, v_mla_g_q=v_mla_g_q, v_mla_g_kv=v_mla_g_kv, v_mla_w_uq=v_mla_w_uq, v_mla_w_ukv=v_mla_w_ukv, v_mla_w_o=v_mla_w_o, v_conv_w_in=v_conv_w_in, v_conv_w=v_conv_w, v_conv_w_out=v_conv_w_out, v_mlp_w_up=v_mlp_w_up, v_mlp_w_down=v_mlp_w_down)
    weights = {n: given[n] for n in TWIN_WEIGHTS}
    shared = {n: given[n] for n in SHARED_INPUTS}
    per_example = {n: given[n] for n in ['x', 'c', 'positions']}
    grad_fn = _jax.value_and_grad(_loss, argnums=(0, 1))

    def one_microbatch(ex, loss_target):
        ex = dict(ex)
        diff = ex.pop(TWIN_DIFF_INPUT)
        return grad_fn(weights, diff, {**shared, **ex}, loss_target)

    if N_MICROBATCH == 1:
        loss, (grad_w, grad_x) = one_microbatch(per_example, given["loss_target"])
    else:
        def body(carry, xs):
            loss_sum, grad_sum = carry
            l_k, (gw_k, gx_k) = one_microbatch(xs[0], xs[1])
            with _jax.named_scope("update"):
                return (loss_sum + l_k, _jax.tree.map(_jnp.add, grad_sum, gw_k)), gx_k

        init = (_jnp.zeros((), _jnp.float32), _jax.tree.map(_jnp.zeros_like, weights))
        (loss, grad_w), grad_x = _jax.lax.scan(body, init, (per_example, given["loss_target"]))
    with _jax.named_scope("update"):
        delta_w, new_m, new_v = {}, {}, {}
        for n in TWIN_WEIGHTS:
            delta_w[n], new_m[n], new_v[n] = _adamw(weights[n], grad_w[n], given["m_" + n], given["v_" + n])
    return (loss, grad_x, *[grad_w[n] for n in TWIN_WEIGHTS], *[delta_w[n] for n in TWIN_WEIGHTS],
            *[new_m[n] for n in TWIN_WEIGHTS], *[new_v[n] for n in TWIN_WEIGHTS])
```

```python
import jax
import jax.numpy as jnp
from jax import lax
from jax.experimental import pallas as pl
from jax.experimental.pallas import tpu as pltpu

F32 = jnp.float32
BF = jnp.bfloat16
MESH = pl.DeviceIdType.MESH
N_DEV = 8
LANES = 128
NORM_EPS = 1e-6
ROPE_THETA = 10000.0
ROPE_DIM = 64
NOPE_DIM = 128
V_DIM = 128
LR, B1, B2, ADAM_EPS, WD, STEP = 0.001, 0.9, 0.999, 1e-08, 0.01, 10
VMEM_LIMIT = 56 * 1024 * 1024
TILE_BYTES = 2 * 1024 * 1024
TM, TN, TK = 1024, 512, 2048
TQ = 512

NN = (((1,), (0,)), ((), ()))
NT = (((1,), (1,)), ((), ()))
TNDIMS = (((0,), (0,)), ((), ()))
HBM_SPEC = pl.BlockSpec(memory_space=pltpu.HBM)
VMEM_SPEC = pl.BlockSpec(memory_space=pltpu.VMEM)


def _cparams(n_axes):
    return pltpu.CompilerParams(dimension_semantics=("arbitrary",) * n_axes, vmem_limit_bytes=VMEM_LIMIT)


def _pick(dim, pref):
    if dim <= pref:
        return dim
    for t in range(pref - pref % LANES, 0, -LANES):
        if t > 0 and dim % t == 0:
            return t
    for t in range(pref, 0, -1):
        if dim % t == 0:
            return t
    return dim


def _rows(r, c, itemsize=4):
    want = max(8, TILE_BYTES // (itemsize * max(c, 1)))
    if r <= want:
        return r
    for t in range(want - want % 8, 0, -8):
        if t > 0 and r % t == 0:
            return t
    return r


def _pos():
    return lax.axis_index("x"), lax.axis_index("y"), lax.axis_index("c")


def _ag_small(name, v):
    m_per, n = v.shape

    def body(x_ref, out_ref, send_sems, recv_sems, local_sem):
        x, y, c = _pos()
        me, sibling = (x, y, c), (x, y, 1 - c)
        chips = [(1 - x, y), (x, 1 - y), (1 - x, 1 - y)]

        def rows(px, py, pc):
            return out_ref.at[pl.ds((4 * px + 2 * py + pc) * m_per, m_per), :]

        def copy(k, block, to, src=None):
            return pltpu.make_async_remote_copy(
                src_ref=rows(*block) if src is None else src, dst_ref=rows(*block),
                send_sem=send_sems.at[k], recv_sem=recv_sems.at[k], device_id=to, device_id_type=MESH)

        mine = pltpu.make_async_copy(x_ref, rows(*me), local_sem)
        mine.start()
        first = [copy(0, me, sibling, src=x_ref)]
        first += [copy(1 + j, me, (*chip, c), src=x_ref) for j, chip in enumerate(chips)]
        for cp in first:
            cp.start()
        passed = [copy(4 + j, (*chip, c), sibling) for j, chip in enumerate(chips)]
        for j, chip in enumerate(chips):
            copy(1 + j, (*chip, c), me).wait_recv()
            passed[j].start()
        copy(0, sibling, me).wait_recv()
        for j, chip in enumerate(chips):
            copy(4 + j, (*chip, 1 - c), me).wait_recv()
        for cp in first + passed:
            cp.wait_send()
        mine.wait()

    return pl.pallas_call(
        body, name=name,
        out_shape=jax.ShapeDtypeStruct((N_DEV * m_per, n), v.dtype),
        in_specs=[VMEM_SPEC], out_specs=VMEM_SPEC,
        scratch_shapes=[pltpu.SemaphoreType.DMA((7,)), pltpu.SemaphoreType.DMA((7,)), pltpu.SemaphoreType.DMA],
    )(v)


def _ag_big(name, arrs):
    n = len(arrs)

    def body(*refs):
        ins, outs = refs[:n], refs[n:2 * n]
        send_sems, recv_sems, local_sems = refs[2 * n:]
        x, y, c = _pos()
        me, sibling = (x, y, c), (x, y, 1 - c)
        chips = [(1 - x, y), (x, 1 - y), (1 - x, 1 - y)]

        def copy(a, k, block, to, src=None):
            dst = outs[a].at[4 * block[0] + 2 * block[1] + block[2]]
            return pltpu.make_async_remote_copy(
                src_ref=dst if src is None else src, dst_ref=dst,
                send_sem=send_sems.at[7 * a + k], recv_sem=recv_sems.at[7 * a + k],
                device_id=to, device_id_type=MESH)

        mine = [pltpu.make_async_copy(ins[a], outs[a].at[4 * x + 2 * y + c], local_sems.at[a]) for a in range(n)]
        for cp in mine:
            cp.start()
        first = []
        for a in range(n):
            first.append(copy(a, 0, me, sibling, src=ins[a]))
            first += [copy(a, 1 + j, me, (*chip, c), src=ins[a]) for j, chip in enumerate(chips)]
        for cp in first:
            cp.start()
        passed = []
        for j, chip in enumerate(chips):
            for a in range(n):
                copy(a, 1 + j, (*chip, c), me).wait_recv()
                cp = copy(a, 4 + j, (*chip, c), sibling)
                cp.start()
                passed.append(cp)
        for a in range(n):
            copy(a, 0, sibling, me).wait_recv()
            for j, chip in enumerate(chips):
                copy(a, 4 + j, (*chip, 1 - c), me).wait_recv()
        for cp in first + passed:
            cp.wait_send()
        for cp in mine:
            cp.wait()

    return pl.pallas_call(
        body, name=name,
        out_shape=[jax.ShapeDtypeStruct((N_DEV,) + a.shape, a.dtype) for a in arrs],
        in_specs=[HBM_SPEC] * n, out_specs=[HBM_SPEC] * n,
        scratch_shapes=[pltpu.SemaphoreType.DMA((7 * n,)), pltpu.SemaphoreType.DMA((7 * n,)),
                        pltpu.SemaphoreType.DMA((n,))],
    )(*arrs)


def _rs_sibling(name, arrs):
    n = len(arrs)

    def body(*refs):
        ins, outs = refs[:n], refs[n:2 * n]
        send_sems, recv_sems = refs[2 * n:]
        x, y, c = _pos()
        copies = []
        for a in range(n):
            for j in range(4):
                copies.append(pltpu.make_async_remote_copy(
                    src_ref=ins[a].at[2 * j + (1 - c)], dst_ref=outs[a].at[j],
                    send_sem=send_sems.at[4 * a + j], recv_sem=recv_sems.at[4 * a + j],
                    device_id=(x, y, 1 - c), device_id_type=MESH))
        for cp in copies:
            cp.start()
        for cp in copies:
            cp.wait()

    return pl.pallas_call(
        body, name=name,
        out_shape=[jax.ShapeDtypeStruct((4,) + a.shape[1:], a.dtype) for a in arrs],
        in_specs=[HBM_SPEC] * n, out_specs=[HBM_SPEC] * n,
        scratch_shapes=[pltpu.SemaphoreType.DMA((4 * n,)), pltpu.SemaphoreType.DMA((4 * n,))],
    )(*arrs)


def _rs_chips(name, arrs):
    n = len(arrs)

    def body(*refs):
        ins, outs = refs[:n], refs[n:2 * n]
        send_sems, recv_sems = refs[2 * n:]
        x, y, c = _pos()
        chips = [(1 - x, y), (x, 1 - y), (1 - x, 1 - y)]
        copies = []
        for a in range(n):
            for k, chip in enumerate(chips):
                copies.append(pltpu.make_async_remote_copy(
                    src_ref=ins[a].at[2 * chip[0] + chip[1]], dst_ref=outs[a].at[k],
                    send_sem=send_sems.at[3 * a + k], recv_sem=recv_sems.at[3 * a + k],
                    device_id=(*chip, c), device_id_type=MESH))
        for cp in copies:
            cp.start()
        for cp in copies:
            cp.wait()

    return pl.pallas_call(
        body, name=name,
        out_shape=[jax.ShapeDtypeStruct((3,) + a.shape[1:], a.dtype) for a in arrs],
        in_specs=[HBM_SPEC] * n, out_specs=[HBM_SPEC] * n,
        scratch_shapes=[pltpu.SemaphoreType.DMA((3 * n,)), pltpu.SemaphoreType.DMA((3 * n,))],
    )(*arrs)


def _pair_add(name, g, r1, c_idx):
    _, r, cdim = g.shape
    tr = _rows(r, cdim, 2)

    def body(c_ref, g_ref, r_ref, o_ref):
        o_ref[...] = (g_ref[...].astype(F32) + r_ref[...].astype(F32)).astype(o_ref.dtype)

    return pl.pallas_call(
        body, name=name,
        out_shape=jax.ShapeDtypeStruct((4, r, cdim), BF),
        grid_spec=pltpu.PrefetchScalarGridSpec(
            num_scalar_prefetch=1, grid=(4, r // tr),
            in_specs=[pl.BlockSpec((None, tr, cdim), lambda j, i, c_ref: (2 * j + c_ref[0], i, 0)),
                      pl.BlockSpec((None, tr, cdim), lambda j, i, c_ref: (j, i, 0))],
            out_specs=pl.BlockSpec((None, tr, cdim), lambda j, i, c_ref: (j, i, 0))),
        compiler_params=_cparams(2),
    )(c_idx, g, r1)


def _chip_sum(name, p, r2, chip_idx):
    _, r, cdim = p.shape
    tr = _rows(r, cdim, 4)

    def body(s_ref, p_ref, a_ref, b_ref, c_ref, o_ref):
        o_ref[...] = ((p_ref[...].astype(F32) + a_ref[...].astype(F32)) + b_ref[...].astype(F32)) + c_ref[...].astype(F32)

    def other(k):
        return pl.BlockSpec((None, tr, cdim), lambda i, s_ref: (k, i, 0))

    return pl.pallas_call(
        body, name=name,
        out_shape=jax.ShapeDtypeStruct((r, cdim), F32),
        grid_spec=pltpu.PrefetchScalarGridSpec(
            num_scalar_prefetch=1, grid=(r // tr,),
            in_specs=[pl.BlockSpec((None, tr, cdim), lambda i, s_ref: (s_ref[0], i, 0)), other(0), other(1), other(2)],
            out_specs=pl.BlockSpec((tr, cdim), lambda i, s_ref: (i, 0))),
        compiler_params=_cparams(1),
    )(chip_idx, p, r2, r2, r2)


def _adam_math(w, g, m, v):
    m = B1 * m + (1.0 - B1) * g
    v = B2 * v + (1.0 - B2) * (g * g)
    m_hat = m / (1.0 - B1 ** STEP)
    v_hat = v / (1.0 - B2 ** STEP)
    delta = -LR * (m_hat / (jnp.sqrt(v_hat) + ADAM_EPS) + WD * w)
    return delta, m, v


def _adam(name, w, g, m, v):
    shape = w.shape
    cdim = shape[-1]
    r = w.size // cdim
    flat = [a.reshape(r, cdim) for a in (w, g, m, v)]
    tr = _rows(r, cdim, 4)

    def body(w_ref, g_ref, m_ref, v_ref, d_ref, mo_ref, vo_ref):
        d, mn, vn = _adam_math(w_ref[...], g_ref[...], m_ref[...], v_ref[...])
        d_ref[...] = d
        mo_ref[...] = mn
        vo_ref[...] = vn

    spec = pl.BlockSpec((tr, cdim), lambda i: (i, 0))
    outs = pl.pallas_call(
        body, name=name,
        out_shape=[jax.ShapeDtypeStruct((r, cdim), F32)] * 3,
        grid=(r // tr,), in_specs=[spec] * 4, out_specs=[spec] * 3,
        compiler_params=_cparams(1),
    )(*flat)
    return [o.reshape(shape) for o in outs]


def _mod_matvec(name, cond_all, w_loc):
    d, n_loc = w_loc.shape
    tn = _pick(n_loc, 512)

    def body(c_ref, w_ref, o_ref):
        o_ref[...] = jnp.dot(c_ref[...].astype(BF), w_ref[...].astype(BF), preferred_element_type=F32)

    return pl.pallas_call(
        body, name=name,
        out_shape=jax.ShapeDtypeStruct((N_DEV, n_loc), F32),
        grid=(n_loc // tn,),
        in_specs=[pl.BlockSpec((N_DEV, d), lambda j: (0, 0)), pl.BlockSpec((d, tn), lambda j: (0, j))],
        out_specs=pl.BlockSpec((N_DEV, tn), lambda j: (0, j)),
        compiler_params=_cparams(1),
    )(cond_all, w_loc)


def _mod_grad_adam(name, cond_t, dmod, w, m, v):
    d, n_loc = w.shape
    tr = _rows(d, n_loc, 4)

    def body(ct_ref, dm_ref, w_ref, m_ref, v_ref, g_ref, d_ref, mo_ref, vo_ref):
        g = ct_ref[:, 0:1] * dm_ref[0:1, :]
        for b in range(1, N_DEV):
            g = g + ct_ref[:, b:b + 1] * dm_ref[b:b + 1, :]
        dl, mn, vn = _adam_math(w_ref[...], g, m_ref[...], v_ref[...])
        g_ref[...] = g
        d_ref[...] = dl
        mo_ref[...] = mn
        vo_ref[...] = vn

    spec = pl.BlockSpec((tr, n_loc), lambda i: (i, 0))
    return pl.pallas_call(
        body, name=name,
        out_shape=[jax.ShapeDtypeStruct((d, n_loc), F32)] * 4,
        grid=(d // tr,),
        in_specs=[pl.BlockSpec((tr, N_DEV), lambda i: (i, 0)), pl.BlockSpec((N_DEV, n_loc), lambda i: (0, 0)), spec, spec, spec],
        out_specs=[spec] * 4,
        compiler_params=_cparams(1),
    )(cond_t, dmod, w, m, v)


def _sum_devices(name, gathered):
    _, m, n = gathered.shape

    def body(g_ref, o_ref):
        acc = g_ref[0]
        for d in range(1, N_DEV):
            acc = acc + g_ref[d]
        o_ref[...] = acc

    return pl.pallas_call(
        body, name=name, out_shape=jax.ShapeDtypeStruct((m, n), F32),
        in_specs=[VMEM_SPEC], out_specs=VMEM_SPEC,
    )(gathered)


def _silu(name, c_pad):
    def body(c_ref, o_ref):
        c = c_ref[...]
        o_ref[...] = c * (1.0 / (1.0 + jnp.exp(-c)))

    return pl.pallas_call(body, name=name, out_shape=jax.ShapeDtypeStruct(c_pad.shape, F32),
                          in_specs=[VMEM_SPEC], out_specs=VMEM_SPEC)(c_pad)


def _mm(name, a, b, dims, grid, a_spec, b_spec, outs, extras=(), epilogue=None):
    nk = grid[2]
    n_ex, n_out = len(extras), len(outs)

    def body(*refs):
        a_ref, b_ref = refs[0], refs[1]
        ex_refs = refs[2:2 + n_ex]
        out_refs = refs[2 + n_ex:2 + n_ex + n_out]
        part = lax.dot_general(a_ref[...].astype(BF), b_ref[...].astype(BF), dims, preferred_element_type=F32)

        def finish(acc):
            vals = (acc,) if epilogue is None else epilogue(acc, *[e[...] for e in ex_refs])
            for o_ref, val in zip(out_refs, vals):
                o_ref[...] = val.astype(o_ref.dtype)

        if nk == 1:
            finish(part)
        else:
            acc_ref = refs[-1]
            k = pl.program_id(2)

            @pl.when(k == 0)
            def _():
                acc_ref[...] = part

            @pl.when(k > 0)
            def _():
                acc_ref[...] += part

            @pl.when(k == nk - 1)
            def _():
                finish(acc_ref[...])

    tile = outs[0][2].block_shape
    acc_shape = tuple(t for t in tile if t is not None)
    res = pl.pallas_call(
        body, name=name,
        out_shape=[jax.ShapeDtypeStruct(s, d) for s, d, _ in outs],
        grid=grid,
        in_specs=[a_spec, b_spec] + [s for _, s in extras],
        out_specs=[s for _, _, s in outs],
        scratch_shapes=[] if nk == 1 else [pltpu.VMEM(acc_shape, F32)],
        compiler_params=_cparams(3),
    )(a, b, *[e for e, _ in extras])
    return res


def _w_nn_spec(w, tk, tn):
    if w.ndim == 2:
        return pl.BlockSpec((tk, tn), lambda i, j, k: (k, j))
    per = w.shape[2] // tn
    return pl.BlockSpec((None, tk, tn), lambda i, j, k: (j // per, k, j % per))


def _w_nt_spec(w, tn, tk):
    if w.ndim == 2:
        return pl.BlockSpec((tn, tk), lambda i, j, k: (j, k))
    per = w.shape[2] // tk
    return pl.BlockSpec((None, tn, tk), lambda i, j, k: (k // per, j, k % per))


def _w_cols(w):
    return w.shape[1] if w.ndim == 2 else w.shape[0] * w.shape[2]


def _w_rows(w):
    return w.shape[0] if w.ndim == 2 else w.shape[1]


def _col_tile(w, pref):
    return _pick(w.shape[1] if w.ndim == 2 else w.shape[2], pref)


def _linear(name, a, w, out_dtypes=(F32,), epilogue=None, extras=()):
    t, kdim = a.shape
    n = _w_cols(w)
    tm, tn, tk = _pick(t, TM), _col_tile(w, TN), _pick(kdim, TK)
    o_spec = pl.BlockSpec((tm, tn), lambda i, j, k: (i, j))
    return _mm(name, a, w, NN, (t // tm, n // tn, kdim // tk),
               pl.BlockSpec((tm, tk), lambda i, j, k: (i, k)), _w_nn_spec(w, tk, tn),
               [((t, n), dt, o_spec) for dt in out_dtypes],
               extras=[(e, o_spec) for e in extras], epilogue=epilogue)


def _linear_t(name, dy, w, out_dtypes=(F32,), epilogue=None, extras=()):
    t, n = dy.shape
    kdim = _w_rows(w)
    tm, tn, tk = _pick(t, TM), _pick(kdim, TN), _col_tile(w, TK)
    o_spec = pl.BlockSpec((tm, tn), lambda i, j, k: (i, j))
    return _mm(name, dy, w, NT, (t // tm, kdim // tn, n // tk),
               pl.BlockSpec((tm, tk), lambda i, j, k: (i, k)), _w_nt_spec(w, tn, tk),
               [((t, kdim), dt, o_spec) for dt in out_dtypes],
               extras=[(e, o_spec) for e in extras], epilogue=epilogue)


def _wgrad(name, a, dy, like):
    t, kdim = a.shape
    n = dy.shape[1]
    tm, tn, tk = _pick(kdim, TM), _col_tile(like, TM), _pick(t, 1024)
    if like.ndim == 2:
        o_spec = pl.BlockSpec((tm, tn), lambda i, j, k: (i, j))
    else:
        per = like.shape[2] // tn
        o_spec = pl.BlockSpec((None, tm, tn), lambda i, j, k: (j // per, i, j % per))
    return _mm(name, a, dy, TNDIMS, (kdim // tm, n // tn, t // tk),
               pl.BlockSpec((tk, tm), lambda i, j, k: (k, i)), pl.BlockSpec((tk, tn), lambda i, j, k: (k, j)),
               [(like.shape, BF, o_spec)])[0]


def _rstd(x):
    return lax.rsqrt(jnp.mean(x * x, axis=-1, keepdims=True) + NORM_EPS)


def _colsum(x):
    return jnp.sum(x, axis=0, keepdims=True)


def _norm_fwd(name, x, y, vec, target, has_post, has_pre, has_loss):
    t, d = x.shape
    tm = _rows(t, d, 4)
    ins, in_specs = [x], [pl.BlockSpec((tm, d), lambda i: (i, 0))]
    row_spec = pl.BlockSpec((tm, d), lambda i: (i, 0))
    if has_post:
        ins.append(y)
        in_specs.append(row_spec)
    ins.append(vec)
    in_specs.append(pl.BlockSpec((8, d), lambda i: (0, 0)))
    if has_loss:
        ins.append(target)
        in_specs.append(row_spec)
    out_shape, out_specs = [], []
    if has_post and not has_loss:
        out_shape.append(jax.ShapeDtypeStruct((t, d), F32))
        out_specs.append(row_spec)
    if has_pre:
        out_shape.append(jax.ShapeDtypeStruct((t, d), BF))
        out_specs.append(row_spec)
    if has_loss:
        out_shape += [jax.ShapeDtypeStruct((t, d), F32), jax.ShapeDtypeStruct((8, d), F32)]
        out_specs += [row_spec, pl.BlockSpec((8, d), lambda i: (0, 0))]

    def body(*refs):
        it = iter(refs)
        x_ref = next(it)
        y_ref = next(it) if has_post else None
        vec_ref = next(it)
        tgt_ref = next(it) if has_loss else None
        xv = x_ref[...]
        if has_post:
            yv = y_ref[...]
            xv = xv + vec_ref[1:2, :] * ((yv * _rstd(yv)) * vec_ref[0:1, :])
            if not has_loss:
                next(it)[...] = xv
        if has_pre:
            hv = ((xv * _rstd(xv)) * vec_ref[2:3, :]) * (1.0 + vec_ref[3:4, :]) + vec_ref[4:5, :]
            next(it)[...] = hv.astype(BF)
        if has_loss:
            e = xv - tgt_ref[...]
            next(it)[...] = e * (1.0 / d)
            acc_ref = next(it)
            i = pl.program_id(0)

            @pl.when(i == 0)
            def _():
                acc_ref[...] = jnp.zeros_like(acc_ref)

            acc_ref[0:1, :] += _colsum(e * e)

    return pl.pallas_call(
        body, name=name, out_shape=out_shape, grid=(t // tm,), in_specs=in_specs, out_specs=out_specs,
        compiler_params=_cparams(1),
    )(*ins)


def _norm_bwd(name, dx_in, dh, x, y, vec, has_pre, has_post):
    t, d = dx_in.shape
    tm = _rows(t, d, 4)
    row_spec = pl.BlockSpec((tm, d), lambda i: (i, 0))
    vec_spec = pl.BlockSpec((8, d), lambda i: (0, 0))
    ins, in_specs = [dx_in], [row_spec]
    if has_pre:
        ins += [dh, x]
        in_specs += [row_spec, row_spec]
    if has_post:
        ins.append(y)
        in_specs.append(row_spec)
    ins.append(vec)
    in_specs.append(vec_spec)
    out_shape, out_specs = [], []
    if has_pre:
        out_shape.append(jax.ShapeDtypeStruct((t, d), F32))
        out_specs.append(row_spec)
    if has_post:
        out_shape.append(jax.ShapeDtypeStruct((t, d), BF))
        out_specs.append(row_spec)
    out_shape.append(jax.ShapeDtypeStruct((8, d), F32))
    out_specs.append(vec_spec)

    def body(*refs):
        it = iter(refs)
        dx = next(it)[...]
        dh_ref = next(it) if has_pre else None
        x_ref = next(it) if has_pre else None
        y_ref = next(it) if has_post else None
        vec_ref = next(it)
        dx_ref = next(it) if has_pre else None
        dy_ref = next(it) if has_post else None
        sums_ref = next(it)
        i = pl.program_id(0)

        @pl.when(i == 0)
        def _():
            sums_ref[...] = jnp.zeros_like(sums_ref)

        if has_pre:
            dhv, xv = dh_ref[...], x_ref[...]
            rs = _rstd(xv)
            xhat = xv * rs
            na = vec_ref[2:3, :]
            sums_ref[0:1, :] += _colsum(dhv)
            sums_ref[1:2, :] += _colsum(dhv * (xhat * na))
            tt = dhv * (1.0 + vec_ref[3:4, :])
            sums_ref[2:3, :] += _colsum(tt * xhat)
            u = tt * na
            dx = dx + rs * (u - xhat * jnp.mean(u * xhat, axis=-1, keepdims=True))
            dx_ref[...] = dx
        if has_post:
            yv = y_ref[...]
            rs = _rstd(yv)
            yhat = yv * rs
            nb = vec_ref[0:1, :]
            sums_ref[3:4, :] += _colsum(dx * (yhat * nb))
            tt = dx * vec_ref[1:2, :]
            sums_ref[4:5, :] += _colsum(tt * yhat)
            u = tt * nb
            dy_ref[...] = (rs * (u - yhat * jnp.mean(u * yhat, axis=-1, keepdims=True))).astype(BF)

    return pl.pallas_call(
        body, name=name, out_shape=out_shape, grid=(t // tm,), in_specs=in_specs, out_specs=out_specs,
        compiler_params=_cparams(1),
    )(*ins)


def _rope128(tv, cs, s1, s2):
    return tv * cs + pltpu.roll(tv, 96, 1) * s1 + pltpu.roll(tv, 32, 1) * s2


def _lat_post(name, lat, gq, gkv, cs, s1, s2, rq, rkv):
    t, w = lat.shape
    tm = _rows(t, w, 4)

    def body(lat_ref, gq_ref, gkv_ref, cs_ref, s1_ref, s2_ref, cq_ref, ckv_ref, kr_ref):
        lq = lat_ref[:, 0:rq]
        lkv = lat_ref[:, rq:rq + rkv]
        cq_ref[...] = ((lq * _rstd(lq)) * gq_ref[...]).astype(BF)
        ckv_ref[...] = ((lkv * _rstd(lkv)) * gkv_ref[...]).astype(BF)
        kr_ref[...] = _rope128(lat_ref[:, rq + rkv:rq + rkv + LANES], cs_ref[...], s1_ref[...], s2_ref[...]).astype(BF)

    def rows(c):
        return pl.BlockSpec((tm, c), lambda i: (i, 0))

    def vecs(c):
        return pl.BlockSpec((1, c), lambda i: (0, 0))

    return pl.pallas_call(
        body, name=name,
        out_shape=[jax.ShapeDtypeStruct((t, rq), BF), jax.ShapeDtypeStruct((t, rkv), BF), jax.ShapeDtypeStruct((t, LANES), BF)],
        grid=(t // tm,),
        in_specs=[rows(w), vecs(rq), vecs(rkv), rows(LANES), rows(LANES), rows(LANES)],
        out_specs=[rows(rq), rows(rkv), rows(LANES)],
        compiler_params=_cparams(1),
    )(lat, gq, gkv, cs, s1, s2)


def _lat_bwd(name, dcq, dckv, dkr, lat, gq, gkv, cs, s1, s2, rq, rkv, heads):
    t, w = lat.shape
    tm = _rows(t, max(w, heads * LANES), 4)
    assert rq == rkv

    def body(dcq_ref, dckv_ref, dkr_ref, lat_ref, gq_ref, gkv_ref, cs_ref, s1_ref, s2_ref, dlat_ref, sums_ref):
        i = pl.program_id(0)

        @pl.when(i == 0)
        def _():
            sums_ref[...] = jnp.zeros_like(sums_ref)

        def rms_bwd(dc, lv, g, row):
            rs = _rstd(lv)
            lhat = lv * rs
            sums_ref[row:row + 1, :] += _colsum(dc * lhat)
            u = dc * g
            return rs * (u - lhat * jnp.mean(u * lhat, axis=-1, keepdims=True))

        dlat_ref[:, 0:rq] = rms_bwd(dcq_ref[...], lat_ref[:, 0:rq], gq_ref[...], 0).astype(BF)
        dlat_ref[:, rq:rq + rkv] = rms_bwd(dckv_ref[...], lat_ref[:, rq:rq + rkv], gkv_ref[...], 1).astype(BF)
        dk = dkr_ref[:, 0:LANES]
        for h in range(1, heads):
            dk = dk + dkr_ref[:, h * LANES:(h + 1) * LANES]
        dlat_ref[:, rq + rkv:rq + rkv + LANES] = _rope128(dk, cs_ref[...], s1_ref[...], s2_ref[...]).astype(BF)

    def rows(c):
        return pl.BlockSpec((tm, c), lambda i: (i, 0))

    def vecs(c):
        return pl.BlockSpec((1, c), lambda i: (0, 0))

    return pl.pallas_call(
        body, name=name,
        out_shape=[jax.ShapeDtypeStruct((t, w), BF), jax.ShapeDtypeStruct((8, rq), F32)],
        grid=(t // tm,),
        in_specs=[rows(rq), rows(rkv), rows(heads * LANES), rows(w), vecs(rq), vecs(rkv), rows(LANES), rows(LANES), rows(LANES)],
        out_specs=[rows(w), pl.BlockSpec((8, rq), lambda i: (0, 0))],
        compiler_params=_cparams(1),
    )(dcq, dckv, dkr, lat, gq, gkv, cs, s1, s2)


def _rope_heads(name, xa, a_blk, xb, b_blk, cs, s1, s2, heads):
    t = xa.shape[0]
    hw = heads * LANES
    tm = _rows(t, 2 * hw, 4)

    def body(a_ref, b_ref, cs_ref, s1_ref, s2_ref, o_ref):
        o_ref[:, 0:hw] = a_ref[...].astype(BF)
        csv, s1v, s2v = cs_ref[...], s1_ref[...], s2_ref[...]
        for h in range(heads):
            sl = slice(h * LANES, (h + 1) * LANES)
            o_ref[:, hw + h * LANES:hw + (h + 1) * LANES] = _rope128(b_ref[:, sl], csv, s1v, s2v).astype(BF)

    def rows(c):
        return pl.BlockSpec((tm, c), lambda i: (i, 0))

    return pl.pallas_call(
        body, name=name,
        out_shape=jax.ShapeDtypeStruct((t, 2 * hw), BF),
        grid=(t // tm,),
        in_specs=[pl.BlockSpec((tm, hw), lambda i: (i, a_blk)), pl.BlockSpec((tm, hw), lambda i: (i, b_blk)),
                  rows(LANES), rows(LANES), rows(LANES)],
        out_specs=rows(2 * hw),
        compiler_params=_cparams(1),
    )(xa, xb, cs, s1, s2)


def _causal_mask(s):
    row = lax.broadcasted_iota(jnp.int32, s.shape, 0)
    col = lax.broadcasted_iota(jnp.int32, s.shape, 1)
    return col <= row


def _flash_fwd(name, qb, kv, kr, heads, scale):
    t = qb.shape[0]
    tq = _pick(t, TQ)
    nq = t // tq

    def body(qn_ref, qr_ref, kv_ref, kr_ref, o_ref, lse_ref):
        i = pl.program_id(1)
        q = jnp.concatenate([qn_ref[...], qr_ref[...]], axis=-1)

        def block(j, carry, masked):
            m_prev, l_prev, acc = carry
            rows = pl.ds(pl.multiple_of(j * tq, tq), tq)
            k = jnp.concatenate([kv_ref[rows, 0:LANES], kr_ref[rows, :]], axis=-1)
            s = lax.dot_general(q, k, NT, preferred_element_type=F32) * scale
            if masked:
                s = jnp.where(_causal_mask(s), s, -1e30)
            m_new = jnp.maximum(m_prev, jnp.max(s, axis=-1, keepdims=True))
            alpha = jnp.exp(m_prev - m_new)
            p = jnp.exp(s - m_new)
            l_new = alpha * l_prev + jnp.sum(p, axis=-1, keepdims=True)
            pv = jnp.dot(p.astype(BF), kv_ref[rows, LANES:2 * LANES], preferred_element_type=F32)
            return m_new, l_new, alpha * acc + pv

        init = (jnp.full((tq, 1), -1e30, F32), jnp.zeros((tq, 1), F32), jnp.zeros((tq, V_DIM), F32))
        carry = lax.fori_loop(0, i, lambda j, cr: block(j, cr, False), init)
        m_fin, l_fin, acc = block(i, carry, True)
        o_ref[...] = (acc / l_fin).astype(BF)
        lse_ref[...] = jnp.broadcast_to(m_fin + jnp.log(l_fin), (tq, LANES))

    return pl.pallas_call(
        body, name=name,
        out_shape=[jax.ShapeDtypeStruct((t, heads * V_DIM), BF), jax.ShapeDtypeStruct((t, heads * LANES), F32)],
        grid=(heads, nq),
        in_specs=[pl.BlockSpec((tq, LANES), lambda h, i: (i, h)),
                  pl.BlockSpec((tq, LANES), lambda h, i: (i, heads + h)),
                  pl.BlockSpec((t, 2 * LANES), lambda h, i: (0, h)),
                  pl.BlockSpec((t, LANES), lambda h, i: (0, 0))],
        out_specs=[pl.BlockSpec((tq, V_DIM), lambda h, i: (i, h)), pl.BlockSpec((tq, LANES), lambda h, i: (i, h))],
        compiler_params=_cparams(2),
    )(qb, qb, kv, kr)


def _flash_bwd_q(name, qb, kv, kr, o, lse, do, heads, scale):
    t = qb.shape[0]
    tq = _pick(t, TQ)
    nq = t // tq

    def body(qn_ref, qr_ref, kv_ref, kr_ref, o_ref, lse_ref, do_ref, dqn_ref, dqr_ref):
        i = pl.program_id(1)
        q = jnp.concatenate([qn_ref[...], qr_ref[...]], axis=-1)
        dov = do_ref[...]
        delta = jnp.sum(dov.astype(F32) * o_ref[...].astype(F32), axis=-1, keepdims=True)
        lse_col = lse_ref[:, 0:1]

        def block(j, dq, masked):
            rows = pl.ds(pl.multiple_of(j * tq, tq), tq)
            k = jnp.concatenate([kv_ref[rows, 0:LANES], kr_ref[rows, :]], axis=-1)
            s = lax.dot_general(q, k, NT, preferred_element_type=F32) * scale
            p = jnp.exp(s - lse_col)
            if masked:
                p = jnp.where(_causal_mask(s), p, 0.0)
            dp = lax.dot_general(dov, kv_ref[rows, LANES:2 * LANES], NT, preferred_element_type=F32)
            ds = (p * (dp - delta) * scale).astype(BF)
            return dq + jnp.dot(ds, k, preferred_element_type=F32)

        dq = lax.fori_loop(0, i, lambda j, acc: block(j, acc, False), jnp.zeros((tq, 2 * LANES), F32))
        dq = block(i, dq, True)
        dqn_ref[...] = dq[:, 0:LANES]
        dqr_ref[...] = dq[:, LANES:2 * LANES]

    head_blk = pl.BlockSpec((tq, LANES), lambda h, i: (i, h))
    return pl.pallas_call(
        body, name=name,
        out_shape=[jax.ShapeDtypeStruct((t, heads * LANES), F32)] * 2,
        grid=(heads, nq),
        in_specs=[head_blk, pl.BlockSpec((tq, LANES), lambda h, i: (i, heads + h)),
                  pl.BlockSpec((t, 2 * LANES), lambda h, i: (0, h)), pl.BlockSpec((t, LANES), lambda h, i: (0, 0)),
                  head_blk, head_blk, head_blk],
        out_specs=[head_blk, head_blk],
        compiler_params=_cparams(2),
    )(qb, qb, kv, kr, o, lse, do)


def _flash_bwd_kv(name, qb, kv, kr, o, lse, do, heads, scale):
    t = qb.shape[0]
    tq = _pick(t, TQ)
    nq = t // tq

    def body(qn_ref, qr_ref, kv_ref, kr_ref, o_ref, lse_ref, do_ref, dkv_ref, dkr_ref):
        j = pl.program_id(1)
        k = jnp.concatenate([kv_ref[:, 0:LANES], kr_ref[...]], axis=-1)
        v = kv_ref[:, LANES:2 * LANES]

        def block(i, carry, masked):
            dk, dv = carry
            rows = pl.ds(pl.multiple_of(i * tq, tq), tq)
            q = jnp.concatenate([qn_ref[rows, :], qr_ref[rows, :]], axis=-1)
            dov = do_ref[rows, :]
            delta = jnp.sum(dov.astype(F32) * o_ref[rows, :].astype(F32), axis=-1, keepdims=True)
            s = lax.dot_general(q, k, NT, preferred_element_type=F32) * scale
            p = jnp.exp(s - lse_ref[rows, 0:1])
            if masked:
                p = jnp.where(_causal_mask(s), p, 0.0)
            dv = dv + lax.dot_general(p.astype(BF), dov, TNDIMS, preferred_element_type=F32)
            dp = lax.dot_general(dov, v, NT, preferred_element_type=F32)
            ds = (p * (dp - delta) * scale).astype(BF)
            dk = dk + lax.dot_general(ds, q, TNDIMS, preferred_element_type=F32)
            return dk, dv

        carry = block(j, (jnp.zeros((tq, 2 * LANES), F32), jnp.zeros((tq, V_DIM), F32)), True)
        dk, dv = lax.fori_loop(j + 1, nq, lambda i, cr: block(i, cr, False), carry)
        dkv_ref[:, 0:LANES] = dk[:, 0:LANES].astype(BF)
        dkv_ref[:, LANES:2 * LANES] = dv.astype(BF)
        dkr_ref[...] = dk[:, LANES:2 * LANES]

    def full(hoff):
        return pl.BlockSpec((t, LANES), lambda h, j: (0, hoff + h))

    return pl.pallas_call(
        body, name=name,
        out_shape=[jax.ShapeDtypeStruct((t, heads * 2 * LANES), BF), jax.ShapeDtypeStruct((t, heads * LANES), F32)],
        grid=(heads, nq),
        in_specs=[full(0), full(heads), pl.BlockSpec((tq, 2 * LANES), lambda h, j: (j, h)),
                  pl.BlockSpec((tq, LANES), lambda h, j: (j, 0)), full(0), full(0), full(0)],
        out_specs=[pl.BlockSpec((tq, 2 * LANES), lambda h, j: (j, h)), pl.BlockSpec((tq, LANES), lambda h, j: (j, h))],
        compiler_params=_cparams(2),
    )(qb, qb, kv, kr, o, lse, do)


def _shift_down(z, n):
    row = lax.broadcasted_iota(jnp.int32, z.shape, 0)
    return jnp.where(row >= n, pltpu.roll(z, n, 0), 0.0)


def _shift_up(z, n):
    t = z.shape[0]
    row = lax.broadcasted_iota(jnp.int32, z.shape, 0)
    return jnp.where(row < t - n, pltpu.roll(z, t - n, 0), 0.0)


def _conv_fwd(name, proj, cw):
    t, d3 = proj.shape
    d = d3 // 3
    tn = _pick(d, LANES)
    nb = d // tn

    def body(b_ref, c_ref, u_ref, w_ref, o_ref):
        z = c_ref[...] * u_ref[...]
        zc = w_ref[0:1, :] * _shift_down(z, 2) + w_ref[1:2, :] * _shift_down(z, 1) + w_ref[2:3, :] * z
        o_ref[...] = (b_ref[...] * zc).astype(BF)

    def part(p):
        return pl.BlockSpec((t, tn), lambda j: (0, p * nb + j))

    return pl.pallas_call(
        body, name=name, out_shape=jax.ShapeDtypeStruct((t, d), BF), grid=(nb,),
        in_specs=[part(0), part(1), part(2), pl.BlockSpec((8, tn), lambda j: (0, j))],
        out_specs=pl.BlockSpec((t, tn), lambda j: (0, j)),
        compiler_params=_cparams(1),
    )(proj, proj, proj, cw)


def _conv_bwd(name, dbz, proj, cw):
    t, d3 = proj.shape
    d = d3 // 3
    tn = _pick(d, LANES)
    nb = d // tn

    def body(g_ref, b_ref, c_ref, u_ref, w_ref, o_ref, sums_ref):
        p = pl.program_id(1)
        w0, w1, w2 = w_ref[0:1, :], w_ref[1:2, :], w_ref[2:3, :]

        @pl.when(p == 0)
        def _():
            z = c_ref[...] * u_ref[...]
            z1, z2 = _shift_down(z, 1), _shift_down(z, 2)
            gv = g_ref[...]
            o_ref[...] = (gv * (w0 * z2 + w1 * z1 + w2 * z)).astype(BF)
            dzc = gv * b_ref[...]
            sums_ref[...] = jnp.zeros_like(sums_ref)
            sums_ref[0:1, :] = _colsum(dzc * z2)
            sums_ref[1:2, :] = _colsum(dzc * z1)
            sums_ref[2:3, :] = _colsum(dzc * z)

        @pl.when(p > 0)
        def _():
            dzc = g_ref[...] * b_ref[...]
            dz = w2 * dzc + w1 * _shift_up(dzc, 1) + w0 * _shift_up(dzc, 2)
            other = jnp.where(p == 1, u_ref[...], c_ref[...])
            o_ref[...] = (dz * other).astype(BF)

    def part(q):
        return pl.BlockSpec((t, tn), lambda j, p: (0, q * nb + j))

    return pl.pallas_call(
        body, name=name,
        out_shape=[jax.ShapeDtypeStruct((t, d3), BF), jax.ShapeDtypeStruct((8, d), F32)],
        grid=(nb, 3),
        in_specs=[pl.BlockSpec((t, tn), lambda j, p: (0, j)), part(0), part(1), part(2),
                  pl.BlockSpec((8, tn), lambda j, p: (0, j))],
        out_specs=[pl.BlockSpec((t, tn), lambda j, p: (0, p * nb + j)), pl.BlockSpec((8, tn), lambda j, p: (0, j))],
        compiler_params=_cparams(2),
    )(dbz, proj, proj, proj, cw)


def _pad_rows8(v):
    return jnp.pad(v, ((0, 8 - v.shape[0]), (0, 0)))


def kernel(x, c, positions, w_mod, b_mod, norm_g, mla_w_in, mla_g_q, mla_g_kv, mla_w_uq, mla_w_ukv, mla_w_o, conv_w_in, conv_w, conv_w_out, mlp_w_up, mlp_w_down, loss_target, m_w_mod, m_b_mod, m_norm_g, m_mla_w_in, m_mla_g_q, m_mla_g_kv, m_mla_w_uq, m_mla_w_ukv, m_mla_w_o, m_conv_w_in, m_conv_w, m_conv_w_out, m_mlp_w_up, m_mlp_w_down, v_w_mod, v_b_mod, v_norm_g, v_mla_w_in, v_mla_g_q, v_mla_g_kv, v_mla_w_uq, v_mla_w_ukv, v_mla_w_o, v_conv_w_in, v_conv_w, v_conv_w_out, v_mlp_w_up, v_mlp_w_down):
    t, d = x.shape[1], x.shape[2]
    depth = w_mod.shape[0]
    n_mod_loc = w_mod.shape[2]
    d_loc = norm_g.shape[2]
    rq = mla_g_q.shape[1]
    rkv = mla_g_kv.shape[1]
    heads = mla_w_uq.shape[2]
    lat_w = mla_w_in.shape[2]
    lat_pad = rq + rkv + LANES
    scale = (NOPE_DIM + ROPE_DIM) ** -0.5
    xi, yi, ci = _pos()
    dev = 4 * xi + 2 * yi + ci
    c_idx = jnp.reshape(ci, (1,)).astype(jnp.int32)
    chip_idx = jnp.reshape(2 * xi + yi, (1,)).astype(jnp.int32)
    x0 = x[0]
    target = loss_target[0]

    w_in_loc = jnp.pad(mla_w_in[0], ((0, 0), (0, lat_pad - lat_w))).astype(BF)
    uq = mla_w_uq[0]
    uq_loc = jnp.concatenate(
        [uq[:, :, :NOPE_DIM].reshape(uq.shape[0], heads * NOPE_DIM),
         jnp.pad(uq[:, :, NOPE_DIM:], ((0, 0), (0, 0), (0, LANES - ROPE_DIM))).reshape(uq.shape[0], heads * LANES)],
        axis=1).astype(BF)
    ukv_loc = mla_w_ukv[0].reshape(mla_w_ukv.shape[1], heads * (NOPE_DIM + V_DIM)).astype(BF)
    shards = [w_in_loc, uq_loc, ukv_loc, mla_w_o[0].astype(BF), conv_w_in[0].astype(BF), conv_w_out[0].astype(BF)]
    shards += [mlp_w_up[i].astype(BF) for i in range(depth)] + [mlp_w_down[i].astype(BF) for i in range(depth)]
    gathered = _ag_big("ag_weights", shards)
    g_win, g_uq, g_ukv, g_wo, g_cin, g_cout = gathered[:6]
    g_up, g_dn = gathered[6:6 + depth], gathered[6 + depth:6 + 2 * depth]

    def merged(g):
        return g.reshape(g.shape[0] * g.shape[1], g.shape[2])

    g_win, g_uq, g_ukv, g_wo, g_cout = [merged(g) for g in (g_win, g_uq, g_ukv, g_wo, g_cout)]
    g_dn = [merged(g) for g in g_dn]

    n_ng = depth * 4 * d_loc
    small = jnp.concatenate([c.reshape(-1), norm_g.reshape(-1), conv_w.reshape(-1)])
    small_n = small.shape[0]
    small_rows = -(-small_n // (8 * LANES)) * 8
    small = jnp.pad(small, (0, small_rows * LANES - small_n)).reshape(small_rows, LANES)
    small_all = _ag_small("ag_small", small).reshape(N_DEV, small_rows * LANES)
    c_all = small_all[:, :d]
    ng_all = small_all[:, d:d + n_ng].reshape(N_DEV, depth, 4, d_loc).transpose(1, 2, 0, 3).reshape(depth, 4, d)
    cw_all = small_all[:, d + n_ng:d + n_ng + 3 * d_loc].reshape(N_DEV, 3, d_loc).transpose(1, 0, 2).reshape(3, d)
    cw8 = _pad_rows8(cw_all)
    cond_all = _silu("silu", c_all)

    mod_parts = jnp.stack([_mod_matvec(f"mod_matvec{i}", cond_all, w_mod[i]) for i in range(depth)])
    mod_rows = depth * N_DEV * n_mod_loc // LANES
    mod_all = _ag_small("ag_mod", mod_parts.reshape(mod_rows, LANES)).reshape(N_DEV, depth, N_DEV, n_mod_loc)
    mod_mine = lax.dynamic_index_in_dim(mod_all, dev, axis=2, keepdims=False)
    mod = mod_mine.transpose(1, 0, 2).reshape(depth, N_DEV * n_mod_loc) + b_mod
    mod = mod.reshape(depth, 6, d)

    def fwd_vec(nb, gate, na, sc, sh):
        zero = jnp.zeros((d,), F32)
        rows = [zero if r is None else r for r in (nb, gate, na, sc, sh)]
        return _pad_rows8(jnp.stack(rows))

    inv_freq = ROPE_THETA ** (-jnp.arange(0, ROPE_DIM, 2, dtype=F32) / ROPE_DIM)
    ang = positions[0].astype(F32)[:, None] * inv_freq
    cos, sin = jnp.cos(ang), jnp.sin(ang)
    zh = jnp.zeros_like(cos)
    zpad = jnp.zeros((t, LANES - ROPE_DIM), F32)
    rope_c = jnp.concatenate([cos, cos, zpad], axis=1)
    rope_s1 = jnp.concatenate([-sin, zh, zpad], axis=1)
    rope_s2 = jnp.concatenate([zh, sin, zpad], axis=1)

    (h0,) = _norm_fwd("pre0", x0, None, fwd_vec(None, None, ng_all[0, 0], mod[0, 1], mod[0, 0]), None, False, True, False)
    (lat,) = _linear("mla_lat", h0, g_win)
    cq, ckv, kr = _lat_post("mla_lat_post", lat, mla_g_q, mla_g_kv, rope_c, rope_s1, rope_s2, rq, rkv)
    (q_raw,) = _linear("mla_q", cq, g_uq)
    qb = _rope_heads("mla_q_rope", q_raw, 0, q_raw, 1, rope_c, rope_s1, rope_s2, heads)
    (kvb,) = _linear("mla_kv", ckv, g_ukv, out_dtypes=(BF,))
    o, lse = _flash_fwd("mla_attn", qb, kvb, kr, heads, scale)
    (y0,) = _linear("mla_out", o, g_wo)
    x1, h1 = _norm_fwd("postpre1", x0, y0, fwd_vec(ng_all[0, 1], mod[0, 2], ng_all[0, 2], mod[0, 4], mod[0, 3]), None, True, True, False)

    def sq_relu(acc):
        a = jnp.maximum(acc, 0.0)
        return acc, a * a

    u1, act1 = _linear("mlp0_up", h1, g_up[0], out_dtypes=(F32, BF), epilogue=sq_relu)
    (y1,) = _linear("mlp0_down", act1, g_dn[0])
    x2, h2 = _norm_fwd("postpre2", x1, y1, fwd_vec(ng_all[0, 3], mod[0, 5], ng_all[1, 0], mod[1, 1], mod[1, 0]), None, True, True, False)

    (proj,) = _linear("conv_in", h2, g_cin)
    bz = _conv_fwd("conv_mix", proj, cw8)
    (y2,) = _linear("conv_out", bz, g_cout)
    x3, h3 = _norm_fwd("postpre3", x2, y2, fwd_vec(ng_all[1, 1], mod[1, 2], ng_all[1, 2], mod[1, 4], mod[1, 3]), None, True, True, False)

    u3, act3 = _linear("mlp1_up", h3, g_up[1], out_dtypes=(F32, BF), epilogue=sq_relu)
    (y3,) = _linear("mlp1_down", act3, g_dn[1])
    dx4, loss_cols = _norm_fwd("post_loss", x3, y3, fwd_vec(ng_all[1, 3], mod[1, 5], None, None, None), target, True, False, True)
    loss = lax.psum(0.5 * jnp.sum(loss_cols[0]) / d, ("x", "y", "c"))

    def bwd_vec(nb, gate, na, sc):
        zero = jnp.zeros((d,), F32)
        rows = [zero if r is None else r for r in (nb, gate, na, sc)]
        return _pad_rows8(jnp.stack(rows))

    def mlp_bwd(tag, dy, h, u, act, w_up, w_dn):
        def relu_grad(acc, uv):
            return (acc * (2.0 * jnp.maximum(uv, 0.0)),)

        (du,) = _linear_t(f"{tag}_dact", dy, w_dn, out_dtypes=(BF,), epilogue=relu_grad, extras=(u,))
        dw_dn = _wgrad(f"{tag}_dw_down", act, dy, w_dn)
        dw_up = _wgrad(f"{tag}_dw_up", h, du, w_up)
        (dh,) = _linear_t(f"{tag}_dh", du, w_up)
        return dh, dw_up, dw_dn

    dy3, sums4 = _norm_bwd("bwd_post3", dx4, None, None, y3, bwd_vec(ng_all[1, 3], mod[1, 5], None, None), False, True)
    dh3, dw_up1, dw_dn1 = mlp_bwd("mlp1", dy3, h3, u3, act3, g_up[1], g_dn[1])
    dx3, dy2, sums3 = _norm_bwd("bwd_norm3", dx4, dh3, x3, y2, bwd_vec(ng_all[1, 1], mod[1, 2], ng_all[1, 2], mod[1, 4]), True, True)

    (dbz,) = _linear_t("conv_dbz", dy2, g_cout)
    dw_cout = _wgrad("conv_dw_out", bz, dy2, g_cout)
    dproj, conv_sums = _conv_bwd("conv_mix_bwd", dbz, proj, cw8)
    dw_cin = _wgrad("conv_dw_in", h2, dproj, g_cin)
    (dh2,) = _linear_t("conv_dh", dproj, g_cin)
    dx2, dy1, sums2 = _norm_bwd("bwd_norm2", dx3, dh2, x2, y1, bwd_vec(ng_all[0, 3], mod[0, 5], ng_all[1, 0], mod[1, 1]), True, True)

    dh1, dw_up0, dw_dn0 = mlp_bwd("mlp0", dy1, h1, u1, act1, g_up[0], g_dn[0])
    dx1, dy0, sums1 = _norm_bwd("bwd_norm1", dx2, dh1, x1, y0, bwd_vec(ng_all[0, 1], mod[0, 2], ng_all[0, 2], mod[0, 4]), True, True)

    (do,) = _linear_t("mla_do", dy0, g_wo, out_dtypes=(BF,))
    dw_o = _wgrad("mla_dw_o", o, dy0, g_wo)
    dkv, dkr = _flash_bwd_kv("mla_attn_bwd_kv", qb, kvb, kr, o, lse, do, heads, scale)
    dqn, dqr = _flash_bwd_q("mla_attn_bwd_q", qb, kvb, kr, o, lse, do, heads, scale)
    dq = _rope_heads("mla_dq_rope", dqn, 0, dqr, 0, rope_c, -rope_s1, -rope_s2, heads)
    dw_uq = _wgrad("mla_dw_uq", cq, dq, g_uq)
    (dcq,) = _linear_t("mla_dcq", dq, g_uq)
    dw_ukv = _wgrad("mla_dw_ukv", ckv, dkv, g_ukv)
    (dckv,) = _linear_t("mla_dckv", dkv, g_ukv)
    dlat, lat_sums = _lat_bwd("mla_lat_bwd", dcq, dckv, dkr, lat, mla_g_q, mla_g_kv, rope_c, -rope_s1, -rope_s2, rq, rkv, heads)
    dw_win = _wgrad("mla_dw_in", h0, dlat, g_win)
    (dh0,) = _linear_t("mla_dh", dlat, g_win)
    grad_x, sums0 = _norm_bwd("bwd_pre0", dx1, dh0, x0, None, bwd_vec(None, None, ng_all[0, 0], mod[0, 1]), True, False)

    def blocks(g, like):
        return g.reshape((N_DEV,) + like.shape)

    big = [blocks(dw_win, w_in_loc), blocks(dw_uq, uq_loc), blocks(dw_ukv, ukv_loc), blocks(dw_o, shards[3]),
           dw_cin, blocks(dw_cout, shards[5]), dw_up0, dw_up1, blocks(dw_dn0, shards[8]), blocks(dw_dn1, shards[9])]
    from_sibling = _rs_sibling("rs_sibling", big)
    pair = [_pair_add(f"pair_add{a}", big[a], from_sibling[a], c_idx) for a in range(len(big))]
    from_chips = _rs_chips("rs_chips", pair)
    red = [_chip_sum(f"chip_sum{a}", pair[a], from_chips[a], chip_idx) for a in range(len(big))]

    g_mla_w_in = red[0][:, :lat_w][None]
    r_uq = red[1]
    g_mla_w_uq = jnp.concatenate(
        [r_uq[:, :heads * NOPE_DIM].reshape(-1, heads, NOPE_DIM),
         r_uq[:, heads * NOPE_DIM:].reshape(-1, heads, LANES)[:, :, :ROPE_DIM]], axis=2)[None]
    g_mla_w_ukv = red[2].reshape(mla_w_ukv.shape)
    g_mla_w_o = red[3][None]
    g_conv_w_in = red[4][None]
    g_conv_w_out = red[5][None]
    g_mlp_w_up = jnp.stack([red[6], red[7]])
    g_mlp_w_down = jnp.stack([red[8], red[9]])

    def sums_rows(s, rows):
        return [s[r] for r in rows]

    dmod = jnp.stack([
        jnp.stack(sums_rows(sums0, (0, 1)) + sums_rows(sums1, (3,)) + sums_rows(sums1, (0, 1)) + sums_rows(sums2, (3,))),
        jnp.stack(sums_rows(sums2, (0, 1)) + sums_rows(sums3, (3,)) + sums_rows(sums3, (0, 1)) + sums_rows(sums4, (3,))),
    ])
    dng = jnp.stack([
        jnp.stack([sums0[2], sums1[4], sums1[2], sums2[4]]),
        jnp.stack([sums2[2], sums3[4], sums3[2], sums4[4]]),
    ])
    parts = [dmod.reshape(-1), dng.reshape(-1), lat_sums[0], lat_sums[1], conv_sums[0:3].reshape(-1)]
    sizes = [p.shape[0] for p in parts]
    packed = jnp.concatenate(parts)
    pk_rows = -(-packed.shape[0] // (8 * LANES)) * 8
    packed = jnp.pad(packed, (0, pk_rows * LANES - packed.shape[0])).reshape(pk_rows, LANES)
    packed_all = _ag_small("ag_small_grads", packed).reshape(N_DEV, pk_rows, LANES)
    total = _sum_devices("sum_small_grads", packed_all).reshape(-1)
    offs = [0]
    for s in sizes:
        offs.append(offs[-1] + s)
    g_b_mod = total[offs[0]:offs[1]].reshape(depth, 6 * d)
    ng_full = total[offs[1]:offs[2]].reshape(depth, 4, d)
    g_norm_g = lax.dynamic_slice_in_dim(ng_full, dev * d_loc, d_loc, axis=2)
    g_g_q = total[offs[2]:offs[3]].reshape(1, rq)
    g_g_kv = total[offs[3]:offs[4]].reshape(1, rkv)
    cw_full = total[offs[4]:offs[5]].reshape(1, 3, d)
    g_conv_w = lax.dynamic_slice_in_dim(cw_full, dev * d_loc, d_loc, axis=2)

    dmod_all = packed_all.reshape(N_DEV, -1)[:, :sizes[0]].reshape(N_DEV, depth, 6 * d)
    dmod_cols = lax.dynamic_slice_in_dim(dmod_all, dev * n_mod_loc, n_mod_loc, axis=2)
    cond_t = cond_all.T
    mod_out = [_mod_grad_adam(f"w_mod_adam{i}", cond_t, dmod_cols[:, i, :], w_mod[i], m_w_mod[i], v_w_mod[i]) for i in range(depth)]
    g_w_mod, d_w_mod, nm_w_mod, nv_w_mod = [jnp.stack([mod_out[i][k] for i in range(depth)]) for k in range(4)]

    grads = [g_w_mod, g_b_mod, g_norm_g, g_mla_w_in, g_g_q, g_g_kv, g_mla_w_uq, g_mla_w_ukv, g_mla_w_o,
             g_conv_w_in, g_conv_w, g_conv_w_out, g_mlp_w_up, g_mlp_w_down]
    weights = [w_mod, b_mod, norm_g, mla_w_in, mla_g_q, mla_g_kv, mla_w_uq, mla_w_ukv, mla_w_o, conv_w_in, conv_w, conv_w_out, mlp_w_up, mlp_w_down]
    ms = [m_w_mod, m_b_mod, m_norm_g, m_mla_w_in, m_mla_g_q, m_mla_g_kv, m_mla_w_uq, m_mla_w_ukv, m_mla_w_o, m_conv_w_in, m_conv_w, m_conv_w_out, m_mlp_w_up, m_mlp_w_down]
    vs = [v_w_mod, v_b_mod, v_norm_g, v_mla_w_in, v_mla_g_q, v_mla_g_kv, v_mla_w_uq, v_mla_w_ukv, v_mla_w_o, v_conv_w_in, v_conv_w, v_conv_w_out, v_mlp_w_up, v_mlp_w_down]
    names = ["w_mod", "b_mod", "norm_g", "mla_w_in", "mla_g_q", "mla_g_kv", "mla_w_uq", "mla_w_ukv", "mla_w_o", "conv_w_in", "conv_w", "conv_w_out", "mlp_w_up", "mlp_w_down"]
    deltas, new_ms, new_vs = [d_w_mod], [nm_w_mod], [nv_w_mod]
    for k in range(1, len(weights)):
        dl, nm, nv = _adam(f"adam_{names[k]}", weights[k], grads[k], ms[k], vs[k])
        deltas.append(dl)
        new_ms.append(nm)
        new_vs.append(nv)

    return (loss, grad_x[None], *grads, *deltas, *new_ms, *new_vs)
```

```python
import jax
import jax.numpy as jnp
from jax import lax
from jax.experimental import pallas as pl
from jax.experimental.pallas import tpu as pltpu

F32 = jnp.float32
BF = jnp.bfloat16
MESH = pl.DeviceIdType.MESH
N_DEV = 8
LANES = 128
NORM_EPS = 1e-6
ROPE_THETA = 10000.0
ROPE_DIM = 64
NOPE_DIM = 128
V_DIM = 128
LR, B1, B2, ADAM_EPS, WD, STEP = 0.001, 0.9, 0.999, 1e-08, 0.01, 10
VMEM_LIMIT = 56 * 1024 * 1024
TILE_BYTES = 2 * 1024 * 1024
TM, TN, TK = 1024, 512, 2048
TQ = 512

NN = (((1,), (0,)), ((), ()))
NT = (((1,), (1,)), ((), ()))
TNDIMS = (((0,), (0,)), ((), ()))
HBM_SPEC = pl.BlockSpec(memory_space=pltpu.HBM)
VMEM_SPEC = pl.BlockSpec(memory_space=pltpu.VMEM)


def _cparams(n_axes):
    return pltpu.CompilerParams(dimension_semantics=("arbitrary",) * n_axes, vmem_limit_bytes=VMEM_LIMIT)


def _pick(dim, pref):
    if dim <= pref:
        return dim
    for t in range(pref - pref % LANES, 0, -LANES):
        if t > 0 and dim % t == 0:
            return t
    for t in range(pref, 0, -1):
        if dim % t == 0:
            return t
    return dim


def _rows(r, c, itemsize=4):
    want = max(8, TILE_BYTES // (itemsize * max(c, 1)))
    if r <= want:
        return r
    for t in range(want - want % 8, 0, -8):
        if t > 0 and r % t == 0:
            return t
    return r


def _pos():
    return lax.axis_index("x"), lax.axis_index("y"), lax.axis_index("c")


def _ag_small(name, v):
    m_per, n = v.shape

    def body(x_ref, out_ref, send_sems, recv_sems, local_sem):
        x, y, c = _pos()
        me, sibling = (x, y, c), (x, y, 1 - c)
        chips = [(1 - x, y), (x, 1 - y), (1 - x, 1 - y)]

        def rows(px, py, pc):
            return out_ref.at[pl.ds((4 * px + 2 * py + pc) * m_per, m_per), :]

        def copy(k, block, to, src=None):
            return pltpu.make_async_remote_copy(
                src_ref=rows(*block) if src is None else src, dst_ref=rows(*block),
                send_sem=send_sems.at[k], recv_sem=recv_sems.at[k], device_id=to, device_id_type=MESH)

        mine = pltpu.make_async_copy(x_ref, rows(*me), local_sem)
        mine.start()
        first = [copy(0, me, sibling, src=x_ref)]
        first += [copy(1 + j, me, (*chip, c), src=x_ref) for j, chip in enumerate(chips)]
        for cp in first:
            cp.start()
        passed = [copy(4 + j, (*chip, c), sibling) for j, chip in enumerate(chips)]
        for j, chip in enumerate(chips):
            copy(1 + j, (*chip, c), me).wait_recv()
            passed[j].start()
        copy(0, sibling, me).wait_recv()
        for j, chip in enumerate(chips):
            copy(4 + j, (*chip, 1 - c), me).wait_recv()
        for cp in first + passed:
            cp.wait_send()
        mine.wait()

    return pl.pallas_call(
        body, name=name,
        out_shape=jax.ShapeDtypeStruct((N_DEV * m_per, n), v.dtype),
        in_specs=[VMEM_SPEC], out_specs=VMEM_SPEC,
        scratch_shapes=[pltpu.SemaphoreType.DMA((7,)), pltpu.SemaphoreType.DMA((7,)), pltpu.SemaphoreType.DMA],
    )(v)


def _ag_big(name, arrs):
    n = len(arrs)

    def body(*refs):
        ins, outs = refs[:n], refs[n:2 * n]
        send_sems, recv_sems, local_sems = refs[2 * n:]
        x, y, c = _pos()
        me, sibling = (x, y, c), (x, y, 1 - c)
        chips = [(1 - x, y), (x, 1 - y), (1 - x, 1 - y)]

        def copy(a, k, block, to, src=None):
            dst = outs[a].at[4 * block[0] + 2 * block[1] + block[2]]
            return pltpu.make_async_remote_copy(
                src_ref=dst if src is None else src, dst_ref=dst,
                send_sem=send_sems.at[7 * a + k], recv_sem=recv_sems.at[7 * a + k],
                device_id=to, device_id_type=MESH)

        mine = [pltpu.make_async_copy(ins[a], outs[a].at[4 * x + 2 * y + c], local_sems.at[a]) for a in range(n)]
        for cp in mine:
            cp.start()
        first = []
        for a in range(n):
            first.append(copy(a, 0, me, sibling, src=ins[a]))
            first += [copy(a, 1 + j, me, (*chip, c), src=ins[a]) for j, chip in enumerate(chips)]
        for cp in first:
            cp.start()
        passed = []
        for j, chip in enumerate(chips):
            for a in range(n):
                copy(a, 1 + j, (*chip, c), me).wait_recv()
                cp = copy(a, 4 + j, (*chip, c), sibling)
                cp.start()
                passed.append(cp)
        for a in range(n):
            copy(a, 0, sibling, me).wait_recv()
            for j, chip in enumerate(chips):
                copy(a, 4 + j, (*chip, 1 - c), me).wait_recv()
        for cp in first + passed:
            cp.wait_send()
        for cp in mine:
            cp.wait()

    return pl.pallas_call(
        body, name=name,
        out_shape=[jax.ShapeDtypeStruct((N_DEV,) + a.shape, a.dtype) for a in arrs],
        in_specs=[HBM_SPEC] * n, out_specs=[HBM_SPEC] * n,
        scratch_shapes=[pltpu.SemaphoreType.DMA((7 * n,)), pltpu.SemaphoreType.DMA((7 * n,)),
                        pltpu.SemaphoreType.DMA((n,))],
    )(*arrs)


def _rs_sibling(name, arrs):
    n = len(arrs)

    def body(*refs):
        ins, outs = refs[:n], refs[n:2 * n]
        send_sems, recv_sems = refs[2 * n:]
        x, y, c = _pos()
        copies = []
        for a in range(n):
            for j in range(4):
                copies.append(pltpu.make_async_remote_copy(
                    src_ref=ins[a].at[2 * j + (1 - c)], dst_ref=outs[a].at[j],
                    send_sem=send_sems.at[4 * a + j], recv_sem=recv_sems.at[4 * a + j],
                    device_id=(x, y, 1 - c), device_id_type=MESH))
        for cp in copies:
            cp.start()
        for cp in copies:
            cp.wait()

    return pl.pallas_call(
        body, name=name,
        out_shape=[jax.ShapeDtypeStruct((4,) + a.shape[1:], a.dtype) for a in arrs],
        in_specs=[HBM_SPEC] * n, out_specs=[HBM_SPEC] * n,
        scratch_shapes=[pltpu.SemaphoreType.DMA((4 * n,)), pltpu.SemaphoreType.DMA((4 * n,))],
    )(*arrs)


def _rs_chips(name, arrs):
    n = len(arrs)

    def body(*refs):
        ins, outs = refs[:n], refs[n:2 * n]
        send_sems, recv_sems = refs[2 * n:]
        x, y, c = _pos()
        chips = [(1 - x, y), (x, 1 - y), (1 - x, 1 - y)]
        copies = []
        for a in range(n):
            for k, chip in enumerate(chips):
                copies.append(pltpu.make_async_remote_copy(
                    src_ref=ins[a].at[2 * chip[0] + chip[1]], dst_ref=outs[a].at[k],
                    send_sem=send_sems.at[3 * a + k], recv_sem=recv_sems.at[3 * a + k],
                    device_id=(*chip, c), device_id_type=MESH))
        for cp in copies:
            cp.start()
        for cp in copies:
            cp.wait()

    return pl.pallas_call(
        body, name=name,
        out_shape=[jax.ShapeDtypeStruct((3,) + a.shape[1:], a.dtype) for a in arrs],
        in_specs=[HBM_SPEC] * n, out_specs=[HBM_SPEC] * n,
        scratch_shapes=[pltpu.SemaphoreType.DMA((3 * n,)), pltpu.SemaphoreType.DMA((3 * n,))],
    )(*arrs)


SEM_SPEC = pl.BlockSpec(memory_space=pltpu.SEMAPHORE)
ANY_SPEC = pl.BlockSpec(memory_space=pl.ANY)
EFFECT = pltpu.SideEffectType.DATAFLOW_SIDE_EFFECTING


def _hbm(a):
    return pltpu.with_memory_space_constraint(a, pltpu.HBM)


def _own_blocks(name, arrs):
    n = len(arrs)

    def body(*refs):
        ins, outs, sems = refs[:n], refs[n:2 * n], refs[2 * n]
        x, y, c = _pos()
        copies = [pltpu.make_async_copy(ins[a], outs[a].at[4 * x + 2 * y + c], sems.at[a]) for a in range(n)]
        for cp in copies:
            cp.start()
        for cp in copies:
            cp.wait()

    return pl.pallas_call(
        body, name=name,
        out_shape=[jax.ShapeDtypeStruct((N_DEV,) + a.shape, a.dtype) for a in arrs],
        in_specs=[HBM_SPEC] * n, out_specs=[HBM_SPEC] * n,
        scratch_shapes=[pltpu.SemaphoreType.DMA((n,))],
    )(*arrs)


def _ag_start(name, shards, lands, groups, after):
    n = len(shards)
    ng = len(groups)

    def body(*refs):
        ins, lnd = refs[:n], refs[n:2 * n]
        sems = refs[2 * n + 1:2 * n + 1 + 2 * ng]
        token = refs[-1]
        x, y, c = _pos()
        targets = [(x, y, 1 - c), (1 - x, y, c), (x, 1 - y, c), (1 - x, 1 - y, c)]
        for gi, members in enumerate(groups):
            for pos, a in enumerate(members):
                for k, to in enumerate(targets):
                    pltpu.make_async_remote_copy(
                        src_ref=ins[a], dst_ref=lnd[a].at[4 * x + 2 * y + c],
                        send_sem=sems[2 * gi].at[4 * pos + k], recv_sem=sems[2 * gi + 1].at[4 * pos + k],
                        device_id=to, device_id_type=MESH).start()
        token[...] = jnp.zeros_like(token)

    sem_shapes = []
    for members in groups:
        sem_shapes += [pltpu.SemaphoreType.DMA((4 * len(members),))] * 2
    res = pl.pallas_call(
        body, name=name,
        out_shape=(*sem_shapes, *[pltpu.HBM(a.shape, a.dtype) for a in shards], *[pltpu.HBM(a.shape, a.dtype) for a in lands],
                   jax.ShapeDtypeStruct((8, LANES), F32)),
        in_specs=[HBM_SPEC] * (2 * n) + [ANY_SPEC],
        out_specs=(*[SEM_SPEC] * (2 * ng), *[HBM_SPEC] * (2 * n), VMEM_SPEC),
        input_output_aliases={i: 2 * ng + i for i in range(2 * n)},
        compiler_params=pltpu.CompilerParams(has_side_effects=EFFECT),
    )(*[_hbm(a) for a in shards], *[_hbm(a) for a in lands], after)
    sem_pairs = [(res[2 * g], res[2 * g + 1]) for g in range(ng)]
    return sem_pairs, list(res[2 * ng:2 * ng + n]), list(res[2 * ng + n:2 * ng + 2 * n]), res[-1]


def _ag_forward(name, shards, lands, send_sems, recv_sems, after):
    n = len(shards)

    def body(*refs):
        ins, lnd = refs[:n], refs[n:2 * n]
        s_sems, r_sems = refs[2 * n], refs[2 * n + 1]
        fs_sems, fr_sems = refs[2 * n + 3], refs[2 * n + 4]
        token = refs[-1]
        x, y, c = _pos()
        sources = [(x, y, 1 - c), (1 - x, y, c), (x, 1 - y, c), (1 - x, 1 - y, c)]
        for a in range(n):
            for k, src in enumerate(sources):
                blk = lnd[a].at[4 * src[0] + 2 * src[1] + src[2]]
                arrived = pltpu.make_async_remote_copy(
                    src_ref=ins[a], dst_ref=blk, send_sem=s_sems.at[4 * a + k], recv_sem=r_sems.at[4 * a + k],
                    device_id=src, device_id_type=MESH)
                arrived.wait_send()
                arrived.wait_recv()
                if k > 0:
                    pltpu.make_async_remote_copy(
                        src_ref=blk, dst_ref=blk, send_sem=fs_sems.at[3 * a + k - 1], recv_sem=fr_sems.at[3 * a + k - 1],
                        device_id=(x, y, 1 - c), device_id_type=MESH).start()
        token[...] = jnp.zeros_like(token)

    res = pl.pallas_call(
        body, name=name,
        out_shape=(pltpu.SemaphoreType.DMA((3 * n,)), pltpu.SemaphoreType.DMA((3 * n,)),
                   *[pltpu.HBM(a.shape, a.dtype) for a in lands], jax.ShapeDtypeStruct((8, LANES), F32)),
        in_specs=[HBM_SPEC] * (2 * n) + [SEM_SPEC, SEM_SPEC, ANY_SPEC],
        out_specs=(SEM_SPEC, SEM_SPEC, *[HBM_SPEC] * n, VMEM_SPEC),
        input_output_aliases={n + i: 2 + i for i in range(n)},
        compiler_params=pltpu.CompilerParams(has_side_effects=EFFECT),
    )(*shards, *lands, send_sems, recv_sems, after)
    return res[0], res[1], list(res[2:2 + n]), res[-1]


def _ag_finish(name, lands, fs_sems, fr_sems, after):
    n = len(lands)

    def body(*refs):
        lnd = refs[:n]
        fs, fr = refs[n], refs[n + 1]
        x, y, c = _pos()
        chips = [(1 - x, y), (x, 1 - y), (1 - x, 1 - y)]
        for a in range(n):
            for j, chip in enumerate(chips):
                mine = lnd[a].at[4 * chip[0] + 2 * chip[1] + c]
                theirs = lnd[a].at[4 * chip[0] + 2 * chip[1] + 1 - c]
                cp = pltpu.make_async_remote_copy(
                    src_ref=mine, dst_ref=theirs, send_sem=fs.at[3 * a + j], recv_sem=fr.at[3 * a + j],
                    device_id=(x, y, 1 - c), device_id_type=MESH)
                cp.wait_send()
                cp.wait_recv()

    res = pl.pallas_call(
        body, name=name,
        out_shape=tuple(pltpu.HBM(a.shape, a.dtype) for a in lands),
        in_specs=[HBM_SPEC] * n + [SEM_SPEC, SEM_SPEC, ANY_SPEC],
        out_specs=tuple([HBM_SPEC] * n),
        input_output_aliases={i: i for i in range(n)},
        compiler_params=pltpu.CompilerParams(has_side_effects=EFFECT),
    )(*lands, fs_sems, fr_sems, after)
    return list(res)


def _rs_chips_start(name, arrs):
    n = len(arrs)

    def body(*refs):
        ins, lnd = refs[:n], refs[n:2 * n]
        s_sems, r_sems = refs[2 * n], refs[2 * n + 1]
        token = refs[-1]
        x, y, c = _pos()
        chips = [(1 - x, y), (x, 1 - y), (1 - x, 1 - y)]
        for a in range(n):
            for k, chip in enumerate(chips):
                pltpu.make_async_remote_copy(
                    src_ref=ins[a].at[2 * chip[0] + chip[1]], dst_ref=lnd[a].at[k],
                    send_sem=s_sems.at[3 * a + k], recv_sem=r_sems.at[3 * a + k],
                    device_id=(*chip, c), device_id_type=MESH).start()
        token[...] = jnp.zeros_like(token)

    lands = [lax.empty((3,) + a.shape[1:], a.dtype) for a in arrs]
    res = pl.pallas_call(
        body, name=name,
        out_shape=(pltpu.SemaphoreType.DMA((3 * n,)), pltpu.SemaphoreType.DMA((3 * n,)),
                   *[pltpu.HBM(a.shape, a.dtype) for a in arrs], *[pltpu.HBM(a.shape, a.dtype) for a in lands],
                   jax.ShapeDtypeStruct((8, LANES), F32)),
        in_specs=[HBM_SPEC] * (2 * n),
        out_specs=(SEM_SPEC, SEM_SPEC, *[HBM_SPEC] * (2 * n), VMEM_SPEC),
        input_output_aliases={i: 2 + i for i in range(2 * n)},
        compiler_params=pltpu.CompilerParams(has_side_effects=EFFECT),
    )(*[_hbm(a) for a in arrs], *[_hbm(a) for a in lands])
    return res[0], res[1], list(res[2:2 + n]), list(res[2 + n:2 + 2 * n]), res[-1]


def _rs_chips_wait(name, arrs, lands, send_sems, recv_sems, after):
    n = len(arrs)

    def body(*refs):
        ins, lnd = refs[:n], refs[n:2 * n]
        s_sems, r_sems = refs[2 * n], refs[2 * n + 1]
        x, y, c = _pos()
        chips = [(1 - x, y), (x, 1 - y), (1 - x, 1 - y)]
        for a in range(n):
            for k, chip in enumerate(chips):
                cp = pltpu.make_async_remote_copy(
                    src_ref=ins[a].at[2 * chip[0] + chip[1]], dst_ref=lnd[a].at[k],
                    send_sem=s_sems.at[3 * a + k], recv_sem=r_sems.at[3 * a + k],
                    device_id=(*chip, c), device_id_type=MESH)
                cp.wait_send()
                cp.wait_recv()

    res = pl.pallas_call(
        body, name=name,
        out_shape=tuple(pltpu.HBM(a.shape, a.dtype) for a in list(arrs) + list(lands)),
        in_specs=[HBM_SPEC] * (2 * n) + [SEM_SPEC, SEM_SPEC, ANY_SPEC],
        out_specs=tuple([HBM_SPEC] * (2 * n)),
        input_output_aliases={i: i for i in range(2 * n)},
        compiler_params=pltpu.CompilerParams(has_side_effects=EFFECT),
    )(*arrs, *lands, send_sems, recv_sems, after)
    return list(res[:n]), list(res[n:])


def _pair_add(name, g, r1, c_idx):
    _, r, cdim = g.shape
    tr = _rows(r, cdim, 2)

    def body(c_ref, g_ref, r_ref, o_ref):
        o_ref[...] = (g_ref[...].astype(F32) + r_ref[...].astype(F32)).astype(o_ref.dtype)

    return pl.pallas_call(
        body, name=name,
        out_shape=jax.ShapeDtypeStruct((4, r, cdim), BF),
        grid_spec=pltpu.PrefetchScalarGridSpec(
            num_scalar_prefetch=1, grid=(4, r // tr),
            in_specs=[pl.BlockSpec((None, tr, cdim), lambda j, i, c_ref: (2 * j + c_ref[0], i, 0)),
                      pl.BlockSpec((None, tr, cdim), lambda j, i, c_ref: (j, i, 0))],
            out_specs=pl.BlockSpec((None, tr, cdim), lambda j, i, c_ref: (j, i, 0))),
        compiler_params=_cparams(2),
    )(c_idx, g, r1)


def _chip_sum(name, p, r2, chip_idx):
    _, r, cdim = p.shape
    tr = _rows(r, cdim, 4)

    def body(s_ref, p_ref, a_ref, b_ref, c_ref, o_ref):
        o_ref[...] = ((p_ref[...].astype(F32) + a_ref[...].astype(F32)) + b_ref[...].astype(F32)) + c_ref[...].astype(F32)

    def other(k):
        return pl.BlockSpec((None, tr, cdim), lambda i, s_ref: (k, i, 0))

    return pl.pallas_call(
        body, name=name,
        out_shape=jax.ShapeDtypeStruct((r, cdim), F32),
        grid_spec=pltpu.PrefetchScalarGridSpec(
            num_scalar_prefetch=1, grid=(r // tr,),
            in_specs=[pl.BlockSpec((None, tr, cdim), lambda i, s_ref: (s_ref[0], i, 0)), other(0), other(1), other(2)],
            out_specs=pl.BlockSpec((tr, cdim), lambda i, s_ref: (i, 0))),
        compiler_params=_cparams(1),
    )(chip_idx, p, r2, r2, r2)


def _adam_math(w, g, m, v):
    m = B1 * m + (1.0 - B1) * g
    v = B2 * v + (1.0 - B2) * (g * g)
    m_hat = m / (1.0 - B1 ** STEP)
    v_hat = v / (1.0 - B2 ** STEP)
    delta = -LR * (m_hat / (jnp.sqrt(v_hat) + ADAM_EPS) + WD * w)
    return delta, m, v


def _adam(name, w, g, m, v):
    shape = w.shape
    cdim = shape[-1]
    r = w.size // cdim
    flat = [a.reshape(r, cdim) for a in (w, g, m, v)]
    tr = _rows(r, cdim, 4)

    def body(w_ref, g_ref, m_ref, v_ref, d_ref, mo_ref, vo_ref):
        d, mn, vn = _adam_math(w_ref[...], g_ref[...], m_ref[...], v_ref[...])
        d_ref[...] = d
        mo_ref[...] = mn
        vo_ref[...] = vn

    spec = pl.BlockSpec((tr, cdim), lambda i: (i, 0))
    outs = pl.pallas_call(
        body, name=name,
        out_shape=[jax.ShapeDtypeStruct((r, cdim), F32)] * 3,
        grid=(r // tr,), in_specs=[spec] * 4, out_specs=[spec] * 3,
        compiler_params=_cparams(1),
    )(*flat)
    return [o.reshape(shape) for o in outs]


def _mod_matvec(name, cond_all, w_loc):
    d, n_loc = w_loc.shape
    tn = _pick(n_loc, 512)

    def body(c_ref, w_ref, o_ref):
        o_ref[...] = jnp.dot(c_ref[...].astype(BF), w_ref[...].astype(BF), preferred_element_type=F32)

    return pl.pallas_call(
        body, name=name,
        out_shape=jax.ShapeDtypeStruct((N_DEV, n_loc), F32),
        grid=(n_loc // tn,),
        in_specs=[pl.BlockSpec((N_DEV, d), lambda j: (0, 0)), pl.BlockSpec((d, tn), lambda j: (0, j))],
        out_specs=pl.BlockSpec((N_DEV, tn), lambda j: (0, j)),
        compiler_params=_cparams(1),
    )(cond_all, w_loc)


def _mod_grad_adam(name, cond_t, dmod, w, m, v):
    d, n_loc = w.shape
    tr = _rows(d, n_loc, 4)

    def body(ct_ref, dm_ref, w_ref, m_ref, v_ref, g_ref, d_ref, mo_ref, vo_ref):
        g = ct_ref[:, 0:1] * dm_ref[0:1, :]
        for b in range(1, N_DEV):
            g = g + ct_ref[:, b:b + 1] * dm_ref[b:b + 1, :]
        dl, mn, vn = _adam_math(w_ref[...], g, m_ref[...], v_ref[...])
        g_ref[...] = g
        d_ref[...] = dl
        mo_ref[...] = mn
        vo_ref[...] = vn

    spec = pl.BlockSpec((tr, n_loc), lambda i: (i, 0))
    return pl.pallas_call(
        body, name=name,
        out_shape=[jax.ShapeDtypeStruct((d, n_loc), F32)] * 4,
        grid=(d // tr,),
        in_specs=[pl.BlockSpec((tr, N_DEV), lambda i: (i, 0)), pl.BlockSpec((N_DEV, n_loc), lambda i: (0, 0)), spec, spec, spec],
        out_specs=[spec] * 4,
        compiler_params=_cparams(1),
    )(cond_t, dmod, w, m, v)


def _sum_devices(name, gathered):
    _, m, n = gathered.shape

    def body(g_ref, o_ref):
        acc = g_ref[0]
        for d in range(1, N_DEV):
            acc = acc + g_ref[d]
        o_ref[...] = acc

    return pl.pallas_call(
        body, name=name, out_shape=jax.ShapeDtypeStruct((m, n), F32),
        in_specs=[VMEM_SPEC], out_specs=VMEM_SPEC,
    )(gathered)


def _silu(name, c_pad):
    def body(c_ref, o_ref):
        c = c_ref[...]
        o_ref[...] = c * (1.0 / (1.0 + jnp.exp(-c)))

    return pl.pallas_call(body, name=name, out_shape=jax.ShapeDtypeStruct(c_pad.shape, F32),
                          in_specs=[VMEM_SPEC], out_specs=VMEM_SPEC)(c_pad)


def _mm(name, a, b, dims, grid, a_spec, b_spec, outs, extras=(), epilogue=None):
    nk = grid[2]
    n_ex, n_out = len(extras), len(outs)

    def body(*refs):
        a_ref, b_ref = refs[0], refs[1]
        ex_refs = refs[2:2 + n_ex]
        out_refs = refs[2 + n_ex:2 + n_ex + n_out]
        part = lax.dot_general(a_ref[...].astype(BF), b_ref[...].astype(BF), dims, preferred_element_type=F32)

        def finish(acc):
            vals = (acc,) if epilogue is None else epilogue(acc, *[e[...] for e in ex_refs])
            for o_ref, val in zip(out_refs, vals):
                o_ref[...] = val.astype(o_ref.dtype)

        if nk == 1:
            finish(part)
        else:
            acc_ref = refs[-1]
            k = pl.program_id(2)

            @pl.when(k == 0)
            def _():
                acc_ref[...] = part

            @pl.when(k > 0)
            def _():
                acc_ref[...] += part

            @pl.when(k == nk - 1)
            def _():
                finish(acc_ref[...])

    tile = outs[0][2].block_shape
    acc_shape = tuple(t for t in tile if t is not None)
    res = pl.pallas_call(
        body, name=name,
        out_shape=[jax.ShapeDtypeStruct(s, d) for s, d, _ in outs],
        grid=grid,
        in_specs=[a_spec, b_spec] + [s for _, s in extras],
        out_specs=[s for _, _, s in outs],
        scratch_shapes=[] if nk == 1 else [pltpu.VMEM(acc_shape, F32)],
        compiler_params=_cparams(3),
    )(a, b, *[e for e, _ in extras])
    return res


def _w_nn_spec(w, tk, tn):
    if w.ndim == 2:
        return pl.BlockSpec((tk, tn), lambda i, j, k: (k, j))
    per = w.shape[2] // tn
    return pl.BlockSpec((None, tk, tn), lambda i, j, k: (j // per, k, j % per))


def _w_nt_spec(w, tn, tk):
    if w.ndim == 2:
        return pl.BlockSpec((tn, tk), lambda i, j, k: (j, k))
    per = w.shape[2] // tk
    return pl.BlockSpec((None, tn, tk), lambda i, j, k: (k // per, j, k % per))


def _w_cols(w):
    return w.shape[1] if w.ndim == 2 else w.shape[0] * w.shape[2]


def _w_rows(w):
    return w.shape[0] if w.ndim == 2 else w.shape[1]


def _col_tile(w, pref):
    return _pick(w.shape[1] if w.ndim == 2 else w.shape[2], pref)


def _linear(name, a, w, out_dtypes=(F32,), epilogue=None, extras=()):
    t, kdim = a.shape
    n = _w_cols(w)
    tm, tn, tk = _pick(t, TM), _col_tile(w, TN), _pick(kdim, TK)
    o_spec = pl.BlockSpec((tm, tn), lambda i, j, k: (i, j))
    return _mm(name, a, w, NN, (t // tm, n // tn, kdim // tk),
               pl.BlockSpec((tm, tk), lambda i, j, k: (i, k)), _w_nn_spec(w, tk, tn),
               [((t, n), dt, o_spec) for dt in out_dtypes],
               extras=[(e, o_spec) for e in extras], epilogue=epilogue)


def _linear_t(name, dy, w, out_dtypes=(F32,), epilogue=None, extras=()):
    t, n = dy.shape
    kdim = _w_rows(w)
    tm, tn, tk = _pick(t, TM), _pick(kdim, TN), _col_tile(w, TK)
    o_spec = pl.BlockSpec((tm, tn), lambda i, j, k: (i, j))
    return _mm(name, dy, w, NT, (t // tm, kdim // tn, n // tk),
               pl.BlockSpec((tm, tk), lambda i, j, k: (i, k)), _w_nt_spec(w, tn, tk),
               [((t, kdim), dt, o_spec) for dt in out_dtypes],
               extras=[(e, o_spec) for e in extras], epilogue=epilogue)


def _wgrad(name, a, dy, like):
    t, kdim = a.shape
    n = dy.shape[1]
    tm, tn, tk = _pick(kdim, TM), _col_tile(like, TM), _pick(t, 1024)
    if like.ndim == 2:
        o_spec = pl.BlockSpec((tm, tn), lambda i, j, k: (i, j))
    else:
        per = like.shape[2] // tn
        o_spec = pl.BlockSpec((None, tm, tn), lambda i, j, k: (j // per, i, j % per))
    return _mm(name, a, dy, TNDIMS, (kdim // tm, n // tn, t // tk),
               pl.BlockSpec((tk, tm), lambda i, j, k: (k, i)), pl.BlockSpec((tk, tn), lambda i, j, k: (k, j)),
               [(like.shape, BF, o_spec)])[0]


def _rstd(x):
    return lax.rsqrt(jnp.mean(x * x, axis=-1, keepdims=True) + NORM_EPS)


def _colsum(x):
    return jnp.sum(x, axis=0, keepdims=True)


def _norm_fwd(name, x, y, vec, target, has_post, has_pre, has_loss):
    t, d = x.shape
    tm = _rows(t, d, 4)
    ins, in_specs = [x], [pl.BlockSpec((tm, d), lambda i: (i, 0))]
    row_spec = pl.BlockSpec((tm, d), lambda i: (i, 0))
    if has_post:
        ins.append(y)
        in_specs.append(row_spec)
    ins.append(vec)
    in_specs.append(pl.BlockSpec((8, d), lambda i: (0, 0)))
    if has_loss:
        ins.append(target)
        in_specs.append(row_spec)
    out_shape, out_specs = [], []
    if has_post and not has_loss:
        out_shape.append(jax.ShapeDtypeStruct((t, d), F32))
        out_specs.append(row_spec)
    if has_pre:
        out_shape.append(jax.ShapeDtypeStruct((t, d), BF))
        out_specs.append(row_spec)
    if has_loss:
        out_shape += [jax.ShapeDtypeStruct((t, d), F32), jax.ShapeDtypeStruct((8, d), F32)]
        out_specs += [row_spec, pl.BlockSpec((8, d), lambda i: (0, 0))]

    def body(*refs):
        it = iter(refs)
        x_ref = next(it)
        y_ref = next(it) if has_post else None
        vec_ref = next(it)
        tgt_ref = next(it) if has_loss else None
        xv = x_ref[...]
        if has_post:
            yv = y_ref[...]
            xv = xv + vec_ref[1:2, :] * ((yv * _rstd(yv)) * vec_ref[0:1, :])
            if not has_loss:
                next(it)[...] = xv
        if has_pre:
            hv = ((xv * _rstd(xv)) * vec_ref[2:3, :]) * (1.0 + vec_ref[3:4, :]) + vec_ref[4:5, :]
            next(it)[...] = hv.astype(BF)
        if has_loss:
            e = xv - tgt_ref[...]
            next(it)[...] = e * (1.0 / d)
            acc_ref = next(it)
            i = pl.program_id(0)

            @pl.when(i == 0)
            def _():
                acc_ref[...] = jnp.zeros_like(acc_ref)

            acc_ref[0:1, :] += _colsum(e * e)

    return pl.pallas_call(
        body, name=name, out_shape=out_shape, grid=(t // tm,), in_specs=in_specs, out_specs=out_specs,
        compiler_params=_cparams(1),
    )(*ins)


def _norm_bwd(name, dx_in, dh, x, y, vec, has_pre, has_post):
    t, d = dx_in.shape
    tm = _rows(t, d, 4)
    row_spec = pl.BlockSpec((tm, d), lambda i: (i, 0))
    vec_spec = pl.BlockSpec((8, d), lambda i: (0, 0))
    ins, in_specs = [dx_in], [row_spec]
    if has_pre:
        ins += [dh, x]
        in_specs += [row_spec, row_spec]
    if has_post:
        ins.append(y)
        in_specs.append(row_spec)
    ins.append(vec)
    in_specs.append(vec_spec)
    out_shape, out_specs = [], []
    if has_pre:
        out_shape.append(jax.ShapeDtypeStruct((t, d), F32))
        out_specs.append(row_spec)
    if has_post:
        out_shape.append(jax.ShapeDtypeStruct((t, d), BF))
        out_specs.append(row_spec)
    out_shape.append(jax.ShapeDtypeStruct((8, d), F32))
    out_specs.append(vec_spec)

    def body(*refs):
        it = iter(refs)
        dx = next(it)[...]
        dh_ref = next(it) if has_pre else None
        x_ref = next(it) if has_pre else None
        y_ref = next(it) if has_post else None
        vec_ref = next(it)
        dx_ref = next(it) if has_pre else None
        dy_ref = next(it) if has_post else None
        sums_ref = next(it)
        i = pl.program_id(0)

        @pl.when(i == 0)
        def _():
            sums_ref[...] = jnp.zeros_like(sums_ref)

        if has_pre:
            dhv, xv = dh_ref[...], x_ref[...]
            rs = _rstd(xv)
            xhat = xv * rs
            na = vec_ref[2:3, :]
            sums_ref[0:1, :] += _colsum(dhv)
            sums_ref[1:2, :] += _colsum(dhv * (xhat * na))
            tt = dhv * (1.0 + vec_ref[3:4, :])
            sums_ref[2:3, :] += _colsum(tt * xhat)
            u = tt * na
            dx = dx + rs * (u - xhat * jnp.mean(u * xhat, axis=-1, keepdims=True))
            dx_ref[...] = dx
        if has_post:
            yv = y_ref[...]
            rs = _rstd(yv)
            yhat = yv * rs
            nb = vec_ref[0:1, :]
            sums_ref[3:4, :] += _colsum(dx * (yhat * nb))
            tt = dx * vec_ref[1:2, :]
            sums_ref[4:5, :] += _colsum(tt * yhat)
            u = tt * nb
            dy_ref[...] = (rs * (u - yhat * jnp.mean(u * yhat, axis=-1, keepdims=True))).astype(BF)

    return pl.pallas_call(
        body, name=name, out_shape=out_shape, grid=(t // tm,), in_specs=in_specs, out_specs=out_specs,
        compiler_params=_cparams(1),
    )(*ins)


def _rope128(tv, cs, s1, s2):
    return tv * cs + pltpu.roll(tv, 96, 1) * s1 + pltpu.roll(tv, 32, 1) * s2


def _lat_post(name, lat, gq, gkv, cs, s1, s2, rq, rkv):
    t, w = lat.shape
    tm = _rows(t, w, 4)

    def body(lat_ref, gq_ref, gkv_ref, cs_ref, s1_ref, s2_ref, cq_ref, ckv_ref, kr_ref):
        lq = lat_ref[:, 0:rq]
        lkv = lat_ref[:, rq:rq + rkv]
        cq_ref[...] = ((lq * _rstd(lq)) * gq_ref[...]).astype(BF)
        ckv_ref[...] = ((lkv * _rstd(lkv)) * gkv_ref[...]).astype(BF)
        kr_ref[...] = _rope128(lat_ref[:, rq + rkv:rq + rkv + LANES], cs_ref[...], s1_ref[...], s2_ref[...]).astype(BF)

    def rows(c):
        return pl.BlockSpec((tm, c), lambda i: (i, 0))

    def vecs(c):
        return pl.BlockSpec((1, c), lambda i: (0, 0))

    return pl.pallas_call(
        body, name=name,
        out_shape=[jax.ShapeDtypeStruct((t, rq), BF), jax.ShapeDtypeStruct((t, rkv), BF), jax.ShapeDtypeStruct((t, LANES), BF)],
        grid=(t // tm,),
        in_specs=[rows(w), vecs(rq), vecs(rkv), rows(LANES), rows(LANES), rows(LANES)],
        out_specs=[rows(rq), rows(rkv), rows(LANES)],
        compiler_params=_cparams(1),
    )(lat, gq, gkv, cs, s1, s2)


def _lat_bwd(name, dcq, dckv, dkr, lat, gq, gkv, cs, s1, s2, rq, rkv, heads):
    t, w = lat.shape
    tm = _rows(t, max(w, heads * LANES), 4)
    assert rq == rkv

    def body(dcq_ref, dckv_ref, dkr_ref, lat_ref, gq_ref, gkv_ref, cs_ref, s1_ref, s2_ref, dlat_ref, sums_ref):
        i = pl.program_id(0)

        @pl.when(i == 0)
        def _():
            sums_ref[...] = jnp.zeros_like(sums_ref)

        def rms_bwd(dc, lv, g, row):
            rs = _rstd(lv)
            lhat = lv * rs
            sums_ref[row:row + 1, :] += _colsum(dc * lhat)
            u = dc * g
            return rs * (u - lhat * jnp.mean(u * lhat, axis=-1, keepdims=True))

        dlat_ref[:, 0:rq] = rms_bwd(dcq_ref[...], lat_ref[:, 0:rq], gq_ref[...], 0).astype(BF)
        dlat_ref[:, rq:rq + rkv] = rms_bwd(dckv_ref[...], lat_ref[:, rq:rq + rkv], gkv_ref[...], 1).astype(BF)
        dk = dkr_ref[:, 0:LANES]
        for h in range(1, heads):
            dk = dk + dkr_ref[:, h * LANES:(h + 1) * LANES]
        dlat_ref[:, rq + rkv:rq + rkv + LANES] = _rope128(dk, cs_ref[...], s1_ref[...], s2_ref[...]).astype(BF)

    def rows(c):
        return pl.BlockSpec((tm, c), lambda i: (i, 0))

    def vecs(c):
        return pl.BlockSpec((1, c), lambda i: (0, 0))

    return pl.pallas_call(
        body, name=name,
        out_shape=[jax.ShapeDtypeStruct((t, w), BF), jax.ShapeDtypeStruct((8, rq), F32)],
        grid=(t // tm,),
        in_specs=[rows(rq), rows(rkv), rows(heads * LANES), rows(w), vecs(rq), vecs(rkv), rows(LANES), rows(LANES), rows(LANES)],
        out_specs=[rows(w), pl.BlockSpec((8, rq), lambda i: (0, 0))],
        compiler_params=_cparams(1),
    )(dcq, dckv, dkr, lat, gq, gkv, cs, s1, s2)


def _rope_heads(name, xa, a_blk, xb, b_blk, cs, s1, s2, heads):
    t = xa.shape[0]
    hw = heads * LANES
    tm = _rows(t, 2 * hw, 4)

    def body(a_ref, b_ref, cs_ref, s1_ref, s2_ref, o_ref):
        o_ref[:, 0:hw] = a_ref[...].astype(BF)
        csv, s1v, s2v = cs_ref[...], s1_ref[...], s2_ref[...]
        for h in range(heads):
            sl = slice(h * LANES, (h + 1) * LANES)
            o_ref[:, hw + h * LANES:hw + (h + 1) * LANES] = _rope128(b_ref[:, sl], csv, s1v, s2v).astype(BF)

    def rows(c):
        return pl.BlockSpec((tm, c), lambda i: (i, 0))

    return pl.pallas_call(
        body, name=name,
        out_shape=jax.ShapeDtypeStruct((t, 2 * hw), BF),
        grid=(t // tm,),
        in_specs=[pl.BlockSpec((tm, hw), lambda i: (i, a_blk)), pl.BlockSpec((tm, hw), lambda i: (i, b_blk)),
                  rows(LANES), rows(LANES), rows(LANES)],
        out_specs=rows(2 * hw),
        compiler_params=_cparams(1),
    )(xa, xb, cs, s1, s2)


def _causal_mask(s):
    row = lax.broadcasted_iota(jnp.int32, s.shape, 0)
    col = lax.broadcasted_iota(jnp.int32, s.shape, 1)
    return col <= row


def _flash_fwd(name, qb, kv, kr, heads, scale):
    t = qb.shape[0]
    tq = _pick(t, TQ)
    nq = t // tq

    def body(qn_ref, qr_ref, kv_ref, kr_ref, o_ref, lse_ref):
        i = pl.program_id(1)
        q = jnp.concatenate([qn_ref[...], qr_ref[...]], axis=-1)

        def block(j, carry, masked):
            m_prev, l_prev, acc = carry
            rows = pl.ds(pl.multiple_of(j * tq, tq), tq)
            k = jnp.concatenate([kv_ref[rows, 0:LANES], kr_ref[rows, :]], axis=-1)
            s = lax.dot_general(q, k, NT, preferred_element_type=F32) * scale
            if masked:
                s = jnp.where(_causal_mask(s), s, -1e30)
            m_new = jnp.maximum(m_prev, jnp.max(s, axis=-1, keepdims=True))
            alpha = jnp.exp(m_prev - m_new)
            p = jnp.exp(s - m_new)
            l_new = alpha * l_prev + jnp.sum(p, axis=-1, keepdims=True)
            pv = jnp.dot(p.astype(BF), kv_ref[rows, LANES:2 * LANES], preferred_element_type=F32)
            return m_new, l_new, alpha * acc + pv

        init = (jnp.full((tq, 1), -1e30, F32), jnp.zeros((tq, 1), F32), jnp.zeros((tq, V_DIM), F32))
        carry = lax.fori_loop(0, i, lambda j, cr: block(j, cr, False), init)
        m_fin, l_fin, acc = block(i, carry, True)
        o_ref[...] = (acc / l_fin).astype(BF)
        lse_ref[...] = jnp.broadcast_to(m_fin + jnp.log(l_fin), (tq, LANES))

    return pl.pallas_call(
        body, name=name,
        out_shape=[jax.ShapeDtypeStruct((t, heads * V_DIM), BF), jax.ShapeDtypeStruct((t, heads * LANES), F32)],
        grid=(heads, nq),
        in_specs=[pl.BlockSpec((tq, LANES), lambda h, i: (i, h)),
                  pl.BlockSpec((tq, LANES), lambda h, i: (i, heads + h)),
                  pl.BlockSpec((t, 2 * LANES), lambda h, i: (0, h)),
                  pl.BlockSpec((t, LANES), lambda h, i: (0, 0))],
        out_specs=[pl.BlockSpec((tq, V_DIM), lambda h, i: (i, h)), pl.BlockSpec((tq, LANES), lambda h, i: (i, h))],
        compiler_params=_cparams(2),
    )(qb, qb, kv, kr)


def _flash_bwd_q(name, qb, kv, kr, o, lse, do, heads, scale):
    t = qb.shape[0]
    tq = _pick(t, TQ)
    nq = t // tq

    def body(qn_ref, qr_ref, kv_ref, kr_ref, o_ref, lse_ref, do_ref, dqn_ref, dqr_ref):
        i = pl.program_id(1)
        q = jnp.concatenate([qn_ref[...], qr_ref[...]], axis=-1)
        dov = do_ref[...]
        delta = jnp.sum(dov.astype(F32) * o_ref[...].astype(F32), axis=-1, keepdims=True)
        lse_col = lse_ref[:, 0:1]

        def block(j, dq, masked):
            rows = pl.ds(pl.multiple_of(j * tq, tq), tq)
            k = jnp.concatenate([kv_ref[rows, 0:LANES], kr_ref[rows, :]], axis=-1)
            s = lax.dot_general(q, k, NT, preferred_element_type=F32) * scale
            p = jnp.exp(s - lse_col)
            if masked:
                p = jnp.where(_causal_mask(s), p, 0.0)
            dp = lax.dot_general(dov, kv_ref[rows, LANES:2 * LANES], NT, preferred_element_type=F32)
            ds = (p * (dp - delta) * scale).astype(BF)
            return dq + jnp.dot(ds, k, preferred_element_type=F32)

        dq = lax.fori_loop(0, i, lambda j, acc: block(j, acc, False), jnp.zeros((tq, 2 * LANES), F32))
        dq = block(i, dq, True)
        dqn_ref[...] = dq[:, 0:LANES]
        dqr_ref[...] = dq[:, LANES:2 * LANES]

    head_blk = pl.BlockSpec((tq, LANES), lambda h, i: (i, h))
    return pl.pallas_call(
        body, name=name,
        out_shape=[jax.ShapeDtypeStruct((t, heads * LANES), F32)] * 2,
        grid=(heads, nq),
        in_specs=[head_blk, pl.BlockSpec((tq, LANES), lambda h, i: (i, heads + h)),
                  pl.BlockSpec((t, 2 * LANES), lambda h, i: (0, h)), pl.BlockSpec((t, LANES), lambda h, i: (0, 0)),
                  head_blk, head_blk, head_blk],
        out_specs=[head_blk, head_blk],
        compiler_params=_cparams(2),
    )(qb, qb, kv, kr, o, lse, do)


def _flash_bwd_kv(name, qb, kv, kr, o, lse, do, heads, scale):
    t = qb.shape[0]
    tq = _pick(t, TQ)
    nq = t // tq

    def body(qn_ref, qr_ref, kv_ref, kr_ref, o_ref, lse_ref, do_ref, dkv_ref, dkr_ref):
        j = pl.program_id(1)
        k = jnp.concatenate([kv_ref[:, 0:LANES], kr_ref[...]], axis=-1)
        v = kv_ref[:, LANES:2 * LANES]

        def block(i, carry, masked):
            dk, dv = carry
            rows = pl.ds(pl.multiple_of(i * tq, tq), tq)
            q = jnp.concatenate([qn_ref[rows, :], qr_ref[rows, :]], axis=-1)
            dov = do_ref[rows, :]
            delta = jnp.sum(dov.astype(F32) * o_ref[rows, :].astype(F32), axis=-1, keepdims=True)
            s = lax.dot_general(q, k, NT, preferred_element_type=F32) * scale
            p = jnp.exp(s - lse_ref[rows, 0:1])
            if masked:
                p = jnp.where(_causal_mask(s), p, 0.0)
            dv = dv + lax.dot_general(p.astype(BF), dov, TNDIMS, preferred_element_type=F32)
            dp = lax.dot_general(dov, v, NT, preferred_element_type=F32)
            ds = (p * (dp - delta) * scale).astype(BF)
            dk = dk + lax.dot_general(ds, q, TNDIMS, preferred_element_type=F32)
            return dk, dv

        carry = block(j, (jnp.zeros((tq, 2 * LANES), F32), jnp.zeros((tq, V_DIM), F32)), True)
        dk, dv = lax.fori_loop(j + 1, nq, lambda i, cr: block(i, cr, False), carry)
        dkv_ref[:, 0:LANES] = dk[:, 0:LANES].astype(BF)
        dkv_ref[:, LANES:2 * LANES] = dv.astype(BF)
        dkr_ref[...] = dk[:, LANES:2 * LANES]

    def full(hoff):
        return pl.BlockSpec((t, LANES), lambda h, j: (0, hoff + h))

    return pl.pallas_call(
        body, name=name,
        out_shape=[jax.ShapeDtypeStruct((t, heads * 2 * LANES), BF), jax.ShapeDtypeStruct((t, heads * LANES), F32)],
        grid=(heads, nq),
        in_specs=[full(0), full(heads), pl.BlockSpec((tq, 2 * LANES), lambda h, j: (j, h)),
                  pl.BlockSpec((tq, LANES), lambda h, j: (j, 0)), full(0), full(0), full(0)],
        out_specs=[pl.BlockSpec((tq, 2 * LANES), lambda h, j: (j, h)), pl.BlockSpec((tq, LANES), lambda h, j: (j, h))],
        compiler_params=_cparams(2),
    )(qb, qb, kv, kr, o, lse, do)


def _shift_down(z, n):
    row = lax.broadcasted_iota(jnp.int32, z.shape, 0)
    return jnp.where(row >= n, pltpu.roll(z, n, 0), 0.0)


def _shift_up(z, n):
    t = z.shape[0]
    row = lax.broadcasted_iota(jnp.int32, z.shape, 0)
    return jnp.where(row < t - n, pltpu.roll(z, t - n, 0), 0.0)


def _conv_fwd(name, proj, cw):
    t, d3 = proj.shape
    d = d3 // 3
    tn = _pick(d, LANES)
    nb = d // tn

    def body(b_ref, c_ref, u_ref, w_ref, o_ref):
        z = c_ref[...] * u_ref[...]
        zc = w_ref[0:1, :] * _shift_down(z, 2) + w_ref[1:2, :] * _shift_down(z, 1) + w_ref[2:3, :] * z
        o_ref[...] = (b_ref[...] * zc).astype(BF)

    def part(p):
        return pl.BlockSpec((t, tn), lambda j: (0, p * nb + j))

    return pl.pallas_call(
        body, name=name, out_shape=jax.ShapeDtypeStruct((t, d), BF), grid=(nb,),
        in_specs=[part(0), part(1), part(2), pl.BlockSpec((8, tn), lambda j: (0, j))],
        out_specs=pl.BlockSpec((t, tn), lambda j: (0, j)),
        compiler_params=_cparams(1),
    )(proj, proj, proj, cw)


def _conv_bwd(name, dbz, proj, cw):
    t, d3 = proj.shape
    d = d3 // 3
    tn = _pick(d, LANES)
    nb = d // tn

    def body(g_ref, b_ref, c_ref, u_ref, w_ref, o_ref, sums_ref):
        p = pl.program_id(1)
        w0, w1, w2 = w_ref[0:1, :], w_ref[1:2, :], w_ref[2:3, :]

        @pl.when(p == 0)
        def _():
            z = c_ref[...] * u_ref[...]
            z1, z2 = _shift_down(z, 1), _shift_down(z, 2)
            gv = g_ref[...]
            o_ref[...] = (gv * (w0 * z2 + w1 * z1 + w2 * z)).astype(BF)
            dzc = gv * b_ref[...]
            sums_ref[...] = jnp.zeros_like(sums_ref)
            sums_ref[0:1, :] = _colsum(dzc * z2)
            sums_ref[1:2, :] = _colsum(dzc * z1)
            sums_ref[2:3, :] = _colsum(dzc * z)

        @pl.when(p > 0)
        def _():
            dzc = g_ref[...] * b_ref[...]
            dz = w2 * dzc + w1 * _shift_up(dzc, 1) + w0 * _shift_up(dzc, 2)
            other = jnp.where(p == 1, u_ref[...], c_ref[...])
            o_ref[...] = (dz * other).astype(BF)

    def part(q):
        return pl.BlockSpec((t, tn), lambda j, p: (0, q * nb + j))

    return pl.pallas_call(
        body, name=name,
        out_shape=[jax.ShapeDtypeStruct((t, d3), BF), jax.ShapeDtypeStruct((8, d), F32)],
        grid=(nb, 3),
        in_specs=[pl.BlockSpec((t, tn), lambda j, p: (0, j)), part(0), part(1), part(2),
                  pl.BlockSpec((8, tn), lambda j, p: (0, j))],
        out_specs=[pl.BlockSpec((t, tn), lambda j, p: (0, p * nb + j)), pl.BlockSpec((8, tn), lambda j, p: (0, j))],
        compiler_params=_cparams(2),
    )(dbz, proj, proj, proj, cw)


def _pad_rows8(v):
    return jnp.pad(v, ((0, 8 - v.shape[0]), (0, 0)))


def kernel(x, c, positions, w_mod, b_mod, norm_g, mla_w_in, mla_g_q, mla_g_kv, mla_w_uq, mla_w_ukv, mla_w_o, conv_w_in, conv_w, conv_w_out, mlp_w_up, mlp_w_down, loss_target, m_w_mod, m_b_mod, m_norm_g, m_mla_w_in, m_mla_g_q, m_mla_g_kv, m_mla_w_uq, m_mla_w_ukv, m_mla_w_o, m_conv_w_in, m_conv_w, m_conv_w_out, m_mlp_w_up, m_mlp_w_down, v_w_mod, v_b_mod, v_norm_g, v_mla_w_in, v_mla_g_q, v_mla_g_kv, v_mla_w_uq, v_mla_w_ukv, v_mla_w_o, v_conv_w_in, v_conv_w, v_conv_w_out, v_mlp_w_up, v_mlp_w_down):
    t, d = x.shape[1], x.shape[2]
    depth = w_mod.shape[0]
    n_mod_loc = w_mod.shape[2]
    d_loc = norm_g.shape[2]
    rq = mla_g_q.shape[1]
    rkv = mla_g_kv.shape[1]
    heads = mla_w_uq.shape[2]
    lat_w = mla_w_in.shape[2]
    lat_pad = rq + rkv + LANES
    scale = (NOPE_DIM + ROPE_DIM) ** -0.5
    xi, yi, ci = _pos()
    dev = 4 * xi + 2 * yi + ci
    c_idx = jnp.reshape(ci, (1,)).astype(jnp.int32)
    chip_idx = jnp.reshape(2 * xi + yi, (1,)).astype(jnp.int32)
    x0 = x[0]
    target = loss_target[0]

    w_in_loc = jnp.pad(mla_w_in[0], ((0, 0), (0, lat_pad - lat_w))).astype(BF)
    uq = mla_w_uq[0]
    uq_loc = jnp.concatenate(
        [uq[:, :, :NOPE_DIM].reshape(uq.shape[0], heads * NOPE_DIM),
         jnp.pad(uq[:, :, NOPE_DIM:], ((0, 0), (0, 0), (0, LANES - ROPE_DIM))).reshape(uq.shape[0], heads * LANES)],
        axis=1).astype(BF)
    ukv_loc = mla_w_ukv[0].reshape(mla_w_ukv.shape[1], heads * (NOPE_DIM + V_DIM)).astype(BF)
    shards = [w_in_loc, uq_loc, ukv_loc, mla_w_o[0].astype(BF), conv_w_in[0].astype(BF), conv_w_out[0].astype(BF)]
    shards += [mlp_w_up[i].astype(BF) for i in range(depth)] + [mlp_w_down[i].astype(BF) for i in range(depth)]
    def merged(g):
        return g.reshape(g.shape[0] * g.shape[1], g.shape[2])

    g_win, g_uq, g_ukv, g_wo = [merged(g) for g in _ag_big("ag_mla", shards[:4])]
    later = [shards[6], shards[8], shards[4], shards[5], shards[7], shards[9]]
    ag_groups = [[0, 1], [2, 3, 4, 5]]
    ag_sems, later_thru, lands_thru, ag_token = _ag_start("ag_start", later, _own_blocks("ag_own_blocks", later), ag_groups, g_wo)

    n_ng = depth * 4 * d_loc
    small = jnp.concatenate([c.reshape(-1), norm_g.reshape(-1), conv_w.reshape(-1)])
    small_n = small.shape[0]
    small_rows = -(-small_n // (8 * LANES)) * 8
    small = jnp.pad(small, (0, small_rows * LANES - small_n)).reshape(small_rows, LANES)
    small_all = _ag_small("ag_small", small).reshape(N_DEV, small_rows * LANES)
    c_all = small_all[:, :d]
    ng_all = small_all[:, d:d + n_ng].reshape(N_DEV, depth, 4, d_loc).transpose(1, 2, 0, 3).reshape(depth, 4, d)
    cw_all = small_all[:, d + n_ng:d + n_ng + 3 * d_loc].reshape(N_DEV, 3, d_loc).transpose(1, 0, 2).reshape(3, d)
    cw8 = _pad_rows8(cw_all)
    cond_all = _silu("silu", c_all)

    mod_parts = jnp.stack([_mod_matvec(f"mod_matvec{i}", cond_all, w_mod[i]) for i in range(depth)])
    mod_rows = depth * N_DEV * n_mod_loc // LANES
    mod_all = _ag_small("ag_mod", mod_parts.reshape(mod_rows, LANES)).reshape(N_DEV, depth, N_DEV, n_mod_loc)
    mod_mine = lax.dynamic_index_in_dim(mod_all, dev, axis=2, keepdims=False)
    mod = mod_mine.transpose(1, 0, 2).reshape(depth, N_DEV * n_mod_loc) + b_mod
    mod = mod.reshape(depth, 6, d)

    def fwd_vec(nb, gate, na, sc, sh, token=None):
        zero = jnp.zeros((d,), F32)
        rows = [zero if r is None else r for r in (nb, gate, na, sc, sh)]
        vec = _pad_rows8(jnp.stack(rows))
        return vec if token is None else vec + token[0, 0]

    inv_freq = ROPE_THETA ** (-jnp.arange(0, ROPE_DIM, 2, dtype=F32) / ROPE_DIM)
    ang = positions[0].astype(F32)[:, None] * inv_freq
    cos, sin = jnp.cos(ang), jnp.sin(ang)
    zh = jnp.zeros_like(cos)
    zpad = jnp.zeros((t, LANES - ROPE_DIM), F32)
    rope_c = jnp.concatenate([cos, cos, zpad], axis=1)
    rope_s1 = jnp.concatenate([-sin, zh, zpad], axis=1)
    rope_s2 = jnp.concatenate([zh, sin, zpad], axis=1)

    (h0,) = _norm_fwd("pre0", x0, None, fwd_vec(None, None, ng_all[0, 0], mod[0, 1], mod[0, 0], ag_token), None, False, True, False)
    (lat,) = _linear("mla_lat", h0, g_win)
    cq, ckv, kr = _lat_post("mla_lat_post", lat, mla_g_q, mla_g_kv, rope_c, rope_s1, rope_s2, rq, rkv)
    (q_raw,) = _linear("mla_q", cq, g_uq)
    qb = _rope_heads("mla_q_rope", q_raw, 0, q_raw, 1, rope_c, rope_s1, rope_s2, heads)
    (kvb,) = _linear("mla_kv", ckv, g_ukv, out_dtypes=(BF,))
    o, lse = _flash_fwd("mla_attn", qb, kvb, kr, heads, scale)
    fs_a, fr_a, lands_a, _ = _ag_forward("ag_forward_a", later_thru[0:2], lands_thru[0:2], *ag_sems[0], o)
    (y0,) = _linear("mla_out", o, g_wo)
    x1, h1 = _norm_fwd("postpre1", x0, y0, fwd_vec(ng_all[0, 1], mod[0, 2], ng_all[0, 2], mod[0, 4], mod[0, 3]), None, True, True, False)
    g_up0, g_dn0 = _ag_finish("ag_finish_a", lands_a, fs_a, fr_a, x1)
    g_dn0 = merged(g_dn0)

    def sq_relu(acc):
        a = jnp.maximum(acc, 0.0)
        return acc, a * a

    u1, act1 = _linear("mlp0_up", h1, g_up0, out_dtypes=(F32, BF), epilogue=sq_relu)
    (y1,) = _linear("mlp0_down", act1, g_dn0)
    fs_b, fr_b, lands_b, _ = _ag_forward("ag_forward_b", later_thru[2:6], lands_thru[2:6], *ag_sems[1], y1)
    x2, h2 = _norm_fwd("postpre2", x1, y1, fwd_vec(ng_all[0, 3], mod[0, 5], ng_all[1, 0], mod[1, 1], mod[1, 0]), None, True, True, False)
    g_cin, g_cout, g_up1, g_dn1 = _ag_finish("ag_finish_b", lands_b, fs_b, fr_b, x2)
    g_cout, g_dn1 = merged(g_cout), merged(g_dn1)
    g_up, g_dn = [g_up0, g_up1], [g_dn0, g_dn1]

    (proj,) = _linear("conv_in", h2, g_cin)
    bz = _conv_fwd("conv_mix", proj, cw8)
    (y2,) = _linear("conv_out", bz, g_cout)
    x3, h3 = _norm_fwd("postpre3", x2, y2, fwd_vec(ng_all[1, 1], mod[1, 2], ng_all[1, 2], mod[1, 4], mod[1, 3]), None, True, True, False)

    u3, act3 = _linear("mlp1_up", h3, g_up[1], out_dtypes=(F32, BF), epilogue=sq_relu)
    (y3,) = _linear("mlp1_down", act3, g_dn[1])
    dx4, loss_cols = _norm_fwd("post_loss", x3, y3, fwd_vec(ng_all[1, 3], mod[1, 5], None, None, None), target, True, False, True)
    loss = lax.psum(0.5 * jnp.sum(loss_cols[0]) / d, ("x", "y", "c"))

    def bwd_vec(nb, gate, na, sc, token=None):
        zero = jnp.zeros((d,), F32)
        rows = [zero if r is None else r for r in (nb, gate, na, sc)]
        vec = _pad_rows8(jnp.stack(rows))
        return vec if token is None else vec + token[0, 0]

    def blocks(g, like):
        return g.reshape((N_DEV,) + like.shape)

    def rs_begin(tag, grads):
        from_sibling = _rs_sibling(f"rs_sibling_{tag}", grads)
        pair = [_pair_add(f"pair_add_{tag}{a}", grads[a], from_sibling[a], c_idx) for a in range(len(grads))]
        return _rs_chips_start(f"rs_start_{tag}", pair)

    def mlp_bwd(tag, dy, h, u, act, w_up, w_dn):
        def relu_grad(acc, uv):
            return (acc * (2.0 * jnp.maximum(uv, 0.0)),)

        (du,) = _linear_t(f"{tag}_dact", dy, w_dn, out_dtypes=(BF,), epilogue=relu_grad, extras=(u,))
        dw_dn = _wgrad(f"{tag}_dw_down", act, dy, w_dn)
        dw_up = _wgrad(f"{tag}_dw_up", h, du, w_up)
        (dh,) = _linear_t(f"{tag}_dh", du, w_up)
        return dh, dw_up, dw_dn

    dy3, sums4 = _norm_bwd("bwd_post3", dx4, None, None, y3, bwd_vec(ng_all[1, 3], mod[1, 5], None, None), False, True)
    dh3, dw_up1, dw_dn1 = mlp_bwd("mlp1", dy3, h3, u3, act3, g_up[1], g_dn[1])
    rs_mlp1 = rs_begin("mlp1", [dw_up1, blocks(dw_dn1, shards[9])])
    dx3, dy2, sums3 = _norm_bwd("bwd_norm3", dx4, dh3, x3, y2, bwd_vec(ng_all[1, 1], mod[1, 2], ng_all[1, 2], mod[1, 4], rs_mlp1[4]), True, True)

    (dbz,) = _linear_t("conv_dbz", dy2, g_cout)
    dw_cout = _wgrad("conv_dw_out", bz, dy2, g_cout)
    dproj, conv_sums = _conv_bwd("conv_mix_bwd", dbz, proj, cw8)
    dw_cin = _wgrad("conv_dw_in", h2, dproj, g_cin)
    rs_conv = rs_begin("conv", [dw_cin, blocks(dw_cout, shards[5])])
    (dh2,) = _linear_t("conv_dh", dproj, g_cin)
    dx2, dy1, sums2 = _norm_bwd("bwd_norm2", dx3, dh2, x2, y1, bwd_vec(ng_all[0, 3], mod[0, 5], ng_all[1, 0], mod[1, 1], rs_conv[4]), True, True)

    dh1, dw_up0, dw_dn0 = mlp_bwd("mlp0", dy1, h1, u1, act1, g_up[0], g_dn[0])
    rs_mlp0 = rs_begin("mlp0", [dw_up0, blocks(dw_dn0, shards[8])])
    dx1, dy0, sums1 = _norm_bwd("bwd_norm1", dx2, dh1, x1, y0, bwd_vec(ng_all[0, 1], mod[0, 2], ng_all[0, 2], mod[0, 4], rs_mlp0[4]), True, True)

    (do,) = _linear_t("mla_do", dy0, g_wo, out_dtypes=(BF,))
    dw_o = _wgrad("mla_dw_o", o, dy0, g_wo)
    dkv, dkr = _flash_bwd_kv("mla_attn_bwd_kv", qb, kvb, kr, o, lse, do, heads, scale)
    dqn, dqr = _flash_bwd_q("mla_attn_bwd_q", qb, kvb, kr, o, lse, do, heads, scale)
    dq = _rope_heads("mla_dq_rope", dqn, 0, dqr, 0, rope_c, -rope_s1, -rope_s2, heads)
    dw_uq = _wgrad("mla_dw_uq", cq, dq, g_uq)
    (dcq,) = _linear_t("mla_dcq", dq, g_uq)
    dw_ukv = _wgrad("mla_dw_ukv", ckv, dkv, g_ukv)
    (dckv,) = _linear_t("mla_dckv", dkv, g_ukv)
    dlat, lat_sums = _lat_bwd("mla_lat_bwd", dcq, dckv, dkr, lat, mla_g_q, mla_g_kv, rope_c, -rope_s1, -rope_s2, rq, rkv, heads)
    dw_win = _wgrad("mla_dw_in", h0, dlat, g_win)
    rs_mla = rs_begin("mla", [blocks(dw_win, w_in_loc), blocks(dw_uq, uq_loc), blocks(dw_ukv, ukv_loc), blocks(dw_o, shards[3])])
    (dh0,) = _linear_t("mla_dh", dlat, g_win)
    grad_x, sums0 = _norm_bwd("bwd_pre0", dx1, dh0, x0, None, bwd_vec(None, None, ng_all[0, 0], mod[0, 1], rs_mla[4]), True, False)

    def sums_rows(s, rows):
        return [s[r] for r in rows]

    dmod = jnp.stack([
        jnp.stack(sums_rows(sums0, (0, 1)) + sums_rows(sums1, (3,)) + sums_rows(sums1, (0, 1)) + sums_rows(sums2, (3,))),
        jnp.stack(sums_rows(sums2, (0, 1)) + sums_rows(sums3, (3,)) + sums_rows(sums3, (0, 1)) + sums_rows(sums4, (3,))),
    ])
    dng = jnp.stack([
        jnp.stack([sums0[2], sums1[4], sums1[2], sums2[4]]),
        jnp.stack([sums2[2], sums3[4], sums3[2], sums4[4]]),
    ])
    parts = [dmod.reshape(-1), dng.reshape(-1), lat_sums[0], lat_sums[1], conv_sums[0:3].reshape(-1)]
    sizes = [p.shape[0] for p in parts]
    packed = jnp.concatenate(parts)
    pk_rows = -(-packed.shape[0] // (8 * LANES)) * 8
    packed = jnp.pad(packed, (0, pk_rows * LANES - packed.shape[0])).reshape(pk_rows, LANES)
    packed_all = _ag_small("ag_small_grads", packed).reshape(N_DEV, pk_rows, LANES)
    total = _sum_devices("sum_small_grads", packed_all).reshape(-1)
    offs = [0]
    for s in sizes:
        offs.append(offs[-1] + s)
    g_b_mod = total[offs[0]:offs[1]].reshape(depth, 6 * d)
    ng_full = total[offs[1]:offs[2]].reshape(depth, 4, d)
    g_norm_g = lax.dynamic_slice_in_dim(ng_full, dev * d_loc, d_loc, axis=2)
    g_g_q = total[offs[2]:offs[3]].reshape(1, rq)
    g_g_kv = total[offs[3]:offs[4]].reshape(1, rkv)
    cw_full = total[offs[4]:offs[5]].reshape(1, 3, d)
    g_conv_w = lax.dynamic_slice_in_dim(cw_full, dev * d_loc, d_loc, axis=2)

    dmod_all = packed_all.reshape(N_DEV, -1)[:, :sizes[0]].reshape(N_DEV, depth, 6 * d)
    dmod_cols = lax.dynamic_slice_in_dim(dmod_all, dev * n_mod_loc, n_mod_loc, axis=2)
    cond_t = cond_all.T
    mod_out = [_mod_grad_adam(f"w_mod_adam{i}", cond_t, dmod_cols[:, i, :], w_mod[i], m_w_mod[i], v_w_mod[i]) for i in range(depth)]
    g_w_mod, d_w_mod, nm_w_mod, nv_w_mod = [jnp.stack([mod_out[i][k] for i in range(depth)]) for k in range(4)]

    def rs_end(tag, started):
        send_sems, recv_sems, pair, lands, _ = started
        pair, lands = _rs_chips_wait(f"rs_wait_{tag}", pair, lands, send_sems, recv_sems, g_w_mod)
        return [_chip_sum(f"chip_sum_{tag}{a}", pair[a], lands[a], chip_idx) for a in range(len(pair))]

    red_up1, red_dn1 = rs_end("mlp1", rs_mlp1)
    red_cin, red_cout = rs_end("conv", rs_conv)
    red_up0, red_dn0 = rs_end("mlp0", rs_mlp0)
    red_win, red_uq, red_ukv, red_wo = rs_end("mla", rs_mla)

    g_mla_w_in = red_win[:, :lat_w][None]
    g_mla_w_uq = jnp.concatenate(
        [red_uq[:, :heads * NOPE_DIM].reshape(-1, heads, NOPE_DIM),
         red_uq[:, heads * NOPE_DIM:].reshape(-1, heads, LANES)[:, :, :ROPE_DIM]], axis=2)[None]
    g_mla_w_ukv = red_ukv.reshape(mla_w_ukv.shape)
    g_mla_w_o = red_wo[None]
    g_conv_w_in = red_cin[None]
    g_conv_w_out = red_cout[None]
    g_mlp_w_up = jnp.stack([red_up0, red_up1])
    g_mlp_w_down = jnp.stack([red_dn0, red_dn1])

    grads = [g_w_mod, g_b_mod, g_norm_g, g_mla_w_in, g_g_q, g_g_kv, g_mla_w_uq, g_mla_w_ukv, g_mla_w_o,
             g_conv_w_in, g_conv_w, g_conv_w_out, g_mlp_w_up, g_mlp_w_down]
    weights = [w_mod, b_mod, norm_g, mla_w_in, mla_g_q, mla_g_kv, mla_w_uq, mla_w_ukv, mla_w_o, conv_w_in, conv_w, conv_w_out, mlp_w_up, mlp_w_down]
    ms = [m_w_mod, m_b_mod, m_norm_g, m_mla_w_in, m_mla_g_q, m_mla_g_kv, m_mla_w_uq, m_mla_w_ukv, m_mla_w_o, m_conv_w_in, m_conv_w, m_conv_w_out, m_mlp_w_up, m_mlp_w_down]
    vs = [v_w_mod, v_b_mod, v_norm_g, v_mla_w_in, v_mla_g_q, v_mla_g_kv, v_mla_w_uq, v_mla_w_ukv, v_mla_w_o, v_conv_w_in, v_conv_w, v_conv_w_out, v_mlp_w_up, v_mlp_w_down]
    names = ["w_mod", "b_mod", "norm_g", "mla_w_in", "mla_g_q", "mla_g_kv", "mla_w_uq", "mla_w_ukv", "mla_w_o", "conv_w_in", "conv_w", "conv_w_out", "mlp_w_up", "mlp_w_down"]
    deltas, new_ms, new_vs = [d_w_mod], [nm_w_mod], [nv_w_mod]
    for k in range(1, len(weights)):
        dl, nm, nv = _adam(f"adam_{names[k]}", weights[k], grads[k], ms[k], vs[k])
        deltas.append(dl)
        new_ms.append(nm)
        new_vs.append(nv)

    return (loss, grad_x[None], *grads, *deltas, *new_ms, *new_vs)
```

```python
import jax
import jax.numpy as jnp
from jax import lax
from jax.experimental import pallas as pl
from jax.experimental.pallas import tpu as pltpu

F32 = jnp.float32
BF = jnp.bfloat16
MESH = pl.DeviceIdType.MESH
N_DEV = 8
LANES = 128
NORM_EPS = 1e-6
ROPE_THETA = 10000.0
ROPE_DIM = 64
NOPE_DIM = 128
V_DIM = 128
LR, B1, B2, ADAM_EPS, WD, STEP = 0.001, 0.9, 0.999, 1e-08, 0.01, 10
VMEM_LIMIT = 56 * 1024 * 1024
TILE_BYTES = 2 * 1024 * 1024
TM, TN, TK = 1024, 512, 2048
TQ = 512

NN = (((1,), (0,)), ((), ()))
NT = (((1,), (1,)), ((), ()))
TNDIMS = (((0,), (0,)), ((), ()))
HBM_SPEC = pl.BlockSpec(memory_space=pltpu.HBM)
VMEM_SPEC = pl.BlockSpec(memory_space=pltpu.VMEM)


def _cparams(n_axes):
    return pltpu.CompilerParams(dimension_semantics=("arbitrary",) * n_axes, vmem_limit_bytes=VMEM_LIMIT)


def _pick(dim, pref):
    if dim <= pref:
        return dim
    for t in range(pref - pref % LANES, 0, -LANES):
        if t > 0 and dim % t == 0:
            return t
    for t in range(pref, 0, -1):
        if dim % t == 0:
            return t
    return dim


def _rows(r, c, itemsize=4):
    want = max(8, TILE_BYTES // (itemsize * max(c, 1)))
    if r <= want:
        return r
    for t in range(want - want % 8, 0, -8):
        if t > 0 and r % t == 0:
            return t
    return r


def _pos():
    return lax.axis_index("x"), lax.axis_index("y"), lax.axis_index("c")


def _ag_small(name, v):
    m_per, n = v.shape

    def body(x_ref, out_ref, send_sems, recv_sems, local_sem):
        x, y, c = _pos()
        me, sibling = (x, y, c), (x, y, 1 - c)
        chips = [(1 - x, y), (x, 1 - y), (1 - x, 1 - y)]

        def rows(px, py, pc):
            return out_ref.at[pl.ds((4 * px + 2 * py + pc) * m_per, m_per), :]

        def copy(k, block, to, src=None):
            return pltpu.make_async_remote_copy(
                src_ref=rows(*block) if src is None else src, dst_ref=rows(*block),
                send_sem=send_sems.at[k], recv_sem=recv_sems.at[k], device_id=to, device_id_type=MESH)

        mine = pltpu.make_async_copy(x_ref, rows(*me), local_sem)
        mine.start()
        first = [copy(0, me, sibling, src=x_ref)]
        first += [copy(1 + j, me, (*chip, c), src=x_ref) for j, chip in enumerate(chips)]
        for cp in first:
            cp.start()
        passed = [copy(4 + j, (*chip, c), sibling) for j, chip in enumerate(chips)]
        for j, chip in enumerate(chips):
            copy(1 + j, (*chip, c), me).wait_recv()
            passed[j].start()
        copy(0, sibling, me).wait_recv()
        for j, chip in enumerate(chips):
            copy(4 + j, (*chip, 1 - c), me).wait_recv()
        for cp in first + passed:
            cp.wait_send()
        mine.wait()

    return pl.pallas_call(
        body, name=name,
        out_shape=jax.ShapeDtypeStruct((N_DEV * m_per, n), v.dtype),
        in_specs=[VMEM_SPEC], out_specs=VMEM_SPEC,
        scratch_shapes=[pltpu.SemaphoreType.DMA((7,)), pltpu.SemaphoreType.DMA((7,)), pltpu.SemaphoreType.DMA],
    )(v)


def _ag_big(name, arrs):
    n = len(arrs)

    def body(*refs):
        ins, outs = refs[:n], refs[n:2 * n]
        send_sems, recv_sems, local_sems = refs[2 * n:]
        x, y, c = _pos()
        me, sibling = (x, y, c), (x, y, 1 - c)
        chips = [(1 - x, y), (x, 1 - y), (1 - x, 1 - y)]

        def copy(a, k, block, to, src=None):
            dst = outs[a].at[4 * block[0] + 2 * block[1] + block[2]]
            return pltpu.make_async_remote_copy(
                src_ref=dst if src is None else src, dst_ref=dst,
                send_sem=send_sems.at[7 * a + k], recv_sem=recv_sems.at[7 * a + k],
                device_id=to, device_id_type=MESH)

        mine = [pltpu.make_async_copy(ins[a], outs[a].at[4 * x + 2 * y + c], local_sems.at[a]) for a in range(n)]
        for cp in mine:
            cp.start()
        first = []
        for a in range(n):
            first.append(copy(a, 0, me, sibling, src=ins[a]))
            first += [copy(a, 1 + j, me, (*chip, c), src=ins[a]) for j, chip in enumerate(chips)]
        for cp in first:
            cp.start()
        passed = []
        for j, chip in enumerate(chips):
            for a in range(n):
                copy(a, 1 + j, (*chip, c), me).wait_recv()
                cp = copy(a, 4 + j, (*chip, c), sibling)
                cp.start()
                passed.append(cp)
        for a in range(n):
            copy(a, 0, sibling, me).wait_recv()
            for j, chip in enumerate(chips):
                copy(a, 4 + j, (*chip, 1 - c), me).wait_recv()
        for cp in first + passed:
            cp.wait_send()
        for cp in mine:
            cp.wait()

    return pl.pallas_call(
        body, name=name,
        out_shape=[jax.ShapeDtypeStruct((N_DEV,) + a.shape, a.dtype) for a in arrs],
        in_specs=[HBM_SPEC] * n, out_specs=[HBM_SPEC] * n,
        scratch_shapes=[pltpu.SemaphoreType.DMA((7 * n,)), pltpu.SemaphoreType.DMA((7 * n,)),
                        pltpu.SemaphoreType.DMA((n,))],
    )(*arrs)


def _rs_sibling(name, arrs):
    n = len(arrs)

    def body(*refs):
        ins, outs = refs[:n], refs[n:2 * n]
        send_sems, recv_sems = refs[2 * n:]
        x, y, c = _pos()
        copies = []
        for a in range(n):
            for j in range(4):
                copies.append(pltpu.make_async_remote_copy(
                    src_ref=ins[a].at[2 * j + (1 - c)], dst_ref=outs[a].at[j],
                    send_sem=send_sems.at[4 * a + j], recv_sem=recv_sems.at[4 * a + j],
                    device_id=(x, y, 1 - c), device_id_type=MESH))
        for cp in copies:
            cp.start()
        for cp in copies:
            cp.wait()

    return pl.pallas_call(
        body, name=name,
        out_shape=[jax.ShapeDtypeStruct((4,) + a.shape[1:], a.dtype) for a in arrs],
        in_specs=[HBM_SPEC] * n, out_specs=[HBM_SPEC] * n,
        scratch_shapes=[pltpu.SemaphoreType.DMA((4 * n,)), pltpu.SemaphoreType.DMA((4 * n,))],
    )(*arrs)


SEM_SPEC = pl.BlockSpec(memory_space=pltpu.SEMAPHORE)
ANY_SPEC = pl.BlockSpec(memory_space=pl.ANY)
EFFECT = pltpu.SideEffectType.DATAFLOW_SIDE_EFFECTING


def _hbm(a):
    return pltpu.with_memory_space_constraint(a, pltpu.HBM)


def _cast_own_block(name, w, dev_idx):
    r, cdim = w.shape
    tr = _rows(r, cdim, 4)

    def body(d_ref, w_ref, o_ref):
        o_ref[...] = w_ref[...].astype(BF)

    return pl.pallas_call(
        body, name=name,
        out_shape=jax.ShapeDtypeStruct((N_DEV, r, cdim), BF),
        grid_spec=pltpu.PrefetchScalarGridSpec(
            num_scalar_prefetch=1, grid=(r // tr,),
            in_specs=[pl.BlockSpec((tr, cdim), lambda i, d_ref: (i, 0))],
            out_specs=pl.BlockSpec((None, tr, cdim), lambda i, d_ref: (d_ref[0], i, 0))),
        compiler_params=_cparams(1),
    )(dev_idx, w)


def _ag_start(name, lands, groups, after):
    n = len(lands)
    ng = len(groups)
    n_after = len(after)

    def body(*refs):
        lnd = refs[:n]
        sems = refs[n + n_after:n + n_after + 2 * ng]
        token = refs[-1]
        x, y, c = _pos()
        targets = [(x, y, 1 - c), (1 - x, y, c), (x, 1 - y, c), (1 - x, 1 - y, c)]
        for gi, members in enumerate(groups):
            for pos, a in enumerate(members):
                own = lnd[a].at[4 * x + 2 * y + c]
                for k, to in enumerate(targets):
                    pltpu.make_async_remote_copy(
                        src_ref=own, dst_ref=own,
                        send_sem=sems[2 * gi].at[4 * pos + k], recv_sem=sems[2 * gi + 1].at[4 * pos + k],
                        device_id=to, device_id_type=MESH).start()
        token[...] = jnp.zeros_like(token)

    sem_shapes = []
    for members in groups:
        sem_shapes += [pltpu.SemaphoreType.DMA((4 * len(members),))] * 2
    res = pl.pallas_call(
        body, name=name,
        out_shape=(*sem_shapes, *[pltpu.HBM(a.shape, a.dtype) for a in lands], jax.ShapeDtypeStruct((8, LANES), F32)),
        in_specs=[HBM_SPEC] * n + [ANY_SPEC] * n_after,
        out_specs=(*[SEM_SPEC] * (2 * ng), *[HBM_SPEC] * n, VMEM_SPEC),
        input_output_aliases={i: 2 * ng + i for i in range(n)},
        compiler_params=pltpu.CompilerParams(has_side_effects=EFFECT),
    )(*[_hbm(a) for a in lands], *after)
    sem_pairs = [(res[2 * g], res[2 * g + 1]) for g in range(ng)]
    return sem_pairs, list(res[2 * ng:2 * ng + n]), res[-1]


def _ag_forward(name, lands, send_sems, recv_sems, after):
    n = len(lands)

    def body(*refs):
        lnd = refs[:n]
        s_sems, r_sems = refs[n], refs[n + 1]
        fs_sems, fr_sems = refs[n + 3], refs[n + 4]
        token = refs[-1]
        x, y, c = _pos()
        sources = [(x, y, 1 - c), (1 - x, y, c), (x, 1 - y, c), (1 - x, 1 - y, c)]
        for a in range(n):
            own = lnd[a].at[4 * x + 2 * y + c]
            for k, src in enumerate(sources):
                blk = lnd[a].at[4 * src[0] + 2 * src[1] + src[2]]
                arrived = pltpu.make_async_remote_copy(
                    src_ref=own, dst_ref=blk, send_sem=s_sems.at[4 * a + k], recv_sem=r_sems.at[4 * a + k],
                    device_id=src, device_id_type=MESH)
                arrived.wait_send()
                arrived.wait_recv()
                if k > 0:
                    pltpu.make_async_remote_copy(
                        src_ref=blk, dst_ref=blk, send_sem=fs_sems.at[3 * a + k - 1], recv_sem=fr_sems.at[3 * a + k - 1],
                        device_id=(x, y, 1 - c), device_id_type=MESH).start()
        token[...] = jnp.zeros_like(token)

    res = pl.pallas_call(
        body, name=name,
        out_shape=(pltpu.SemaphoreType.DMA((3 * n,)), pltpu.SemaphoreType.DMA((3 * n,)),
                   *[pltpu.HBM(a.shape, a.dtype) for a in lands], jax.ShapeDtypeStruct((8, LANES), F32)),
        in_specs=[HBM_SPEC] * n + [SEM_SPEC, SEM_SPEC, ANY_SPEC],
        out_specs=(SEM_SPEC, SEM_SPEC, *[HBM_SPEC] * n, VMEM_SPEC),
        input_output_aliases={i: 2 + i for i in range(n)},
        compiler_params=pltpu.CompilerParams(has_side_effects=EFFECT),
    )(*lands, send_sems, recv_sems, after)
    return res[0], res[1], list(res[2:2 + n]), res[-1]


def _ag_finish(name, lands, fs_sems, fr_sems, after):
    n = len(lands)

    def body(*refs):
        lnd = refs[:n]
        fs, fr = refs[n], refs[n + 1]
        x, y, c = _pos()
        chips = [(1 - x, y), (x, 1 - y), (1 - x, 1 - y)]
        for a in range(n):
            for j, chip in enumerate(chips):
                mine = lnd[a].at[4 * chip[0] + 2 * chip[1] + c]
                theirs = lnd[a].at[4 * chip[0] + 2 * chip[1] + 1 - c]
                cp = pltpu.make_async_remote_copy(
                    src_ref=mine, dst_ref=theirs, send_sem=fs.at[3 * a + j], recv_sem=fr.at[3 * a + j],
                    device_id=(x, y, 1 - c), device_id_type=MESH)
                cp.wait_send()
                cp.wait_recv()

    res = pl.pallas_call(
        body, name=name,
        out_shape=tuple(pltpu.HBM(a.shape, a.dtype) for a in lands),
        in_specs=[HBM_SPEC] * n + [SEM_SPEC, SEM_SPEC, ANY_SPEC],
        out_specs=tuple([HBM_SPEC] * n),
        input_output_aliases={i: i for i in range(n)},
        compiler_params=pltpu.CompilerParams(has_side_effects=EFFECT),
    )(*lands, fs_sems, fr_sems, after)
    return list(res)


def _rs_chips_start(name, arrs):
    n = len(arrs)

    def body(*refs):
        ins, lnd = refs[:n], refs[n:2 * n]
        s_sems, r_sems = refs[2 * n], refs[2 * n + 1]
        token = refs[-1]
        x, y, c = _pos()
        chips = [(1 - x, y), (x, 1 - y), (1 - x, 1 - y)]
        for a in range(n):
            for k, chip in enumerate(chips):
                pltpu.make_async_remote_copy(
                    src_ref=ins[a].at[2 * chip[0] + chip[1]], dst_ref=lnd[a].at[k],
                    send_sem=s_sems.at[3 * a + k], recv_sem=r_sems.at[3 * a + k],
                    device_id=(*chip, c), device_id_type=MESH).start()
        token[...] = jnp.zeros_like(token)

    lands = [lax.empty((3,) + a.shape[1:], a.dtype) for a in arrs]
    res = pl.pallas_call(
        body, name=name,
        out_shape=(pltpu.SemaphoreType.DMA((3 * n,)), pltpu.SemaphoreType.DMA((3 * n,)),
                   *[pltpu.HBM(a.shape, a.dtype) for a in arrs], *[pltpu.HBM(a.shape, a.dtype) for a in lands],
                   jax.ShapeDtypeStruct((8, LANES), F32)),
        in_specs=[HBM_SPEC] * (2 * n),
        out_specs=(SEM_SPEC, SEM_SPEC, *[HBM_SPEC] * (2 * n), VMEM_SPEC),
        input_output_aliases={i: 2 + i for i in range(2 * n)},
        compiler_params=pltpu.CompilerParams(has_side_effects=EFFECT),
    )(*[_hbm(a) for a in arrs], *[_hbm(a) for a in lands])
    return res[0], res[1], list(res[2:2 + n]), list(res[2 + n:2 + 2 * n]), res[-1]


def _rs_chips_wait(name, arrs, lands, send_sems, recv_sems, after):
    n = len(arrs)

    def body(*refs):
        ins, lnd = refs[:n], refs[n:2 * n]
        s_sems, r_sems = refs[2 * n], refs[2 * n + 1]
        x, y, c = _pos()
        chips = [(1 - x, y), (x, 1 - y), (1 - x, 1 - y)]
        for a in range(n):
            for k, chip in enumerate(chips):
                cp = pltpu.make_async_remote_copy(
                    src_ref=ins[a].at[2 * chip[0] + chip[1]], dst_ref=lnd[a].at[k],
                    send_sem=s_sems.at[3 * a + k], recv_sem=r_sems.at[3 * a + k],
                    device_id=(*chip, c), device_id_type=MESH)
                cp.wait_send()
                cp.wait_recv()

    res = pl.pallas_call(
        body, name=name,
        out_shape=tuple(pltpu.HBM(a.shape, a.dtype) for a in list(arrs) + list(lands)),
        in_specs=[HBM_SPEC] * (2 * n) + [SEM_SPEC, SEM_SPEC, ANY_SPEC],
        out_specs=tuple([HBM_SPEC] * (2 * n)),
        input_output_aliases={i: i for i in range(2 * n)},
        compiler_params=pltpu.CompilerParams(has_side_effects=EFFECT),
    )(*arrs, *lands, send_sems, recv_sems, after)
    return list(res[:n]), list(res[n:])


def _pair_add(name, g, r1, c_idx):
    _, r, cdim = g.shape
    tr = _rows(r, cdim, 2)

    def body(c_ref, g_ref, r_ref, o_ref):
        o_ref[...] = (g_ref[...].astype(F32) + r_ref[...].astype(F32)).astype(o_ref.dtype)

    return pl.pallas_call(
        body, name=name,
        out_shape=jax.ShapeDtypeStruct((4, r, cdim), BF),
        grid_spec=pltpu.PrefetchScalarGridSpec(
            num_scalar_prefetch=1, grid=(4, r // tr),
            in_specs=[pl.BlockSpec((None, tr, cdim), lambda j, i, c_ref: (2 * j + c_ref[0], i, 0)),
                      pl.BlockSpec((None, tr, cdim), lambda j, i, c_ref: (j, i, 0))],
            out_specs=pl.BlockSpec((None, tr, cdim), lambda j, i, c_ref: (j, i, 0))),
        compiler_params=_cparams(2),
    )(c_idx, g, r1)


def _chip_sum(name, p, r2, chip_idx):
    _, r, cdim = p.shape
    tr = _rows(r, cdim, 4)

    def body(s_ref, p_ref, a_ref, b_ref, c_ref, o_ref):
        o_ref[...] = ((p_ref[...].astype(F32) + a_ref[...].astype(F32)) + b_ref[...].astype(F32)) + c_ref[...].astype(F32)

    def other(k):
        return pl.BlockSpec((None, tr, cdim), lambda i, s_ref: (k, i, 0))

    return pl.pallas_call(
        body, name=name,
        out_shape=jax.ShapeDtypeStruct((r, cdim), F32),
        grid_spec=pltpu.PrefetchScalarGridSpec(
            num_scalar_prefetch=1, grid=(r // tr,),
            in_specs=[pl.BlockSpec((None, tr, cdim), lambda i, s_ref: (s_ref[0], i, 0)), other(0), other(1), other(2)],
            out_specs=pl.BlockSpec((tr, cdim), lambda i, s_ref: (i, 0))),
        compiler_params=_cparams(1),
    )(chip_idx, p, r2, r2, r2)


def _adam_math(w, g, m, v):
    m = B1 * m + (1.0 - B1) * g
    v = B2 * v + (1.0 - B2) * (g * g)
    m_hat = m / (1.0 - B1 ** STEP)
    v_hat = v / (1.0 - B2 ** STEP)
    delta = -LR * (m_hat / (jnp.sqrt(v_hat) + ADAM_EPS) + WD * w)
    return delta, m, v


def _adam(name, w, g, m, v):
    shape = w.shape
    cdim = shape[-1]
    r = w.size // cdim
    flat = [a.reshape(r, cdim) for a in (w, g, m, v)]
    tr = _rows(r, cdim, 4)

    def body(w_ref, g_ref, m_ref, v_ref, d_ref, mo_ref, vo_ref):
        d, mn, vn = _adam_math(w_ref[...], g_ref[...], m_ref[...], v_ref[...])
        d_ref[...] = d
        mo_ref[...] = mn
        vo_ref[...] = vn

    spec = pl.BlockSpec((tr, cdim), lambda i: (i, 0))
    outs = pl.pallas_call(
        body, name=name,
        out_shape=[jax.ShapeDtypeStruct((r, cdim), F32)] * 3,
        grid=(r // tr,), in_specs=[spec] * 4, out_specs=[spec] * 3,
        compiler_params=_cparams(1),
    )(*flat)
    return [o.reshape(shape) for o in outs]


def _mod_matvec(name, cond_all, w_loc):
    d, n_loc = w_loc.shape
    tn = _pick(n_loc, 512)

    def body(c_ref, w_ref, o_ref):
        o_ref[...] = jnp.dot(c_ref[...].astype(BF), w_ref[...].astype(BF), preferred_element_type=F32)

    return pl.pallas_call(
        body, name=name,
        out_shape=jax.ShapeDtypeStruct((N_DEV, n_loc), F32),
        grid=(n_loc // tn,),
        in_specs=[pl.BlockSpec((N_DEV, d), lambda j: (0, 0)), pl.BlockSpec((d, tn), lambda j: (0, j))],
        out_specs=pl.BlockSpec((N_DEV, tn), lambda j: (0, j)),
        compiler_params=_cparams(1),
    )(cond_all, w_loc)


def _mod_grad_adam(name, cond_t, dmod, w, m, v):
    d, n_loc = w.shape
    tr = _rows(d, n_loc, 4)

    def body(ct_ref, dm_ref, w_ref, m_ref, v_ref, g_ref, d_ref, mo_ref, vo_ref):
        g = ct_ref[:, 0:1] * dm_ref[0:1, :]
        for b in range(1, N_DEV):
            g = g + ct_ref[:, b:b + 1] * dm_ref[b:b + 1, :]
        dl, mn, vn = _adam_math(w_ref[...], g, m_ref[...], v_ref[...])
        g_ref[...] = g
        d_ref[...] = dl
        mo_ref[...] = mn
        vo_ref[...] = vn

    spec = pl.BlockSpec((tr, n_loc), lambda i: (i, 0))
    return pl.pallas_call(
        body, name=name,
        out_shape=[jax.ShapeDtypeStruct((d, n_loc), F32)] * 4,
        grid=(d // tr,),
        in_specs=[pl.BlockSpec((tr, N_DEV), lambda i: (i, 0)), pl.BlockSpec((N_DEV, n_loc), lambda i: (0, 0)), spec, spec, spec],
        out_specs=[spec] * 4,
        compiler_params=_cparams(1),
    )(cond_t, dmod, w, m, v)


def _sum_devices(name, gathered):
    _, m, n = gathered.shape

    def body(g_ref, o_ref):
        acc = g_ref[0]
        for d in range(1, N_DEV):
            acc = acc + g_ref[d]
        o_ref[...] = acc

    return pl.pallas_call(
        body, name=name, out_shape=jax.ShapeDtypeStruct((m, n), F32),
        in_specs=[VMEM_SPEC], out_specs=VMEM_SPEC,
    )(gathered)


def _silu(name, c_pad):
    def body(c_ref, o_ref):
        c = c_ref[...]
        o_ref[...] = c * (1.0 / (1.0 + jnp.exp(-c)))

    return pl.pallas_call(body, name=name, out_shape=jax.ShapeDtypeStruct(c_pad.shape, F32),
                          in_specs=[VMEM_SPEC], out_specs=VMEM_SPEC)(c_pad)


def _mm(name, a, b, dims, grid, a_spec, b_spec, outs, extras=(), epilogue=None):
    nk = grid[2]
    n_ex, n_out = len(extras), len(outs)

    def body(*refs):
        a_ref, b_ref = refs[0], refs[1]
        ex_refs = refs[2:2 + n_ex]
        out_refs = refs[2 + n_ex:2 + n_ex + n_out]
        part = lax.dot_general(a_ref[...].astype(BF), b_ref[...].astype(BF), dims, preferred_element_type=F32)

        def finish(acc):
            vals = (acc,) if epilogue is None else epilogue(acc, *[e[...] for e in ex_refs])
            for o_ref, val in zip(out_refs, vals):
                o_ref[...] = val.astype(o_ref.dtype)

        if nk == 1:
            finish(part)
        else:
            acc_ref = refs[-1]
            k = pl.program_id(2)

            @pl.when(k == 0)
            def _():
                acc_ref[...] = part

            @pl.when(k > 0)
            def _():
                acc_ref[...] += part

            @pl.when(k == nk - 1)
            def _():
                finish(acc_ref[...])

    tile = outs[0][2].block_shape
    acc_shape = tuple(t for t in tile if t is not None)
    res = pl.pallas_call(
        body, name=name,
        out_shape=[jax.ShapeDtypeStruct(s, d) for s, d, _ in outs],
        grid=grid,
        in_specs=[a_spec, b_spec] + [s for _, s in extras],
        out_specs=[s for _, _, s in outs],
        scratch_shapes=[] if nk == 1 else [pltpu.VMEM(acc_shape, F32)],
        compiler_params=_cparams(3),
    )(a, b, *[e for e, _ in extras])
    return res


def _w_nn_spec(w, tk, tn):
    if w.ndim == 2:
        return pl.BlockSpec((tk, tn), lambda i, j, k: (k, j))
    per = w.shape[2] // tn
    return pl.BlockSpec((None, tk, tn), lambda i, j, k: (j // per, k, j % per))


def _w_nt_spec(w, tn, tk):
    if w.ndim == 2:
        return pl.BlockSpec((tn, tk), lambda i, j, k: (j, k))
    per = w.shape[2] // tk
    return pl.BlockSpec((None, tn, tk), lambda i, j, k: (k // per, j, k % per))


def _w_cols(w):
    return w.shape[1] if w.ndim == 2 else w.shape[0] * w.shape[2]


def _w_rows(w):
    return w.shape[0] if w.ndim == 2 else w.shape[1]


def _col_tile(w, pref):
    return _pick(w.shape[1] if w.ndim == 2 else w.shape[2], pref)


def _linear(name, a, w, out_dtypes=(F32,), epilogue=None, extras=()):
    t, kdim = a.shape
    n = _w_cols(w)
    tm, tn, tk = _pick(t, TM), _col_tile(w, TN), _pick(kdim, TK)
    o_spec = pl.BlockSpec((tm, tn), lambda i, j, k: (i, j))
    return _mm(name, a, w, NN, (t // tm, n // tn, kdim // tk),
               pl.BlockSpec((tm, tk), lambda i, j, k: (i, k)), _w_nn_spec(w, tk, tn),
               [((t, n), dt, o_spec) for dt in out_dtypes],
               extras=[(e, o_spec) for e in extras], epilogue=epilogue)


def _linear_t(name, dy, w, out_dtypes=(F32,), epilogue=None, extras=()):
    t, n = dy.shape
    kdim = _w_rows(w)
    tm, tn, tk = _pick(t, TM), _pick(kdim, TN), _col_tile(w, TK)
    o_spec = pl.BlockSpec((tm, tn), lambda i, j, k: (i, j))
    return _mm(name, dy, w, NT, (t // tm, kdim // tn, n // tk),
               pl.BlockSpec((tm, tk), lambda i, j, k: (i, k)), _w_nt_spec(w, tn, tk),
               [((t, kdim), dt, o_spec) for dt in out_dtypes],
               extras=[(e, o_spec) for e in extras], epilogue=epilogue)


def _wgrad(name, a, dy, like):
    t, kdim = a.shape
    n = dy.shape[1]
    tm, tn, tk = _pick(kdim, TM), _col_tile(like, TM), _pick(t, 1024)
    if like.ndim == 2:
        o_spec = pl.BlockSpec((tm, tn), lambda i, j, k: (i, j))
    else:
        per = like.shape[2] // tn
        o_spec = pl.BlockSpec((None, tm, tn), lambda i, j, k: (j // per, i, j % per))
    return _mm(name, a, dy, TNDIMS, (kdim // tm, n // tn, t // tk),
               pl.BlockSpec((tk, tm), lambda i, j, k: (k, i)), pl.BlockSpec((tk, tn), lambda i, j, k: (k, j)),
               [(like.shape, BF, o_spec)])[0]


def _rstd(x):
    return lax.rsqrt(jnp.mean(x * x, axis=-1, keepdims=True) + NORM_EPS)


def _colsum(x):
    return jnp.sum(x, axis=0, keepdims=True)


def _norm_fwd(name, x, y, vec, target, has_post, has_pre, has_loss):
    t, d = x.shape
    tm = _rows(t, d, 4)
    ins, in_specs = [x], [pl.BlockSpec((tm, d), lambda i: (i, 0))]
    row_spec = pl.BlockSpec((tm, d), lambda i: (i, 0))
    if has_post:
        ins.append(y)
        in_specs.append(row_spec)
    ins.append(vec)
    in_specs.append(pl.BlockSpec((8, d), lambda i: (0, 0)))
    if has_loss:
        ins.append(target)
        in_specs.append(row_spec)
    out_shape, out_specs = [], []
    if has_post and not has_loss:
        out_shape.append(jax.ShapeDtypeStruct((t, d), F32))
        out_specs.append(row_spec)
    if has_pre:
        out_shape.append(jax.ShapeDtypeStruct((t, d), BF))
        out_specs.append(row_spec)
    if has_loss:
        out_shape += [jax.ShapeDtypeStruct((t, d), F32), jax.ShapeDtypeStruct((8, d), F32)]
        out_specs += [row_spec, pl.BlockSpec((8, d), lambda i: (0, 0))]

    def body(*refs):
        it = iter(refs)
        x_ref = next(it)
        y_ref = next(it) if has_post else None
        vec_ref = next(it)
        tgt_ref = next(it) if has_loss else None
        xv = x_ref[...]
        if has_post:
            yv = y_ref[...]
            xv = xv + vec_ref[1:2, :] * ((yv * _rstd(yv)) * vec_ref[0:1, :])
            if not has_loss:
                next(it)[...] = xv
        if has_pre:
            hv = ((xv * _rstd(xv)) * vec_ref[2:3, :]) * (1.0 + vec_ref[3:4, :]) + vec_ref[4:5, :]
            next(it)[...] = hv.astype(BF)
        if has_loss:
            e = xv - tgt_ref[...]
            next(it)[...] = e * (1.0 / d)
            acc_ref = next(it)
            i = pl.program_id(0)

            @pl.when(i == 0)
            def _():
                acc_ref[...] = jnp.zeros_like(acc_ref)

            acc_ref[0:1, :] += _colsum(e * e)

    return pl.pallas_call(
        body, name=name, out_shape=out_shape, grid=(t // tm,), in_specs=in_specs, out_specs=out_specs,
        compiler_params=_cparams(1),
    )(*ins)


def _norm_bwd(name, dx_in, dh, x, y, vec, has_pre, has_post):
    t, d = dx_in.shape
    tm = _rows(t, d, 4)
    row_spec = pl.BlockSpec((tm, d), lambda i: (i, 0))
    vec_spec = pl.BlockSpec((8, d), lambda i: (0, 0))
    ins, in_specs = [dx_in], [row_spec]
    if has_pre:
        ins += [dh, x]
        in_specs += [row_spec, row_spec]
    if has_post:
        ins.append(y)
        in_specs.append(row_spec)
    ins.append(vec)
    in_specs.append(vec_spec)
    out_shape, out_specs = [], []
    if has_pre:
        out_shape.append(jax.ShapeDtypeStruct((t, d), F32))
        out_specs.append(row_spec)
    if has_post:
        out_shape.append(jax.ShapeDtypeStruct((t, d), BF))
        out_specs.append(row_spec)
    out_shape.append(jax.ShapeDtypeStruct((8, d), F32))
    out_specs.append(vec_spec)

    def body(*refs):
        it = iter(refs)
        dx = next(it)[...]
        dh_ref = next(it) if has_pre else None
        x_ref = next(it) if has_pre else None
        y_ref = next(it) if has_post else None
        vec_ref = next(it)
        dx_ref = next(it) if has_pre else None
        dy_ref = next(it) if has_post else None
        sums_ref = next(it)
        i = pl.program_id(0)

        @pl.when(i == 0)
        def _():
            sums_ref[...] = jnp.zeros_like(sums_ref)

        if has_pre:
            dhv, xv = dh_ref[...], x_ref[...]
            rs = _rstd(xv)
            xhat = xv * rs
            na = vec_ref[2:3, :]
            sums_ref[0:1, :] += _colsum(dhv)
            sums_ref[1:2, :] += _colsum(dhv * (xhat * na))
            tt = dhv * (1.0 + vec_ref[3:4, :])
            sums_ref[2:3, :] += _colsum(tt * xhat)
            u = tt * na
            dx = dx + rs * (u - xhat * jnp.mean(u * xhat, axis=-1, keepdims=True))
            dx_ref[...] = dx
        if has_post:
            yv = y_ref[...]
            rs = _rstd(yv)
            yhat = yv * rs
            nb = vec_ref[0:1, :]
            sums_ref[3:4, :] += _colsum(dx * (yhat * nb))
            tt = dx * vec_ref[1:2, :]
            sums_ref[4:5, :] += _colsum(tt * yhat)
            u = tt * nb
            dy_ref[...] = (rs * (u - yhat * jnp.mean(u * yhat, axis=-1, keepdims=True))).astype(BF)

    return pl.pallas_call(
        body, name=name, out_shape=out_shape, grid=(t // tm,), in_specs=in_specs, out_specs=out_specs,
        compiler_params=_cparams(1),
    )(*ins)


def _rope128(tv, cs, s1, s2):
    return tv * cs + pltpu.roll(tv, 96, 1) * s1 + pltpu.roll(tv, 32, 1) * s2


def _lat_post(name, lat, gq, gkv, cs, s1, s2, rq, rkv):
    t, w = lat.shape
    tm = _rows(t, w, 4)

    def body(lat_ref, gq_ref, gkv_ref, cs_ref, s1_ref, s2_ref, cq_ref, ckv_ref, kr_ref):
        lq = lat_ref[:, 0:rq]
        lkv = lat_ref[:, rq:rq + rkv]
        cq_ref[...] = ((lq * _rstd(lq)) * gq_ref[...]).astype(BF)
        ckv_ref[...] = ((lkv * _rstd(lkv)) * gkv_ref[...]).astype(BF)
        kr_ref[...] = _rope128(lat_ref[:, rq + rkv:rq + rkv + LANES], cs_ref[...], s1_ref[...], s2_ref[...]).astype(BF)

    def rows(c):
        return pl.BlockSpec((tm, c), lambda i: (i, 0))

    def vecs(c):
        return pl.BlockSpec((1, c), lambda i: (0, 0))

    return pl.pallas_call(
        body, name=name,
        out_shape=[jax.ShapeDtypeStruct((t, rq), BF), jax.ShapeDtypeStruct((t, rkv), BF), jax.ShapeDtypeStruct((t, LANES), BF)],
        grid=(t // tm,),
        in_specs=[rows(w), vecs(rq), vecs(rkv), rows(LANES), rows(LANES), rows(LANES)],
        out_specs=[rows(rq), rows(rkv), rows(LANES)],
        compiler_params=_cparams(1),
    )(lat, gq, gkv, cs, s1, s2)


def _lat_bwd(name, dcq, dckv, dkr, lat, gq, gkv, cs, s1, s2, rq, rkv, heads):
    t, w = lat.shape
    tm = _rows(t, max(w, heads * LANES), 4)
    assert rq == rkv

    def body(dcq_ref, dckv_ref, dkr_ref, lat_ref, gq_ref, gkv_ref, cs_ref, s1_ref, s2_ref, dlat_ref, sums_ref):
        i = pl.program_id(0)

        @pl.when(i == 0)
        def _():
            sums_ref[...] = jnp.zeros_like(sums_ref)

        def rms_bwd(dc, lv, g, row):
            rs = _rstd(lv)
            lhat = lv * rs
            sums_ref[row:row + 1, :] += _colsum(dc * lhat)
            u = dc * g
            return rs * (u - lhat * jnp.mean(u * lhat, axis=-1, keepdims=True))

        dlat_ref[:, 0:rq] = rms_bwd(dcq_ref[...], lat_ref[:, 0:rq], gq_ref[...], 0).astype(BF)
        dlat_ref[:, rq:rq + rkv] = rms_bwd(dckv_ref[...], lat_ref[:, rq:rq + rkv], gkv_ref[...], 1).astype(BF)
        dk = dkr_ref[:, 0:LANES]
        for h in range(1, heads):
            dk = dk + dkr_ref[:, h * LANES:(h + 1) * LANES]
        dlat_ref[:, rq + rkv:rq + rkv + LANES] = _rope128(dk, cs_ref[...], s1_ref[...], s2_ref[...]).astype(BF)

    def rows(c):
        return pl.BlockSpec((tm, c), lambda i: (i, 0))

    def vecs(c):
        return pl.BlockSpec((1, c), lambda i: (0, 0))

    return pl.pallas_call(
        body, name=name,
        out_shape=[jax.ShapeDtypeStruct((t, w), BF), jax.ShapeDtypeStruct((8, rq), F32)],
        grid=(t // tm,),
        in_specs=[rows(rq), rows(rkv), rows(heads * LANES), rows(w), vecs(rq), vecs(rkv), rows(LANES), rows(LANES), rows(LANES)],
        out_specs=[rows(w), pl.BlockSpec((8, rq), lambda i: (0, 0))],
        compiler_params=_cparams(1),
    )(dcq, dckv, dkr, lat, gq, gkv, cs, s1, s2)


def _rope_heads(name, xa, a_blk, xb, b_blk, cs, s1, s2, heads):
    t = xa.shape[0]
    hw = heads * LANES
    tm = _rows(t, 2 * hw, 4)

    def body(a_ref, b_ref, cs_ref, s1_ref, s2_ref, o_ref):
        o_ref[:, 0:hw] = a_ref[...].astype(BF)
        csv, s1v, s2v = cs_ref[...], s1_ref[...], s2_ref[...]
        for h in range(heads):
            sl = slice(h * LANES, (h + 1) * LANES)
            o_ref[:, hw + h * LANES:hw + (h + 1) * LANES] = _rope128(b_ref[:, sl], csv, s1v, s2v).astype(BF)

    def rows(c):
        return pl.BlockSpec((tm, c), lambda i: (i, 0))

    return pl.pallas_call(
        body, name=name,
        out_shape=jax.ShapeDtypeStruct((t, 2 * hw), BF),
        grid=(t // tm,),
        in_specs=[pl.BlockSpec((tm, hw), lambda i: (i, a_blk)), pl.BlockSpec((tm, hw), lambda i: (i, b_blk)),
                  rows(LANES), rows(LANES), rows(LANES)],
        out_specs=rows(2 * hw),
        compiler_params=_cparams(1),
    )(xa, xb, cs, s1, s2)


def _causal_mask(s):
    row = lax.broadcasted_iota(jnp.int32, s.shape, 0)
    col = lax.broadcasted_iota(jnp.int32, s.shape, 1)
    return col <= row


def _flash_fwd(name, qb, kv, kr, heads, scale):
    t = qb.shape[0]
    tq = _pick(t, TQ)
    nq = t // tq

    def body(qn_ref, qr_ref, kv_ref, kr_ref, o_ref, lse_ref):
        i = pl.program_id(1)
        q = jnp.concatenate([qn_ref[...], qr_ref[...]], axis=-1)

        def block(j, carry, masked):
            m_prev, l_prev, acc = carry
            rows = pl.ds(pl.multiple_of(j * tq, tq), tq)
            k = jnp.concatenate([kv_ref[rows, 0:LANES], kr_ref[rows, :]], axis=-1)
            s = lax.dot_general(q, k, NT, preferred_element_type=F32) * scale
            if masked:
                s = jnp.where(_causal_mask(s), s, -1e30)
            m_new = jnp.maximum(m_prev, jnp.max(s, axis=-1, keepdims=True))
            alpha = jnp.exp(m_prev - m_new)
            p = jnp.exp(s - m_new)
            l_new = alpha * l_prev + jnp.sum(p, axis=-1, keepdims=True)
            pv = jnp.dot(p.astype(BF), kv_ref[rows, LANES:2 * LANES], preferred_element_type=F32)
            return m_new, l_new, alpha * acc + pv

        init = (jnp.full((tq, 1), -1e30, F32), jnp.zeros((tq, 1), F32), jnp.zeros((tq, V_DIM), F32))
        carry = lax.fori_loop(0, i, lambda j, cr: block(j, cr, False), init)
        m_fin, l_fin, acc = block(i, carry, True)
        o_ref[...] = (acc / l_fin).astype(BF)
        lse_ref[...] = jnp.broadcast_to(m_fin + jnp.log(l_fin), (tq, LANES))

    return pl.pallas_call(
        body, name=name,
        out_shape=[jax.ShapeDtypeStruct((t, heads * V_DIM), BF), jax.ShapeDtypeStruct((t, heads * LANES), F32)],
        grid=(heads, nq),
        in_specs=[pl.BlockSpec((tq, LANES), lambda h, i: (i, h)),
                  pl.BlockSpec((tq, LANES), lambda h, i: (i, heads + h)),
                  pl.BlockSpec((t, 2 * LANES), lambda h, i: (0, h)),
                  pl.BlockSpec((t, LANES), lambda h, i: (0, 0))],
        out_specs=[pl.BlockSpec((tq, V_DIM), lambda h, i: (i, h)), pl.BlockSpec((tq, LANES), lambda h, i: (i, h))],
        compiler_params=_cparams(2),
    )(qb, qb, kv, kr)


def _flash_bwd_q(name, qb, kv, kr, o, lse, do, heads, scale):
    t = qb.shape[0]
    tq = _pick(t, TQ)
    nq = t // tq

    def body(qn_ref, qr_ref, kv_ref, kr_ref, o_ref, lse_ref, do_ref, dqn_ref, dqr_ref):
        i = pl.program_id(1)
        q = jnp.concatenate([qn_ref[...], qr_ref[...]], axis=-1)
        dov = do_ref[...]
        delta = jnp.sum(dov.astype(F32) * o_ref[...].astype(F32), axis=-1, keepdims=True)
        lse_col = lse_ref[:, 0:1]

        def block(j, dq, masked):
            rows = pl.ds(pl.multiple_of(j * tq, tq), tq)
            k = jnp.concatenate([kv_ref[rows, 0:LANES], kr_ref[rows, :]], axis=-1)
            s = lax.dot_general(q, k, NT, preferred_element_type=F32) * scale
            p = jnp.exp(s - lse_col)
            if masked:
                p = jnp.where(_causal_mask(s), p, 0.0)
            dp = lax.dot_general(dov, kv_ref[rows, LANES:2 * LANES], NT, preferred_element_type=F32)
            ds = (p * (dp - delta) * scale).astype(BF)
            return dq + jnp.dot(ds, k, preferred_element_type=F32)

        dq = lax.fori_loop(0, i, lambda j, acc: block(j, acc, False), jnp.zeros((tq, 2 * LANES), F32))
        dq = block(i, dq, True)
        dqn_ref[...] = dq[:, 0:LANES]
        dqr_ref[...] = dq[:, LANES:2 * LANES]

    head_blk = pl.BlockSpec((tq, LANES), lambda h, i: (i, h))
    return pl.pallas_call(
        body, name=name,
        out_shape=[jax.ShapeDtypeStruct((t, heads * LANES), F32)] * 2,
        grid=(heads, nq),
        in_specs=[head_blk, pl.BlockSpec((tq, LANES), lambda h, i: (i, heads + h)),
                  pl.BlockSpec((t, 2 * LANES), lambda h, i: (0, h)), pl.BlockSpec((t, LANES), lambda h, i: (0, 0)),
                  head_blk, head_blk, head_blk],
        out_specs=[head_blk, head_blk],
        compiler_params=_cparams(2),
    )(qb, qb, kv, kr, o, lse, do)


def _flash_bwd_kv(name, qb, kv, kr, o, lse, do, heads, scale):
    t = qb.shape[0]
    tq = _pick(t, TQ)
    nq = t // tq

    def body(qn_ref, qr_ref, kv_ref, kr_ref, o_ref, lse_ref, do_ref, dkv_ref, dkr_ref):
        j = pl.program_id(1)
        k = jnp.concatenate([kv_ref[:, 0:LANES], kr_ref[...]], axis=-1)
        v = kv_ref[:, LANES:2 * LANES]

        def block(i, carry, masked):
            dk, dv = carry
            rows = pl.ds(pl.multiple_of(i * tq, tq), tq)
            q = jnp.concatenate([qn_ref[rows, :], qr_ref[rows, :]], axis=-1)
            dov = do_ref[rows, :]
            delta = jnp.sum(dov.astype(F32) * o_ref[rows, :].astype(F32), axis=-1, keepdims=True)
            s = lax.dot_general(q, k, NT, preferred_element_type=F32) * scale
            p = jnp.exp(s - lse_ref[rows, 0:1])
            if masked:
                p = jnp.where(_causal_mask(s), p, 0.0)
            dv = dv + lax.dot_general(p.astype(BF), dov, TNDIMS, preferred_element_type=F32)
            dp = lax.dot_general(dov, v, NT, preferred_element_type=F32)
            ds = (p * (dp - delta) * scale).astype(BF)
            dk = dk + lax.dot_general(ds, q, TNDIMS, preferred_element_type=F32)
            return dk, dv

        carry = block(j, (jnp.zeros((tq, 2 * LANES), F32), jnp.zeros((tq, V_DIM), F32)), True)
        dk, dv = lax.fori_loop(j + 1, nq, lambda i, cr: block(i, cr, False), carry)
        dkv_ref[:, 0:LANES] = dk[:, 0:LANES].astype(BF)
        dkv_ref[:, LANES:2 * LANES] = dv.astype(BF)
        dkr_ref[...] = dk[:, LANES:2 * LANES]

    def full(hoff):
        return pl.BlockSpec((t, LANES), lambda h, j: (0, hoff + h))

    return pl.pallas_call(
        body, name=name,
        out_shape=[jax.ShapeDtypeStruct((t, heads * 2 * LANES), BF), jax.ShapeDtypeStruct((t, heads * LANES), F32)],
        grid=(heads, nq),
        in_specs=[full(0), full(heads), pl.BlockSpec((tq, 2 * LANES), lambda h, j: (j, h)),
                  pl.BlockSpec((tq, LANES), lambda h, j: (j, 0)), full(0), full(0), full(0)],
        out_specs=[pl.BlockSpec((tq, 2 * LANES), lambda h, j: (j, h)), pl.BlockSpec((tq, LANES), lambda h, j: (j, h))],
        compiler_params=_cparams(2),
    )(qb, qb, kv, kr, o, lse, do)


def _shift_down(z, n):
    row = lax.broadcasted_iota(jnp.int32, z.shape, 0)
    return jnp.where(row >= n, pltpu.roll(z, n, 0), 0.0)


def _shift_up(z, n):
    t = z.shape[0]
    row = lax.broadcasted_iota(jnp.int32, z.shape, 0)
    return jnp.where(row < t - n, pltpu.roll(z, t - n, 0), 0.0)


def _conv_fwd(name, proj, cw):
    t, d3 = proj.shape
    d = d3 // 3
    tn = _pick(d, LANES)
    nb = d // tn

    def body(b_ref, c_ref, u_ref, w_ref, o_ref):
        z = c_ref[...] * u_ref[...]
        zc = w_ref[0:1, :] * _shift_down(z, 2) + w_ref[1:2, :] * _shift_down(z, 1) + w_ref[2:3, :] * z
        o_ref[...] = (b_ref[...] * zc).astype(BF)

    def part(p):
        return pl.BlockSpec((t, tn), lambda j: (0, p * nb + j))

    return pl.pallas_call(
        body, name=name, out_shape=jax.ShapeDtypeStruct((t, d), BF), grid=(nb,),
        in_specs=[part(0), part(1), part(2), pl.BlockSpec((8, tn), lambda j: (0, j))],
        out_specs=pl.BlockSpec((t, tn), lambda j: (0, j)),
        compiler_params=_cparams(1),
    )(proj, proj, proj, cw)


def _conv_bwd(name, dbz, proj, cw):
    t, d3 = proj.shape
    d = d3 // 3
    tn = _pick(d, LANES)
    nb = d // tn

    def body(g_ref, b_ref, c_ref, u_ref, w_ref, o_ref, sums_ref):
        p = pl.program_id(1)
        w0, w1, w2 = w_ref[0:1, :], w_ref[1:2, :], w_ref[2:3, :]

        @pl.when(p == 0)
        def _():
            z = c_ref[...] * u_ref[...]
            z1, z2 = _shift_down(z, 1), _shift_down(z, 2)
            gv = g_ref[...]
            o_ref[...] = (gv * (w0 * z2 + w1 * z1 + w2 * z)).astype(BF)
            dzc = gv * b_ref[...]
            sums_ref[...] = jnp.zeros_like(sums_ref)
            sums_ref[0:1, :] = _colsum(dzc * z2)
            sums_ref[1:2, :] = _colsum(dzc * z1)
            sums_ref[2:3, :] = _colsum(dzc * z)

        @pl.when(p > 0)
        def _():
            dzc = g_ref[...] * b_ref[...]
            dz = w2 * dzc + w1 * _shift_up(dzc, 1) + w0 * _shift_up(dzc, 2)
            other = jnp.where(p == 1, u_ref[...], c_ref[...])
            o_ref[...] = (dz * other).astype(BF)

    def part(q):
        return pl.BlockSpec((t, tn), lambda j, p: (0, q * nb + j))

    return pl.pallas_call(
        body, name=name,
        out_shape=[jax.ShapeDtypeStruct((t, d3), BF), jax.ShapeDtypeStruct((8, d), F32)],
        grid=(nb, 3),
        in_specs=[pl.BlockSpec((t, tn), lambda j, p: (0, j)), part(0), part(1), part(2),
                  pl.BlockSpec((8, tn), lambda j, p: (0, j))],
        out_specs=[pl.BlockSpec((t, tn), lambda j, p: (0, p * nb + j)), pl.BlockSpec((8, tn), lambda j, p: (0, j))],
        compiler_params=_cparams(2),
    )(dbz, proj, proj, proj, cw)


def _pad_rows8(v):
    return jnp.pad(v, ((0, 8 - v.shape[0]), (0, 0)))


def kernel(x, c, positions, w_mod, b_mod, norm_g, mla_w_in, mla_g_q, mla_g_kv, mla_w_uq, mla_w_ukv, mla_w_o, conv_w_in, conv_w, conv_w_out, mlp_w_up, mlp_w_down, loss_target, m_w_mod, m_b_mod, m_norm_g, m_mla_w_in, m_mla_g_q, m_mla_g_kv, m_mla_w_uq, m_mla_w_ukv, m_mla_w_o, m_conv_w_in, m_conv_w, m_conv_w_out, m_mlp_w_up, m_mlp_w_down, v_w_mod, v_b_mod, v_norm_g, v_mla_w_in, v_mla_g_q, v_mla_g_kv, v_mla_w_uq, v_mla_w_ukv, v_mla_w_o, v_conv_w_in, v_conv_w, v_conv_w_out, v_mlp_w_up, v_mlp_w_down):
    t, d = x.shape[1], x.shape[2]
    depth = w_mod.shape[0]
    n_mod_loc = w_mod.shape[2]
    d_loc = norm_g.shape[2]
    rq = mla_g_q.shape[1]
    rkv = mla_g_kv.shape[1]
    heads = mla_w_uq.shape[2]
    lat_w = mla_w_in.shape[2]
    lat_pad = rq + rkv + LANES
    scale = (NOPE_DIM + ROPE_DIM) ** -0.5
    xi, yi, ci = _pos()
    dev = 4 * xi + 2 * yi + ci
    c_idx = jnp.reshape(ci, (1,)).astype(jnp.int32)
    chip_idx = jnp.reshape(2 * xi + yi, (1,)).astype(jnp.int32)
    x0 = x[0]
    target = loss_target[0]

    w_in_loc = jnp.pad(mla_w_in[0], ((0, 0), (0, lat_pad - lat_w))).astype(BF)
    uq = mla_w_uq[0]
    uq_loc = jnp.concatenate(
        [uq[:, :, :NOPE_DIM].reshape(uq.shape[0], heads * NOPE_DIM),
         jnp.pad(uq[:, :, NOPE_DIM:], ((0, 0), (0, 0), (0, LANES - ROPE_DIM))).reshape(uq.shape[0], heads * LANES)],
        axis=1).astype(BF)
    ukv_loc = mla_w_ukv[0].reshape(mla_w_ukv.shape[1], heads * (NOPE_DIM + V_DIM)).astype(BF)
    def merged(g):
        return g.reshape(g.shape[0] * g.shape[1], g.shape[2])

    g_win, g_uq, g_ukv, g_wo = [merged(g) for g in _ag_big("ag_mla", [w_in_loc, uq_loc, ukv_loc, mla_w_o[0].astype(BF)])]

    n_ng = depth * 4 * d_loc
    small = jnp.concatenate([c.reshape(-1), norm_g.reshape(-1), conv_w.reshape(-1)])
    small_n = small.shape[0]
    small_rows = -(-small_n // (8 * LANES)) * 8
    small = jnp.pad(small, (0, small_rows * LANES - small_n)).reshape(small_rows, LANES)
    small_all = _ag_small("ag_small", small).reshape(N_DEV, small_rows * LANES)
    c_all = small_all[:, :d]
    ng_all = small_all[:, d:d + n_ng].reshape(N_DEV, depth, 4, d_loc).transpose(1, 2, 0, 3).reshape(depth, 4, d)
    cw_all = small_all[:, d + n_ng:d + n_ng + 3 * d_loc].reshape(N_DEV, 3, d_loc).transpose(1, 0, 2).reshape(3, d)
    cw8 = _pad_rows8(cw_all)
    cond_all = _silu("silu", c_all)

    mod_parts = jnp.stack([_mod_matvec(f"mod_matvec{i}", cond_all, w_mod[i]) for i in range(depth)])
    mod_rows = depth * N_DEV * n_mod_loc // LANES
    mod_all = _ag_small("ag_mod", mod_parts.reshape(mod_rows, LANES)).reshape(N_DEV, depth, N_DEV, n_mod_loc)
    mod_mine = lax.dynamic_index_in_dim(mod_all, dev, axis=2, keepdims=False)
    mod = mod_mine.transpose(1, 0, 2).reshape(depth, N_DEV * n_mod_loc) + b_mod
    mod = mod.reshape(depth, 6, d)

    dev_idx = jnp.reshape(dev, (1,)).astype(jnp.int32)
    later = [mlp_w_up[0], mlp_w_down[0], conv_w_in[0], conv_w_out[0], mlp_w_up[1], mlp_w_down[1]]
    lands = [_cast_own_block(f"cast_own_block{a}", w, dev_idx) for a, w in enumerate(later)]
    ag_groups = [[0, 1], [2, 3, 4, 5]]
    ag_sems, lands_thru, ag_token = _ag_start("ag_start", lands, ag_groups, [mod, g_wo])

    def fwd_vec(nb, gate, na, sc, sh, token=None):
        zero = jnp.zeros((d,), F32)
        rows = [zero if r is None else r for r in (nb, gate, na, sc, sh)]
        vec = _pad_rows8(jnp.stack(rows))
        return vec if token is None else vec + token[0, 0]

    inv_freq = ROPE_THETA ** (-jnp.arange(0, ROPE_DIM, 2, dtype=F32) / ROPE_DIM)
    ang = positions[0].astype(F32)[:, None] * inv_freq
    cos, sin = jnp.cos(ang), jnp.sin(ang)
    zh = jnp.zeros_like(cos)
    zpad = jnp.zeros((t, LANES - ROPE_DIM), F32)
    rope_c = jnp.concatenate([cos, cos, zpad], axis=1)
    rope_s1 = jnp.concatenate([-sin, zh, zpad], axis=1)
    rope_s2 = jnp.concatenate([zh, sin, zpad], axis=1)

    (h0,) = _norm_fwd("pre0", x0, None, fwd_vec(None, None, ng_all[0, 0], mod[0, 1], mod[0, 0], ag_token), None, False, True, False)
    (lat,) = _linear("mla_lat", h0, g_win)
    cq, ckv, kr = _lat_post("mla_lat_post", lat, mla_g_q, mla_g_kv, rope_c, rope_s1, rope_s2, rq, rkv)
    (q_raw,) = _linear("mla_q", cq, g_uq)
    qb = _rope_heads("mla_q_rope", q_raw, 0, q_raw, 1, rope_c, rope_s1, rope_s2, heads)
    (kvb,) = _linear("mla_kv", ckv, g_ukv, out_dtypes=(BF,))
    o, lse = _flash_fwd("mla_attn", qb, kvb, kr, heads, scale)
    fs_a, fr_a, lands_a, _ = _ag_forward("ag_forward_a", lands_thru[0:2], *ag_sems[0], o)
    (y0,) = _linear("mla_out", o, g_wo)
    x1, h1 = _norm_fwd("postpre1", x0, y0, fwd_vec(ng_all[0, 1], mod[0, 2], ng_all[0, 2], mod[0, 4], mod[0, 3]), None, True, True, False)
    g_up0, g_dn0 = _ag_finish("ag_finish_a", lands_a, fs_a, fr_a, x1)
    g_dn0 = merged(g_dn0)

    def sq_relu(acc):
        a = jnp.maximum(acc, 0.0)
        return acc, a * a

    u1, act1 = _linear("mlp0_up", h1, g_up0, out_dtypes=(F32, BF), epilogue=sq_relu)
    (y1,) = _linear("mlp0_down", act1, g_dn0)
    fs_b, fr_b, lands_b, _ = _ag_forward("ag_forward_b", lands_thru[2:6], *ag_sems[1], y1)
    x2, h2 = _norm_fwd("postpre2", x1, y1, fwd_vec(ng_all[0, 3], mod[0, 5], ng_all[1, 0], mod[1, 1], mod[1, 0]), None, True, True, False)
    g_cin, g_cout, g_up1, g_dn1 = _ag_finish("ag_finish_b", lands_b, fs_b, fr_b, x2)
    g_cout, g_dn1 = merged(g_cout), merged(g_dn1)
    g_up, g_dn = [g_up0, g_up1], [g_dn0, g_dn1]

    (proj,) = _linear("conv_in", h2, g_cin)
    bz = _conv_fwd("conv_mix", proj, cw8)
    (y2,) = _linear("conv_out", bz, g_cout)
    x3, h3 = _norm_fwd("postpre3", x2, y2, fwd_vec(ng_all[1, 1], mod[1, 2], ng_all[1, 2], mod[1, 4], mod[1, 3]), None, True, True, False)

    u3, act3 = _linear("mlp1_up", h3, g_up[1], out_dtypes=(F32, BF), epilogue=sq_relu)
    (y3,) = _linear("mlp1_down", act3, g_dn[1])
    dx4, loss_cols = _norm_fwd("post_loss", x3, y3, fwd_vec(ng_all[1, 3], mod[1, 5], None, None, None), target, True, False, True)
    loss = lax.psum(0.5 * jnp.sum(loss_cols[0]) / d, ("x", "y", "c"))

    def bwd_vec(nb, gate, na, sc, token=None):
        zero = jnp.zeros((d,), F32)
        rows = [zero if r is None else r for r in (nb, gate, na, sc)]
        vec = _pad_rows8(jnp.stack(rows))
        return vec if token is None else vec + token[0, 0]

    def blocks(g, like):
        return g.reshape((N_DEV,) + like.shape)

    def rs_begin(tag, grads):
        from_sibling = _rs_sibling(f"rs_sibling_{tag}", grads)
        pair = [_pair_add(f"pair_add_{tag}{a}", grads[a], from_sibling[a], c_idx) for a in range(len(grads))]
        return _rs_chips_start(f"rs_start_{tag}", pair)

    def mlp_bwd(tag, dy, h, u, act, w_up, w_dn):
        def relu_grad(acc, uv):
            return (acc * (2.0 * jnp.maximum(uv, 0.0)),)

        (du,) = _linear_t(f"{tag}_dact", dy, w_dn, out_dtypes=(BF,), epilogue=relu_grad, extras=(u,))
        dw_dn = _wgrad(f"{tag}_dw_down", act, dy, w_dn)
        dw_up = _wgrad(f"{tag}_dw_up", h, du, w_up)
        (dh,) = _linear_t(f"{tag}_dh", du, w_up)
        return dh, dw_up, dw_dn

    dy3, sums4 = _norm_bwd("bwd_post3", dx4, None, None, y3, bwd_vec(ng_all[1, 3], mod[1, 5], None, None), False, True)
    dh3, dw_up1, dw_dn1 = mlp_bwd("mlp1", dy3, h3, u3, act3, g_up[1], g_dn[1])
    rs_mlp1 = rs_begin("mlp1", [dw_up1, blocks(dw_dn1, mlp_w_down[1])])
    dx3, dy2, sums3 = _norm_bwd("bwd_norm3", dx4, dh3, x3, y2, bwd_vec(ng_all[1, 1], mod[1, 2], ng_all[1, 2], mod[1, 4], rs_mlp1[4]), True, True)

    (dbz,) = _linear_t("conv_dbz", dy2, g_cout)
    dw_cout = _wgrad("conv_dw_out", bz, dy2, g_cout)
    dproj, conv_sums = _conv_bwd("conv_mix_bwd", dbz, proj, cw8)
    dw_cin = _wgrad("conv_dw_in", h2, dproj, g_cin)
    rs_conv = rs_begin("conv", [dw_cin, blocks(dw_cout, conv_w_out[0])])
    (dh2,) = _linear_t("conv_dh", dproj, g_cin)
    dx2, dy1, sums2 = _norm_bwd("bwd_norm2", dx3, dh2, x2, y1, bwd_vec(ng_all[0, 3], mod[0, 5], ng_all[1, 0], mod[1, 1], rs_conv[4]), True, True)

    dh1, dw_up0, dw_dn0 = mlp_bwd("mlp0", dy1, h1, u1, act1, g_up[0], g_dn[0])
    rs_mlp0 = rs_begin("mlp0", [dw_up0, blocks(dw_dn0, mlp_w_down[0])])
    dx1, dy0, sums1 = _norm_bwd("bwd_norm1", dx2, dh1, x1, y0, bwd_vec(ng_all[0, 1], mod[0, 2], ng_all[0, 2], mod[0, 4], rs_mlp0[4]), True, True)

    (do,) = _linear_t("mla_do", dy0, g_wo, out_dtypes=(BF,))
    dw_o = _wgrad("mla_dw_o", o, dy0, g_wo)
    dkv, dkr = _flash_bwd_kv("mla_attn_bwd_kv", qb, kvb, kr, o, lse, do, heads, scale)
    dqn, dqr = _flash_bwd_q("mla_attn_bwd_q", qb, kvb, kr, o, lse, do, heads, scale)
    dq = _rope_heads("mla_dq_rope", dqn, 0, dqr, 0, rope_c, -rope_s1, -rope_s2, heads)
    dw_uq = _wgrad("mla_dw_uq", cq, dq, g_uq)
    (dcq,) = _linear_t("mla_dcq", dq, g_uq)
    dw_ukv = _wgrad("mla_dw_ukv", ckv, dkv, g_ukv)
    (dckv,) = _linear_t("mla_dckv", dkv, g_ukv)
    dlat, lat_sums = _lat_bwd("mla_lat_bwd", dcq, dckv, dkr, lat, mla_g_q, mla_g_kv, rope_c, -rope_s1, -rope_s2, rq, rkv, heads)
    dw_win = _wgrad("mla_dw_in", h0, dlat, g_win)
    rs_mla = rs_begin("mla", [blocks(dw_win, w_in_loc), blocks(dw_uq, uq_loc), blocks(dw_ukv, ukv_loc), blocks(dw_o, mla_w_o[0])])
    (dh0,) = _linear_t("mla_dh", dlat, g_win)
    grad_x, sums0 = _norm_bwd("bwd_pre0", dx1, dh0, x0, None, bwd_vec(None, None, ng_all[0, 0], mod[0, 1], rs_mla[4]), True, False)

    def sums_rows(s, rows):
        return [s[r] for r in rows]

    dmod = jnp.stack([
        jnp.stack(sums_rows(sums0, (0, 1)) + sums_rows(sums1, (3,)) + sums_rows(sums1, (0, 1)) + sums_rows(sums2, (3,))),
        jnp.stack(sums_rows(sums2, (0, 1)) + sums_rows(sums3, (3,)) + sums_rows(sums3, (0, 1)) + sums_rows(sums4, (3,))),
    ])
    dng = jnp.stack([
        jnp.stack([sums0[2], sums1[4], sums1[2], sums2[4]]),
        jnp.stack([sums2[2], sums3[4], sums3[2], sums4[4]]),
    ])
    parts = [dmod.reshape(-1), dng.reshape(-1), lat_sums[0], lat_sums[1], conv_sums[0:3].reshape(-1)]
    sizes = [p.shape[0] for p in parts]
    packed = jnp.concatenate(parts)
    pk_rows = -(-packed.shape[0] // (8 * LANES)) * 8
    packed = jnp.pad(packed, (0, pk_rows * LANES - packed.shape[0])).reshape(pk_rows, LANES)
    packed_all = _ag_small("ag_small_grads", packed).reshape(N_DEV, pk_rows, LANES)
    total = _sum_devices("sum_small_grads", packed_all).reshape(-1)
    offs = [0]
    for s in sizes:
        offs.append(offs[-1] + s)
    g_b_mod = total[offs[0]:offs[1]].reshape(depth, 6 * d)
    ng_full = total[offs[1]:offs[2]].reshape(depth, 4, d)
    g_norm_g = lax.dynamic_slice_in_dim(ng_full, dev * d_loc, d_loc, axis=2)
    g_g_q = total[offs[2]:offs[3]].reshape(1, rq)
    g_g_kv = total[offs[3]:offs[4]].reshape(1, rkv)
    cw_full = total[offs[4]:offs[5]].reshape(1, 3, d)
    g_conv_w = lax.dynamic_slice_in_dim(cw_full, dev * d_loc, d_loc, axis=2)

    dmod_all = packed_all.reshape(N_DEV, -1)[:, :sizes[0]].reshape(N_DEV, depth, 6 * d)
    dmod_cols = lax.dynamic_slice_in_dim(dmod_all, dev * n_mod_loc, n_mod_loc, axis=2)
    cond_t = cond_all.T
    mod_out = [_mod_grad_adam(f"w_mod_adam{i}", cond_t, dmod_cols[:, i, :], w_mod[i], m_w_mod[i], v_w_mod[i]) for i in range(depth)]
    g_w_mod, d_w_mod, nm_w_mod, nv_w_mod = [jnp.stack([mod_out[i][k] for i in range(depth)]) for k in range(4)]

    def rs_end(tag, started):
        send_sems, recv_sems, pair, lands, _ = started
        pair, lands = _rs_chips_wait(f"rs_wait_{tag}", pair, lands, send_sems, recv_sems, g_w_mod)
        return [_chip_sum(f"chip_sum_{tag}{a}", pair[a], lands[a], chip_idx) for a in range(len(pair))]

    red_up1, red_dn1 = rs_end("mlp1", rs_mlp1)
    red_cin, red_cout = rs_end("conv", rs_conv)
    red_up0, red_dn0 = rs_end("mlp0", rs_mlp0)
    red_win, red_uq, red_ukv, red_wo = rs_end("mla", rs_mla)

    g_mla_w_in = red_win[:, :lat_w][None]
    g_mla_w_uq = jnp.concatenate(
        [red_uq[:, :heads * NOPE_DIM].reshape(-1, heads, NOPE_DIM),
         red_uq[:, heads * NOPE_DIM:].reshape(-1, heads, LANES)[:, :, :ROPE_DIM]], axis=2)[None]
    g_mla_w_ukv = red_ukv.reshape(mla_w_ukv.shape)
    g_mla_w_o = red_wo[None]
    g_conv_w_in = red_cin[None]
    g_conv_w_out = red_cout[None]
    g_mlp_w_up = jnp.stack([red_up0, red_up1])
    g_mlp_w_down = jnp.stack([red_dn0, red_dn1])

    grads = [g_w_mod, g_b_mod, g_norm_g, g_mla_w_in, g_g_q, g_g_kv, g_mla_w_uq, g_mla_w_ukv, g_mla_w_o,
             g_conv_w_in, g_conv_w, g_conv_w_out, g_mlp_w_up, g_mlp_w_down]
    weights = [w_mod, b_mod, norm_g, mla_w_in, mla_g_q, mla_g_kv, mla_w_uq, mla_w_ukv, mla_w_o, conv_w_in, conv_w, conv_w_out, mlp_w_up, mlp_w_down]
    ms = [m_w_mod, m_b_mod, m_norm_g, m_mla_w_in, m_mla_g_q, m_mla_g_kv, m_mla_w_uq, m_mla_w_ukv, m_mla_w_o, m_conv_w_in, m_conv_w, m_conv_w_out, m_mlp_w_up, m_mlp_w_down]
    vs = [v_w_mod, v_b_mod, v_norm_g, v_mla_w_in, v_mla_g_q, v_mla_g_kv, v_mla_w_uq, v_mla_w_ukv, v_mla_w_o, v_conv_w_in, v_conv_w, v_conv_w_out, v_mlp_w_up, v_mlp_w_down]
    names = ["w_mod", "b_mod", "norm_g", "mla_w_in", "mla_g_q", "mla_g_kv", "mla_w_uq", "mla_w_ukv", "mla_w_o", "conv_w_in", "conv_w", "conv_w_out", "mlp_w_up", "mlp_w_down"]
    deltas, new_ms, new_vs = [d_w_mod], [nm_w_mod], [nv_w_mod]
    for k in range(1, len(weights)):
        dl, nm, nv = _adam(f"adam_{names[k]}", weights[k], grads[k], ms[k], vs[k])
        deltas.append(dl)
        new_ms.append(nm)
        new_vs.append(nv)

    return (loss, grad_x[None], *grads, *deltas, *new_ms, *new_vs)
```

```python
import jax
import jax.numpy as jnp
from jax import lax
from jax.experimental import pallas as pl
from jax.experimental.pallas import tpu as pltpu

F32 = jnp.float32
BF = jnp.bfloat16
MESH = pl.DeviceIdType.MESH
N_DEV = 8
LANES = 128
NORM_EPS = 1e-6
ROPE_THETA = 10000.0
ROPE_DIM = 64
NOPE_DIM = 128
V_DIM = 128
LR, B1, B2, ADAM_EPS, WD, STEP = 0.001, 0.9, 0.999, 1e-08, 0.01, 10
VMEM_LIMIT = 56 * 1024 * 1024
TILE_BYTES = 2 * 1024 * 1024
TM, TN, TK = 1024, 512, 2048
TQ = 512
Q_CHAINS = 1
LOG2E = 1.4426950408889634

NN = (((1,), (0,)), ((), ()))
NT = (((1,), (1,)), ((), ()))
TNDIMS = (((0,), (0,)), ((), ()))
HBM_SPEC = pl.BlockSpec(memory_space=pltpu.HBM)
VMEM_SPEC = pl.BlockSpec(memory_space=pltpu.VMEM)


def _cparams(n_axes):
    return pltpu.CompilerParams(dimension_semantics=("arbitrary",) * n_axes, vmem_limit_bytes=VMEM_LIMIT)


def _pick(dim, pref):
    if dim <= pref:
        return dim
    for t in range(pref - pref % LANES, 0, -LANES):
        if t > 0 and dim % t == 0:
            return t
    for t in range(pref, 0, -1):
        if dim % t == 0:
            return t
    return dim


def _rows(r, c, itemsize=4):
    want = max(8, TILE_BYTES // (itemsize * max(c, 1)))
    if r <= want:
        return r
    for t in range(want - want % 8, 0, -8):
        if t > 0 and r % t == 0:
            return t
    return r


def _pos():
    return lax.axis_index("x"), lax.axis_index("y"), lax.axis_index("c")


def _ag_small(name, v):
    m_per, n = v.shape

    def body(x_ref, out_ref, send_sems, recv_sems, local_sem):
        x, y, c = _pos()
        me, sibling = (x, y, c), (x, y, 1 - c)
        chips = [(1 - x, y), (x, 1 - y), (1 - x, 1 - y)]

        def rows(px, py, pc):
            return out_ref.at[pl.ds((4 * px + 2 * py + pc) * m_per, m_per), :]

        def copy(k, block, to, src=None):
            return pltpu.make_async_remote_copy(
                src_ref=rows(*block) if src is None else src, dst_ref=rows(*block),
                send_sem=send_sems.at[k], recv_sem=recv_sems.at[k], device_id=to, device_id_type=MESH)

        mine = pltpu.make_async_copy(x_ref, rows(*me), local_sem)
        mine.start()
        first = [copy(0, me, sibling, src=x_ref)]
        first += [copy(1 + j, me, (*chip, c), src=x_ref) for j, chip in enumerate(chips)]
        for cp in first:
            cp.start()
        passed = [copy(4 + j, (*chip, c), sibling) for j, chip in enumerate(chips)]
        for j, chip in enumerate(chips):
            copy(1 + j, (*chip, c), me).wait_recv()
            passed[j].start()
        copy(0, sibling, me).wait_recv()
        for j, chip in enumerate(chips):
            copy(4 + j, (*chip, 1 - c), me).wait_recv()
        for cp in first + passed:
            cp.wait_send()
        mine.wait()

    return pl.pallas_call(
        body, name=name,
        out_shape=jax.ShapeDtypeStruct((N_DEV * m_per, n), v.dtype),
        in_specs=[VMEM_SPEC], out_specs=VMEM_SPEC,
        scratch_shapes=[pltpu.SemaphoreType.DMA((7,)), pltpu.SemaphoreType.DMA((7,)), pltpu.SemaphoreType.DMA],
    )(v)


def _ag_big(name, arrs):
    n = len(arrs)

    def body(*refs):
        ins, outs = refs[:n], refs[n:2 * n]
        send_sems, recv_sems, local_sems = refs[2 * n:]
        x, y, c = _pos()
        me, sibling = (x, y, c), (x, y, 1 - c)
        chips = [(1 - x, y), (x, 1 - y), (1 - x, 1 - y)]

        def copy(a, k, block, to, src=None):
            dst = outs[a].at[4 * block[0] + 2 * block[1] + block[2]]
            return pltpu.make_async_remote_copy(
                src_ref=dst if src is None else src, dst_ref=dst,
                send_sem=send_sems.at[7 * a + k], recv_sem=recv_sems.at[7 * a + k],
                device_id=to, device_id_type=MESH)

        mine = [pltpu.make_async_copy(ins[a], outs[a].at[4 * x + 2 * y + c], local_sems.at[a]) for a in range(n)]
        for cp in mine:
            cp.start()
        first = []
        for a in range(n):
            first.append(copy(a, 0, me, sibling, src=ins[a]))
            first += [copy(a, 1 + j, me, (*chip, c), src=ins[a]) for j, chip in enumerate(chips)]
        for cp in first:
            cp.start()
        passed = []
        for j, chip in enumerate(chips):
            for a in range(n):
                copy(a, 1 + j, (*chip, c), me).wait_recv()
                cp = copy(a, 4 + j, (*chip, c), sibling)
                cp.start()
                passed.append(cp)
        for a in range(n):
            copy(a, 0, sibling, me).wait_recv()
            for j, chip in enumerate(chips):
                copy(a, 4 + j, (*chip, 1 - c), me).wait_recv()
        for cp in first + passed:
            cp.wait_send()
        for cp in mine:
            cp.wait()

    return pl.pallas_call(
        body, name=name,
        out_shape=[jax.ShapeDtypeStruct((N_DEV,) + a.shape, a.dtype) for a in arrs],
        in_specs=[HBM_SPEC] * n, out_specs=[HBM_SPEC] * n,
        scratch_shapes=[pltpu.SemaphoreType.DMA((7 * n,)), pltpu.SemaphoreType.DMA((7 * n,)),
                        pltpu.SemaphoreType.DMA((n,))],
    )(*arrs)


def _rs_sibling(name, arrs):
    n = len(arrs)

    def body(*refs):
        ins, outs = refs[:n], refs[n:2 * n]
        send_sems, recv_sems = refs[2 * n:]
        x, y, c = _pos()
        copies = []
        for a in range(n):
            for j in range(4):
                copies.append(pltpu.make_async_remote_copy(
                    src_ref=ins[a].at[2 * j + (1 - c)], dst_ref=outs[a].at[j],
                    send_sem=send_sems.at[4 * a + j], recv_sem=recv_sems.at[4 * a + j],
                    device_id=(x, y, 1 - c), device_id_type=MESH))
        for cp in copies:
            cp.start()
        for cp in copies:
            cp.wait()

    return pl.pallas_call(
        body, name=name,
        out_shape=[jax.ShapeDtypeStruct((4,) + a.shape[1:], a.dtype) for a in arrs],
        in_specs=[HBM_SPEC] * n, out_specs=[HBM_SPEC] * n,
        scratch_shapes=[pltpu.SemaphoreType.DMA((4 * n,)), pltpu.SemaphoreType.DMA((4 * n,))],
    )(*arrs)


SEM_SPEC = pl.BlockSpec(memory_space=pltpu.SEMAPHORE)
ANY_SPEC = pl.BlockSpec(memory_space=pl.ANY)
EFFECT = pltpu.SideEffectType.DATAFLOW_SIDE_EFFECTING


def _hbm(a):
    return pltpu.with_memory_space_constraint(a, pltpu.HBM)


def _cast_own_block(name, w, dev_idx):
    r, cdim = w.shape
    tr = _rows(r, cdim, 4)

    def body(d_ref, w_ref, o_ref):
        o_ref[...] = w_ref[...].astype(BF)

    return pl.pallas_call(
        body, name=name,
        out_shape=jax.ShapeDtypeStruct((N_DEV, r, cdim), BF),
        grid_spec=pltpu.PrefetchScalarGridSpec(
            num_scalar_prefetch=1, grid=(r // tr,),
            in_specs=[pl.BlockSpec((tr, cdim), lambda i, d_ref: (i, 0))],
            out_specs=pl.BlockSpec((None, tr, cdim), lambda i, d_ref: (d_ref[0], i, 0))),
        compiler_params=_cparams(1),
    )(dev_idx, w)


def _ag_start(name, lands, groups, after):
    n = len(lands)
    ng = len(groups)
    n_after = len(after)

    def body(*refs):
        lnd = refs[:n]
        sems = refs[n + n_after:n + n_after + 2 * ng]
        token = refs[-1]
        x, y, c = _pos()
        targets = [(x, y, 1 - c), (1 - x, y, c), (x, 1 - y, c), (1 - x, 1 - y, c)]
        for gi, members in enumerate(groups):
            for pos, a in enumerate(members):
                own = lnd[a].at[4 * x + 2 * y + c]
                for k, to in enumerate(targets):
                    pltpu.make_async_remote_copy(
                        src_ref=own, dst_ref=own,
                        send_sem=sems[2 * gi].at[4 * pos + k], recv_sem=sems[2 * gi + 1].at[4 * pos + k],
                        device_id=to, device_id_type=MESH).start()
        token[...] = jnp.zeros_like(token)

    sem_shapes = []
    for members in groups:
        sem_shapes += [pltpu.SemaphoreType.DMA((4 * len(members),))] * 2
    res = pl.pallas_call(
        body, name=name,
        out_shape=(*sem_shapes, *[pltpu.HBM(a.shape, a.dtype) for a in lands], jax.ShapeDtypeStruct((8, LANES), F32)),
        in_specs=[HBM_SPEC] * n + [ANY_SPEC] * n_after,
        out_specs=(*[SEM_SPEC] * (2 * ng), *[HBM_SPEC] * n, VMEM_SPEC),
        input_output_aliases={i: 2 * ng + i for i in range(n)},
        compiler_params=pltpu.CompilerParams(has_side_effects=EFFECT),
    )(*[_hbm(a) for a in lands], *after)
    sem_pairs = [(res[2 * g], res[2 * g + 1]) for g in range(ng)]
    return sem_pairs, list(res[2 * ng:2 * ng + n]), res[-1]


def _ag_forward(name, lands, send_sems, recv_sems, after):
    n = len(lands)

    def body(*refs):
        lnd = refs[:n]
        s_sems, r_sems = refs[n], refs[n + 1]
        fs_sems, fr_sems = refs[n + 3], refs[n + 4]
        token = refs[-1]
        x, y, c = _pos()
        sources = [(x, y, 1 - c), (1 - x, y, c), (x, 1 - y, c), (1 - x, 1 - y, c)]
        for a in range(n):
            own = lnd[a].at[4 * x + 2 * y + c]
            for k, src in enumerate(sources):
                blk = lnd[a].at[4 * src[0] + 2 * src[1] + src[2]]
                arrived = pltpu.make_async_remote_copy(
                    src_ref=own, dst_ref=blk, send_sem=s_sems.at[4 * a + k], recv_sem=r_sems.at[4 * a + k],
                    device_id=src, device_id_type=MESH)
                arrived.wait_send()
                arrived.wait_recv()
                if k > 0:
                    pltpu.make_async_remote_copy(
                        src_ref=blk, dst_ref=blk, send_sem=fs_sems.at[3 * a + k - 1], recv_sem=fr_sems.at[3 * a + k - 1],
                        device_id=(x, y, 1 - c), device_id_type=MESH).start()
        token[...] = jnp.zeros_like(token)

    res = pl.pallas_call(
        body, name=name,
        out_shape=(pltpu.SemaphoreType.DMA((3 * n,)), pltpu.SemaphoreType.DMA((3 * n,)),
                   *[pltpu.HBM(a.shape, a.dtype) for a in lands], jax.ShapeDtypeStruct((8, LANES), F32)),
        in_specs=[HBM_SPEC] * n + [SEM_SPEC, SEM_SPEC, ANY_SPEC],
        out_specs=(SEM_SPEC, SEM_SPEC, *[HBM_SPEC] * n, VMEM_SPEC),
        input_output_aliases={i: 2 + i for i in range(n)},
        compiler_params=pltpu.CompilerParams(has_side_effects=EFFECT),
    )(*lands, send_sems, recv_sems, after)
    return res[0], res[1], list(res[2:2 + n]), res[-1]


def _ag_finish(name, lands, fs_sems, fr_sems, after):
    n = len(lands)

    def body(*refs):
        lnd = refs[:n]
        fs, fr = refs[n], refs[n + 1]
        x, y, c = _pos()
        chips = [(1 - x, y), (x, 1 - y), (1 - x, 1 - y)]
        for a in range(n):
            for j, chip in enumerate(chips):
                mine = lnd[a].at[4 * chip[0] + 2 * chip[1] + c]
                theirs = lnd[a].at[4 * chip[0] + 2 * chip[1] + 1 - c]
                cp = pltpu.make_async_remote_copy(
                    src_ref=mine, dst_ref=theirs, send_sem=fs.at[3 * a + j], recv_sem=fr.at[3 * a + j],
                    device_id=(x, y, 1 - c), device_id_type=MESH)
                cp.wait_send()
                cp.wait_recv()

    res = pl.pallas_call(
        body, name=name,
        out_shape=tuple(pltpu.HBM(a.shape, a.dtype) for a in lands),
        in_specs=[HBM_SPEC] * n + [SEM_SPEC, SEM_SPEC, ANY_SPEC],
        out_specs=tuple([HBM_SPEC] * n),
        input_output_aliases={i: i for i in range(n)},
        compiler_params=pltpu.CompilerParams(has_side_effects=EFFECT),
    )(*lands, fs_sems, fr_sems, after)
    return list(res)


def _rs_chips_start(name, arrs):
    n = len(arrs)

    def body(*refs):
        ins, lnd = refs[:n], refs[n:2 * n]
        s_sems, r_sems = refs[2 * n], refs[2 * n + 1]
        token = refs[-1]
        x, y, c = _pos()
        chips = [(1 - x, y), (x, 1 - y), (1 - x, 1 - y)]
        for a in range(n):
            for k, chip in enumerate(chips):
                pltpu.make_async_remote_copy(
                    src_ref=ins[a].at[2 * chip[0] + chip[1]], dst_ref=lnd[a].at[k],
                    send_sem=s_sems.at[3 * a + k], recv_sem=r_sems.at[3 * a + k],
                    device_id=(*chip, c), device_id_type=MESH).start()
        token[...] = jnp.zeros_like(token)

    lands = [lax.empty((3,) + a.shape[1:], a.dtype) for a in arrs]
    res = pl.pallas_call(
        body, name=name,
        out_shape=(pltpu.SemaphoreType.DMA((3 * n,)), pltpu.SemaphoreType.DMA((3 * n,)),
                   *[pltpu.HBM(a.shape, a.dtype) for a in arrs], *[pltpu.HBM(a.shape, a.dtype) for a in lands],
                   jax.ShapeDtypeStruct((8, LANES), F32)),
        in_specs=[HBM_SPEC] * (2 * n),
        out_specs=(SEM_SPEC, SEM_SPEC, *[HBM_SPEC] * (2 * n), VMEM_SPEC),
        input_output_aliases={i: 2 + i for i in range(2 * n)},
        compiler_params=pltpu.CompilerParams(has_side_effects=EFFECT),
    )(*[_hbm(a) for a in arrs], *[_hbm(a) for a in lands])
    return res[0], res[1], list(res[2:2 + n]), list(res[2 + n:2 + 2 * n]), res[-1]


def _rs_chips_wait(name, arrs, lands, send_sems, recv_sems, after):
    n = len(arrs)

    def body(*refs):
        ins, lnd = refs[:n], refs[n:2 * n]
        s_sems, r_sems = refs[2 * n], refs[2 * n + 1]
        x, y, c = _pos()
        chips = [(1 - x, y), (x, 1 - y), (1 - x, 1 - y)]
        for a in range(n):
            for k, chip in enumerate(chips):
                cp = pltpu.make_async_remote_copy(
                    src_ref=ins[a].at[2 * chip[0] + chip[1]], dst_ref=lnd[a].at[k],
                    send_sem=s_sems.at[3 * a + k], recv_sem=r_sems.at[3 * a + k],
                    device_id=(*chip, c), device_id_type=MESH)
                cp.wait_send()
                cp.wait_recv()

    res = pl.pallas_call(
        body, name=name,
        out_shape=tuple(pltpu.HBM(a.shape, a.dtype) for a in list(arrs) + list(lands)),
        in_specs=[HBM_SPEC] * (2 * n) + [SEM_SPEC, SEM_SPEC, ANY_SPEC],
        out_specs=tuple([HBM_SPEC] * (2 * n)),
        input_output_aliases={i: i for i in range(2 * n)},
        compiler_params=pltpu.CompilerParams(has_side_effects=EFFECT),
    )(*arrs, *lands, send_sems, recv_sems, after)
    return list(res[:n]), list(res[n:])


def _pair_add(name, g, r1, c_idx):
    _, r, cdim = g.shape
    tr = _rows(r, cdim, 2)

    def body(c_ref, g_ref, r_ref, o_ref):
        o_ref[...] = (g_ref[...].astype(F32) + r_ref[...].astype(F32)).astype(o_ref.dtype)

    return pl.pallas_call(
        body, name=name,
        out_shape=jax.ShapeDtypeStruct((4, r, cdim), BF),
        grid_spec=pltpu.PrefetchScalarGridSpec(
            num_scalar_prefetch=1, grid=(4, r // tr),
            in_specs=[pl.BlockSpec((None, tr, cdim), lambda j, i, c_ref: (2 * j + c_ref[0], i, 0)),
                      pl.BlockSpec((None, tr, cdim), lambda j, i, c_ref: (j, i, 0))],
            out_specs=pl.BlockSpec((None, tr, cdim), lambda j, i, c_ref: (j, i, 0))),
        compiler_params=_cparams(2),
    )(c_idx, g, r1)


def _chip_sum(name, p, r2, chip_idx):
    _, r, cdim = p.shape
    tr = _rows(r, cdim, 4)

    def body(s_ref, p_ref, a_ref, b_ref, c_ref, o_ref):
        o_ref[...] = ((p_ref[...].astype(F32) + a_ref[...].astype(F32)) + b_ref[...].astype(F32)) + c_ref[...].astype(F32)

    def other(k):
        return pl.BlockSpec((None, tr, cdim), lambda i, s_ref: (k, i, 0))

    return pl.pallas_call(
        body, name=name,
        out_shape=jax.ShapeDtypeStruct((r, cdim), F32),
        grid_spec=pltpu.PrefetchScalarGridSpec(
            num_scalar_prefetch=1, grid=(r // tr,),
            in_specs=[pl.BlockSpec((None, tr, cdim), lambda i, s_ref: (s_ref[0], i, 0)), other(0), other(1), other(2)],
            out_specs=pl.BlockSpec((tr, cdim), lambda i, s_ref: (i, 0))),
        compiler_params=_cparams(1),
    )(chip_idx, p, r2, r2, r2)


def _adam_math(w, g, m, v):
    m = B1 * m + (1.0 - B1) * g
    v = B2 * v + (1.0 - B2) * (g * g)
    m_hat = m / (1.0 - B1 ** STEP)
    v_hat = v / (1.0 - B2 ** STEP)
    delta = -LR * (m_hat / (jnp.sqrt(v_hat) + ADAM_EPS) + WD * w)
    return delta, m, v


def _adam(name, w, g, m, v):
    shape = w.shape
    cdim = shape[-1]
    r = w.size // cdim
    flat = [a.reshape(r, cdim) for a in (w, g, m, v)]
    tr = _rows(r, cdim, 4)

    def body(w_ref, g_ref, m_ref, v_ref, d_ref, mo_ref, vo_ref):
        d, mn, vn = _adam_math(w_ref[...], g_ref[...], m_ref[...], v_ref[...])
        d_ref[...] = d
        mo_ref[...] = mn
        vo_ref[...] = vn

    spec = pl.BlockSpec((tr, cdim), lambda i: (i, 0))
    outs = pl.pallas_call(
        body, name=name,
        out_shape=[jax.ShapeDtypeStruct((r, cdim), F32)] * 3,
        grid=(r // tr,), in_specs=[spec] * 4, out_specs=[spec] * 3,
        compiler_params=_cparams(1),
    )(*flat)
    return [o.reshape(shape) for o in outs]


def _mod_matvec(name, cond_all, w_loc):
    d, n_loc = w_loc.shape
    tn = _pick(n_loc, 512)

    def body(c_ref, w_ref, o_ref):
        o_ref[...] = jnp.dot(c_ref[...].astype(BF), w_ref[...].astype(BF), preferred_element_type=F32)

    return pl.pallas_call(
        body, name=name,
        out_shape=jax.ShapeDtypeStruct((N_DEV, n_loc), F32),
        grid=(n_loc // tn,),
        in_specs=[pl.BlockSpec((N_DEV, d), lambda j: (0, 0)), pl.BlockSpec((d, tn), lambda j: (0, j))],
        out_specs=pl.BlockSpec((N_DEV, tn), lambda j: (0, j)),
        compiler_params=_cparams(1),
    )(cond_all, w_loc)


def _mod_grad_adam(name, cond_t, dmod, w, m, v):
    d, n_loc = w.shape
    tr = _rows(d, n_loc, 4)

    def body(ct_ref, dm_ref, w_ref, m_ref, v_ref, g_ref, d_ref, mo_ref, vo_ref):
        g = ct_ref[:, 0:1] * dm_ref[0:1, :]
        for b in range(1, N_DEV):
            g = g + ct_ref[:, b:b + 1] * dm_ref[b:b + 1, :]
        dl, mn, vn = _adam_math(w_ref[...], g, m_ref[...], v_ref[...])
        g_ref[...] = g
        d_ref[...] = dl
        mo_ref[...] = mn
        vo_ref[...] = vn

    spec = pl.BlockSpec((tr, n_loc), lambda i: (i, 0))
    return pl.pallas_call(
        body, name=name,
        out_shape=[jax.ShapeDtypeStruct((d, n_loc), F32)] * 4,
        grid=(d // tr,),
        in_specs=[pl.BlockSpec((tr, N_DEV), lambda i: (i, 0)), pl.BlockSpec((N_DEV, n_loc), lambda i: (0, 0)), spec, spec, spec],
        out_specs=[spec] * 4,
        compiler_params=_cparams(1),
    )(cond_t, dmod, w, m, v)


def _sum_devices(name, gathered):
    _, m, n = gathered.shape

    def body(g_ref, o_ref):
        acc = g_ref[0]
        for d in range(1, N_DEV):
            acc = acc + g_ref[d]
        o_ref[...] = acc

    return pl.pallas_call(
        body, name=name, out_shape=jax.ShapeDtypeStruct((m, n), F32),
        in_specs=[VMEM_SPEC], out_specs=VMEM_SPEC,
    )(gathered)


def _silu(name, c_pad):
    def body(c_ref, o_ref):
        c = c_ref[...]
        o_ref[...] = c * (1.0 / (1.0 + jnp.exp(-c)))

    return pl.pallas_call(body, name=name, out_shape=jax.ShapeDtypeStruct(c_pad.shape, F32),
                          in_specs=[VMEM_SPEC], out_specs=VMEM_SPEC)(c_pad)


def _mm(name, a, b, dims, grid, a_spec, b_spec, outs, extras=(), epilogue=None):
    nk = grid[2]
    n_ex, n_out = len(extras), len(outs)

    def body(*refs):
        a_ref, b_ref = refs[0], refs[1]
        ex_refs = refs[2:2 + n_ex]
        out_refs = refs[2 + n_ex:2 + n_ex + n_out]
        part = lax.dot_general(a_ref[...].astype(BF), b_ref[...].astype(BF), dims, preferred_element_type=F32)

        def finish(acc):
            vals = (acc,) if epilogue is None else epilogue(acc, *[e[...] for e in ex_refs])
            for o_ref, val in zip(out_refs, vals):
                o_ref[...] = val.astype(o_ref.dtype)

        if nk == 1:
            finish(part)
        else:
            acc_ref = refs[-1]
            k = pl.program_id(2)

            @pl.when(k == 0)
            def _():
                acc_ref[...] = part

            @pl.when(k > 0)
            def _():
                acc_ref[...] += part

            @pl.when(k == nk - 1)
            def _():
                finish(acc_ref[...])

    tile = outs[0][2].block_shape
    acc_shape = tuple(t for t in tile if t is not None)
    res = pl.pallas_call(
        body, name=name,
        out_shape=[jax.ShapeDtypeStruct(s, d) for s, d, _ in outs],
        grid=grid,
        in_specs=[a_spec, b_spec] + [s for _, s in extras],
        out_specs=[s for _, _, s in outs],
        scratch_shapes=[] if nk == 1 else [pltpu.VMEM(acc_shape, F32)],
        compiler_params=_cparams(3),
    )(a, b, *[e for e, _ in extras])
    return res


def _w_nn_spec(w, tk, tn):
    if w.ndim == 2:
        return pl.BlockSpec((tk, tn), lambda i, j, k: (k, j))
    per = w.shape[2] // tn
    return pl.BlockSpec((None, tk, tn), lambda i, j, k: (j // per, k, j % per))


def _w_nt_spec(w, tn, tk):
    if w.ndim == 2:
        return pl.BlockSpec((tn, tk), lambda i, j, k: (j, k))
    per = w.shape[2] // tk
    return pl.BlockSpec((None, tn, tk), lambda i, j, k: (k // per, j, k % per))


def _w_cols(w):
    return w.shape[1] if w.ndim == 2 else w.shape[0] * w.shape[2]


def _w_rows(w):
    return w.shape[0] if w.ndim == 2 else w.shape[1]


def _col_tile(w, pref):
    return _pick(w.shape[1] if w.ndim == 2 else w.shape[2], pref)


def _linear(name, a, w, out_dtypes=(F32,), epilogue=None, extras=()):
    t, kdim = a.shape
    n = _w_cols(w)
    tm, tn, tk = _pick(t, TM), _col_tile(w, TN), _pick(kdim, TK)
    o_spec = pl.BlockSpec((tm, tn), lambda i, j, k: (i, j))
    return _mm(name, a, w, NN, (t // tm, n // tn, kdim // tk),
               pl.BlockSpec((tm, tk), lambda i, j, k: (i, k)), _w_nn_spec(w, tk, tn),
               [((t, n), dt, o_spec) for dt in out_dtypes],
               extras=[(e, o_spec) for e in extras], epilogue=epilogue)


def _linear_t(name, dy, w, out_dtypes=(F32,), epilogue=None, extras=()):
    t, n = dy.shape
    kdim = _w_rows(w)
    tm, tn, tk = _pick(t, TM), _pick(kdim, TN), _col_tile(w, TK)
    o_spec = pl.BlockSpec((tm, tn), lambda i, j, k: (i, j))
    return _mm(name, dy, w, NT, (t // tm, kdim // tn, n // tk),
               pl.BlockSpec((tm, tk), lambda i, j, k: (i, k)), _w_nt_spec(w, tn, tk),
               [((t, kdim), dt, o_spec) for dt in out_dtypes],
               extras=[(e, o_spec) for e in extras], epilogue=epilogue)


def _wgrad(name, a, dy, like):
    t, kdim = a.shape
    n = dy.shape[1]
    tm, tn, tk = _pick(kdim, TM), _col_tile(like, TM), _pick(t, 1024)
    if like.ndim == 2:
        o_spec = pl.BlockSpec((tm, tn), lambda i, j, k: (i, j))
    else:
        per = like.shape[2] // tn
        o_spec = pl.BlockSpec((None, tm, tn), lambda i, j, k: (j // per, i, j % per))
    return _mm(name, a, dy, TNDIMS, (kdim // tm, n // tn, t // tk),
               pl.BlockSpec((tk, tm), lambda i, j, k: (k, i)), pl.BlockSpec((tk, tn), lambda i, j, k: (k, j)),
               [(like.shape, BF, o_spec)])[0]


def _rstd(x):
    return lax.rsqrt(jnp.mean(x * x, axis=-1, keepdims=True) + NORM_EPS)


def _colsum(x):
    return jnp.sum(x, axis=0, keepdims=True)


def _norm_fwd(name, x, y, vec, target, has_post, has_pre, has_loss):
    t, d = x.shape
    tm = _rows(t, d, 4)
    ins, in_specs = [x], [pl.BlockSpec((tm, d), lambda i: (i, 0))]
    row_spec = pl.BlockSpec((tm, d), lambda i: (i, 0))
    if has_post:
        ins.append(y)
        in_specs.append(row_spec)
    ins.append(vec)
    in_specs.append(pl.BlockSpec((8, d), lambda i: (0, 0)))
    if has_loss:
        ins.append(target)
        in_specs.append(row_spec)
    out_shape, out_specs = [], []
    if has_post and not has_loss:
        out_shape.append(jax.ShapeDtypeStruct((t, d), F32))
        out_specs.append(row_spec)
    if has_pre:
        out_shape.append(jax.ShapeDtypeStruct((t, d), BF))
        out_specs.append(row_spec)
    if has_loss:
        out_shape += [jax.ShapeDtypeStruct((t, d), F32), jax.ShapeDtypeStruct((8, d), F32)]
        out_specs += [row_spec, pl.BlockSpec((8, d), lambda i: (0, 0))]

    def body(*refs):
        it = iter(refs)
        x_ref = next(it)
        y_ref = next(it) if has_post else None
        vec_ref = next(it)
        tgt_ref = next(it) if has_loss else None
        xv = x_ref[...]
        if has_post:
            yv = y_ref[...]
            xv = xv + vec_ref[1:2, :] * ((yv * _rstd(yv)) * vec_ref[0:1, :])
            if not has_loss:
                next(it)[...] = xv
        if has_pre:
            hv = ((xv * _rstd(xv)) * vec_ref[2:3, :]) * (1.0 + vec_ref[3:4, :]) + vec_ref[4:5, :]
            next(it)[...] = hv.astype(BF)
        if has_loss:
            e = xv - tgt_ref[...]
            next(it)[...] = e * (1.0 / d)
            acc_ref = next(it)
            i = pl.program_id(0)

            @pl.when(i == 0)
            def _():
                acc_ref[...] = jnp.zeros_like(acc_ref)

            acc_ref[0:1, :] += _colsum(e * e)

    return pl.pallas_call(
        body, name=name, out_shape=out_shape, grid=(t // tm,), in_specs=in_specs, out_specs=out_specs,
        compiler_params=_cparams(1),
    )(*ins)


def _norm_bwd(name, dx_in, dh, x, y, vec, has_pre, has_post):
    t, d = dx_in.shape
    tm = _rows(t, d, 4)
    row_spec = pl.BlockSpec((tm, d), lambda i: (i, 0))
    vec_spec = pl.BlockSpec((8, d), lambda i: (0, 0))
    ins, in_specs = [dx_in], [row_spec]
    if has_pre:
        ins += [dh, x]
        in_specs += [row_spec, row_spec]
    if has_post:
        ins.append(y)
        in_specs.append(row_spec)
    ins.append(vec)
    in_specs.append(vec_spec)
    out_shape, out_specs = [], []
    if has_pre:
        out_shape.append(jax.ShapeDtypeStruct((t, d), F32))
        out_specs.append(row_spec)
    if has_post:
        out_shape.append(jax.ShapeDtypeStruct((t, d), BF))
        out_specs.append(row_spec)
    out_shape.append(jax.ShapeDtypeStruct((8, d), F32))
    out_specs.append(vec_spec)

    def body(*refs):
        it = iter(refs)
        dx = next(it)[...]
        dh_ref = next(it) if has_pre else None
        x_ref = next(it) if has_pre else None
        y_ref = next(it) if has_post else None
        vec_ref = next(it)
        dx_ref = next(it) if has_pre else None
        dy_ref = next(it) if has_post else None
        sums_ref = next(it)
        i = pl.program_id(0)

        @pl.when(i == 0)
        def _():
            sums_ref[...] = jnp.zeros_like(sums_ref)

        if has_pre:
            dhv, xv = dh_ref[...], x_ref[...]
            rs = _rstd(xv)
            xhat = xv * rs
            na = vec_ref[2:3, :]
            sums_ref[0:1, :] += _colsum(dhv)
            sums_ref[1:2, :] += _colsum(dhv * (xhat * na))
            tt = dhv * (1.0 + vec_ref[3:4, :])
            sums_ref[2:3, :] += _colsum(tt * xhat)
            u = tt * na
            dx = dx + rs * (u - xhat * jnp.mean(u * xhat, axis=-1, keepdims=True))
            dx_ref[...] = dx
        if has_post:
            yv = y_ref[...]
            rs = _rstd(yv)
            yhat = yv * rs
            nb = vec_ref[0:1, :]
            sums_ref[3:4, :] += _colsum(dx * (yhat * nb))
            tt = dx * vec_ref[1:2, :]
            sums_ref[4:5, :] += _colsum(tt * yhat)
            u = tt * nb
            dy_ref[...] = (rs * (u - yhat * jnp.mean(u * yhat, axis=-1, keepdims=True))).astype(BF)

    return pl.pallas_call(
        body, name=name, out_shape=out_shape, grid=(t // tm,), in_specs=in_specs, out_specs=out_specs,
        compiler_params=_cparams(1),
    )(*ins)


def _rope128(tv, cs, s1, s2):
    return tv * cs + pltpu.roll(tv, 96, 1) * s1 + pltpu.roll(tv, 32, 1) * s2


def _lat_post(name, lat, gq, gkv, cs, s1, s2, rq, rkv):
    t, w = lat.shape
    tm = _rows(t, w, 4)

    def body(lat_ref, gq_ref, gkv_ref, cs_ref, s1_ref, s2_ref, cq_ref, ckv_ref, kr_ref):
        lq = lat_ref[:, 0:rq]
        lkv = lat_ref[:, rq:rq + rkv]
        cq_ref[...] = ((lq * _rstd(lq)) * gq_ref[...]).astype(BF)
        ckv_ref[...] = ((lkv * _rstd(lkv)) * gkv_ref[...]).astype(BF)
        kr_ref[...] = _rope128(lat_ref[:, rq + rkv:rq + rkv + LANES], cs_ref[...], s1_ref[...], s2_ref[...]).astype(BF)

    def rows(c):
        return pl.BlockSpec((tm, c), lambda i: (i, 0))

    def vecs(c):
        return pl.BlockSpec((1, c), lambda i: (0, 0))

    return pl.pallas_call(
        body, name=name,
        out_shape=[jax.ShapeDtypeStruct((t, rq), BF), jax.ShapeDtypeStruct((t, rkv), BF), jax.ShapeDtypeStruct((t, LANES), BF)],
        grid=(t // tm,),
        in_specs=[rows(w), vecs(rq), vecs(rkv), rows(LANES), rows(LANES), rows(LANES)],
        out_specs=[rows(rq), rows(rkv), rows(LANES)],
        compiler_params=_cparams(1),
    )(lat, gq, gkv, cs, s1, s2)


def _lat_bwd(name, dcq, dckv, dkr, lat, gq, gkv, cs, s1, s2, rq, rkv, heads):
    t, w = lat.shape
    tm = _rows(t, max(w, heads * LANES), 4)
    assert rq == rkv

    def body(dcq_ref, dckv_ref, dkr_ref, lat_ref, gq_ref, gkv_ref, cs_ref, s1_ref, s2_ref, dlat_ref, sums_ref):
        i = pl.program_id(0)

        @pl.when(i == 0)
        def _():
            sums_ref[...] = jnp.zeros_like(sums_ref)

        def rms_bwd(dc, lv, g, row):
            rs = _rstd(lv)
            lhat = lv * rs
            sums_ref[row:row + 1, :] += _colsum(dc * lhat)
            u = dc * g
            return rs * (u - lhat * jnp.mean(u * lhat, axis=-1, keepdims=True))

        dlat_ref[:, 0:rq] = rms_bwd(dcq_ref[...], lat_ref[:, 0:rq], gq_ref[...], 0).astype(BF)
        dlat_ref[:, rq:rq + rkv] = rms_bwd(dckv_ref[...], lat_ref[:, rq:rq + rkv], gkv_ref[...], 1).astype(BF)
        dk = dkr_ref[:, 0:LANES]
        for h in range(1, heads):
            dk = dk + dkr_ref[:, h * LANES:(h + 1) * LANES]
        dlat_ref[:, rq + rkv:rq + rkv + LANES] = _rope128(dk, cs_ref[...], s1_ref[...], s2_ref[...]).astype(BF)

    def rows(c):
        return pl.BlockSpec((tm, c), lambda i: (i, 0))

    def vecs(c):
        return pl.BlockSpec((1, c), lambda i: (0, 0))

    return pl.pallas_call(
        body, name=name,
        out_shape=[jax.ShapeDtypeStruct((t, w), BF), jax.ShapeDtypeStruct((8, rq), F32)],
        grid=(t // tm,),
        in_specs=[rows(rq), rows(rkv), rows(heads * LANES), rows(w), vecs(rq), vecs(rkv), rows(LANES), rows(LANES), rows(LANES)],
        out_specs=[rows(w), pl.BlockSpec((8, rq), lambda i: (0, 0))],
        compiler_params=_cparams(1),
    )(dcq, dckv, dkr, lat, gq, gkv, cs, s1, s2)


def _rope_heads(name, xa, a_blk, xb, b_blk, cs, s1, s2, heads):
    t = xa.shape[0]
    hw = heads * LANES
    tm = _rows(t, 2 * hw, 4)

    def body(a_ref, b_ref, cs_ref, s1_ref, s2_ref, o_ref):
        o_ref[:, 0:hw] = a_ref[...].astype(BF)
        csv, s1v, s2v = cs_ref[...], s1_ref[...], s2_ref[...]
        for h in range(heads):
            sl = slice(h * LANES, (h + 1) * LANES)
            o_ref[:, hw + h * LANES:hw + (h + 1) * LANES] = _rope128(b_ref[:, sl], csv, s1v, s2v).astype(BF)

    def rows(c):
        return pl.BlockSpec((tm, c), lambda i: (i, 0))

    return pl.pallas_call(
        body, name=name,
        out_shape=jax.ShapeDtypeStruct((t, 2 * hw), BF),
        grid=(t // tm,),
        in_specs=[pl.BlockSpec((tm, hw), lambda i: (i, a_blk)), pl.BlockSpec((tm, hw), lambda i: (i, b_blk)),
                  rows(LANES), rows(LANES), rows(LANES)],
        out_specs=rows(2 * hw),
        compiler_params=_cparams(1),
    )(xa, xb, cs, s1, s2)


def _causal_mask(s, row0):
    row = lax.broadcasted_iota(jnp.int32, s.shape, 0) + row0
    col = lax.broadcasted_iota(jnp.int32, s.shape, 1)
    return col <= row


def _flash_fwd(name, qb, kv, kr, heads, scale):
    t = qb.shape[0]
    tq = _pick(t, TQ)
    nq = t // tq
    sub = tq // Q_CHAINS
    c2 = scale * LOG2E

    def body(qn_ref, qr_ref, kv_ref, kr_ref, o_ref, lse_ref):
        i = pl.program_id(1)
        qs = [jnp.concatenate([qn_ref[a * sub:(a + 1) * sub, :], qr_ref[a * sub:(a + 1) * sub, :]], axis=-1)
              for a in range(Q_CHAINS)]

        def block(j, carry, masked):
            rows = pl.ds(pl.multiple_of(j * tq, tq), tq)
            k = jnp.concatenate([kv_ref[rows, 0:LANES], kr_ref[rows, :]], axis=-1)
            v = kv_ref[rows, LANES:2 * LANES]
            out = []
            for a in range(Q_CHAINS):
                m_prev, l_prev, acc = carry[a]
                s = lax.dot_general(qs[a], k, NT, preferred_element_type=F32) * c2
                if masked:
                    s = jnp.where(_causal_mask(s, a * sub), s, -1e30)
                m_new = jnp.maximum(m_prev, jnp.max(s, axis=-1, keepdims=True))
                alpha = jnp.exp2(m_prev - m_new)
                p = jnp.exp2(s - m_new)
                l_new = alpha * l_prev + jnp.sum(p, axis=-1, keepdims=True)
                out.append((m_new, l_new, alpha * acc + jnp.dot(p.astype(BF), v, preferred_element_type=F32)))
            return tuple(out)

        init = tuple((jnp.full((sub, 1), -1e30, F32), jnp.zeros((sub, 1), F32), jnp.zeros((sub, V_DIM), F32))
                     for _ in range(Q_CHAINS))
        carry = lax.fori_loop(0, i, lambda j, cr: block(j, cr, False), init)
        fin = block(i, carry, True)
        for a in range(Q_CHAINS):
            m_fin, l_fin, acc = fin[a]
            o_ref[a * sub:(a + 1) * sub, :] = (acc / l_fin).astype(BF)
            lse_ref[a * sub:(a + 1) * sub, :] = jnp.broadcast_to(m_fin + jnp.log2(l_fin), (sub, LANES))

    return pl.pallas_call(
        body, name=name,
        out_shape=[jax.ShapeDtypeStruct((t, heads * V_DIM), BF), jax.ShapeDtypeStruct((t, heads * LANES), F32)],
        grid=(heads, nq),
        in_specs=[pl.BlockSpec((tq, LANES), lambda h, i: (i, h)),
                  pl.BlockSpec((tq, LANES), lambda h, i: (i, heads + h)),
                  pl.BlockSpec((t, 2 * LANES), lambda h, i: (0, h)),
                  pl.BlockSpec((t, LANES), lambda h, i: (0, 0))],
        out_specs=[pl.BlockSpec((tq, V_DIM), lambda h, i: (i, h)), pl.BlockSpec((tq, LANES), lambda h, i: (i, h))],
        compiler_params=_cparams(2),
    )(qb, qb, kv, kr)


def _flash_bwd(name, qb, kv, kr, o, lse, do, heads, scale):
    t = qb.shape[0]
    tq = _pick(t, TQ)
    nq = t // tq

    c2 = scale * LOG2E

    def body(qn_ref, qr_ref, kv_ref, kr_ref, o_ref, lse_ref, do_ref, dkv_ref, dkr_ref, dqn_ref, dqr_ref):
        j = pl.program_id(1)

        @pl.when(j == 0)
        def _():
            dqn_ref[...] = jnp.zeros_like(dqn_ref)
            dqr_ref[...] = jnp.zeros_like(dqr_ref)

        k = jnp.concatenate([kv_ref[:, 0:LANES], kr_ref[...]], axis=-1)
        v = kv_ref[:, LANES:2 * LANES]

        def block(i, carry, masked):
            dk, dv = carry
            rows = pl.ds(pl.multiple_of(i * tq, tq), tq)
            q = jnp.concatenate([qn_ref[rows, :], qr_ref[rows, :]], axis=-1)
            dov = do_ref[rows, :]
            delta = jnp.sum(dov.astype(F32) * o_ref[rows, :].astype(F32), axis=-1, keepdims=True)
            s = lax.dot_general(q, k, NT, preferred_element_type=F32) * c2
            p = jnp.exp2(s - lse_ref[rows, 0:1])
            if masked:
                p = jnp.where(_causal_mask(s, 0), p, 0.0)
            dv = dv + lax.dot_general(p.astype(BF), dov, TNDIMS, preferred_element_type=F32)
            dp = lax.dot_general(dov, v, NT, preferred_element_type=F32)
            ds = (p * (dp - delta) * scale).astype(BF)
            dk = dk + lax.dot_general(ds, q, TNDIMS, preferred_element_type=F32)
            dq = jnp.dot(ds, k, preferred_element_type=F32)
            dqn_ref[rows, :] += dq[:, 0:LANES]
            dqr_ref[rows, :] += dq[:, LANES:2 * LANES]
            return dk, dv

        carry = block(j, (jnp.zeros((tq, 2 * LANES), F32), jnp.zeros((tq, V_DIM), F32)), True)
        dk, dv = lax.fori_loop(j + 1, nq, lambda i, cr: block(i, cr, False), carry)
        dkv_ref[:, 0:LANES] = dk[:, 0:LANES].astype(BF)
        dkv_ref[:, LANES:2 * LANES] = dv.astype(BF)
        dkr_ref[...] = dk[:, LANES:2 * LANES]

    def full(hoff):
        return pl.BlockSpec((t, LANES), lambda h, j: (0, hoff + h))

    return pl.pallas_call(
        body, name=name,
        out_shape=[jax.ShapeDtypeStruct((t, heads * 2 * LANES), BF), jax.ShapeDtypeStruct((t, heads * LANES), F32),
                   jax.ShapeDtypeStruct((t, heads * LANES), F32), jax.ShapeDtypeStruct((t, heads * LANES), F32)],
        grid=(heads, nq),
        in_specs=[full(0), full(heads), pl.BlockSpec((tq, 2 * LANES), lambda h, j: (j, h)),
                  pl.BlockSpec((tq, LANES), lambda h, j: (j, 0)), full(0), full(0), full(0)],
        out_specs=[pl.BlockSpec((tq, 2 * LANES), lambda h, j: (j, h)), pl.BlockSpec((tq, LANES), lambda h, j: (j, h)),
                   full(0), full(0)],
        compiler_params=_cparams(2),
    )(qb, qb, kv, kr, o, lse, do)


def _shift_down(z, n):
    row = lax.broadcasted_iota(jnp.int32, z.shape, 0)
    return jnp.where(row >= n, pltpu.roll(z, n, 0), 0.0)


def _shift_up(z, n):
    t = z.shape[0]
    row = lax.broadcasted_iota(jnp.int32, z.shape, 0)
    return jnp.where(row < t - n, pltpu.roll(z, t - n, 0), 0.0)


def _conv_fwd(name, proj, cw):
    t, d3 = proj.shape
    d = d3 // 3
    tn = _pick(d, LANES)
    nb = d // tn

    def body(b_ref, c_ref, u_ref, w_ref, o_ref):
        z = c_ref[...] * u_ref[...]
        zc = w_ref[0:1, :] * _shift_down(z, 2) + w_ref[1:2, :] * _shift_down(z, 1) + w_ref[2:3, :] * z
        o_ref[...] = (b_ref[...] * zc).astype(BF)

    def part(p):
        return pl.BlockSpec((t, tn), lambda j: (0, p * nb + j))

    return pl.pallas_call(
        body, name=name, out_shape=jax.ShapeDtypeStruct((t, d), BF), grid=(nb,),
        in_specs=[part(0), part(1), part(2), pl.BlockSpec((8, tn), lambda j: (0, j))],
        out_specs=pl.BlockSpec((t, tn), lambda j: (0, j)),
        compiler_params=_cparams(1),
    )(proj, proj, proj, cw)


def _conv_bwd(name, dbz, proj, cw):
    t, d3 = proj.shape
    d = d3 // 3
    tn = _pick(d, LANES)
    nb = d // tn

    def body(g_ref, b_ref, c_ref, u_ref, w_ref, o_ref, sums_ref):
        p = pl.program_id(1)
        w0, w1, w2 = w_ref[0:1, :], w_ref[1:2, :], w_ref[2:3, :]

        @pl.when(p == 0)
        def _():
            z = c_ref[...] * u_ref[...]
            z1, z2 = _shift_down(z, 1), _shift_down(z, 2)
            gv = g_ref[...]
            o_ref[...] = (gv * (w0 * z2 + w1 * z1 + w2 * z)).astype(BF)
            dzc = gv * b_ref[...]
            sums_ref[...] = jnp.zeros_like(sums_ref)
            sums_ref[0:1, :] = _colsum(dzc * z2)
            sums_ref[1:2, :] = _colsum(dzc * z1)
            sums_ref[2:3, :] = _colsum(dzc * z)

        @pl.when(p > 0)
        def _():
            dzc = g_ref[...] * b_ref[...]
            dz = w2 * dzc + w1 * _shift_up(dzc, 1) + w0 * _shift_up(dzc, 2)
            other = jnp.where(p == 1, u_ref[...], c_ref[...])
            o_ref[...] = (dz * other).astype(BF)

    def part(q):
        return pl.BlockSpec((t, tn), lambda j, p: (0, q * nb + j))

    return pl.pallas_call(
        body, name=name,
        out_shape=[jax.ShapeDtypeStruct((t, d3), BF), jax.ShapeDtypeStruct((8, d), F32)],
        grid=(nb, 3),
        in_specs=[pl.BlockSpec((t, tn), lambda j, p: (0, j)), part(0), part(1), part(2),
                  pl.BlockSpec((8, tn), lambda j, p: (0, j))],
        out_specs=[pl.BlockSpec((t, tn), lambda j, p: (0, p * nb + j)), pl.BlockSpec((8, tn), lambda j, p: (0, j))],
        compiler_params=_cparams(2),
    )(dbz, proj, proj, proj, cw)


def _pad_rows8(v):
    return jnp.pad(v, ((0, 8 - v.shape[0]), (0, 0)))


def kernel(x, c, positions, w_mod, b_mod, norm_g, mla_w_in, mla_g_q, mla_g_kv, mla_w_uq, mla_w_ukv, mla_w_o, conv_w_in, conv_w, conv_w_out, mlp_w_up, mlp_w_down, loss_target, m_w_mod, m_b_mod, m_norm_g, m_mla_w_in, m_mla_g_q, m_mla_g_kv, m_mla_w_uq, m_mla_w_ukv, m_mla_w_o, m_conv_w_in, m_conv_w, m_conv_w_out, m_mlp_w_up, m_mlp_w_down, v_w_mod, v_b_mod, v_norm_g, v_mla_w_in, v_mla_g_q, v_mla_g_kv, v_mla_w_uq, v_mla_w_ukv, v_mla_w_o, v_conv_w_in, v_conv_w, v_conv_w_out, v_mlp_w_up, v_mlp_w_down):
    t, d = x.shape[1], x.shape[2]
    depth = w_mod.shape[0]
    n_mod_loc = w_mod.shape[2]
    d_loc = norm_g.shape[2]
    rq = mla_g_q.shape[1]
    rkv = mla_g_kv.shape[1]
    heads = mla_w_uq.shape[2]
    lat_w = mla_w_in.shape[2]
    lat_pad = rq + rkv + LANES
    scale = (NOPE_DIM + ROPE_DIM) ** -0.5
    xi, yi, ci = _pos()
    dev = 4 * xi + 2 * yi + ci
    c_idx = jnp.reshape(ci, (1,)).astype(jnp.int32)
    chip_idx = jnp.reshape(2 * xi + yi, (1,)).astype(jnp.int32)
    x0 = x[0]
    target = loss_target[0]

    w_in_loc = jnp.pad(mla_w_in[0], ((0, 0), (0, lat_pad - lat_w))).astype(BF)
    uq = mla_w_uq[0]
    uq_loc = jnp.concatenate(
        [uq[:, :, :NOPE_DIM].reshape(uq.shape[0], heads * NOPE_DIM),
         jnp.pad(uq[:, :, NOPE_DIM:], ((0, 0), (0, 0), (0, LANES - ROPE_DIM))).reshape(uq.shape[0], heads * LANES)],
        axis=1).astype(BF)
    ukv_loc = mla_w_ukv[0].reshape(mla_w_ukv.shape[1], heads * (NOPE_DIM + V_DIM)).astype(BF)
    def merged(g):
        return g.reshape(g.shape[0] * g.shape[1], g.shape[2])

    g_win, g_uq, g_ukv, g_wo = [merged(g) for g in _ag_big("ag_mla", [w_in_loc, uq_loc, ukv_loc, mla_w_o[0].astype(BF)])]

    n_ng = depth * 4 * d_loc
    small = jnp.concatenate([c.reshape(-1), norm_g.reshape(-1), conv_w.reshape(-1)])
    small_n = small.shape[0]
    small_rows = -(-small_n // (8 * LANES)) * 8
    small = jnp.pad(small, (0, small_rows * LANES - small_n)).reshape(small_rows, LANES)
    small_all = _ag_small("ag_small", small).reshape(N_DEV, small_rows * LANES)
    c_all = small_all[:, :d]
    ng_all = small_all[:, d:d + n_ng].reshape(N_DEV, depth, 4, d_loc).transpose(1, 2, 0, 3).reshape(depth, 4, d)
    cw_all = small_all[:, d + n_ng:d + n_ng + 3 * d_loc].reshape(N_DEV, 3, d_loc).transpose(1, 0, 2).reshape(3, d)
    cw8 = _pad_rows8(cw_all)
    cond_all = _silu("silu", c_all)

    mod_parts = jnp.stack([_mod_matvec(f"mod_matvec{i}", cond_all, w_mod[i]) for i in range(depth)])
    mod_rows = depth * N_DEV * n_mod_loc // LANES
    mod_all = _ag_small("ag_mod", mod_parts.reshape(mod_rows, LANES)).reshape(N_DEV, depth, N_DEV, n_mod_loc)
    mod_mine = lax.dynamic_index_in_dim(mod_all, dev, axis=2, keepdims=False)
    mod = mod_mine.transpose(1, 0, 2).reshape(depth, N_DEV * n_mod_loc) + b_mod
    mod = mod.reshape(depth, 6, d)

    dev_idx = jnp.reshape(dev, (1,)).astype(jnp.int32)
    later = [mlp_w_up[0], mlp_w_down[0], conv_w_in[0], conv_w_out[0], mlp_w_up[1], mlp_w_down[1]]
    lands = [_cast_own_block(f"cast_own_block{a}", w, dev_idx) for a, w in enumerate(later)]
    ag_groups = [[0, 1], [2, 3, 4, 5]]
    ag_sems, lands_thru, ag_token = _ag_start("ag_start", lands, ag_groups, [mod, g_wo])

    def fwd_vec(nb, gate, na, sc, sh, token=None):
        zero = jnp.zeros((d,), F32)
        rows = [zero if r is None else r for r in (nb, gate, na, sc, sh)]
        vec = _pad_rows8(jnp.stack(rows))
        return vec if token is None else vec + token[0, 0]

    inv_freq = ROPE_THETA ** (-jnp.arange(0, ROPE_DIM, 2, dtype=F32) / ROPE_DIM)
    ang = positions[0].astype(F32)[:, None] * inv_freq
    cos, sin = jnp.cos(ang), jnp.sin(ang)
    zh = jnp.zeros_like(cos)
    zpad = jnp.zeros((t, LANES - ROPE_DIM), F32)
    rope_c = jnp.concatenate([cos, cos, zpad], axis=1)
    rope_s1 = jnp.concatenate([-sin, zh, zpad], axis=1)
    rope_s2 = jnp.concatenate([zh, sin, zpad], axis=1)

    (h0,) = _norm_fwd("pre0", x0, None, fwd_vec(None, None, ng_all[0, 0], mod[0, 1], mod[0, 0], ag_token), None, False, True, False)
    (lat,) = _linear("mla_lat", h0, g_win)
    cq, ckv, kr = _lat_post("mla_lat_post", lat, mla_g_q, mla_g_kv, rope_c, rope_s1, rope_s2, rq, rkv)
    (q_raw,) = _linear("mla_q", cq, g_uq)
    qb = _rope_heads("mla_q_rope", q_raw, 0, q_raw, 1, rope_c, rope_s1, rope_s2, heads)
    (kvb,) = _linear("mla_kv", ckv, g_ukv, out_dtypes=(BF,))
    o, lse = _flash_fwd("mla_attn", qb, kvb, kr, heads, scale)
    fs_a, fr_a, lands_a, _ = _ag_forward("ag_forward_a", lands_thru[0:2], *ag_sems[0], o)
    (y0,) = _linear("mla_out", o, g_wo)
    x1, h1 = _norm_fwd("postpre1", x0, y0, fwd_vec(ng_all[0, 1], mod[0, 2], ng_all[0, 2], mod[0, 4], mod[0, 3]), None, True, True, False)
    g_up0, g_dn0 = _ag_finish("ag_finish_a", lands_a, fs_a, fr_a, x1)
    g_dn0 = merged(g_dn0)

    def sq_relu(acc):
        a = jnp.maximum(acc, 0.0)
        return acc, a * a

    u1, act1 = _linear("mlp0_up", h1, g_up0, out_dtypes=(F32, BF), epilogue=sq_relu)
    (y1,) = _linear("mlp0_down", act1, g_dn0)
    fs_b, fr_b, lands_b, _ = _ag_forward("ag_forward_b", lands_thru[2:6], *ag_sems[1], y1)
    x2, h2 = _norm_fwd("postpre2", x1, y1, fwd_vec(ng_all[0, 3], mod[0, 5], ng_all[1, 0], mod[1, 1], mod[1, 0]), None, True, True, False)
    g_cin, g_cout, g_up1, g_dn1 = _ag_finish("ag_finish_b", lands_b, fs_b, fr_b, x2)
    g_cout, g_dn1 = merged(g_cout), merged(g_dn1)
    g_up, g_dn = [g_up0, g_up1], [g_dn0, g_dn1]

    (proj,) = _linear("conv_in", h2, g_cin)
    bz = _conv_fwd("conv_mix", proj, cw8)
    (y2,) = _linear("conv_out", bz, g_cout)
    x3, h3 = _norm_fwd("postpre3", x2, y2, fwd_vec(ng_all[1, 1], mod[1, 2], ng_all[1, 2], mod[1, 4], mod[1, 3]), None, True, True, False)

    u3, act3 = _linear("mlp1_up", h3, g_up[1], out_dtypes=(F32, BF), epilogue=sq_relu)
    (y3,) = _linear("mlp1_down", act3, g_dn[1])
    dx4, loss_cols = _norm_fwd("post_loss", x3, y3, fwd_vec(ng_all[1, 3], mod[1, 5], None, None, None), target, True, False, True)
    loss = lax.psum(0.5 * jnp.sum(loss_cols[0]) / d, ("x", "y", "c"))

    def bwd_vec(nb, gate, na, sc, token=None):
        zero = jnp.zeros((d,), F32)
        rows = [zero if r is None else r for r in (nb, gate, na, sc)]
        vec = _pad_rows8(jnp.stack(rows))
        return vec if token is None else vec + token[0, 0]

    def blocks(g, like):
        return g.reshape((N_DEV,) + like.shape)

    def rs_begin(tag, grads):
        from_sibling = _rs_sibling(f"rs_sibling_{tag}", grads)
        pair = [_pair_add(f"pair_add_{tag}{a}", grads[a], from_sibling[a], c_idx) for a in range(len(grads))]
        return _rs_chips_start(f"rs_start_{tag}", pair)

    def mlp_bwd(tag, dy, h, u, act, w_up, w_dn):
        def relu_grad(acc, uv):
            return (acc * (2.0 * jnp.maximum(uv, 0.0)),)

        (du,) = _linear_t(f"{tag}_dact", dy, w_dn, out_dtypes=(BF,), epilogue=relu_grad, extras=(u,))
        dw_dn = _wgrad(f"{tag}_dw_down", act, dy, w_dn)
        dw_up = _wgrad(f"{tag}_dw_up", h, du, w_up)
        (dh,) = _linear_t(f"{tag}_dh", du, w_up)
        return dh, dw_up, dw_dn

    dy3, sums4 = _norm_bwd("bwd_post3", dx4, None, None, y3, bwd_vec(ng_all[1, 3], mod[1, 5], None, None), False, True)
    dh3, dw_up1, dw_dn1 = mlp_bwd("mlp1", dy3, h3, u3, act3, g_up[1], g_dn[1])
    rs_mlp1 = rs_begin("mlp1", [dw_up1, blocks(dw_dn1, mlp_w_down[1])])
    dx3, dy2, sums3 = _norm_bwd("bwd_norm3", dx4, dh3, x3, y2, bwd_vec(ng_all[1, 1], mod[1, 2], ng_all[1, 2], mod[1, 4], rs_mlp1[4]), True, True)

    (dbz,) = _linear_t("conv_dbz", dy2, g_cout)
    dw_cout = _wgrad("conv_dw_out", bz, dy2, g_cout)
    dproj, conv_sums = _conv_bwd("conv_mix_bwd", dbz, proj, cw8)
    dw_cin = _wgrad("conv_dw_in", h2, dproj, g_cin)
    rs_conv = rs_begin("conv", [dw_cin, blocks(dw_cout, conv_w_out[0])])
    (dh2,) = _linear_t("conv_dh", dproj, g_cin)
    dx2, dy1, sums2 = _norm_bwd("bwd_norm2", dx3, dh2, x2, y1, bwd_vec(ng_all[0, 3], mod[0, 5], ng_all[1, 0], mod[1, 1], rs_conv[4]), True, True)

    dh1, dw_up0, dw_dn0 = mlp_bwd("mlp0", dy1, h1, u1, act1, g_up[0], g_dn[0])
    rs_mlp0 = rs_begin("mlp0", [dw_up0, blocks(dw_dn0, mlp_w_down[0])])
    dx1, dy0, sums1 = _norm_bwd("bwd_norm1", dx2, dh1, x1, y0, bwd_vec(ng_all[0, 1], mod[0, 2], ng_all[0, 2], mod[0, 4], rs_mlp0[4]), True, True)

    (do,) = _linear_t("mla_do", dy0, g_wo, out_dtypes=(BF,))
    dw_o = _wgrad("mla_dw_o", o, dy0, g_wo)
    dkv, dkr, dqn, dqr = _flash_bwd("mla_attn_bwd", qb, kvb, kr, o, lse, do, heads, scale)
    dq = _rope_heads("mla_dq_rope", dqn, 0, dqr, 0, rope_c, -rope_s1, -rope_s2, heads)
    dw_uq = _wgrad("mla_dw_uq", cq, dq, g_uq)
    (dcq,) = _linear_t("mla_dcq", dq, g_uq)
    dw_ukv = _wgrad("mla_dw_ukv", ckv, dkv, g_ukv)
    (dckv,) = _linear_t("mla_dckv", dkv, g_ukv)
    dlat, lat_sums = _lat_bwd("mla_lat_bwd", dcq, dckv, dkr, lat, mla_g_q, mla_g_kv, rope_c, -rope_s1, -rope_s2, rq, rkv, heads)
    dw_win = _wgrad("mla_dw_in", h0, dlat, g_win)
    rs_mla = rs_begin("mla", [blocks(dw_win, w_in_loc), blocks(dw_uq, uq_loc), blocks(dw_ukv, ukv_loc), blocks(dw_o, mla_w_o[0])])
    (dh0,) = _linear_t("mla_dh", dlat, g_win)
    grad_x, sums0 = _norm_bwd("bwd_pre0", dx1, dh0, x0, None, bwd_vec(None, None, ng_all[0, 0], mod[0, 1], rs_mla[4]), True, False)

    def sums_rows(s, rows):
        return [s[r] for r in rows]

    dmod = jnp.stack([
        jnp.stack(sums_rows(sums0, (0, 1)) + sums_rows(sums1, (3,)) + sums_rows(sums1, (0, 1)) + sums_rows(sums2, (3,))),
        jnp.stack(sums_rows(sums2, (0, 1)) + sums_rows(sums3, (3,)) + sums_rows(sums3, (0, 1)) + sums_rows(sums4, (3,))),
    ])
    dng = jnp.stack([
        jnp.stack([sums0[2], sums1[4], sums1[2], sums2[4]]),
        jnp.stack([sums2[2], sums3[4], sums3[2], sums4[4]]),
    ])
    parts = [dmod.reshape(-1), dng.reshape(-1), lat_sums[0], lat_sums[1], conv_sums[0:3].reshape(-1)]
    sizes = [p.shape[0] for p in parts]
    packed = jnp.concatenate(parts)
    pk_rows = -(-packed.shape[0] // (8 * LANES)) * 8
    packed = jnp.pad(packed, (0, pk_rows * LANES - packed.shape[0])).reshape(pk_rows, LANES)
    packed_all = _ag_small("ag_small_grads", packed).reshape(N_DEV, pk_rows, LANES)
    total = _sum_devices("sum_small_grads", packed_all).reshape(-1)
    offs = [0]
    for s in sizes:
        offs.append(offs[-1] + s)
    g_b_mod = total[offs[0]:offs[1]].reshape(depth, 6 * d)
    ng_full = total[offs[1]:offs[2]].reshape(depth, 4, d)
    g_norm_g = lax.dynamic_slice_in_dim(ng_full, dev * d_loc, d_loc, axis=2)
    g_g_q = total[offs[2]:offs[3]].reshape(1, rq)
    g_g_kv = total[offs[3]:offs[4]].reshape(1, rkv)
    cw_full = total[offs[4]:offs[5]].reshape(1, 3, d)
    g_conv_w = lax.dynamic_slice_in_dim(cw_full, dev * d_loc, d_loc, axis=2)

    dmod_all = packed_all.reshape(N_DEV, -1)[:, :sizes[0]].reshape(N_DEV, depth, 6 * d)
    dmod_cols = lax.dynamic_slice_in_dim(dmod_all, dev * n_mod_loc, n_mod_loc, axis=2)
    cond_t = cond_all.T
    mod_out = [_mod_grad_adam(f"w_mod_adam{i}", cond_t, dmod_cols[:, i, :], w_mod[i], m_w_mod[i], v_w_mod[i]) for i in range(depth)]
    g_w_mod, d_w_mod, nm_w_mod, nv_w_mod = [jnp.stack([mod_out[i][k] for i in range(depth)]) for k in range(4)]

    def rs_end(tag, started):
        send_sems, recv_sems, pair, lands, _ = started
        pair, lands = _rs_chips_wait(f"rs_wait_{tag}", pair, lands, send_sems, recv_sems, g_w_mod)
        return [_chip_sum(f"chip_sum_{tag}{a}", pair[a], lands[a], chip_idx) for a in range(len(pair))]

    red_up1, red_dn1 = rs_end("mlp1", rs_mlp1)
    red_cin, red_cout = rs_end("conv", rs_conv)
    red_up0, red_dn0 = rs_end("mlp0", rs_mlp0)
    red_win, red_uq, red_ukv, red_wo = rs_end("mla", rs_mla)

    g_mla_w_in = red_win[:, :lat_w][None]
    g_mla_w_uq = jnp.concatenate(
        [red_uq[:, :heads * NOPE_DIM].reshape(-1, heads, NOPE_DIM),
         red_uq[:, heads * NOPE_DIM:].reshape(-1, heads, LANES)[:, :, :ROPE_DIM]], axis=2)[None]
    g_mla_w_ukv = red_ukv.reshape(mla_w_ukv.shape)
    g_mla_w_o = red_wo[None]
    g_conv_w_in = red_cin[None]
    g_conv_w_out = red_cout[None]
    g_mlp_w_up = jnp.stack([red_up0, red_up1])
    g_mlp_w_down = jnp.stack([red_dn0, red_dn1])

    grads = [g_w_mod, g_b_mod, g_norm_g, g_mla_w_in, g_g_q, g_g_kv, g_mla_w_uq, g_mla_w_ukv, g_mla_w_o,
             g_conv_w_in, g_conv_w, g_conv_w_out, g_mlp_w_up, g_mlp_w_down]
    weights = [w_mod, b_mod, norm_g, mla_w_in, mla_g_q, mla_g_kv, mla_w_uq, mla_w_ukv, mla_w_o, conv_w_in, conv_w, conv_w_out, mlp_w_up, mlp_w_down]
    ms = [m_w_mod, m_b_mod, m_norm_g, m_mla_w_in, m_mla_g_q, m_mla_g_kv, m_mla_w_uq, m_mla_w_ukv, m_mla_w_o, m_conv_w_in, m_conv_w, m_conv_w_out, m_mlp_w_up, m_mlp_w_down]
    vs = [v_w_mod, v_b_mod, v_norm_g, v_mla_w_in, v_mla_g_q, v_mla_g_kv, v_mla_w_uq, v_mla_w_ukv, v_mla_w_o, v_conv_w_in, v_conv_w, v_conv_w_out, v_mlp_w_up, v_mlp_w_down]
    names = ["w_mod", "b_mod", "norm_g", "mla_w_in", "mla_g_q", "mla_g_kv", "mla_w_uq", "mla_w_ukv", "mla_w_o", "conv_w_in", "conv_w", "conv_w_out", "mlp_w_up", "mlp_w_down"]
    deltas, new_ms, new_vs = [d_w_mod], [nm_w_mod], [nv_w_mod]
    for k in range(1, len(weights)):
        dl, nm, nv = _adam(f"adam_{names[k]}", weights[k], grads[k], ms[k], vs[k])
        deltas.append(dl)
        new_ms.append(nm)
        new_vs.append(nv)

    return (loss, grad_x[None], *grads, *deltas, *new_ms, *new_vs)
```

```python
import jax
import jax.numpy as jnp
from jax import lax
from jax.experimental import pallas as pl
from jax.experimental.pallas import tpu as pltpu

F32 = jnp.float32
BF = jnp.bfloat16
MESH = pl.DeviceIdType.MESH
N_DEV = 8
LANES = 128
NORM_EPS = 1e-6
ROPE_THETA = 10000.0
ROPE_DIM = 64
NOPE_DIM = 128
V_DIM = 128
LR, B1, B2, ADAM_EPS, WD, STEP = 0.001, 0.9, 0.999, 1e-08, 0.01, 10
VMEM_LIMIT = 56 * 1024 * 1024
TILE_BYTES = 2 * 1024 * 1024
MM_TILE_PREFS = ((1024, 1024), (1024, 512), (512, 1024), (512, 512), (256, 512), (256, 256), (128, 256), (128, 128))
MM_VMEM_BUDGET = 36 * 1024 * 1024
TQ = 512
Q_CHAINS = 1
LOG2E = 1.4426950408889634

NN = (((1,), (0,)), ((), ()))
NT = (((1,), (1,)), ((), ()))
TNDIMS = (((0,), (0,)), ((), ()))
HBM_SPEC = pl.BlockSpec(memory_space=pltpu.HBM)
VMEM_SPEC = pl.BlockSpec(memory_space=pltpu.VMEM)


def _cparams(n_axes):
    return pltpu.CompilerParams(dimension_semantics=("arbitrary",) * n_axes, vmem_limit_bytes=VMEM_LIMIT)


def _pick(dim, pref):
    if dim <= pref:
        return dim
    for t in range(pref - pref % LANES, 0, -LANES):
        if t > 0 and dim % t == 0:
            return t
    for t in range(pref, 0, -1):
        if dim % t == 0:
            return t
    return dim


def _rows(r, c, itemsize=4):
    want = max(8, TILE_BYTES // (itemsize * max(c, 1)))
    if r <= want:
        return r
    for t in range(want - want % 8, 0, -8):
        if t > 0 and r % t == 0:
            return t
    return r


def _pos():
    return lax.axis_index("x"), lax.axis_index("y"), lax.axis_index("c")


def _ag_small(name, v):
    m_per, n = v.shape

    def body(x_ref, out_ref, send_sems, recv_sems, local_sem):
        x, y, c = _pos()
        me, sibling = (x, y, c), (x, y, 1 - c)
        chips = [(1 - x, y), (x, 1 - y), (1 - x, 1 - y)]

        def rows(px, py, pc):
            return out_ref.at[pl.ds((4 * px + 2 * py + pc) * m_per, m_per), :]

        def copy(k, block, to, src=None):
            return pltpu.make_async_remote_copy(
                src_ref=rows(*block) if src is None else src, dst_ref=rows(*block),
                send_sem=send_sems.at[k], recv_sem=recv_sems.at[k], device_id=to, device_id_type=MESH)

        mine = pltpu.make_async_copy(x_ref, rows(*me), local_sem)
        mine.start()
        first = [copy(0, me, sibling, src=x_ref)]
        first += [copy(1 + j, me, (*chip, c), src=x_ref) for j, chip in enumerate(chips)]
        for cp in first:
            cp.start()
        passed = [copy(4 + j, (*chip, c), sibling) for j, chip in enumerate(chips)]
        for j, chip in enumerate(chips):
            copy(1 + j, (*chip, c), me).wait_recv()
            passed[j].start()
        copy(0, sibling, me).wait_recv()
        for j, chip in enumerate(chips):
            copy(4 + j, (*chip, 1 - c), me).wait_recv()
        for cp in first + passed:
            cp.wait_send()
        mine.wait()

    return pl.pallas_call(
        body, name=name,
        out_shape=jax.ShapeDtypeStruct((N_DEV * m_per, n), v.dtype),
        in_specs=[VMEM_SPEC], out_specs=VMEM_SPEC,
        scratch_shapes=[pltpu.SemaphoreType.DMA((7,)), pltpu.SemaphoreType.DMA((7,)), pltpu.SemaphoreType.DMA],
    )(v)


def _ag_big(name, arrs):
    n = len(arrs)

    def body(*refs):
        ins, outs = refs[:n], refs[n:2 * n]
        send_sems, recv_sems, local_sems = refs[2 * n:]
        x, y, c = _pos()
        me, sibling = (x, y, c), (x, y, 1 - c)
        chips = [(1 - x, y), (x, 1 - y), (1 - x, 1 - y)]

        def copy(a, k, block, to, src=None):
            dst = outs[a].at[4 * block[0] + 2 * block[1] + block[2]]
            return pltpu.make_async_remote_copy(
                src_ref=dst if src is None else src, dst_ref=dst,
                send_sem=send_sems.at[7 * a + k], recv_sem=recv_sems.at[7 * a + k],
                device_id=to, device_id_type=MESH)

        mine = [pltpu.make_async_copy(ins[a], outs[a].at[4 * x + 2 * y + c], local_sems.at[a]) for a in range(n)]
        for cp in mine:
            cp.start()
        first = []
        for a in range(n):
            first.append(copy(a, 0, me, sibling, src=ins[a]))
            first += [copy(a, 1 + j, me, (*chip, c), src=ins[a]) for j, chip in enumerate(chips)]
        for cp in first:
            cp.start()
        passed = []
        for j, chip in enumerate(chips):
            for a in range(n):
                copy(a, 1 + j, (*chip, c), me).wait_recv()
                cp = copy(a, 4 + j, (*chip, c), sibling)
                cp.start()
                passed.append(cp)
        for a in range(n):
            copy(a, 0, sibling, me).wait_recv()
            for j, chip in enumerate(chips):
                copy(a, 4 + j, (*chip, 1 - c), me).wait_recv()
        for cp in first + passed:
            cp.wait_send()
        for cp in mine:
            cp.wait()

    return pl.pallas_call(
        body, name=name,
        out_shape=[jax.ShapeDtypeStruct((N_DEV,) + a.shape, a.dtype) for a in arrs],
        in_specs=[HBM_SPEC] * n, out_specs=[HBM_SPEC] * n,
        scratch_shapes=[pltpu.SemaphoreType.DMA((7 * n,)), pltpu.SemaphoreType.DMA((7 * n,)),
                        pltpu.SemaphoreType.DMA((n,))],
    )(*arrs)


def _rs_sibling(name, arrs):
    n = len(arrs)

    def body(*refs):
        ins, outs = refs[:n], refs[n:2 * n]
        send_sems, recv_sems = refs[2 * n:]
        x, y, c = _pos()
        copies = []
        for a in range(n):
            for j in range(4):
                copies.append(pltpu.make_async_remote_copy(
                    src_ref=ins[a].at[2 * j + (1 - c)], dst_ref=outs[a].at[j],
                    send_sem=send_sems.at[4 * a + j], recv_sem=recv_sems.at[4 * a + j],
                    device_id=(x, y, 1 - c), device_id_type=MESH))
        for cp in copies:
            cp.start()
        for cp in copies:
            cp.wait()

    return pl.pallas_call(
        body, name=name,
        out_shape=[jax.ShapeDtypeStruct((4,) + a.shape[1:], a.dtype) for a in arrs],
        in_specs=[HBM_SPEC] * n, out_specs=[HBM_SPEC] * n,
        scratch_shapes=[pltpu.SemaphoreType.DMA((4 * n,)), pltpu.SemaphoreType.DMA((4 * n,))],
    )(*arrs)


SEM_SPEC = pl.BlockSpec(memory_space=pltpu.SEMAPHORE)
ANY_SPEC = pl.BlockSpec(memory_space=pl.ANY)
EFFECT = pltpu.SideEffectType.DATAFLOW_SIDE_EFFECTING


def _hbm(a):
    return pltpu.with_memory_space_constraint(a, pltpu.HBM)


def _cast_own_block(name, w, dev_idx):
    r, cdim = w.shape
    tr = _rows(r, cdim, 4)

    def body(d_ref, w_ref, o_ref):
        o_ref[...] = w_ref[...].astype(BF)

    return pl.pallas_call(
        body, name=name,
        out_shape=jax.ShapeDtypeStruct((N_DEV, r, cdim), BF),
        grid_spec=pltpu.PrefetchScalarGridSpec(
            num_scalar_prefetch=1, grid=(r // tr,),
            in_specs=[pl.BlockSpec((tr, cdim), lambda i, d_ref: (i, 0))],
            out_specs=pl.BlockSpec((None, tr, cdim), lambda i, d_ref: (d_ref[0], i, 0))),
        compiler_params=_cparams(1),
    )(dev_idx, w)


def _ag_start(name, lands, groups, after):
    n = len(lands)
    ng = len(groups)
    n_after = len(after)

    def body(*refs):
        lnd = refs[:n]
        sems = refs[n + n_after:n + n_after + 2 * ng]
        token = refs[-1]
        x, y, c = _pos()
        targets = [(x, y, 1 - c), (1 - x, y, c), (x, 1 - y, c), (1 - x, 1 - y, c)]
        for gi, members in enumerate(groups):
            for pos, a in enumerate(members):
                own = lnd[a].at[4 * x + 2 * y + c]
                for k, to in enumerate(targets):
                    pltpu.make_async_remote_copy(
                        src_ref=own, dst_ref=own,
                        send_sem=sems[2 * gi].at[4 * pos + k], recv_sem=sems[2 * gi + 1].at[4 * pos + k],
                        device_id=to, device_id_type=MESH).start()
        token[...] = jnp.zeros_like(token)

    sem_shapes = []
    for members in groups:
        sem_shapes += [pltpu.SemaphoreType.DMA((4 * len(members),))] * 2
    res = pl.pallas_call(
        body, name=name,
        out_shape=(*sem_shapes, *[pltpu.HBM(a.shape, a.dtype) for a in lands], jax.ShapeDtypeStruct((8, LANES), F32)),
        in_specs=[HBM_SPEC] * n + [ANY_SPEC] * n_after,
        out_specs=(*[SEM_SPEC] * (2 * ng), *[HBM_SPEC] * n, VMEM_SPEC),
        input_output_aliases={i: 2 * ng + i for i in range(n)},
        compiler_params=pltpu.CompilerParams(has_side_effects=EFFECT),
    )(*[_hbm(a) for a in lands], *after)
    sem_pairs = [(res[2 * g], res[2 * g + 1]) for g in range(ng)]
    return sem_pairs, list(res[2 * ng:2 * ng + n]), res[-1]


def _ag_forward(name, lands, send_sems, recv_sems, after):
    n = len(lands)

    def body(*refs):
        lnd = refs[:n]
        s_sems, r_sems = refs[n], refs[n + 1]
        fs_sems, fr_sems = refs[n + 3], refs[n + 4]
        token = refs[-1]
        x, y, c = _pos()
        sources = [(x, y, 1 - c), (1 - x, y, c), (x, 1 - y, c), (1 - x, 1 - y, c)]
        for a in range(n):
            own = lnd[a].at[4 * x + 2 * y + c]
            for k, src in enumerate(sources):
                blk = lnd[a].at[4 * src[0] + 2 * src[1] + src[2]]
                arrived = pltpu.make_async_remote_copy(
                    src_ref=own, dst_ref=blk, send_sem=s_sems.at[4 * a + k], recv_sem=r_sems.at[4 * a + k],
                    device_id=src, device_id_type=MESH)
                arrived.wait_send()
                arrived.wait_recv()
                if k > 0:
                    pltpu.make_async_remote_copy(
                        src_ref=blk, dst_ref=blk, send_sem=fs_sems.at[3 * a + k - 1], recv_sem=fr_sems.at[3 * a + k - 1],
                        device_id=(x, y, 1 - c), device_id_type=MESH).start()
        token[...] = jnp.zeros_like(token)

    res = pl.pallas_call(
        body, name=name,
        out_shape=(pltpu.SemaphoreType.DMA((3 * n,)), pltpu.SemaphoreType.DMA((3 * n,)),
                   *[pltpu.HBM(a.shape, a.dtype) for a in lands], jax.ShapeDtypeStruct((8, LANES), F32)),
        in_specs=[HBM_SPEC] * n + [SEM_SPEC, SEM_SPEC, ANY_SPEC],
        out_specs=(SEM_SPEC, SEM_SPEC, *[HBM_SPEC] * n, VMEM_SPEC),
        input_output_aliases={i: 2 + i for i in range(n)},
        compiler_params=pltpu.CompilerParams(has_side_effects=EFFECT),
    )(*lands, send_sems, recv_sems, after)
    return res[0], res[1], list(res[2:2 + n]), res[-1]


def _ag_finish(name, lands, fs_sems, fr_sems, after):
    n = len(lands)

    def body(*refs):
        lnd = refs[:n]
        fs, fr = refs[n], refs[n + 1]
        x, y, c = _pos()
        chips = [(1 - x, y), (x, 1 - y), (1 - x, 1 - y)]
        for a in range(n):
            for j, chip in enumerate(chips):
                mine = lnd[a].at[4 * chip[0] + 2 * chip[1] + c]
                theirs = lnd[a].at[4 * chip[0] + 2 * chip[1] + 1 - c]
                cp = pltpu.make_async_remote_copy(
                    src_ref=mine, dst_ref=theirs, send_sem=fs.at[3 * a + j], recv_sem=fr.at[3 * a + j],
                    device_id=(x, y, 1 - c), device_id_type=MESH)
                cp.wait_send()
                cp.wait_recv()

    res = pl.pallas_call(
        body, name=name,
        out_shape=tuple(pltpu.HBM(a.shape, a.dtype) for a in lands),
        in_specs=[HBM_SPEC] * n + [SEM_SPEC, SEM_SPEC, ANY_SPEC],
        out_specs=tuple([HBM_SPEC] * n),
        input_output_aliases={i: i for i in range(n)},
        compiler_params=pltpu.CompilerParams(has_side_effects=EFFECT),
    )(*lands, fs_sems, fr_sems, after)
    return list(res)


def _rs_chips_start(name, arrs):
    n = len(arrs)

    def body(*refs):
        ins, lnd = refs[:n], refs[n:2 * n]
        s_sems, r_sems = refs[2 * n], refs[2 * n + 1]
        token = refs[-1]
        x, y, c = _pos()
        chips = [(1 - x, y), (x, 1 - y), (1 - x, 1 - y)]
        for a in range(n):
            for k, chip in enumerate(chips):
                pltpu.make_async_remote_copy(
                    src_ref=ins[a].at[2 * chip[0] + chip[1]], dst_ref=lnd[a].at[k],
                    send_sem=s_sems.at[3 * a + k], recv_sem=r_sems.at[3 * a + k],
                    device_id=(*chip, c), device_id_type=MESH).start()
        token[...] = jnp.zeros_like(token)

    lands = [lax.empty((3,) + a.shape[1:], a.dtype) for a in arrs]
    res = pl.pallas_call(
        body, name=name,
        out_shape=(pltpu.SemaphoreType.DMA((3 * n,)), pltpu.SemaphoreType.DMA((3 * n,)),
                   *[pltpu.HBM(a.shape, a.dtype) for a in arrs], *[pltpu.HBM(a.shape, a.dtype) for a in lands],
                   jax.ShapeDtypeStruct((8, LANES), F32)),
        in_specs=[HBM_SPEC] * (2 * n),
        out_specs=(SEM_SPEC, SEM_SPEC, *[HBM_SPEC] * (2 * n), VMEM_SPEC),
        input_output_aliases={i: 2 + i for i in range(2 * n)},
        compiler_params=pltpu.CompilerParams(has_side_effects=EFFECT),
    )(*[_hbm(a) for a in arrs], *[_hbm(a) for a in lands])
    return res[0], res[1], list(res[2:2 + n]), list(res[2 + n:2 + 2 * n]), res[-1]


def _rs_chips_wait(name, arrs, lands, send_sems, recv_sems, after):
    n = len(arrs)

    def body(*refs):
        ins, lnd = refs[:n], refs[n:2 * n]
        s_sems, r_sems = refs[2 * n], refs[2 * n + 1]
        x, y, c = _pos()
        chips = [(1 - x, y), (x, 1 - y), (1 - x, 1 - y)]
        for a in range(n):
            for k, chip in enumerate(chips):
                cp = pltpu.make_async_remote_copy(
                    src_ref=ins[a].at[2 * chip[0] + chip[1]], dst_ref=lnd[a].at[k],
                    send_sem=s_sems.at[3 * a + k], recv_sem=r_sems.at[3 * a + k],
                    device_id=(*chip, c), device_id_type=MESH)
                cp.wait_send()
                cp.wait_recv()

    res = pl.pallas_call(
        body, name=name,
        out_shape=tuple(pltpu.HBM(a.shape, a.dtype) for a in list(arrs) + list(lands)),
        in_specs=[HBM_SPEC] * (2 * n) + [SEM_SPEC, SEM_SPEC, ANY_SPEC],
        out_specs=tuple([HBM_SPEC] * (2 * n)),
        input_output_aliases={i: i for i in range(2 * n)},
        compiler_params=pltpu.CompilerParams(has_side_effects=EFFECT),
    )(*arrs, *lands, send_sems, recv_sems, after)
    return list(res[:n]), list(res[n:])


def _pair_add(name, g, r1, c_idx):
    _, r, cdim = g.shape
    tr = _rows(r, cdim, 2)

    def body(c_ref, g_ref, r_ref, o_ref):
        o_ref[...] = (g_ref[...].astype(F32) + r_ref[...].astype(F32)).astype(o_ref.dtype)

    return pl.pallas_call(
        body, name=name,
        out_shape=jax.ShapeDtypeStruct((4, r, cdim), BF),
        grid_spec=pltpu.PrefetchScalarGridSpec(
            num_scalar_prefetch=1, grid=(4, r // tr),
            in_specs=[pl.BlockSpec((None, tr, cdim), lambda j, i, c_ref: (2 * j + c_ref[0], i, 0)),
                      pl.BlockSpec((None, tr, cdim), lambda j, i, c_ref: (j, i, 0))],
            out_specs=pl.BlockSpec((None, tr, cdim), lambda j, i, c_ref: (j, i, 0))),
        compiler_params=_cparams(2),
    )(c_idx, g, r1)


def _chip_sum(name, p, r2, chip_idx):
    _, r, cdim = p.shape
    tr = _rows(r, cdim, 4)

    def body(s_ref, p_ref, a_ref, b_ref, c_ref, o_ref):
        o_ref[...] = ((p_ref[...].astype(F32) + a_ref[...].astype(F32)) + b_ref[...].astype(F32)) + c_ref[...].astype(F32)

    def other(k):
        return pl.BlockSpec((None, tr, cdim), lambda i, s_ref: (k, i, 0))

    return pl.pallas_call(
        body, name=name,
        out_shape=jax.ShapeDtypeStruct((r, cdim), F32),
        grid_spec=pltpu.PrefetchScalarGridSpec(
            num_scalar_prefetch=1, grid=(r // tr,),
            in_specs=[pl.BlockSpec((None, tr, cdim), lambda i, s_ref: (s_ref[0], i, 0)), other(0), other(1), other(2)],
            out_specs=pl.BlockSpec((tr, cdim), lambda i, s_ref: (i, 0))),
        compiler_params=_cparams(1),
    )(chip_idx, p, r2, r2, r2)


def _adam_math(w, g, m, v):
    m = B1 * m + (1.0 - B1) * g
    v = B2 * v + (1.0 - B2) * (g * g)
    m_hat = m / (1.0 - B1 ** STEP)
    v_hat = v / (1.0 - B2 ** STEP)
    delta = -LR * (m_hat / (jnp.sqrt(v_hat) + ADAM_EPS) + WD * w)
    return delta, m, v


def _adam(name, w, g, m, v):
    shape = w.shape
    cdim = shape[-1]
    r = w.size // cdim
    flat = [a.reshape(r, cdim) for a in (w, g, m, v)]
    tr = _rows(r, cdim, 4)

    def body(w_ref, g_ref, m_ref, v_ref, d_ref, mo_ref, vo_ref):
        d, mn, vn = _adam_math(w_ref[...], g_ref[...], m_ref[...], v_ref[...])
        d_ref[...] = d
        mo_ref[...] = mn
        vo_ref[...] = vn

    spec = pl.BlockSpec((tr, cdim), lambda i: (i, 0))
    outs = pl.pallas_call(
        body, name=name,
        out_shape=[jax.ShapeDtypeStruct((r, cdim), F32)] * 3,
        grid=(r // tr,), in_specs=[spec] * 4, out_specs=[spec] * 3,
        compiler_params=_cparams(1),
    )(*flat)
    return [o.reshape(shape) for o in outs]


def _mod_matvec(name, cond_all, w_loc):
    d, n_loc = w_loc.shape
    tn = _pick(n_loc, 512)

    def body(c_ref, w_ref, o_ref):
        o_ref[...] = jnp.dot(c_ref[...].astype(BF), w_ref[...].astype(BF), preferred_element_type=F32)

    return pl.pallas_call(
        body, name=name,
        out_shape=jax.ShapeDtypeStruct((N_DEV, n_loc), F32),
        grid=(n_loc // tn,),
        in_specs=[pl.BlockSpec((N_DEV, d), lambda j: (0, 0)), pl.BlockSpec((d, tn), lambda j: (0, j))],
        out_specs=pl.BlockSpec((N_DEV, tn), lambda j: (0, j)),
        compiler_params=_cparams(1),
    )(cond_all, w_loc)


def _mod_grad_adam(name, cond_t, dmod, w, m, v):
    d, n_loc = w.shape
    tr = _rows(d, n_loc, 4)

    def body(ct_ref, dm_ref, w_ref, m_ref, v_ref, g_ref, d_ref, mo_ref, vo_ref):
        g = ct_ref[:, 0:1] * dm_ref[0:1, :]
        for b in range(1, N_DEV):
            g = g + ct_ref[:, b:b + 1] * dm_ref[b:b + 1, :]
        dl, mn, vn = _adam_math(w_ref[...], g, m_ref[...], v_ref[...])
        g_ref[...] = g
        d_ref[...] = dl
        mo_ref[...] = mn
        vo_ref[...] = vn

    spec = pl.BlockSpec((tr, n_loc), lambda i: (i, 0))
    return pl.pallas_call(
        body, name=name,
        out_shape=[jax.ShapeDtypeStruct((d, n_loc), F32)] * 4,
        grid=(d // tr,),
        in_specs=[pl.BlockSpec((tr, N_DEV), lambda i: (i, 0)), pl.BlockSpec((N_DEV, n_loc), lambda i: (0, 0)), spec, spec, spec],
        out_specs=[spec] * 4,
        compiler_params=_cparams(1),
    )(cond_t, dmod, w, m, v)


def _sum_devices(name, gathered):
    _, m, n = gathered.shape

    def body(g_ref, o_ref):
        acc = g_ref[0]
        for d in range(1, N_DEV):
            acc = acc + g_ref[d]
        o_ref[...] = acc

    return pl.pallas_call(
        body, name=name, out_shape=jax.ShapeDtypeStruct((m, n), F32),
        in_specs=[VMEM_SPEC], out_specs=VMEM_SPEC,
    )(gathered)


def _silu(name, c_pad):
    def body(c_ref, o_ref):
        c = c_ref[...]
        o_ref[...] = c * (1.0 / (1.0 + jnp.exp(-c)))

    return pl.pallas_call(body, name=name, out_shape=jax.ShapeDtypeStruct(c_pad.shape, F32),
                          in_specs=[VMEM_SPEC], out_specs=VMEM_SPEC)(c_pad)


def _mm_tiles(m, n_cap, k, bytes_per_out_elem):
    for pm, pn in MM_TILE_PREFS:
        tm, tn = _pick(m, pm), _pick(n_cap, pn)
        need = 2 * (2 * tm * k + 2 * k * tn + bytes_per_out_elem * tm * tn) + 4 * tm * tn
        if need <= MM_VMEM_BUDGET:
            return tm, tn
    return _pick(m, 8), _pick(n_cap, LANES)


def _mm(name, a, b, grid, a_spec, b_spec, outs, dot, extras=(), epilogue=None, token=None):
    n_ex, n_out = len(extras), len(outs)

    def body(*refs):
        ex_refs = refs[2:2 + n_ex]
        out_refs = refs[2 + n_ex:2 + n_ex + n_out]
        acc = dot(refs[0], refs[1])
        vals = (acc,) if epilogue is None else epilogue(acc, *[e[...] for e in ex_refs])
        for o_ref, val in zip(out_refs, vals):
            o_ref[...] = val.astype(o_ref.dtype)

    ins = [a, b] + [e for e, _ in extras]
    in_specs = [a_spec, b_spec] + [s for _, s in extras]
    if token is not None:
        ins.append(token)
        in_specs.append(pl.BlockSpec(token.shape, lambda i, j: (0, 0)))
    return pl.pallas_call(
        body, name=name,
        out_shape=[jax.ShapeDtypeStruct(s, d) for s, d, _ in outs],
        grid=grid, in_specs=in_specs, out_specs=[s for _, _, s in outs],
        compiler_params=_cparams(2),
    )(*ins)


def _dot(dims):
    return lambda a_ref, b_ref: lax.dot_general(a_ref[...], b_ref[...], dims, preferred_element_type=F32)


def _w_cols(w):
    return w.shape[1] if w.ndim == 2 else w.shape[0] * w.shape[2]


def _w_rows(w):
    return w.shape[0] if w.ndim == 2 else w.shape[1]


def _out_bytes(out_dtypes, extras):
    return sum(jnp.dtype(d).itemsize for d in out_dtypes) + sum(e.dtype.itemsize for e in extras)


def _linear(name, a, w, out_dtypes=(F32,), epilogue=None, extras=(), token=None):
    t, kdim = a.shape
    n = _w_cols(w)
    tm, tn = _mm_tiles(t, w.shape[-1], kdim, _out_bytes(out_dtypes, extras))
    if w.ndim == 2:
        w_spec = pl.BlockSpec((kdim, tn), lambda i, j: (0, j))
    else:
        per = w.shape[2] // tn
        w_spec = pl.BlockSpec((None, kdim, tn), lambda i, j: (j // per, 0, j % per))
    o_spec = pl.BlockSpec((tm, tn), lambda i, j: (i, j))
    return _mm(name, a, w, (t // tm, n // tn), pl.BlockSpec((tm, kdim), lambda i, j: (i, 0)), w_spec,
               [((t, n), dt, o_spec) for dt in out_dtypes], _dot(NN),
               extras=[(e, o_spec) for e in extras], epilogue=epilogue, token=token)


def _linear_t(name, dy, w, out_dtypes=(F32,), epilogue=None, extras=(), token=None):
    t, n = dy.shape
    kdim = _w_rows(w)
    tm, tn = _mm_tiles(t, kdim, n, _out_bytes(out_dtypes, extras))
    o_spec = pl.BlockSpec((tm, tn), lambda i, j: (i, j))
    if w.ndim == 2:
        w_spec = pl.BlockSpec((tn, n), lambda i, j: (j, 0))
        dot = _dot(NT)
    else:
        n_loc = w.shape[2]
        w_spec = pl.BlockSpec((N_DEV, tn, n_loc), lambda i, j: (0, j, 0))

        def dot(a_ref, w_ref):
            acc = lax.dot_general(a_ref[:, 0:n_loc], w_ref[0], NT, preferred_element_type=F32)
            for dev in range(1, N_DEV):
                acc = acc + lax.dot_general(a_ref[:, dev * n_loc:(dev + 1) * n_loc], w_ref[dev], NT, preferred_element_type=F32)
            return acc

    return _mm(name, dy, w, (t // tm, kdim // tn), pl.BlockSpec((tm, n), lambda i, j: (i, 0)), w_spec,
               [((t, kdim), dt, o_spec) for dt in out_dtypes], dot,
               extras=[(e, o_spec) for e in extras], epilogue=epilogue, token=token)


def _wgrad(name, a, dy, like):
    t, kdim = a.shape
    n = dy.shape[1]
    tm, tn = _mm_tiles(kdim, like.shape[-1], t, 2)
    if like.ndim == 2:
        o_spec = pl.BlockSpec((tm, tn), lambda i, j: (i, j))
    else:
        per = like.shape[2] // tn
        o_spec = pl.BlockSpec((None, tm, tn), lambda i, j: (j // per, i, j % per))
    return _mm(name, a, dy, (kdim // tm, n // tn), pl.BlockSpec((t, tm), lambda i, j: (0, i)),
               pl.BlockSpec((t, tn), lambda i, j: (0, j)), [(like.shape, BF, o_spec)], _dot(TNDIMS))[0]


def _rstd(x):
    return lax.rsqrt(jnp.mean(x * x, axis=-1, keepdims=True) + NORM_EPS)


def _colsum(x):
    return jnp.sum(x, axis=0, keepdims=True)


def _norm_fwd(name, x, y, vec, target, has_post, has_pre, has_loss):
    t, d = x.shape
    tm = _rows(t, d, 4)
    ins, in_specs = [x], [pl.BlockSpec((tm, d), lambda i: (i, 0))]
    row_spec = pl.BlockSpec((tm, d), lambda i: (i, 0))
    if has_post:
        ins.append(y)
        in_specs.append(row_spec)
    ins.append(vec)
    in_specs.append(pl.BlockSpec((8, d), lambda i: (0, 0)))
    if has_loss:
        ins.append(target)
        in_specs.append(row_spec)
    out_shape, out_specs = [], []
    if has_post and not has_loss:
        out_shape.append(jax.ShapeDtypeStruct((t, d), F32))
        out_specs.append(row_spec)
    if has_pre:
        out_shape.append(jax.ShapeDtypeStruct((t, d), BF))
        out_specs.append(row_spec)
    if has_loss:
        out_shape += [jax.ShapeDtypeStruct((t, d), F32), jax.ShapeDtypeStruct((8, d), F32)]
        out_specs += [row_spec, pl.BlockSpec((8, d), lambda i: (0, 0))]

    def body(*refs):
        it = iter(refs)
        x_ref = next(it)
        y_ref = next(it) if has_post else None
        vec_ref = next(it)
        tgt_ref = next(it) if has_loss else None
        xv = x_ref[...]
        if has_post:
            yv = y_ref[...]
            xv = xv + vec_ref[1:2, :] * ((yv * _rstd(yv)) * vec_ref[0:1, :])
            if not has_loss:
                next(it)[...] = xv
        if has_pre:
            hv = ((xv * _rstd(xv)) * vec_ref[2:3, :]) * (1.0 + vec_ref[3:4, :]) + vec_ref[4:5, :]
            next(it)[...] = hv.astype(BF)
        if has_loss:
            e = xv - tgt_ref[...]
            next(it)[...] = e * (1.0 / d)
            acc_ref = next(it)
            i = pl.program_id(0)

            @pl.when(i == 0)
            def _():
                acc_ref[...] = jnp.zeros_like(acc_ref)

            acc_ref[0:1, :] += _colsum(e * e)

    return pl.pallas_call(
        body, name=name, out_shape=out_shape, grid=(t // tm,), in_specs=in_specs, out_specs=out_specs,
        compiler_params=_cparams(1),
    )(*ins)


def _norm_bwd(name, dx_in, dh, x, y, vec, has_pre, has_post):
    t, d = dx_in.shape
    tm = _rows(t, d, 4)
    row_spec = pl.BlockSpec((tm, d), lambda i: (i, 0))
    vec_spec = pl.BlockSpec((8, d), lambda i: (0, 0))
    ins, in_specs = [dx_in], [row_spec]
    if has_pre:
        ins += [dh, x]
        in_specs += [row_spec, row_spec]
    if has_post:
        ins.append(y)
        in_specs.append(row_spec)
    ins.append(vec)
    in_specs.append(vec_spec)
    out_shape, out_specs = [], []
    if has_pre:
        out_shape.append(jax.ShapeDtypeStruct((t, d), F32))
        out_specs.append(row_spec)
    if has_post:
        out_shape.append(jax.ShapeDtypeStruct((t, d), BF))
        out_specs.append(row_spec)
    out_shape.append(jax.ShapeDtypeStruct((8, d), F32))
    out_specs.append(vec_spec)

    def body(*refs):
        it = iter(refs)
        dx = next(it)[...]
        dh_ref = next(it) if has_pre else None
        x_ref = next(it) if has_pre else None
        y_ref = next(it) if has_post else None
        vec_ref = next(it)
        dx_ref = next(it) if has_pre else None
        dy_ref = next(it) if has_post else None
        sums_ref = next(it)
        i = pl.program_id(0)

        @pl.when(i == 0)
        def _():
            sums_ref[...] = jnp.zeros_like(sums_ref)

        if has_pre:
            dhv, xv = dh_ref[...], x_ref[...]
            rs = _rstd(xv)
            xhat = xv * rs
            na = vec_ref[2:3, :]
            sums_ref[0:1, :] += _colsum(dhv)
            sums_ref[1:2, :] += _colsum(dhv * (xhat * na))
            tt = dhv * (1.0 + vec_ref[3:4, :])
            sums_ref[2:3, :] += _colsum(tt * xhat)
            u = tt * na
            dx = dx + rs * (u - xhat * jnp.mean(u * xhat, axis=-1, keepdims=True))
            dx_ref[...] = dx
        if has_post:
            yv = y_ref[...]
            rs = _rstd(yv)
            yhat = yv * rs
            nb = vec_ref[0:1, :]
            sums_ref[3:4, :] += _colsum(dx * (yhat * nb))
            tt = dx * vec_ref[1:2, :]
            sums_ref[4:5, :] += _colsum(tt * yhat)
            u = tt * nb
            dy_ref[...] = (rs * (u - yhat * jnp.mean(u * yhat, axis=-1, keepdims=True))).astype(BF)

    return pl.pallas_call(
        body, name=name, out_shape=out_shape, grid=(t // tm,), in_specs=in_specs, out_specs=out_specs,
        compiler_params=_cparams(1),
    )(*ins)


def _rope128(tv, cs, s1, s2):
    return tv * cs + pltpu.roll(tv, 96, 1) * s1 + pltpu.roll(tv, 32, 1) * s2


def _lat_post(name, lat, gq, gkv, cs, s1, s2, rq, rkv):
    t, w = lat.shape
    tm = _rows(t, w, 4)

    def body(lat_ref, gq_ref, gkv_ref, cs_ref, s1_ref, s2_ref, cq_ref, ckv_ref, kr_ref):
        lq = lat_ref[:, 0:rq]
        lkv = lat_ref[:, rq:rq + rkv]
        cq_ref[...] = ((lq * _rstd(lq)) * gq_ref[...]).astype(BF)
        ckv_ref[...] = ((lkv * _rstd(lkv)) * gkv_ref[...]).astype(BF)
        kr_ref[...] = _rope128(lat_ref[:, rq + rkv:rq + rkv + LANES], cs_ref[...], s1_ref[...], s2_ref[...]).astype(BF)

    def rows(c):
        return pl.BlockSpec((tm, c), lambda i: (i, 0))

    def vecs(c):
        return pl.BlockSpec((1, c), lambda i: (0, 0))

    return pl.pallas_call(
        body, name=name,
        out_shape=[jax.ShapeDtypeStruct((t, rq), BF), jax.ShapeDtypeStruct((t, rkv), BF), jax.ShapeDtypeStruct((t, LANES), BF)],
        grid=(t // tm,),
        in_specs=[rows(w), vecs(rq), vecs(rkv), rows(LANES), rows(LANES), rows(LANES)],
        out_specs=[rows(rq), rows(rkv), rows(LANES)],
        compiler_params=_cparams(1),
    )(lat, gq, gkv, cs, s1, s2)


def _lat_bwd(name, dcq, dckv, dkr, lat, gq, gkv, cs, s1, s2, rq, rkv, heads):
    t, w = lat.shape
    tm = _rows(t, max(w, heads * LANES), 4)
    assert rq == rkv

    def body(dcq_ref, dckv_ref, dkr_ref, lat_ref, gq_ref, gkv_ref, cs_ref, s1_ref, s2_ref, dlat_ref, sums_ref):
        i = pl.program_id(0)

        @pl.when(i == 0)
        def _():
            sums_ref[...] = jnp.zeros_like(sums_ref)

        def rms_bwd(dc, lv, g, row):
            rs = _rstd(lv)
            lhat = lv * rs
            sums_ref[row:row + 1, :] += _colsum(dc * lhat)
            u = dc * g
            return rs * (u - lhat * jnp.mean(u * lhat, axis=-1, keepdims=True))

        dlat_ref[:, 0:rq] = rms_bwd(dcq_ref[...], lat_ref[:, 0:rq], gq_ref[...], 0).astype(BF)
        dlat_ref[:, rq:rq + rkv] = rms_bwd(dckv_ref[...], lat_ref[:, rq:rq + rkv], gkv_ref[...], 1).astype(BF)
        dk = dkr_ref[:, 0:LANES]
        for h in range(1, heads):
            dk = dk + dkr_ref[:, h * LANES:(h + 1) * LANES]
        dlat_ref[:, rq + rkv:rq + rkv + LANES] = _rope128(dk, cs_ref[...], s1_ref[...], s2_ref[...]).astype(BF)

    def rows(c):
        return pl.BlockSpec((tm, c), lambda i: (i, 0))

    def vecs(c):
        return pl.BlockSpec((1, c), lambda i: (0, 0))

    return pl.pallas_call(
        body, name=name,
        out_shape=[jax.ShapeDtypeStruct((t, w), BF), jax.ShapeDtypeStruct((8, rq), F32)],
        grid=(t // tm,),
        in_specs=[rows(rq), rows(rkv), rows(heads * LANES), rows(w), vecs(rq), vecs(rkv), rows(LANES), rows(LANES), rows(LANES)],
        out_specs=[rows(w), pl.BlockSpec((8, rq), lambda i: (0, 0))],
        compiler_params=_cparams(1),
    )(dcq, dckv, dkr, lat, gq, gkv, cs, s1, s2)


def _rope_heads(name, xa, a_blk, xb, b_blk, cs, s1, s2, heads):
    t = xa.shape[0]
    hw = heads * LANES
    tm = _rows(t, 2 * hw, 4)

    def body(a_ref, b_ref, cs_ref, s1_ref, s2_ref, o_ref):
        o_ref[:, 0:hw] = a_ref[...].astype(BF)
        csv, s1v, s2v = cs_ref[...], s1_ref[...], s2_ref[...]
        for h in range(heads):
            sl = slice(h * LANES, (h + 1) * LANES)
            o_ref[:, hw + h * LANES:hw + (h + 1) * LANES] = _rope128(b_ref[:, sl], csv, s1v, s2v).astype(BF)

    def rows(c):
        return pl.BlockSpec((tm, c), lambda i: (i, 0))

    return pl.pallas_call(
        body, name=name,
        out_shape=jax.ShapeDtypeStruct((t, 2 * hw), BF),
        grid=(t // tm,),
        in_specs=[pl.BlockSpec((tm, hw), lambda i: (i, a_blk)), pl.BlockSpec((tm, hw), lambda i: (i, b_blk)),
                  rows(LANES), rows(LANES), rows(LANES)],
        out_specs=rows(2 * hw),
        compiler_params=_cparams(1),
    )(xa, xb, cs, s1, s2)


def _causal_mask(s, row0):
    row = lax.broadcasted_iota(jnp.int32, s.shape, 0) + row0
    col = lax.broadcasted_iota(jnp.int32, s.shape, 1)
    return col <= row


def _flash_fwd(name, qb, kv, kr, heads, scale):
    t = qb.shape[0]
    tq = _pick(t, TQ)
    nq = t // tq
    sub = tq // Q_CHAINS
    c2 = scale * LOG2E

    def body(qn_ref, qr_ref, kv_ref, kr_ref, o_ref, lse_ref):
        i = pl.program_id(1)
        qs = [jnp.concatenate([qn_ref[a * sub:(a + 1) * sub, :], qr_ref[a * sub:(a + 1) * sub, :]], axis=-1)
              for a in range(Q_CHAINS)]

        def block(j, carry, masked):
            rows = pl.ds(pl.multiple_of(j * tq, tq), tq)
            k = jnp.concatenate([kv_ref[rows, 0:LANES], kr_ref[rows, :]], axis=-1)
            v = kv_ref[rows, LANES:2 * LANES]
            out = []
            for a in range(Q_CHAINS):
                m_prev, l_prev, acc = carry[a]
                s = lax.dot_general(qs[a], k, NT, preferred_element_type=F32) * c2
                if masked:
                    s = jnp.where(_causal_mask(s, a * sub), s, -1e30)
                m_new = jnp.maximum(m_prev, jnp.max(s, axis=-1, keepdims=True))
                alpha = jnp.exp2(m_prev - m_new)
                p = jnp.exp2(s - m_new)
                l_new = alpha * l_prev + jnp.sum(p, axis=-1, keepdims=True)
                out.append((m_new, l_new, alpha * acc + jnp.dot(p.astype(BF), v, preferred_element_type=F32)))
            return tuple(out)

        init = tuple((jnp.full((sub, 1), -1e30, F32), jnp.zeros((sub, 1), F32), jnp.zeros((sub, V_DIM), F32))
                     for _ in range(Q_CHAINS))
        carry = lax.fori_loop(0, i, lambda j, cr: block(j, cr, False), init)
        fin = block(i, carry, True)
        for a in range(Q_CHAINS):
            m_fin, l_fin, acc = fin[a]
            o_ref[a * sub:(a + 1) * sub, :] = (acc / l_fin).astype(BF)
            lse_ref[a * sub:(a + 1) * sub, :] = jnp.broadcast_to(m_fin + jnp.log2(l_fin), (sub, LANES))

    return pl.pallas_call(
        body, name=name,
        out_shape=[jax.ShapeDtypeStruct((t, heads * V_DIM), BF), jax.ShapeDtypeStruct((t, heads * LANES), F32)],
        grid=(heads, nq),
        in_specs=[pl.BlockSpec((tq, LANES), lambda h, i: (i, h)),
                  pl.BlockSpec((tq, LANES), lambda h, i: (i, heads + h)),
                  pl.BlockSpec((t, 2 * LANES), lambda h, i: (0, h)),
                  pl.BlockSpec((t, LANES), lambda h, i: (0, 0))],
        out_specs=[pl.BlockSpec((tq, V_DIM), lambda h, i: (i, h)), pl.BlockSpec((tq, LANES), lambda h, i: (i, h))],
        compiler_params=_cparams(2),
    )(qb, qb, kv, kr)


def _flash_bwd(name, qb, kv, kr, o, lse, do, heads, scale):
    t = qb.shape[0]
    tq = _pick(t, TQ)
    nq = t // tq

    c2 = scale * LOG2E

    def body(qn_ref, qr_ref, kv_ref, kr_ref, o_ref, lse_ref, do_ref, dkv_ref, dkr_ref, dqn_ref, dqr_ref):
        j = pl.program_id(1)

        @pl.when(j == 0)
        def _():
            dqn_ref[...] = jnp.zeros_like(dqn_ref)
            dqr_ref[...] = jnp.zeros_like(dqr_ref)

        k = jnp.concatenate([kv_ref[:, 0:LANES], kr_ref[...]], axis=-1)
        v = kv_ref[:, LANES:2 * LANES]

        def block(i, carry, masked):
            dk, dv = carry
            rows = pl.ds(pl.multiple_of(i * tq, tq), tq)
            q = jnp.concatenate([qn_ref[rows, :], qr_ref[rows, :]], axis=-1)
            dov = do_ref[rows, :]
            delta = jnp.sum(dov.astype(F32) * o_ref[rows, :].astype(F32), axis=-1, keepdims=True)
            s = lax.dot_general(q, k, NT, preferred_element_type=F32) * c2
            p = jnp.exp2(s - lse_ref[rows, 0:1])
            if masked:
                p = jnp.where(_causal_mask(s, 0), p, 0.0)
            dv = dv + lax.dot_general(p.astype(BF), dov, TNDIMS, preferred_element_type=F32)
            dp = lax.dot_general(dov, v, NT, preferred_element_type=F32)
            ds = (p * (dp - delta) * scale).astype(BF)
            dk = dk + lax.dot_general(ds, q, TNDIMS, preferred_element_type=F32)
            dq = jnp.dot(ds, k, preferred_element_type=F32)
            dqn_ref[rows, :] += dq[:, 0:LANES]
            dqr_ref[rows, :] += dq[:, LANES:2 * LANES]
            return dk, dv

        carry = block(j, (jnp.zeros((tq, 2 * LANES), F32), jnp.zeros((tq, V_DIM), F32)), True)
        dk, dv = lax.fori_loop(j + 1, nq, lambda i, cr: block(i, cr, False), carry)
        dkv_ref[:, 0:LANES] = dk[:, 0:LANES].astype(BF)
        dkv_ref[:, LANES:2 * LANES] = dv.astype(BF)
        dkr_ref[...] = dk[:, LANES:2 * LANES]

    def full(hoff):
        return pl.BlockSpec((t, LANES), lambda h, j: (0, hoff + h))

    return pl.pallas_call(
        body, name=name,
        out_shape=[jax.ShapeDtypeStruct((t, heads * 2 * LANES), BF), jax.ShapeDtypeStruct((t, heads * LANES), F32),
                   jax.ShapeDtypeStruct((t, heads * LANES), F32), jax.ShapeDtypeStruct((t, heads * LANES), F32)],
        grid=(heads, nq),
        in_specs=[full(0), full(heads), pl.BlockSpec((tq, 2 * LANES), lambda h, j: (j, h)),
                  pl.BlockSpec((tq, LANES), lambda h, j: (j, 0)), full(0), full(0), full(0)],
        out_specs=[pl.BlockSpec((tq, 2 * LANES), lambda h, j: (j, h)), pl.BlockSpec((tq, LANES), lambda h, j: (j, h)),
                   full(0), full(0)],
        compiler_params=_cparams(2),
    )(qb, qb, kv, kr, o, lse, do)


def _shift_down(z, n):
    row = lax.broadcasted_iota(jnp.int32, z.shape, 0)
    return jnp.where(row >= n, pltpu.roll(z, n, 0), 0.0)


def _shift_up(z, n):
    t = z.shape[0]
    row = lax.broadcasted_iota(jnp.int32, z.shape, 0)
    return jnp.where(row < t - n, pltpu.roll(z, t - n, 0), 0.0)


def _conv_fwd(name, proj, cw):
    t, d3 = proj.shape
    d = d3 // 3
    tn = _pick(d, LANES)
    nb = d // tn

    def body(b_ref, c_ref, u_ref, w_ref, o_ref):
        z = c_ref[...] * u_ref[...]
        zc = w_ref[0:1, :] * _shift_down(z, 2) + w_ref[1:2, :] * _shift_down(z, 1) + w_ref[2:3, :] * z
        o_ref[...] = (b_ref[...] * zc).astype(BF)

    def part(p):
        return pl.BlockSpec((t, tn), lambda j: (0, p * nb + j))

    return pl.pallas_call(
        body, name=name, out_shape=jax.ShapeDtypeStruct((t, d), BF), grid=(nb,),
        in_specs=[part(0), part(1), part(2), pl.BlockSpec((8, tn), lambda j: (0, j))],
        out_specs=pl.BlockSpec((t, tn), lambda j: (0, j)),
        compiler_params=_cparams(1),
    )(proj, proj, proj, cw)


def _conv_bwd(name, dbz, proj, cw):
    t, d3 = proj.shape
    d = d3 // 3
    tn = _pick(d, LANES)
    nb = d // tn

    def body(g_ref, b_ref, c_ref, u_ref, w_ref, o_ref, sums_ref):
        p = pl.program_id(1)
        w0, w1, w2 = w_ref[0:1, :], w_ref[1:2, :], w_ref[2:3, :]

        @pl.when(p == 0)
        def _():
            z = c_ref[...] * u_ref[...]
            z1, z2 = _shift_down(z, 1), _shift_down(z, 2)
            gv = g_ref[...]
            o_ref[...] = (gv * (w0 * z2 + w1 * z1 + w2 * z)).astype(BF)
            dzc = gv * b_ref[...]
            sums_ref[...] = jnp.zeros_like(sums_ref)
            sums_ref[0:1, :] = _colsum(dzc * z2)
            sums_ref[1:2, :] = _colsum(dzc * z1)
            sums_ref[2:3, :] = _colsum(dzc * z)

        @pl.when(p > 0)
        def _():
            dzc = g_ref[...] * b_ref[...]
            dz = w2 * dzc + w1 * _shift_up(dzc, 1) + w0 * _shift_up(dzc, 2)
            other = jnp.where(p == 1, u_ref[...], c_ref[...])
            o_ref[...] = (dz * other).astype(BF)

    def part(q):
        return pl.BlockSpec((t, tn), lambda j, p: (0, q * nb + j))

    return pl.pallas_call(
        body, name=name,
        out_shape=[jax.ShapeDtypeStruct((t, d3), BF), jax.ShapeDtypeStruct((8, d), F32)],
        grid=(nb, 3),
        in_specs=[pl.BlockSpec((t, tn), lambda j, p: (0, j)), part(0), part(1), part(2),
                  pl.BlockSpec((8, tn), lambda j, p: (0, j))],
        out_specs=[pl.BlockSpec((t, tn), lambda j, p: (0, p * nb + j)), pl.BlockSpec((8, tn), lambda j, p: (0, j))],
        compiler_params=_cparams(2),
    )(dbz, proj, proj, proj, cw)


def _pad_rows8(v):
    return jnp.pad(v, ((0, 8 - v.shape[0]), (0, 0)))


def kernel(x, c, positions, w_mod, b_mod, norm_g, mla_w_in, mla_g_q, mla_g_kv, mla_w_uq, mla_w_ukv, mla_w_o, conv_w_in, conv_w, conv_w_out, mlp_w_up, mlp_w_down, loss_target, m_w_mod, m_b_mod, m_norm_g, m_mla_w_in, m_mla_g_q, m_mla_g_kv, m_mla_w_uq, m_mla_w_ukv, m_mla_w_o, m_conv_w_in, m_conv_w, m_conv_w_out, m_mlp_w_up, m_mlp_w_down, v_w_mod, v_b_mod, v_norm_g, v_mla_w_in, v_mla_g_q, v_mla_g_kv, v_mla_w_uq, v_mla_w_ukv, v_mla_w_o, v_conv_w_in, v_conv_w, v_conv_w_out, v_mlp_w_up, v_mlp_w_down):
    t, d = x.shape[1], x.shape[2]
    depth = w_mod.shape[0]
    n_mod_loc = w_mod.shape[2]
    d_loc = norm_g.shape[2]
    rq = mla_g_q.shape[1]
    rkv = mla_g_kv.shape[1]
    heads = mla_w_uq.shape[2]
    lat_w = mla_w_in.shape[2]
    lat_pad = rq + rkv + LANES
    scale = (NOPE_DIM + ROPE_DIM) ** -0.5
    xi, yi, ci = _pos()
    dev = 4 * xi + 2 * yi + ci
    c_idx = jnp.reshape(ci, (1,)).astype(jnp.int32)
    chip_idx = jnp.reshape(2 * xi + yi, (1,)).astype(jnp.int32)
    x0 = x[0]
    target = loss_target[0]

    w_in_loc = jnp.pad(mla_w_in[0], ((0, 0), (0, lat_pad - lat_w))).astype(BF)
    uq = mla_w_uq[0]
    uq_loc = jnp.concatenate(
        [uq[:, :, :NOPE_DIM].reshape(uq.shape[0], heads * NOPE_DIM),
         jnp.pad(uq[:, :, NOPE_DIM:], ((0, 0), (0, 0), (0, LANES - ROPE_DIM))).reshape(uq.shape[0], heads * LANES)],
        axis=1).astype(BF)
    ukv_loc = mla_w_ukv[0].reshape(mla_w_ukv.shape[1], heads * (NOPE_DIM + V_DIM)).astype(BF)
    def merged(g):
        return g.reshape(g.shape[0] * g.shape[1], g.shape[2])

    g_win, g_uq, g_ukv, g_wo = [merged(g) for g in _ag_big("ag_mla", [w_in_loc, uq_loc, ukv_loc, mla_w_o[0].astype(BF)])]

    n_ng = depth * 4 * d_loc
    small = jnp.concatenate([c.reshape(-1), norm_g.reshape(-1), conv_w.reshape(-1)])
    small_n = small.shape[0]
    small_rows = -(-small_n // (8 * LANES)) * 8
    small = jnp.pad(small, (0, small_rows * LANES - small_n)).reshape(small_rows, LANES)
    small_all = _ag_small("ag_small", small).reshape(N_DEV, small_rows * LANES)
    c_all = small_all[:, :d]
    ng_all = small_all[:, d:d + n_ng].reshape(N_DEV, depth, 4, d_loc).transpose(1, 2, 0, 3).reshape(depth, 4, d)
    cw_all = small_all[:, d + n_ng:d + n_ng + 3 * d_loc].reshape(N_DEV, 3, d_loc).transpose(1, 0, 2).reshape(3, d)
    cw8 = _pad_rows8(cw_all)
    cond_all = _silu("silu", c_all)

    mod_parts = jnp.stack([_mod_matvec(f"mod_matvec{i}", cond_all, w_mod[i]) for i in range(depth)])
    mod_rows = depth * N_DEV * n_mod_loc // LANES
    mod_all = _ag_small("ag_mod", mod_parts.reshape(mod_rows, LANES)).reshape(N_DEV, depth, N_DEV, n_mod_loc)
    mod_mine = lax.dynamic_index_in_dim(mod_all, dev, axis=2, keepdims=False)
    mod = mod_mine.transpose(1, 0, 2).reshape(depth, N_DEV * n_mod_loc) + b_mod
    mod = mod.reshape(depth, 6, d)

    dev_idx = jnp.reshape(dev, (1,)).astype(jnp.int32)
    later = [mlp_w_up[0], mlp_w_down[0], conv_w_in[0], conv_w_out[0], mlp_w_up[1], mlp_w_down[1]]
    lands = [_cast_own_block(f"cast_own_block{a}", w, dev_idx) for a, w in enumerate(later)]
    ag_groups = [[0, 1], [2, 3, 4, 5]]
    ag_sems, lands_thru, ag_token = _ag_start("ag_start", lands, ag_groups, [mod, g_wo])

    def fwd_vec(nb, gate, na, sc, sh, token=None):
        zero = jnp.zeros((d,), F32)
        rows = [zero if r is None else r for r in (nb, gate, na, sc, sh)]
        vec = _pad_rows8(jnp.stack(rows))
        return vec if token is None else vec + token[0, 0]

    inv_freq = ROPE_THETA ** (-jnp.arange(0, ROPE_DIM, 2, dtype=F32) / ROPE_DIM)
    ang = positions[0].astype(F32)[:, None] * inv_freq
    cos, sin = jnp.cos(ang), jnp.sin(ang)
    zh = jnp.zeros_like(cos)
    zpad = jnp.zeros((t, LANES - ROPE_DIM), F32)
    rope_c = jnp.concatenate([cos, cos, zpad], axis=1)
    rope_s1 = jnp.concatenate([-sin, zh, zpad], axis=1)
    rope_s2 = jnp.concatenate([zh, sin, zpad], axis=1)

    (h0,) = _norm_fwd("pre0", x0, None, fwd_vec(None, None, ng_all[0, 0], mod[0, 1], mod[0, 0], ag_token), None, False, True, False)
    (lat,) = _linear("mla_lat", h0, g_win)
    cq, ckv, kr = _lat_post("mla_lat_post", lat, mla_g_q, mla_g_kv, rope_c, rope_s1, rope_s2, rq, rkv)
    (q_raw,) = _linear("mla_q", cq, g_uq)
    qb = _rope_heads("mla_q_rope", q_raw, 0, q_raw, 1, rope_c, rope_s1, rope_s2, heads)
    (kvb,) = _linear("mla_kv", ckv, g_ukv, out_dtypes=(BF,))
    o, lse = _flash_fwd("mla_attn", qb, kvb, kr, heads, scale)
    fs_a, fr_a, lands_a, _ = _ag_forward("ag_forward_a", lands_thru[0:2], *ag_sems[0], o)
    (y0,) = _linear("mla_out", o, g_wo)
    x1, h1 = _norm_fwd("postpre1", x0, y0, fwd_vec(ng_all[0, 1], mod[0, 2], ng_all[0, 2], mod[0, 4], mod[0, 3]), None, True, True, False)
    g_up0, g_dn0 = _ag_finish("ag_finish_a", lands_a, fs_a, fr_a, x1)
    g_dn0 = merged(g_dn0)

    def sq_relu(acc):
        a = jnp.maximum(acc, 0.0)
        return acc, a * a

    u1, act1 = _linear("mlp0_up", h1, g_up0, out_dtypes=(F32, BF), epilogue=sq_relu)
    (y1,) = _linear("mlp0_down", act1, g_dn0)
    fs_b, fr_b, lands_b, _ = _ag_forward("ag_forward_b", lands_thru[2:6], *ag_sems[1], y1)
    x2, h2 = _norm_fwd("postpre2", x1, y1, fwd_vec(ng_all[0, 3], mod[0, 5], ng_all[1, 0], mod[1, 1], mod[1, 0]), None, True, True, False)
    g_cin, g_cout, g_up1, g_dn1 = _ag_finish("ag_finish_b", lands_b, fs_b, fr_b, x2)
    g_cout, g_dn1 = merged(g_cout), merged(g_dn1)
    g_up, g_dn = [g_up0, g_up1], [g_dn0, g_dn1]

    (proj,) = _linear("conv_in", h2, g_cin)
    bz = _conv_fwd("conv_mix", proj, cw8)
    (y2,) = _linear("conv_out", bz, g_cout)
    x3, h3 = _norm_fwd("postpre3", x2, y2, fwd_vec(ng_all[1, 1], mod[1, 2], ng_all[1, 2], mod[1, 4], mod[1, 3]), None, True, True, False)

    u3, act3 = _linear("mlp1_up", h3, g_up[1], out_dtypes=(F32, BF), epilogue=sq_relu)
    (y3,) = _linear("mlp1_down", act3, g_dn[1])
    dx4, loss_cols = _norm_fwd("post_loss", x3, y3, fwd_vec(ng_all[1, 3], mod[1, 5], None, None, None), target, True, False, True)
    loss = lax.psum(0.5 * jnp.sum(loss_cols[0]) / d, ("x", "y", "c"))

    def bwd_vec(nb, gate, na, sc, token=None):
        zero = jnp.zeros((d,), F32)
        rows = [zero if r is None else r for r in (nb, gate, na, sc)]
        vec = _pad_rows8(jnp.stack(rows))
        return vec if token is None else vec + token[0, 0]

    def blocks(g, like):
        return g.reshape((N_DEV,) + like.shape)

    def rs_begin(tag, grads):
        from_sibling = _rs_sibling(f"rs_sibling_{tag}", grads)
        pair = [_pair_add(f"pair_add_{tag}{a}", grads[a], from_sibling[a], c_idx) for a in range(len(grads))]
        return _rs_chips_start(f"rs_start_{tag}", pair)

    def mlp_bwd(tag, dy, h, u, act, w_up, w_dn):
        def relu_grad(acc, uv):
            return (acc * (2.0 * jnp.maximum(uv, 0.0)),)

        (du,) = _linear_t(f"{tag}_dact", dy, w_dn, out_dtypes=(BF,), epilogue=relu_grad, extras=(u,))
        dw_dn = _wgrad(f"{tag}_dw_down", act, dy, w_dn)
        dw_up = _wgrad(f"{tag}_dw_up", h, du, w_up)
        (dh,) = _linear_t(f"{tag}_dh", du, w_up)
        return dh, dw_up, dw_dn

    dy3, sums4 = _norm_bwd("bwd_post3", dx4, None, None, y3, bwd_vec(ng_all[1, 3], mod[1, 5], None, None), False, True)
    dh3, dw_up1, dw_dn1 = mlp_bwd("mlp1", dy3, h3, u3, act3, g_up[1], g_dn[1])
    rs_mlp1 = rs_begin("mlp1", [dw_up1, blocks(dw_dn1, mlp_w_down[1])])
    dx3, dy2, sums3 = _norm_bwd("bwd_norm3", dx4, dh3, x3, y2, bwd_vec(ng_all[1, 1], mod[1, 2], ng_all[1, 2], mod[1, 4], rs_mlp1[4]), True, True)

    (dbz,) = _linear_t("conv_dbz", dy2, g_cout)
    dw_cout = _wgrad("conv_dw_out", bz, dy2, g_cout)
    dproj, conv_sums = _conv_bwd("conv_mix_bwd", dbz, proj, cw8)
    dw_cin = _wgrad("conv_dw_in", h2, dproj, g_cin)
    rs_conv = rs_begin("conv", [dw_cin, blocks(dw_cout, conv_w_out[0])])
    (dh2,) = _linear_t("conv_dh", dproj, g_cin)
    dx2, dy1, sums2 = _norm_bwd("bwd_norm2", dx3, dh2, x2, y1, bwd_vec(ng_all[0, 3], mod[0, 5], ng_all[1, 0], mod[1, 1], rs_conv[4]), True, True)

    dh1, dw_up0, dw_dn0 = mlp_bwd("mlp0", dy1, h1, u1, act1, g_up[0], g_dn[0])
    rs_mlp0 = rs_begin("mlp0", [dw_up0, blocks(dw_dn0, mlp_w_down[0])])
    dx1, dy0, sums1 = _norm_bwd("bwd_norm1", dx2, dh1, x1, y0, bwd_vec(ng_all[0, 1], mod[0, 2], ng_all[0, 2], mod[0, 4], rs_mlp0[4]), True, True)

    (do,) = _linear_t("mla_do", dy0, g_wo, out_dtypes=(BF,))
    dw_o = _wgrad("mla_dw_o", o, dy0, g_wo)
    dkv, dkr, dqn, dqr = _flash_bwd("mla_attn_bwd", qb, kvb, kr, o, lse, do, heads, scale)
    dq = _rope_heads("mla_dq_rope", dqn, 0, dqr, 0, rope_c, -rope_s1, -rope_s2, heads)
    dw_uq = _wgrad("mla_dw_uq", cq, dq, g_uq)
    (dcq,) = _linear_t("mla_dcq", dq, g_uq)
    dw_ukv = _wgrad("mla_dw_ukv", ckv, dkv, g_ukv)
    (dckv,) = _linear_t("mla_dckv", dkv, g_ukv)
    dlat, lat_sums = _lat_bwd("mla_lat_bwd", dcq, dckv, dkr, lat, mla_g_q, mla_g_kv, rope_c, -rope_s1, -rope_s2, rq, rkv, heads)
    dw_win = _wgrad("mla_dw_in", h0, dlat, g_win)
    rs_mla = rs_begin("mla", [blocks(dw_win, w_in_loc), blocks(dw_uq, uq_loc), blocks(dw_ukv, ukv_loc), blocks(dw_o, mla_w_o[0])])
    (dh0,) = _linear_t("mla_dh", dlat, g_win)
    grad_x, sums0 = _norm_bwd("bwd_pre0", dx1, dh0, x0, None, bwd_vec(None, None, ng_all[0, 0], mod[0, 1], rs_mla[4]), True, False)

    def sums_rows(s, rows):
        return [s[r] for r in rows]

    dmod = jnp.stack([
        jnp.stack(sums_rows(sums0, (0, 1)) + sums_rows(sums1, (3,)) + sums_rows(sums1, (0, 1)) + sums_rows(sums2, (3,))),
        jnp.stack(sums_rows(sums2, (0, 1)) + sums_rows(sums3, (3,)) + sums_rows(sums3, (0, 1)) + sums_rows(sums4, (3,))),
    ])
    dng = jnp.stack([
        jnp.stack([sums0[2], sums1[4], sums1[2], sums2[4]]),
        jnp.stack([sums2[2], sums3[4], sums3[2], sums4[4]]),
    ])
    parts = [dmod.reshape(-1), dng.reshape(-1), lat_sums[0], lat_sums[1], conv_sums[0:3].reshape(-1)]
    sizes = [p.shape[0] for p in parts]
    packed = jnp.concatenate(parts)
    pk_rows = -(-packed.shape[0] // (8 * LANES)) * 8
    packed = jnp.pad(packed, (0, pk_rows * LANES - packed.shape[0])).reshape(pk_rows, LANES)
    packed_all = _ag_small("ag_small_grads", packed).reshape(N_DEV, pk_rows, LANES)
    total = _sum_devices("sum_small_grads", packed_all).reshape(-1)
    offs = [0]
    for s in sizes:
        offs.append(offs[-1] + s)
    g_b_mod = total[offs[0]:offs[1]].reshape(depth, 6 * d)
    ng_full = total[offs[1]:offs[2]].reshape(depth, 4, d)
    g_norm_g = lax.dynamic_slice_in_dim(ng_full, dev * d_loc, d_loc, axis=2)
    g_g_q = total[offs[2]:offs[3]].reshape(1, rq)
    g_g_kv = total[offs[3]:offs[4]].reshape(1, rkv)
    cw_full = total[offs[4]:offs[5]].reshape(1, 3, d)
    g_conv_w = lax.dynamic_slice_in_dim(cw_full, dev * d_loc, d_loc, axis=2)

    dmod_all = packed_all.reshape(N_DEV, -1)[:, :sizes[0]].reshape(N_DEV, depth, 6 * d)
    dmod_cols = lax.dynamic_slice_in_dim(dmod_all, dev * n_mod_loc, n_mod_loc, axis=2)
    cond_t = cond_all.T
    mod_out = [_mod_grad_adam(f"w_mod_adam{i}", cond_t, dmod_cols[:, i, :], w_mod[i], m_w_mod[i], v_w_mod[i]) for i in range(depth)]
    g_w_mod, d_w_mod, nm_w_mod, nv_w_mod = [jnp.stack([mod_out[i][k] for i in range(depth)]) for k in range(4)]

    def rs_end(tag, started):
        send_sems, recv_sems, pair, lands, _ = started
        pair, lands = _rs_chips_wait(f"rs_wait_{tag}", pair, lands, send_sems, recv_sems, g_w_mod)
        return [_chip_sum(f"chip_sum_{tag}{a}", pair[a], lands[a], chip_idx) for a in range(len(pair))]

    red_up1, red_dn1 = rs_end("mlp1", rs_mlp1)
    red_cin, red_cout = rs_end("conv", rs_conv)
    red_up0, red_dn0 = rs_end("mlp0", rs_mlp0)
    red_win, red_uq, red_ukv, red_wo = rs_end("mla", rs_mla)

    g_mla_w_in = red_win[:, :lat_w][None]
    g_mla_w_uq = jnp.concatenate(
        [red_uq[:, :heads * NOPE_DIM].reshape(-1, heads, NOPE_DIM),
         red_uq[:, heads * NOPE_DIM:].reshape(-1, heads, LANES)[:, :, :ROPE_DIM]], axis=2)[None]
    g_mla_w_ukv = red_ukv.reshape(mla_w_ukv.shape)
    g_mla_w_o = red_wo[None]
    g_conv_w_in = red_cin[None]
    g_conv_w_out = red_cout[None]
    g_mlp_w_up = jnp.stack([red_up0, red_up1])
    g_mlp_w_down = jnp.stack([red_dn0, red_dn1])

    grads = [g_w_mod, g_b_mod, g_norm_g, g_mla_w_in, g_g_q, g_g_kv, g_mla_w_uq, g_mla_w_ukv, g_mla_w_o,
             g_conv_w_in, g_conv_w, g_conv_w_out, g_mlp_w_up, g_mlp_w_down]
    weights = [w_mod, b_mod, norm_g, mla_w_in, mla_g_q, mla_g_kv, mla_w_uq, mla_w_ukv, mla_w_o, conv_w_in, conv_w, conv_w_out, mlp_w_up, mlp_w_down]
    ms = [m_w_mod, m_b_mod, m_norm_g, m_mla_w_in, m_mla_g_q, m_mla_g_kv, m_mla_w_uq, m_mla_w_ukv, m_mla_w_o, m_conv_w_in, m_conv_w, m_conv_w_out, m_mlp_w_up, m_mlp_w_down]
    vs = [v_w_mod, v_b_mod, v_norm_g, v_mla_w_in, v_mla_g_q, v_mla_g_kv, v_mla_w_uq, v_mla_w_ukv, v_mla_w_o, v_conv_w_in, v_conv_w, v_conv_w_out, v_mlp_w_up, v_mlp_w_down]
    names = ["w_mod", "b_mod", "norm_g", "mla_w_in", "mla_g_q", "mla_g_kv", "mla_w_uq", "mla_w_ukv", "mla_w_o", "conv_w_in", "conv_w", "conv_w_out", "mlp_w_up", "mlp_w_down"]
    deltas, new_ms, new_vs = [d_w_mod], [nm_w_mod], [nv_w_mod]
    for k in range(1, len(weights)):
        dl, nm, nv = _adam(f"adam_{names[k]}", weights[k], grads[k], ms[k], vs[k])
        deltas.append(dl)
        new_ms.append(nm)
        new_vs.append(nv)

    return (loss, grad_x[None], *grads, *deltas, *new_ms, *new_vs)
```

```python
import jax
import jax.numpy as jnp
from jax import lax
from jax.experimental import pallas as pl
from jax.experimental.pallas import tpu as pltpu

F32 = jnp.float32
BF = jnp.bfloat16
MESH = pl.DeviceIdType.MESH
N_DEV = 8
LANES = 128
NORM_EPS = 1e-6
ROPE_THETA = 10000.0
ROPE_DIM = 64
NOPE_DIM = 128
V_DIM = 128
LR, B1, B2, ADAM_EPS, WD, STEP = 0.001, 0.9, 0.999, 1e-08, 0.01, 10
VMEM_LIMIT = 56 * 1024 * 1024
TILE_BYTES = 2 * 1024 * 1024
MM_TILE_PREFS = ((1024, 1024), (1024, 512), (512, 1024), (512, 512), (256, 512), (256, 256), (128, 256), (128, 128))
MM_VMEM_BUDGET = 36 * 1024 * 1024
TQ = 512
Q_CHAINS = 1
LOG2E = 1.4426950408889634

NN = (((1,), (0,)), ((), ()))
NT = (((1,), (1,)), ((), ()))
TNDIMS = (((0,), (0,)), ((), ()))
HBM_SPEC = pl.BlockSpec(memory_space=pltpu.HBM)
VMEM_SPEC = pl.BlockSpec(memory_space=pltpu.VMEM)


def _cparams(n_axes):
    return pltpu.CompilerParams(dimension_semantics=("arbitrary",) * n_axes, vmem_limit_bytes=VMEM_LIMIT)


def _pick(dim, pref):
    if dim <= pref:
        return dim
    for t in range(pref - pref % LANES, 0, -LANES):
        if t > 0 and dim % t == 0:
            return t
    for t in range(pref, 0, -1):
        if dim % t == 0:
            return t
    return dim


def _rows(r, c, itemsize=4):
    want = max(8, TILE_BYTES // (itemsize * max(c, 1)))
    if r <= want:
        return r
    for t in range(want - want % 8, 0, -8):
        if t > 0 and r % t == 0:
            return t
    return r


def _pos():
    return lax.axis_index("x"), lax.axis_index("y"), lax.axis_index("c")


def _ag_small(name, v):
    m_per, n = v.shape

    def body(x_ref, out_ref, send_sems, recv_sems, local_sem):
        x, y, c = _pos()
        me, sibling = (x, y, c), (x, y, 1 - c)
        chips = [(1 - x, y), (x, 1 - y), (1 - x, 1 - y)]

        def rows(px, py, pc):
            return out_ref.at[pl.ds((4 * px + 2 * py + pc) * m_per, m_per), :]

        def copy(k, block, to, src=None):
            return pltpu.make_async_remote_copy(
                src_ref=rows(*block) if src is None else src, dst_ref=rows(*block),
                send_sem=send_sems.at[k], recv_sem=recv_sems.at[k], device_id=to, device_id_type=MESH)

        mine = pltpu.make_async_copy(x_ref, rows(*me), local_sem)
        mine.start()
        first = [copy(0, me, sibling, src=x_ref)]
        first += [copy(1 + j, me, (*chip, c), src=x_ref) for j, chip in enumerate(chips)]
        for cp in first:
            cp.start()
        passed = [copy(4 + j, (*chip, c), sibling) for j, chip in enumerate(chips)]
        for j, chip in enumerate(chips):
            copy(1 + j, (*chip, c), me).wait_recv()
            passed[j].start()
        copy(0, sibling, me).wait_recv()
        for j, chip in enumerate(chips):
            copy(4 + j, (*chip, 1 - c), me).wait_recv()
        for cp in first + passed:
            cp.wait_send()
        mine.wait()

    return pl.pallas_call(
        body, name=name,
        out_shape=jax.ShapeDtypeStruct((N_DEV * m_per, n), v.dtype),
        in_specs=[VMEM_SPEC], out_specs=VMEM_SPEC,
        scratch_shapes=[pltpu.SemaphoreType.DMA((7,)), pltpu.SemaphoreType.DMA((7,)), pltpu.SemaphoreType.DMA],
    )(v)


def _ag_big(name, arrs):
    n = len(arrs)

    def body(*refs):
        ins, outs = refs[:n], refs[n:2 * n]
        send_sems, recv_sems, local_sems = refs[2 * n:]
        x, y, c = _pos()
        me, sibling = (x, y, c), (x, y, 1 - c)
        chips = [(1 - x, y), (x, 1 - y), (1 - x, 1 - y)]

        def copy(a, k, block, to, src=None):
            dst = outs[a].at[4 * block[0] + 2 * block[1] + block[2]]
            return pltpu.make_async_remote_copy(
                src_ref=dst if src is None else src, dst_ref=dst,
                send_sem=send_sems.at[7 * a + k], recv_sem=recv_sems.at[7 * a + k],
                device_id=to, device_id_type=MESH)

        mine = [pltpu.make_async_copy(ins[a], outs[a].at[4 * x + 2 * y + c], local_sems.at[a]) for a in range(n)]
        for cp in mine:
            cp.start()
        first = []
        for a in range(n):
            first.append(copy(a, 0, me, sibling, src=ins[a]))
            first += [copy(a, 1 + j, me, (*chip, c), src=ins[a]) for j, chip in enumerate(chips)]
        for cp in first:
            cp.start()
        passed = []
        for j, chip in enumerate(chips):
            for a in range(n):
                copy(a, 1 + j, (*chip, c), me).wait_recv()
                cp = copy(a, 4 + j, (*chip, c), sibling)
                cp.start()
                passed.append(cp)
        for a in range(n):
            copy(a, 0, sibling, me).wait_recv()
            for j, chip in enumerate(chips):
                copy(a, 4 + j, (*chip, 1 - c), me).wait_recv()
        for cp in first + passed:
            cp.wait_send()
        for cp in mine:
            cp.wait()

    return pl.pallas_call(
        body, name=name,
        out_shape=[jax.ShapeDtypeStruct((N_DEV,) + a.shape, a.dtype) for a in arrs],
        in_specs=[HBM_SPEC] * n, out_specs=[HBM_SPEC] * n,
        scratch_shapes=[pltpu.SemaphoreType.DMA((7 * n,)), pltpu.SemaphoreType.DMA((7 * n,)),
                        pltpu.SemaphoreType.DMA((n,))],
    )(*arrs)


SEM_SPEC = pl.BlockSpec(memory_space=pltpu.SEMAPHORE)
ANY_SPEC = pl.BlockSpec(memory_space=pl.ANY)
EFFECT = pltpu.SideEffectType.DATAFLOW_SIDE_EFFECTING


def _hbm(a):
    return pltpu.with_memory_space_constraint(a, pltpu.HBM)


def _cast_own_block(name, w, dev_idx):
    r, cdim = w.shape
    tr = _rows(r, cdim, 4)

    def body(d_ref, w_ref, o_ref):
        o_ref[...] = w_ref[...].astype(BF)

    return pl.pallas_call(
        body, name=name,
        out_shape=jax.ShapeDtypeStruct((N_DEV, r, cdim), BF),
        grid_spec=pltpu.PrefetchScalarGridSpec(
            num_scalar_prefetch=1, grid=(r // tr,),
            in_specs=[pl.BlockSpec((tr, cdim), lambda i, d_ref: (i, 0))],
            out_specs=pl.BlockSpec((None, tr, cdim), lambda i, d_ref: (d_ref[0], i, 0))),
        compiler_params=_cparams(1),
    )(dev_idx, w)


def _ag_start(name, lands, groups, after):
    n = len(lands)
    ng = len(groups)
    n_after = len(after)

    def body(*refs):
        lnd = refs[:n]
        sems = refs[n + n_after:n + n_after + 2 * ng]
        token = refs[-1]
        x, y, c = _pos()
        targets = [(x, y, 1 - c), (1 - x, y, c), (x, 1 - y, c), (1 - x, 1 - y, c)]
        for gi, members in enumerate(groups):
            for pos, a in enumerate(members):
                own = lnd[a].at[4 * x + 2 * y + c]
                for k, to in enumerate(targets):
                    pltpu.make_async_remote_copy(
                        src_ref=own, dst_ref=own,
                        send_sem=sems[2 * gi].at[4 * pos + k], recv_sem=sems[2 * gi + 1].at[4 * pos + k],
                        device_id=to, device_id_type=MESH).start()
        token[...] = jnp.zeros_like(token)

    sem_shapes = []
    for members in groups:
        sem_shapes += [pltpu.SemaphoreType.DMA((4 * len(members),))] * 2
    res = pl.pallas_call(
        body, name=name,
        out_shape=(*sem_shapes, *[pltpu.HBM(a.shape, a.dtype) for a in lands], jax.ShapeDtypeStruct((8, LANES), F32)),
        in_specs=[HBM_SPEC] * n + [ANY_SPEC] * n_after,
        out_specs=(*[SEM_SPEC] * (2 * ng), *[HBM_SPEC] * n, VMEM_SPEC),
        input_output_aliases={i: 2 * ng + i for i in range(n)},
        compiler_params=pltpu.CompilerParams(has_side_effects=EFFECT),
    )(*[_hbm(a) for a in lands], *after)
    sem_pairs = [(res[2 * g], res[2 * g + 1]) for g in range(ng)]
    return sem_pairs, list(res[2 * ng:2 * ng + n]), res[-1]


def _ag_forward(name, lands, send_sems, recv_sems, after):
    n = len(lands)

    def body(*refs):
        lnd = refs[:n]
        s_sems, r_sems = refs[n], refs[n + 1]
        fs_sems, fr_sems = refs[n + 3], refs[n + 4]
        token = refs[-1]
        x, y, c = _pos()
        sources = [(x, y, 1 - c), (1 - x, y, c), (x, 1 - y, c), (1 - x, 1 - y, c)]
        for a in range(n):
            own = lnd[a].at[4 * x + 2 * y + c]
            for k, src in enumerate(sources):
                blk = lnd[a].at[4 * src[0] + 2 * src[1] + src[2]]
                arrived = pltpu.make_async_remote_copy(
                    src_ref=own, dst_ref=blk, send_sem=s_sems.at[4 * a + k], recv_sem=r_sems.at[4 * a + k],
                    device_id=src, device_id_type=MESH)
                arrived.wait_send()
                arrived.wait_recv()
                if k > 0:
                    pltpu.make_async_remote_copy(
                        src_ref=blk, dst_ref=blk, send_sem=fs_sems.at[3 * a + k - 1], recv_sem=fr_sems.at[3 * a + k - 1],
                        device_id=(x, y, 1 - c), device_id_type=MESH).start()
        token[...] = jnp.zeros_like(token)

    res = pl.pallas_call(
        body, name=name,
        out_shape=(pltpu.SemaphoreType.DMA((3 * n,)), pltpu.SemaphoreType.DMA((3 * n,)),
                   *[pltpu.HBM(a.shape, a.dtype) for a in lands], jax.ShapeDtypeStruct((8, LANES), F32)),
        in_specs=[HBM_SPEC] * n + [SEM_SPEC, SEM_SPEC, ANY_SPEC],
        out_specs=(SEM_SPEC, SEM_SPEC, *[HBM_SPEC] * n, VMEM_SPEC),
        input_output_aliases={i: 2 + i for i in range(n)},
        compiler_params=pltpu.CompilerParams(has_side_effects=EFFECT),
    )(*lands, send_sems, recv_sems, after)
    return res[0], res[1], list(res[2:2 + n]), res[-1]


def _ag_finish(name, lands, fs_sems, fr_sems, after):
    n = len(lands)

    def body(*refs):
        lnd = refs[:n]
        fs, fr = refs[n], refs[n + 1]
        x, y, c = _pos()
        chips = [(1 - x, y), (x, 1 - y), (1 - x, 1 - y)]
        for a in range(n):
            for j, chip in enumerate(chips):
                mine = lnd[a].at[4 * chip[0] + 2 * chip[1] + c]
                theirs = lnd[a].at[4 * chip[0] + 2 * chip[1] + 1 - c]
                cp = pltpu.make_async_remote_copy(
                    src_ref=mine, dst_ref=theirs, send_sem=fs.at[3 * a + j], recv_sem=fr.at[3 * a + j],
                    device_id=(x, y, 1 - c), device_id_type=MESH)
                cp.wait_send()
                cp.wait_recv()

    res = pl.pallas_call(
        body, name=name,
        out_shape=tuple(pltpu.HBM(a.shape, a.dtype) for a in lands),
        in_specs=[HBM_SPEC] * n + [SEM_SPEC, SEM_SPEC, ANY_SPEC],
        out_specs=tuple([HBM_SPEC] * n),
        input_output_aliases={i: i for i in range(n)},
        compiler_params=pltpu.CompilerParams(has_side_effects=EFFECT),
    )(*lands, fs_sems, fr_sems, after)
    return list(res)


def _sibling_routes(x, y, c):
    return [(2 * j + (1 - c), (x, y, 1 - c)) for j in range(4)]


def _chip_routes(x, y, c):
    return [(2 * cx + cy, (cx, cy, c)) for cx, cy in [(1 - x, y), (x, 1 - y), (1 - x, 1 - y)]]


def _push_start(name, arrs, routes):
    n = len(arrs)
    nr = len(routes(0, 0, 0))

    def body(*refs):
        ins, lnd = refs[:n], refs[n:2 * n]
        s_sems, r_sems = refs[2 * n], refs[2 * n + 1]
        token = refs[-1]
        for a in range(n):
            for k, (blk, to) in enumerate(routes(*_pos())):
                pltpu.make_async_remote_copy(
                    src_ref=ins[a].at[blk], dst_ref=lnd[a].at[k],
                    send_sem=s_sems.at[nr * a + k], recv_sem=r_sems.at[nr * a + k],
                    device_id=to, device_id_type=MESH).start()
        token[...] = jnp.zeros_like(token)

    lands = [lax.empty((nr,) + a.shape[1:], a.dtype) for a in arrs]
    res = pl.pallas_call(
        body, name=name,
        out_shape=(pltpu.SemaphoreType.DMA((nr * n,)), pltpu.SemaphoreType.DMA((nr * n,)),
                   *[pltpu.HBM(a.shape, a.dtype) for a in arrs], *[pltpu.HBM(a.shape, a.dtype) for a in lands],
                   jax.ShapeDtypeStruct((8, LANES), F32)),
        in_specs=[HBM_SPEC] * (2 * n),
        out_specs=(SEM_SPEC, SEM_SPEC, *[HBM_SPEC] * (2 * n), VMEM_SPEC),
        input_output_aliases={i: 2 + i for i in range(2 * n)},
        compiler_params=pltpu.CompilerParams(has_side_effects=EFFECT),
    )(*[_hbm(a) for a in arrs], *[_hbm(a) for a in lands])
    return res[0], res[1], list(res[2:2 + n]), list(res[2 + n:2 + 2 * n]), res[-1]


def _push_wait(name, arrs, lands, send_sems, recv_sems, routes, after):
    n = len(arrs)
    nr = len(routes(0, 0, 0))

    def body(*refs):
        ins, lnd = refs[:n], refs[n:2 * n]
        s_sems, r_sems = refs[2 * n], refs[2 * n + 1]
        for a in range(n):
            for k, (blk, to) in enumerate(routes(*_pos())):
                cp = pltpu.make_async_remote_copy(
                    src_ref=ins[a].at[blk], dst_ref=lnd[a].at[k],
                    send_sem=s_sems.at[nr * a + k], recv_sem=r_sems.at[nr * a + k],
                    device_id=to, device_id_type=MESH)
                cp.wait_send()
                cp.wait_recv()

    res = pl.pallas_call(
        body, name=name,
        out_shape=tuple(pltpu.HBM(a.shape, a.dtype) for a in list(arrs) + list(lands)),
        in_specs=[HBM_SPEC] * (2 * n) + [SEM_SPEC, SEM_SPEC, ANY_SPEC],
        out_specs=tuple([HBM_SPEC] * (2 * n)),
        input_output_aliases={i: i for i in range(2 * n)},
        compiler_params=pltpu.CompilerParams(has_side_effects=EFFECT),
    )(*arrs, *lands, send_sems, recv_sems, after)
    return list(res[:n]), list(res[n:])


def _pair_add(name, g, r1, c_idx):
    _, r, cdim = g.shape
    tr = _rows(r, cdim, 2)

    def body(c_ref, g_ref, r_ref, o_ref):
        o_ref[...] = (g_ref[...].astype(F32) + r_ref[...].astype(F32)).astype(o_ref.dtype)

    return pl.pallas_call(
        body, name=name,
        out_shape=jax.ShapeDtypeStruct((4, r, cdim), BF),
        grid_spec=pltpu.PrefetchScalarGridSpec(
            num_scalar_prefetch=1, grid=(4, r // tr),
            in_specs=[pl.BlockSpec((None, tr, cdim), lambda j, i, c_ref: (2 * j + c_ref[0], i, 0)),
                      pl.BlockSpec((None, tr, cdim), lambda j, i, c_ref: (j, i, 0))],
            out_specs=pl.BlockSpec((None, tr, cdim), lambda j, i, c_ref: (j, i, 0))),
        compiler_params=_cparams(2),
    )(c_idx, g, r1)


def _chip_sum(name, p, r2, chip_idx):
    _, r, cdim = p.shape
    tr = _rows(r, cdim, 4)

    def body(s_ref, p_ref, a_ref, b_ref, c_ref, o_ref):
        o_ref[...] = ((p_ref[...].astype(F32) + a_ref[...].astype(F32)) + b_ref[...].astype(F32)) + c_ref[...].astype(F32)

    def other(k):
        return pl.BlockSpec((None, tr, cdim), lambda i, s_ref: (k, i, 0))

    return pl.pallas_call(
        body, name=name,
        out_shape=jax.ShapeDtypeStruct((r, cdim), F32),
        grid_spec=pltpu.PrefetchScalarGridSpec(
            num_scalar_prefetch=1, grid=(r // tr,),
            in_specs=[pl.BlockSpec((None, tr, cdim), lambda i, s_ref: (s_ref[0], i, 0)), other(0), other(1), other(2)],
            out_specs=pl.BlockSpec((tr, cdim), lambda i, s_ref: (i, 0))),
        compiler_params=_cparams(1),
    )(chip_idx, p, r2, r2, r2)


def _adam_math(w, g, m, v):
    m = B1 * m + (1.0 - B1) * g
    v = B2 * v + (1.0 - B2) * (g * g)
    m_hat = m / (1.0 - B1 ** STEP)
    v_hat = v / (1.0 - B2 ** STEP)
    delta = -LR * (m_hat / (jnp.sqrt(v_hat) + ADAM_EPS) + WD * w)
    return delta, m, v


def _adam(name, w, g, m, v):
    shape = w.shape
    cdim = shape[-1]
    r = w.size // cdim
    flat = [a.reshape(r, cdim) for a in (w, g, m, v)]
    tr = _rows(r, cdim, 4)

    def body(w_ref, g_ref, m_ref, v_ref, d_ref, mo_ref, vo_ref):
        d, mn, vn = _adam_math(w_ref[...], g_ref[...], m_ref[...], v_ref[...])
        d_ref[...] = d
        mo_ref[...] = mn
        vo_ref[...] = vn

    spec = pl.BlockSpec((tr, cdim), lambda i: (i, 0))
    outs = pl.pallas_call(
        body, name=name,
        out_shape=[jax.ShapeDtypeStruct((r, cdim), F32)] * 3,
        grid=(r // tr,), in_specs=[spec] * 4, out_specs=[spec] * 3,
        compiler_params=_cparams(1),
    )(*flat)
    return [o.reshape(shape) for o in outs]


def _adam_sum(name, p, r2, chip_idx, w, m, v, layer, prev=None):
    nl, r, cdim = w.shape
    tr = _rows(r, cdim, 4)

    def body(s_ref, p_ref, a_ref, b_ref, c_ref, w_ref, m_ref, v_ref, *rest):
        g_ref, d_ref, mo_ref, vo_ref = rest[-4:]
        g = ((p_ref[...].astype(F32) + a_ref[...].astype(F32)) + b_ref[...].astype(F32)) + c_ref[...].astype(F32)
        dl, mn, vn = _adam_math(w_ref[...], g, m_ref[...], v_ref[...])
        g_ref[...] = g
        d_ref[...] = dl
        mo_ref[...] = mn
        vo_ref[...] = vn

    def slot(k):
        return pl.BlockSpec((None, tr, cdim), lambda i, s_ref: (k, i, 0))

    in_specs = [pl.BlockSpec((None, tr, cdim), lambda i, s_ref: (s_ref[0], i, 0)), slot(0), slot(1), slot(2)] + [slot(layer)] * 3
    operands = [chip_idx, p, r2, r2, r2, w, m, v]
    aliases = {}
    if prev is not None:
        in_specs += [ANY_SPEC] * 4
        aliases = {len(operands) + k: k for k in range(4)}
        operands += list(prev)
    return pl.pallas_call(
        body, name=name,
        out_shape=[jax.ShapeDtypeStruct((nl, r, cdim), F32)] * 4,
        grid_spec=pltpu.PrefetchScalarGridSpec(
            num_scalar_prefetch=1, grid=(r // tr,), in_specs=in_specs, out_specs=[slot(layer)] * 4),
        input_output_aliases=aliases,
        compiler_params=_cparams(1),
    )(*operands)


def _mod_matvec(name, cond_all, w_loc):
    d, n_loc = w_loc.shape
    tn = _pick(n_loc, 512)

    def body(c_ref, w_ref, o_ref):
        o_ref[...] = jnp.dot(c_ref[...].astype(BF), w_ref[...].astype(BF), preferred_element_type=F32)

    return pl.pallas_call(
        body, name=name,
        out_shape=jax.ShapeDtypeStruct((N_DEV, n_loc), F32),
        grid=(n_loc // tn,),
        in_specs=[pl.BlockSpec((N_DEV, d), lambda j: (0, 0)), pl.BlockSpec((d, tn), lambda j: (0, j))],
        out_specs=pl.BlockSpec((N_DEV, tn), lambda j: (0, j)),
        compiler_params=_cparams(1),
    )(cond_all, w_loc)


def _mod_grad_adam(name, cond_t, dmod, w, m, v):
    d, n_loc = w.shape
    tr = _rows(d, n_loc, 4)

    def body(ct_ref, dm_ref, w_ref, m_ref, v_ref, g_ref, d_ref, mo_ref, vo_ref):
        g = ct_ref[:, 0:1] * dm_ref[0:1, :]
        for b in range(1, N_DEV):
            g = g + ct_ref[:, b:b + 1] * dm_ref[b:b + 1, :]
        dl, mn, vn = _adam_math(w_ref[...], g, m_ref[...], v_ref[...])
        g_ref[...] = g
        d_ref[...] = dl
        mo_ref[...] = mn
        vo_ref[...] = vn

    spec = pl.BlockSpec((tr, n_loc), lambda i: (i, 0))
    return pl.pallas_call(
        body, name=name,
        out_shape=[jax.ShapeDtypeStruct((d, n_loc), F32)] * 4,
        grid=(d // tr,),
        in_specs=[pl.BlockSpec((tr, N_DEV), lambda i: (i, 0)), pl.BlockSpec((N_DEV, n_loc), lambda i: (0, 0)), spec, spec, spec],
        out_specs=[spec] * 4,
        compiler_params=_cparams(1),
    )(cond_t, dmod, w, m, v)


def _sum_devices(name, gathered):
    _, m, n = gathered.shape

    def body(g_ref, o_ref):
        acc = g_ref[0]
        for d in range(1, N_DEV):
            acc = acc + g_ref[d]
        o_ref[...] = acc

    return pl.pallas_call(
        body, name=name, out_shape=jax.ShapeDtypeStruct((m, n), F32),
        in_specs=[VMEM_SPEC], out_specs=VMEM_SPEC,
    )(gathered)


def _silu(name, c_pad):
    def body(c_ref, o_ref):
        c = c_ref[...]
        o_ref[...] = c * (1.0 / (1.0 + jnp.exp(-c)))

    return pl.pallas_call(body, name=name, out_shape=jax.ShapeDtypeStruct(c_pad.shape, F32),
                          in_specs=[VMEM_SPEC], out_specs=VMEM_SPEC)(c_pad)


def _mm_tiles(m, n_cap, k, bytes_per_out_elem):
    for pm, pn in MM_TILE_PREFS:
        tm, tn = _pick(m, pm), _pick(n_cap, pn)
        need = 2 * (2 * tm * k + 2 * k * tn + bytes_per_out_elem * tm * tn) + 4 * tm * tn
        if need <= MM_VMEM_BUDGET:
            return tm, tn
    return _pick(m, 8), _pick(n_cap, LANES)


def _mm(name, a, b, grid, a_spec, b_spec, outs, dot, extras=(), epilogue=None, token=None):
    n_ex, n_out = len(extras), len(outs)
    n_in = 2 + n_ex + (0 if token is None else 1)

    def body(*refs):
        ex_refs = refs[2:2 + n_ex]
        out_refs = refs[n_in:n_in + n_out]
        acc = dot(refs[0], refs[1])
        vals = (acc,) if epilogue is None else epilogue(acc, *[e[...] for e in ex_refs])
        for o_ref, val in zip(out_refs, vals):
            o_ref[...] = val.astype(o_ref.dtype)

    ins = [a, b] + [e for e, _ in extras]
    in_specs = [a_spec, b_spec] + [s for _, s in extras]
    if token is not None:
        ins.append(token)
        in_specs.append(pl.BlockSpec(token.shape, lambda i, j: (0, 0)))
    return pl.pallas_call(
        body, name=name,
        out_shape=[jax.ShapeDtypeStruct(s, d) for s, d, _ in outs],
        grid=grid, in_specs=in_specs, out_specs=[s for _, _, s in outs],
        compiler_params=_cparams(2),
    )(*ins)


def _dot(dims):
    return lambda a_ref, b_ref: lax.dot_general(a_ref[...], b_ref[...], dims, preferred_element_type=F32)


def _w_cols(w):
    return w.shape[1] if w.ndim == 2 else w.shape[0] * w.shape[2]


def _w_rows(w):
    return w.shape[0] if w.ndim == 2 else w.shape[1]


def _out_bytes(out_dtypes, extras):
    return sum(jnp.dtype(d).itemsize for d in out_dtypes) + sum(e.dtype.itemsize for e in extras)


def _linear(name, a, w, out_dtypes=(F32,), epilogue=None, extras=(), token=None):
    t, kdim = a.shape
    n = _w_cols(w)
    tm, tn = _mm_tiles(t, w.shape[-1], kdim, _out_bytes(out_dtypes, extras))
    if w.ndim == 2:
        w_spec = pl.BlockSpec((kdim, tn), lambda i, j: (0, j))
    else:
        per = w.shape[2] // tn
        w_spec = pl.BlockSpec((None, kdim, tn), lambda i, j: (j // per, 0, j % per))
    o_spec = pl.BlockSpec((tm, tn), lambda i, j: (i, j))
    return _mm(name, a, w, (t // tm, n // tn), pl.BlockSpec((tm, kdim), lambda i, j: (i, 0)), w_spec,
               [((t, n), dt, o_spec) for dt in out_dtypes], _dot(NN),
               extras=[(e, o_spec) for e in extras], epilogue=epilogue, token=token)


def _linear_t(name, dy, w, out_dtypes=(F32,), epilogue=None, extras=(), token=None):
    t, n = dy.shape
    kdim = _w_rows(w)
    tm, tn = _mm_tiles(t, kdim, n, _out_bytes(out_dtypes, extras))
    o_spec = pl.BlockSpec((tm, tn), lambda i, j: (i, j))
    if w.ndim == 2:
        w_spec = pl.BlockSpec((tn, n), lambda i, j: (j, 0))
        dot = _dot(NT)
    else:
        n_loc = w.shape[2]
        w_spec = pl.BlockSpec((N_DEV, tn, n_loc), lambda i, j: (0, j, 0))

        def dot(a_ref, w_ref):
            acc = lax.dot_general(a_ref[:, 0:n_loc], w_ref[0], NT, preferred_element_type=F32)
            for dev in range(1, N_DEV):
                acc = acc + lax.dot_general(a_ref[:, dev * n_loc:(dev + 1) * n_loc], w_ref[dev], NT, preferred_element_type=F32)
            return acc

    return _mm(name, dy, w, (t // tm, kdim // tn), pl.BlockSpec((tm, n), lambda i, j: (i, 0)), w_spec,
               [((t, kdim), dt, o_spec) for dt in out_dtypes], dot,
               extras=[(e, o_spec) for e in extras], epilogue=epilogue, token=token)


def _wgrad(name, a, dy, like):
    t, kdim = a.shape
    n = dy.shape[1]
    tm, tn = _mm_tiles(kdim, like.shape[-1], t, 2)
    if like.ndim == 2:
        o_spec = pl.BlockSpec((tm, tn), lambda i, j: (i, j))
    else:
        per = like.shape[2] // tn
        o_spec = pl.BlockSpec((None, tm, tn), lambda i, j: (j // per, i, j % per))
    return _mm(name, a, dy, (kdim // tm, n // tn), pl.BlockSpec((t, tm), lambda i, j: (0, i)),
               pl.BlockSpec((t, tn), lambda i, j: (0, j)), [(like.shape, BF, o_spec)], _dot(TNDIMS))[0]


def _rstd(x):
    return lax.rsqrt(jnp.mean(x * x, axis=-1, keepdims=True) + NORM_EPS)


def _colsum(x):
    return jnp.sum(x, axis=0, keepdims=True)


def _norm_fwd(name, x, y, vec, target, has_post, has_pre, has_loss):
    t, d = x.shape
    tm = _rows(t, d, 4)
    ins, in_specs = [x], [pl.BlockSpec((tm, d), lambda i: (i, 0))]
    row_spec = pl.BlockSpec((tm, d), lambda i: (i, 0))
    if has_post:
        ins.append(y)
        in_specs.append(row_spec)
    ins.append(vec)
    in_specs.append(pl.BlockSpec((8, d), lambda i: (0, 0)))
    if has_loss:
        ins.append(target)
        in_specs.append(row_spec)
    out_shape, out_specs = [], []
    if has_post and not has_loss:
        out_shape.append(jax.ShapeDtypeStruct((t, d), F32))
        out_specs.append(row_spec)
    if has_pre:
        out_shape.append(jax.ShapeDtypeStruct((t, d), BF))
        out_specs.append(row_spec)
    if has_loss:
        out_shape += [jax.ShapeDtypeStruct((t, d), F32), jax.ShapeDtypeStruct((8, d), F32)]
        out_specs += [row_spec, pl.BlockSpec((8, d), lambda i: (0, 0))]

    def body(*refs):
        it = iter(refs)
        x_ref = next(it)
        y_ref = next(it) if has_post else None
        vec_ref = next(it)
        tgt_ref = next(it) if has_loss else None
        xv = x_ref[...]
        if has_post:
            yv = y_ref[...]
            xv = xv + vec_ref[1:2, :] * ((yv * _rstd(yv)) * vec_ref[0:1, :])
            if not has_loss:
                next(it)[...] = xv
        if has_pre:
            hv = ((xv * _rstd(xv)) * vec_ref[2:3, :]) * (1.0 + vec_ref[3:4, :]) + vec_ref[4:5, :]
            next(it)[...] = hv.astype(BF)
        if has_loss:
            e = xv - tgt_ref[...]
            next(it)[...] = e * (1.0 / d)
            acc_ref = next(it)
            i = pl.program_id(0)

            @pl.when(i == 0)
            def _():
                acc_ref[...] = jnp.zeros_like(acc_ref)

            acc_ref[0:1, :] += _colsum(e * e)

    return pl.pallas_call(
        body, name=name, out_shape=out_shape, grid=(t // tm,), in_specs=in_specs, out_specs=out_specs,
        compiler_params=_cparams(1),
    )(*ins)


def _norm_bwd(name, dx_in, dh, x, y, vec, has_pre, has_post):
    t, d = dx_in.shape
    tm = _rows(t, d, 4)
    row_spec = pl.BlockSpec((tm, d), lambda i: (i, 0))
    vec_spec = pl.BlockSpec((8, d), lambda i: (0, 0))
    ins, in_specs = [dx_in], [row_spec]
    if has_pre:
        ins += [dh, x]
        in_specs += [row_spec, row_spec]
    if has_post:
        ins.append(y)
        in_specs.append(row_spec)
    ins.append(vec)
    in_specs.append(vec_spec)
    out_shape, out_specs = [], []
    if has_pre:
        out_shape.append(jax.ShapeDtypeStruct((t, d), F32))
        out_specs.append(row_spec)
    if has_post:
        out_shape.append(jax.ShapeDtypeStruct((t, d), BF))
        out_specs.append(row_spec)
    out_shape.append(jax.ShapeDtypeStruct((8, d), F32))
    out_specs.append(vec_spec)

    def body(*refs):
        it = iter(refs)
        dx = next(it)[...]
        dh_ref = next(it) if has_pre else None
        x_ref = next(it) if has_pre else None
        y_ref = next(it) if has_post else None
        vec_ref = next(it)
        dx_ref = next(it) if has_pre else None
        dy_ref = next(it) if has_post else None
        sums_ref = next(it)
        i = pl.program_id(0)

        @pl.when(i == 0)
        def _():
            sums_ref[...] = jnp.zeros_like(sums_ref)

        if has_pre:
            dhv, xv = dh_ref[...], x_ref[...]
            rs = _rstd(xv)
            xhat = xv * rs
            na = vec_ref[2:3, :]
            sums_ref[0:1, :] += _colsum(dhv)
            sums_ref[1:2, :] += _colsum(dhv * (xhat * na))
            tt = dhv * (1.0 + vec_ref[3:4, :])
            sums_ref[2:3, :] += _colsum(tt * xhat)
            u = tt * na
            dx = dx + rs * (u - xhat * jnp.mean(u * xhat, axis=-1, keepdims=True))
            dx_ref[...] = dx
        if has_post:
            yv = y_ref[...]
            rs = _rstd(yv)
            yhat = yv * rs
            nb = vec_ref[0:1, :]
            sums_ref[3:4, :] += _colsum(dx * (yhat * nb))
            tt = dx * vec_ref[1:2, :]
            sums_ref[4:5, :] += _colsum(tt * yhat)
            u = tt * nb
            dy_ref[...] = (rs * (u - yhat * jnp.mean(u * yhat, axis=-1, keepdims=True))).astype(BF)

    return pl.pallas_call(
        body, name=name, out_shape=out_shape, grid=(t // tm,), in_specs=in_specs, out_specs=out_specs,
        compiler_params=_cparams(1),
    )(*ins)


def _rope128(tv, cs, s1, s2):
    return tv * cs + pltpu.roll(tv, 96, 1) * s1 + pltpu.roll(tv, 32, 1) * s2


def _lat_post(name, lat, gq, gkv, cs, s1, s2, rq, rkv):
    t, w = lat.shape
    tm = _rows(t, w, 4)

    def body(lat_ref, gq_ref, gkv_ref, cs_ref, s1_ref, s2_ref, cq_ref, ckv_ref, kr_ref):
        lq = lat_ref[:, 0:rq]
        lkv = lat_ref[:, rq:rq + rkv]
        cq_ref[...] = ((lq * _rstd(lq)) * gq_ref[...]).astype(BF)
        ckv_ref[...] = ((lkv * _rstd(lkv)) * gkv_ref[...]).astype(BF)
        kr_ref[...] = _rope128(lat_ref[:, rq + rkv:rq + rkv + LANES], cs_ref[...], s1_ref[...], s2_ref[...]).astype(BF)

    def rows(c):
        return pl.BlockSpec((tm, c), lambda i: (i, 0))

    def vecs(c):
        return pl.BlockSpec((1, c), lambda i: (0, 0))

    return pl.pallas_call(
        body, name=name,
        out_shape=[jax.ShapeDtypeStruct((t, rq), BF), jax.ShapeDtypeStruct((t, rkv), BF), jax.ShapeDtypeStruct((t, LANES), BF)],
        grid=(t // tm,),
        in_specs=[rows(w), vecs(rq), vecs(rkv), rows(LANES), rows(LANES), rows(LANES)],
        out_specs=[rows(rq), rows(rkv), rows(LANES)],
        compiler_params=_cparams(1),
    )(lat, gq, gkv, cs, s1, s2)


def _lat_bwd(name, dcq, dckv, dkr, lat, gq, gkv, cs, s1, s2, rq, rkv, heads):
    t, w = lat.shape
    tm = _rows(t, max(w, heads * LANES), 4)
    assert rq == rkv

    def body(dcq_ref, dckv_ref, dkr_ref, lat_ref, gq_ref, gkv_ref, cs_ref, s1_ref, s2_ref, dlat_ref, sums_ref):
        i = pl.program_id(0)

        @pl.when(i == 0)
        def _():
            sums_ref[...] = jnp.zeros_like(sums_ref)

        def rms_bwd(dc, lv, g, row):
            rs = _rstd(lv)
            lhat = lv * rs
            sums_ref[row:row + 1, :] += _colsum(dc * lhat)
            u = dc * g
            return rs * (u - lhat * jnp.mean(u * lhat, axis=-1, keepdims=True))

        dlat_ref[:, 0:rq] = rms_bwd(dcq_ref[...], lat_ref[:, 0:rq], gq_ref[...], 0).astype(BF)
        dlat_ref[:, rq:rq + rkv] = rms_bwd(dckv_ref[...], lat_ref[:, rq:rq + rkv], gkv_ref[...], 1).astype(BF)
        dk = dkr_ref[:, 0:LANES]
        for h in range(1, heads):
            dk = dk + dkr_ref[:, h * LANES:(h + 1) * LANES]
        dlat_ref[:, rq + rkv:rq + rkv + LANES] = _rope128(dk, cs_ref[...], s1_ref[...], s2_ref[...]).astype(BF)

    def rows(c):
        return pl.BlockSpec((tm, c), lambda i: (i, 0))

    def vecs(c):
        return pl.BlockSpec((1, c), lambda i: (0, 0))

    return pl.pallas_call(
        body, name=name,
        out_shape=[jax.ShapeDtypeStruct((t, w), BF), jax.ShapeDtypeStruct((8, rq), F32)],
        grid=(t // tm,),
        in_specs=[rows(rq), rows(rkv), rows(heads * LANES), rows(w), vecs(rq), vecs(rkv), rows(LANES), rows(LANES), rows(LANES)],
        out_specs=[rows(w), pl.BlockSpec((8, rq), lambda i: (0, 0))],
        compiler_params=_cparams(1),
    )(dcq, dckv, dkr, lat, gq, gkv, cs, s1, s2)


def _rope_heads(name, xa, a_blk, xb, b_blk, cs, s1, s2, heads):
    t = xa.shape[0]
    hw = heads * LANES
    tm = _rows(t, 2 * hw, 4)

    def body(a_ref, b_ref, cs_ref, s1_ref, s2_ref, o_ref):
        o_ref[:, 0:hw] = a_ref[...].astype(BF)
        csv, s1v, s2v = cs_ref[...], s1_ref[...], s2_ref[...]
        for h in range(heads):
            sl = slice(h * LANES, (h + 1) * LANES)
            o_ref[:, hw + h * LANES:hw + (h + 1) * LANES] = _rope128(b_ref[:, sl], csv, s1v, s2v).astype(BF)

    def rows(c):
        return pl.BlockSpec((tm, c), lambda i: (i, 0))

    return pl.pallas_call(
        body, name=name,
        out_shape=jax.ShapeDtypeStruct((t, 2 * hw), BF),
        grid=(t // tm,),
        in_specs=[pl.BlockSpec((tm, hw), lambda i: (i, a_blk)), pl.BlockSpec((tm, hw), lambda i: (i, b_blk)),
                  rows(LANES), rows(LANES), rows(LANES)],
        out_specs=rows(2 * hw),
        compiler_params=_cparams(1),
    )(xa, xb, cs, s1, s2)


def _causal_mask(s, row0):
    row = lax.broadcasted_iota(jnp.int32, s.shape, 0) + row0
    col = lax.broadcasted_iota(jnp.int32, s.shape, 1)
    return col <= row


def _flash_fwd(name, qb, kv, kr, heads, scale):
    t = qb.shape[0]
    tq = _pick(t, TQ)
    nq = t // tq
    sub = tq // Q_CHAINS
    c2 = scale * LOG2E

    def body(qn_ref, qr_ref, kv_ref, kr_ref, o_ref, lse_ref):
        i = pl.program_id(1)
        qs = [jnp.concatenate([qn_ref[a * sub:(a + 1) * sub, :], qr_ref[a * sub:(a + 1) * sub, :]], axis=-1)
              for a in range(Q_CHAINS)]

        def block(j, carry, masked):
            rows = pl.ds(pl.multiple_of(j * tq, tq), tq)
            k = jnp.concatenate([kv_ref[rows, 0:LANES], kr_ref[rows, :]], axis=-1)
            v = kv_ref[rows, LANES:2 * LANES]
            out = []
            scores = [lax.dot_general(qs[a], k, NT, preferred_element_type=F32) for a in range(Q_CHAINS)]
            for a in range(Q_CHAINS):
                m_prev, l_prev, acc = carry[a]
                s = scores[a] * c2
                if masked:
                    s = jnp.where(_causal_mask(s, a * sub), s, -1e30)
                m_new = jnp.maximum(m_prev, jnp.max(s, axis=-1, keepdims=True))
                alpha = jnp.exp2(m_prev - m_new)
                p = jnp.exp2(s - m_new)
                l_new = alpha * l_prev + jnp.sum(p, axis=-1, keepdims=True)
                out.append((m_new, l_new, alpha * acc + jnp.dot(p.astype(BF), v, preferred_element_type=F32)))
            return tuple(out)

        init = tuple((jnp.full((sub, 1), -1e30, F32), jnp.zeros((sub, 1), F32), jnp.zeros((sub, V_DIM), F32))
                     for _ in range(Q_CHAINS))
        carry = lax.fori_loop(0, i, lambda j, cr: block(j, cr, False), init)
        fin = block(i, carry, True)
        for a in range(Q_CHAINS):
            m_fin, l_fin, acc = fin[a]
            o_ref[a * sub:(a + 1) * sub, :] = (acc / l_fin).astype(BF)
            lse_ref[a * sub:(a + 1) * sub, :] = jnp.broadcast_to(m_fin + jnp.log2(l_fin), (sub, LANES))

    return pl.pallas_call(
        body, name=name,
        out_shape=[jax.ShapeDtypeStruct((t, heads * V_DIM), BF), jax.ShapeDtypeStruct((t, heads * LANES), F32)],
        grid=(heads, nq),
        in_specs=[pl.BlockSpec((tq, LANES), lambda h, i: (i, h)),
                  pl.BlockSpec((tq, LANES), lambda h, i: (i, heads + h)),
                  pl.BlockSpec((t, 2 * LANES), lambda h, i: (0, h)),
                  pl.BlockSpec((t, LANES), lambda h, i: (0, 0))],
        out_specs=[pl.BlockSpec((tq, V_DIM), lambda h, i: (i, h)), pl.BlockSpec((tq, LANES), lambda h, i: (i, h))],
        compiler_params=_cparams(2),
    )(qb, qb, kv, kr)


def _flash_bwd(name, qb, kv, kr, o, lse, do, heads, scale):
    t = qb.shape[0]
    tq = _pick(t, TQ)
    nq = t // tq

    c2 = scale * LOG2E

    def body(qn_ref, qr_ref, kv_ref, kr_ref, o_ref, lse_ref, do_ref, dkv_ref, dkr_ref, dqn_ref, dqr_ref):
        j = pl.program_id(1)

        @pl.when(j == 0)
        def _():
            dqn_ref[...] = jnp.zeros_like(dqn_ref)
            dqr_ref[...] = jnp.zeros_like(dqr_ref)

        k = jnp.concatenate([kv_ref[:, 0:LANES], kr_ref[...]], axis=-1)
        v = kv_ref[:, LANES:2 * LANES]

        def block(i, carry, masked):
            dk, dv = carry
            rows = pl.ds(pl.multiple_of(i * tq, tq), tq)
            q = jnp.concatenate([qn_ref[rows, :], qr_ref[rows, :]], axis=-1)
            dov = do_ref[rows, :]
            delta = jnp.sum(dov.astype(F32) * o_ref[rows, :].astype(F32), axis=-1, keepdims=True)
            s = lax.dot_general(q, k, NT, preferred_element_type=F32) * c2
            p = jnp.exp2(s - lse_ref[rows, 0:1])
            if masked:
                p = jnp.where(_causal_mask(s, 0), p, 0.0)
            dv = dv + lax.dot_general(p.astype(BF), dov, TNDIMS, preferred_element_type=F32)
            dp = lax.dot_general(dov, v, NT, preferred_element_type=F32)
            ds = (p * (dp - delta) * scale).astype(BF)
            dk = dk + lax.dot_general(ds, q, TNDIMS, preferred_element_type=F32)
            dq = jnp.dot(ds, k, preferred_element_type=F32)
            dqn_ref[rows, :] += dq[:, 0:LANES]
            dqr_ref[rows, :] += dq[:, LANES:2 * LANES]
            return dk, dv

        carry = block(j, (jnp.zeros((tq, 2 * LANES), F32), jnp.zeros((tq, V_DIM), F32)), True)
        dk, dv = lax.fori_loop(j + 1, nq, lambda i, cr: block(i, cr, False), carry)
        dkv_ref[:, 0:LANES] = dk[:, 0:LANES].astype(BF)
        dkv_ref[:, LANES:2 * LANES] = dv.astype(BF)
        dkr_ref[...] = dk[:, LANES:2 * LANES]

    def full(hoff):
        return pl.BlockSpec((t, LANES), lambda h, j: (0, hoff + h))

    return pl.pallas_call(
        body, name=name,
        out_shape=[jax.ShapeDtypeStruct((t, heads * 2 * LANES), BF), jax.ShapeDtypeStruct((t, heads * LANES), F32),
                   jax.ShapeDtypeStruct((t, heads * LANES), F32), jax.ShapeDtypeStruct((t, heads * LANES), F32)],
        grid=(heads, nq),
        in_specs=[full(0), full(heads), pl.BlockSpec((tq, 2 * LANES), lambda h, j: (j, h)),
                  pl.BlockSpec((tq, LANES), lambda h, j: (j, 0)), full(0), full(0), full(0)],
        out_specs=[pl.BlockSpec((tq, 2 * LANES), lambda h, j: (j, h)), pl.BlockSpec((tq, LANES), lambda h, j: (j, h)),
                   full(0), full(0)],
        compiler_params=_cparams(2),
    )(qb, qb, kv, kr, o, lse, do)


def _shift_down(z, n):
    row = lax.broadcasted_iota(jnp.int32, z.shape, 0)
    return jnp.where(row >= n, pltpu.roll(z, n, 0), 0.0)


def _shift_up(z, n):
    t = z.shape[0]
    row = lax.broadcasted_iota(jnp.int32, z.shape, 0)
    return jnp.where(row < t - n, pltpu.roll(z, t - n, 0), 0.0)


def _conv_fwd(name, proj, cw):
    t, d3 = proj.shape
    d = d3 // 3
    tn = _pick(d, LANES)
    nb = d // tn

    def body(b_ref, c_ref, u_ref, w_ref, o_ref):
        z = c_ref[...] * u_ref[...]
        zc = w_ref[0:1, :] * _shift_down(z, 2) + w_ref[1:2, :] * _shift_down(z, 1) + w_ref[2:3, :] * z
        o_ref[...] = (b_ref[...] * zc).astype(BF)

    def part(p):
        return pl.BlockSpec((t, tn), lambda j: (0, p * nb + j))

    return pl.pallas_call(
        body, name=name, out_shape=jax.ShapeDtypeStruct((t, d), BF), grid=(nb,),
        in_specs=[part(0), part(1), part(2), pl.BlockSpec((8, tn), lambda j: (0, j))],
        out_specs=pl.BlockSpec((t, tn), lambda j: (0, j)),
        compiler_params=_cparams(1),
    )(proj, proj, proj, cw)


def _conv_bwd(name, dbz, proj, cw):
    t, d3 = proj.shape
    d = d3 // 3
    tn = _pick(d, LANES)
    nb = d // tn

    def body(g_ref, b_ref, c_ref, u_ref, w_ref, o_ref, sums_ref):
        p = pl.program_id(1)
        w0, w1, w2 = w_ref[0:1, :], w_ref[1:2, :], w_ref[2:3, :]

        @pl.when(p == 0)
        def _():
            z = c_ref[...] * u_ref[...]
            z1, z2 = _shift_down(z, 1), _shift_down(z, 2)
            gv = g_ref[...]
            o_ref[...] = (gv * (w0 * z2 + w1 * z1 + w2 * z)).astype(BF)
            dzc = gv * b_ref[...]
            sums_ref[...] = jnp.zeros_like(sums_ref)
            sums_ref[0:1, :] = _colsum(dzc * z2)
            sums_ref[1:2, :] = _colsum(dzc * z1)
            sums_ref[2:3, :] = _colsum(dzc * z)

        @pl.when(p > 0)
        def _():
            dzc = g_ref[...] * b_ref[...]
            dz = w2 * dzc + w1 * _shift_up(dzc, 1) + w0 * _shift_up(dzc, 2)
            other = jnp.where(p == 1, u_ref[...], c_ref[...])
            o_ref[...] = (dz * other).astype(BF)

    def part(q):
        return pl.BlockSpec((t, tn), lambda j, p: (0, q * nb + j))

    return pl.pallas_call(
        body, name=name,
        out_shape=[jax.ShapeDtypeStruct((t, d3), BF), jax.ShapeDtypeStruct((8, d), F32)],
        grid=(nb, 3),
        in_specs=[pl.BlockSpec((t, tn), lambda j, p: (0, j)), part(0), part(1), part(2),
                  pl.BlockSpec((8, tn), lambda j, p: (0, j))],
        out_specs=[pl.BlockSpec((t, tn), lambda j, p: (0, p * nb + j)), pl.BlockSpec((8, tn), lambda j, p: (0, j))],
        compiler_params=_cparams(2),
    )(dbz, proj, proj, proj, cw)


def _pad_rows8(v):
    return jnp.pad(v, ((0, 8 - v.shape[0]), (0, 0)))


def kernel(x, c, positions, w_mod, b_mod, norm_g, mla_w_in, mla_g_q, mla_g_kv, mla_w_uq, mla_w_ukv, mla_w_o, conv_w_in, conv_w, conv_w_out, mlp_w_up, mlp_w_down, loss_target, m_w_mod, m_b_mod, m_norm_g, m_mla_w_in, m_mla_g_q, m_mla_g_kv, m_mla_w_uq, m_mla_w_ukv, m_mla_w_o, m_conv_w_in, m_conv_w, m_conv_w_out, m_mlp_w_up, m_mlp_w_down, v_w_mod, v_b_mod, v_norm_g, v_mla_w_in, v_mla_g_q, v_mla_g_kv, v_mla_w_uq, v_mla_w_ukv, v_mla_w_o, v_conv_w_in, v_conv_w, v_conv_w_out, v_mlp_w_up, v_mlp_w_down):
    t, d = x.shape[1], x.shape[2]
    depth = w_mod.shape[0]
    n_mod_loc = w_mod.shape[2]
    d_loc = norm_g.shape[2]
    rq = mla_g_q.shape[1]
    rkv = mla_g_kv.shape[1]
    heads = mla_w_uq.shape[2]
    lat_w = mla_w_in.shape[2]
    lat_pad = rq + rkv + LANES
    scale = (NOPE_DIM + ROPE_DIM) ** -0.5
    xi, yi, ci = _pos()
    dev = 4 * xi + 2 * yi + ci
    c_idx = jnp.reshape(ci, (1,)).astype(jnp.int32)
    chip_idx = jnp.reshape(2 * xi + yi, (1,)).astype(jnp.int32)
    x0 = x[0]
    target = loss_target[0]

    w_in_loc = jnp.pad(mla_w_in[0], ((0, 0), (0, lat_pad - lat_w))).astype(BF)
    uq = mla_w_uq[0]
    uq_loc = jnp.concatenate(
        [uq[:, :, :NOPE_DIM].reshape(uq.shape[0], heads * NOPE_DIM),
         jnp.pad(uq[:, :, NOPE_DIM:], ((0, 0), (0, 0), (0, LANES - ROPE_DIM))).reshape(uq.shape[0], heads * LANES)],
        axis=1).astype(BF)
    ukv_loc = mla_w_ukv[0].reshape(mla_w_ukv.shape[1], heads * (NOPE_DIM + V_DIM)).astype(BF)
    def merged(g):
        return g.reshape(g.shape[0] * g.shape[1], g.shape[2])

    g_win, g_uq, g_ukv = [merged(g) for g in _ag_big("ag_mla", [w_in_loc, uq_loc, ukv_loc])]

    n_ng = depth * 4 * d_loc
    small = jnp.concatenate([c.reshape(-1), norm_g.reshape(-1), conv_w.reshape(-1)])
    small_n = small.shape[0]
    small_rows = -(-small_n // (8 * LANES)) * 8
    small = jnp.pad(small, (0, small_rows * LANES - small_n)).reshape(small_rows, LANES)
    small_all = _ag_small("ag_small", small).reshape(N_DEV, small_rows * LANES)
    c_all = small_all[:, :d]
    ng_all = small_all[:, d:d + n_ng].reshape(N_DEV, depth, 4, d_loc).transpose(1, 2, 0, 3).reshape(depth, 4, d)
    cw_all = small_all[:, d + n_ng:d + n_ng + 3 * d_loc].reshape(N_DEV, 3, d_loc).transpose(1, 0, 2).reshape(3, d)
    cw8 = _pad_rows8(cw_all)
    cond_all = _silu("silu", c_all)

    mod_parts = jnp.stack([_mod_matvec(f"mod_matvec{i}", cond_all, w_mod[i]) for i in range(depth)])
    mod_rows = depth * N_DEV * n_mod_loc // LANES
    mod_all = _ag_small("ag_mod", mod_parts.reshape(mod_rows, LANES)).reshape(N_DEV, depth, N_DEV, n_mod_loc)
    mod_mine = lax.dynamic_index_in_dim(mod_all, dev, axis=2, keepdims=False)
    mod = mod_mine.transpose(1, 0, 2).reshape(depth, N_DEV * n_mod_loc) + b_mod
    mod = mod.reshape(depth, 6, d)

    dev_idx = jnp.reshape(dev, (1,)).astype(jnp.int32)
    later = [mla_w_o[0], mlp_w_up[0], mlp_w_down[0], conv_w_in[0], conv_w_out[0], mlp_w_up[1], mlp_w_down[1]]
    lands = [_cast_own_block(f"cast_own_block{a}", w, dev_idx) for a, w in enumerate(later)]
    ag_groups = [[0], [1, 2], [3, 4, 5, 6]]
    ag_sems, lands_thru, ag_token = _ag_start("ag_start", lands, ag_groups, [mod, g_ukv])

    def fwd_vec(nb, gate, na, sc, sh, token=None):
        zero = jnp.zeros((d,), F32)
        rows = [zero if r is None else r for r in (nb, gate, na, sc, sh)]
        vec = _pad_rows8(jnp.stack(rows))
        return vec if token is None else vec + token[0, 0]

    inv_freq = ROPE_THETA ** (-jnp.arange(0, ROPE_DIM, 2, dtype=F32) / ROPE_DIM)
    ang = positions[0].astype(F32)[:, None] * inv_freq
    cos, sin = jnp.cos(ang), jnp.sin(ang)
    zh = jnp.zeros_like(cos)
    zpad = jnp.zeros((t, LANES - ROPE_DIM), F32)
    rope_c = jnp.concatenate([cos, cos, zpad], axis=1)
    rope_s1 = jnp.concatenate([-sin, zh, zpad], axis=1)
    rope_s2 = jnp.concatenate([zh, sin, zpad], axis=1)

    (h0,) = _norm_fwd("pre0", x0, None, fwd_vec(None, None, ng_all[0, 0], mod[0, 1], mod[0, 0], ag_token), None, False, True, False)
    (lat,) = _linear("mla_lat", h0, g_win)
    cq, ckv, kr = _lat_post("mla_lat_post", lat, mla_g_q, mla_g_kv, rope_c, rope_s1, rope_s2, rq, rkv)
    (q_raw,) = _linear("mla_q", cq, g_uq)
    qb = _rope_heads("mla_q_rope", q_raw, 0, q_raw, 1, rope_c, rope_s1, rope_s2, heads)
    fs_o, fr_o, lands_o, _ = _ag_forward("ag_forward_o", lands_thru[0:1], *ag_sems[0], qb)
    (kvb,) = _linear("mla_kv", ckv, g_ukv, out_dtypes=(BF,))
    (g_wo,) = _ag_finish("ag_finish_o", lands_o, fs_o, fr_o, kvb)
    g_wo = merged(g_wo)
    o, lse = _flash_fwd("mla_attn", qb, kvb, kr, heads, scale)
    fs_a, fr_a, lands_a, _ = _ag_forward("ag_forward_a", lands_thru[1:3], *ag_sems[1], o)
    (y0,) = _linear("mla_out", o, g_wo)
    x1, h1 = _norm_fwd("postpre1", x0, y0, fwd_vec(ng_all[0, 1], mod[0, 2], ng_all[0, 2], mod[0, 4], mod[0, 3]), None, True, True, False)
    g_up0, g_dn0 = _ag_finish("ag_finish_a", lands_a, fs_a, fr_a, x1)
    g_dn0 = merged(g_dn0)

    def sq_relu(acc):
        a = jnp.maximum(acc, 0.0)
        return acc, a * a

    u1, act1 = _linear("mlp0_up", h1, g_up0, out_dtypes=(F32, BF), epilogue=sq_relu)
    (y1,) = _linear("mlp0_down", act1, g_dn0)
    fs_b, fr_b, lands_b, _ = _ag_forward("ag_forward_b", lands_thru[3:7], *ag_sems[2], y1)
    x2, h2 = _norm_fwd("postpre2", x1, y1, fwd_vec(ng_all[0, 3], mod[0, 5], ng_all[1, 0], mod[1, 1], mod[1, 0]), None, True, True, False)
    g_cin, g_cout, g_up1, g_dn1 = _ag_finish("ag_finish_b", lands_b, fs_b, fr_b, x2)
    g_cout, g_dn1 = merged(g_cout), merged(g_dn1)
    g_up, g_dn = [g_up0, g_up1], [g_dn0, g_dn1]

    (proj,) = _linear("conv_in", h2, g_cin)
    bz = _conv_fwd("conv_mix", proj, cw8)
    (y2,) = _linear("conv_out", bz, g_cout)
    x3, h3 = _norm_fwd("postpre3", x2, y2, fwd_vec(ng_all[1, 1], mod[1, 2], ng_all[1, 2], mod[1, 4], mod[1, 3]), None, True, True, False)

    u3, act3 = _linear("mlp1_up", h3, g_up[1], out_dtypes=(F32, BF), epilogue=sq_relu)
    (y3,) = _linear("mlp1_down", act3, g_dn[1])
    dx4, loss_cols = _norm_fwd("post_loss", x3, y3, fwd_vec(ng_all[1, 3], mod[1, 5], None, None, None), target, True, False, True)
    loss = lax.psum(0.5 * jnp.sum(loss_cols[0]) / d, ("x", "y", "c"))

    def bwd_vec(nb, gate, na, sc, token=None):
        zero = jnp.zeros((d,), F32)
        rows = [zero if r is None else r for r in (nb, gate, na, sc)]
        vec = _pad_rows8(jnp.stack(rows))
        return vec if token is None else vec + token[0, 0]

    def blocks(g, like):
        return g.reshape((N_DEV,) + like.shape)

    def rs_begin(tag, grads):
        return _push_start(f"rs_sibling_start_{tag}", grads, _sibling_routes)

    def rs_continue(tag, started, after):
        send_sems, recv_sems, grads, from_sibling, _ = started
        grads, from_sibling = _push_wait(f"rs_sibling_wait_{tag}", grads, from_sibling, send_sems, recv_sems, _sibling_routes, after)
        pair = [_pair_add(f"pair_add_{tag}{a}", grads[a], from_sibling[a], c_idx) for a in range(len(grads))]
        return _push_start(f"rs_start_{tag}", pair, _chip_routes)

    def mlp_bwd(tag, dy, h, u, act, w_up, w_dn, w_dn_shard):
        def relu_grad(acc, uv):
            return (acc * (2.0 * jnp.maximum(uv, 0.0)),)

        (du,) = _linear_t(f"{tag}_dact", dy, w_dn, out_dtypes=(BF,), epilogue=relu_grad, extras=(u,))
        dw_dn = _wgrad(f"{tag}_dw_down", act, dy, w_dn)
        dw_up = _wgrad(f"{tag}_dw_up", h, du, w_up)
        sib = rs_begin(tag, [dw_up, blocks(dw_dn, w_dn_shard)])
        (dh,) = _linear_t(f"{tag}_dh", du, w_up, token=sib[4])
        return dh, rs_continue(tag, sib, dh)

    dy3, sums4 = _norm_bwd("bwd_post3", dx4, None, None, y3, bwd_vec(ng_all[1, 3], mod[1, 5], None, None), False, True)
    dh3, rs_mlp1 = mlp_bwd("mlp1", dy3, h3, u3, act3, g_up[1], g_dn[1], mlp_w_down[1])
    dx3, dy2, sums3 = _norm_bwd("bwd_norm3", dx4, dh3, x3, y2, bwd_vec(ng_all[1, 1], mod[1, 2], ng_all[1, 2], mod[1, 4], rs_mlp1[4]), True, True)

    (dbz,) = _linear_t("conv_dbz", dy2, g_cout)
    dw_cout = _wgrad("conv_dw_out", bz, dy2, g_cout)
    dproj, conv_sums = _conv_bwd("conv_mix_bwd", dbz, proj, cw8)
    dw_cin = _wgrad("conv_dw_in", h2, dproj, g_cin)
    sib_conv = rs_begin("conv", [dw_cin, blocks(dw_cout, conv_w_out[0])])
    (dh2,) = _linear_t("conv_dh", dproj, g_cin, token=sib_conv[4])
    rs_conv = rs_continue("conv", sib_conv, dh2)
    dx2, dy1, sums2 = _norm_bwd("bwd_norm2", dx3, dh2, x2, y1, bwd_vec(ng_all[0, 3], mod[0, 5], ng_all[1, 0], mod[1, 1], rs_conv[4]), True, True)

    dh1, rs_mlp0 = mlp_bwd("mlp0", dy1, h1, u1, act1, g_up[0], g_dn[0], mlp_w_down[0])
    dx1, dy0, sums1 = _norm_bwd("bwd_norm1", dx2, dh1, x1, y0, bwd_vec(ng_all[0, 1], mod[0, 2], ng_all[0, 2], mod[0, 4], rs_mlp0[4]), True, True)

    (do,) = _linear_t("mla_do", dy0, g_wo, out_dtypes=(BF,))
    dw_o = _wgrad("mla_dw_o", o, dy0, g_wo)
    dkv, dkr, dqn, dqr = _flash_bwd("mla_attn_bwd", qb, kvb, kr, o, lse, do, heads, scale)
    dq = _rope_heads("mla_dq_rope", dqn, 0, dqr, 0, rope_c, -rope_s1, -rope_s2, heads)
    dw_uq = _wgrad("mla_dw_uq", cq, dq, g_uq)
    (dcq,) = _linear_t("mla_dcq", dq, g_uq)
    dw_ukv = _wgrad("mla_dw_ukv", ckv, dkv, g_ukv)
    (dckv,) = _linear_t("mla_dckv", dkv, g_ukv)
    dlat, lat_sums = _lat_bwd("mla_lat_bwd", dcq, dckv, dkr, lat, mla_g_q, mla_g_kv, rope_c, -rope_s1, -rope_s2, rq, rkv, heads)
    dw_win = _wgrad("mla_dw_in", h0, dlat, g_win)
    sib_mla = rs_begin("mla", [blocks(dw_win, w_in_loc), blocks(dw_uq, uq_loc), blocks(dw_ukv, ukv_loc), blocks(dw_o, mla_w_o[0])])
    (dh0,) = _linear_t("mla_dh", dlat, g_win, token=sib_mla[4])
    rs_mla = rs_continue("mla", sib_mla, dh0)
    grad_x, sums0 = _norm_bwd("bwd_pre0", dx1, dh0, x0, None, bwd_vec(None, None, ng_all[0, 0], mod[0, 1], rs_mla[4]), True, False)

    def sums_rows(s, rows):
        return [s[r] for r in rows]

    dmod = jnp.stack([
        jnp.stack(sums_rows(sums0, (0, 1)) + sums_rows(sums1, (3,)) + sums_rows(sums1, (0, 1)) + sums_rows(sums2, (3,))),
        jnp.stack(sums_rows(sums2, (0, 1)) + sums_rows(sums3, (3,)) + sums_rows(sums3, (0, 1)) + sums_rows(sums4, (3,))),
    ])
    dng = jnp.stack([
        jnp.stack([sums0[2], sums1[4], sums1[2], sums2[4]]),
        jnp.stack([sums2[2], sums3[4], sums3[2], sums4[4]]),
    ])
    parts = [dmod.reshape(-1), dng.reshape(-1), lat_sums[0], lat_sums[1], conv_sums[0:3].reshape(-1)]
    sizes = [p.shape[0] for p in parts]
    packed = jnp.concatenate(parts)
    pk_rows = -(-packed.shape[0] // (8 * LANES)) * 8
    packed = jnp.pad(packed, (0, pk_rows * LANES - packed.shape[0])).reshape(pk_rows, LANES)
    packed_all = _ag_small("ag_small_grads", packed).reshape(N_DEV, pk_rows, LANES)
    total = _sum_devices("sum_small_grads", packed_all).reshape(-1)
    offs = [0]
    for s in sizes:
        offs.append(offs[-1] + s)
    g_b_mod = total[offs[0]:offs[1]].reshape(depth, 6 * d)
    ng_full = total[offs[1]:offs[2]].reshape(depth, 4, d)
    g_norm_g = lax.dynamic_slice_in_dim(ng_full, dev * d_loc, d_loc, axis=2)
    g_g_q = total[offs[2]:offs[3]].reshape(1, rq)
    g_g_kv = total[offs[3]:offs[4]].reshape(1, rkv)
    cw_full = total[offs[4]:offs[5]].reshape(1, 3, d)
    g_conv_w = lax.dynamic_slice_in_dim(cw_full, dev * d_loc, d_loc, axis=2)

    dmod_all = packed_all.reshape(N_DEV, -1)[:, :sizes[0]].reshape(N_DEV, depth, 6 * d)
    dmod_cols = lax.dynamic_slice_in_dim(dmod_all, dev * n_mod_loc, n_mod_loc, axis=2)
    cond_t = cond_all.T
    mod_out = [_mod_grad_adam(f"w_mod_adam{i}", cond_t, dmod_cols[:, i, :], w_mod[i], m_w_mod[i], v_w_mod[i]) for i in range(depth)]
    g_w_mod, d_w_mod, nm_w_mod, nv_w_mod = [jnp.stack([mod_out[i][k] for i in range(depth)]) for k in range(4)]

    def rs_end(tag, started):
        send_sems, recv_sems, pair, lands, _ = started
        return _push_wait(f"rs_wait_{tag}", pair, lands, send_sems, recv_sems, _chip_routes, g_w_mod)

    pair_mlp1, lands_mlp1 = rs_end("mlp1", rs_mlp1)
    pair_conv, lands_conv = rs_end("conv", rs_conv)
    pair_mlp0, lands_mlp0 = rs_end("mlp0", rs_mlp0)
    pair_mla, lands_mla = rs_end("mla", rs_mla)

    results = {"w_mod": (g_w_mod, d_w_mod, nm_w_mod, nv_w_mod)}
    given = {"b_mod": (b_mod, m_b_mod, v_b_mod), "norm_g": (norm_g, m_norm_g, v_norm_g),
             "mla_w_in": (mla_w_in, m_mla_w_in, v_mla_w_in), "mla_g_q": (mla_g_q, m_mla_g_q, v_mla_g_q),
             "mla_g_kv": (mla_g_kv, m_mla_g_kv, v_mla_g_kv), "mla_w_uq": (mla_w_uq, m_mla_w_uq, v_mla_w_uq),
             "mla_w_ukv": (mla_w_ukv, m_mla_w_ukv, v_mla_w_ukv), "mla_w_o": (mla_w_o, m_mla_w_o, v_mla_w_o),
             "conv_w_in": (conv_w_in, m_conv_w_in, v_conv_w_in), "conv_w": (conv_w, m_conv_w, v_conv_w),
             "conv_w_out": (conv_w_out, m_conv_w_out, v_conv_w_out), "mlp_w_up": (mlp_w_up, m_mlp_w_up, v_mlp_w_up),
             "mlp_w_down": (mlp_w_down, m_mlp_w_down, v_mlp_w_down)}

    def fused(name, parts, prev=None):
        for layer, (pair, lands) in enumerate(parts):
            prev = _adam_sum(f"adam_{name}{layer}", pair, lands, chip_idx, *given[name], layer, prev)
        results[name] = tuple(prev)

    fused("mlp_w_up", [(pair_mlp0[0], lands_mlp0[0]), (pair_mlp1[0], lands_mlp1[0])])
    fused("mlp_w_down", [(pair_mlp0[1], lands_mlp0[1]), (pair_mlp1[1], lands_mlp1[1])])
    fused("conv_w_in", [(pair_conv[0], lands_conv[0])])
    fused("conv_w_out", [(pair_conv[1], lands_conv[1])])
    fused("mla_w_o", [(pair_mla[3], lands_mla[3])])

    red_win, red_uq, red_ukv = [_chip_sum(f"chip_sum_mla{a}", pair_mla[a], lands_mla[a], chip_idx) for a in range(3)]
    plain = {"b_mod": g_b_mod, "norm_g": g_norm_g, "mla_g_q": g_g_q, "mla_g_kv": g_g_kv, "conv_w": g_conv_w,
             "mla_w_in": red_win[:, :lat_w][None],
             "mla_w_uq": jnp.concatenate(
                 [red_uq[:, :heads * NOPE_DIM].reshape(-1, heads, NOPE_DIM),
                  red_uq[:, heads * NOPE_DIM:].reshape(-1, heads, LANES)[:, :, :ROPE_DIM]], axis=2)[None],
             "mla_w_ukv": red_ukv.reshape(mla_w_ukv.shape)}
    for name, g in plain.items():
        results[name] = (g, *_adam(f"adam_{name}", given[name][0], g, given[name][1], given[name][2]))

    order = ["w_mod", "b_mod", "norm_g", "mla_w_in", "mla_g_q", "mla_g_kv", "mla_w_uq", "mla_w_ukv", "mla_w_o",
             "conv_w_in", "conv_w", "conv_w_out", "mlp_w_up", "mlp_w_down"]
    return (loss, grad_x[None], *[results[n][0] for n in order], *[results[n][1] for n in order],
            *[results[n][2] for n in order], *[results[n][3] for n in order])
```

```python
import jax
import jax.numpy as jnp
from jax import lax
from jax.experimental import pallas as pl
from jax.experimental.pallas import tpu as pltpu

F32 = jnp.float32
BF = jnp.bfloat16
MESH = pl.DeviceIdType.MESH
N_DEV = 8
LANES = 128
NORM_EPS = 1e-6
ROPE_THETA = 10000.0
ROPE_DIM = 64
NOPE_DIM = 128
V_DIM = 128
LR, B1, B2, ADAM_EPS, WD, STEP = 0.001, 0.9, 0.999, 1e-08, 0.01, 10
VMEM_LIMIT = 56 * 1024 * 1024
TILE_BYTES = 2 * 1024 * 1024
MM_TILE_PREFS = ((1024, 1024), (1024, 512), (512, 1024), (512, 512), (256, 512), (256, 256), (128, 256), (128, 128))
MM_VMEM_BUDGET = 36 * 1024 * 1024
TQ = 512
Q_CHAINS = 1
LOG2E = 1.4426950408889634

NN = (((1,), (0,)), ((), ()))
NT = (((1,), (1,)), ((), ()))
TNDIMS = (((0,), (0,)), ((), ()))
HBM_SPEC = pl.BlockSpec(memory_space=pltpu.HBM)
VMEM_SPEC = pl.BlockSpec(memory_space=pltpu.VMEM)


def _cparams(n_axes):
    return pltpu.CompilerParams(dimension_semantics=("arbitrary",) * n_axes, vmem_limit_bytes=VMEM_LIMIT)


def _pick(dim, pref):
    if dim <= pref:
        return dim
    for t in range(pref - pref % LANES, 0, -LANES):
        if t > 0 and dim % t == 0:
            return t
    for t in range(pref, 0, -1):
        if dim % t == 0:
            return t
    return dim


def _rows(r, c, itemsize=4):
    want = max(8, TILE_BYTES // (itemsize * max(c, 1)))
    if r <= want:
        return r
    for t in range(want - want % 8, 0, -8):
        if t > 0 and r % t == 0:
            return t
    return r


def _pos():
    return lax.axis_index("x"), lax.axis_index("y"), lax.axis_index("c")


def _ag_small(name, v):
    m_per, n = v.shape

    def body(x_ref, out_ref, send_sems, recv_sems, local_sem):
        x, y, c = _pos()
        me, sibling = (x, y, c), (x, y, 1 - c)
        chips = [(1 - x, y), (x, 1 - y), (1 - x, 1 - y)]

        def rows(px, py, pc):
            return out_ref.at[pl.ds((4 * px + 2 * py + pc) * m_per, m_per), :]

        def copy(k, block, to, src=None):
            return pltpu.make_async_remote_copy(
                src_ref=rows(*block) if src is None else src, dst_ref=rows(*block),
                send_sem=send_sems.at[k], recv_sem=recv_sems.at[k], device_id=to, device_id_type=MESH)

        mine = pltpu.make_async_copy(x_ref, rows(*me), local_sem)
        mine.start()
        first = [copy(0, me, sibling, src=x_ref)]
        first += [copy(1 + j, me, (*chip, c), src=x_ref) for j, chip in enumerate(chips)]
        for cp in first:
            cp.start()
        passed = [copy(4 + j, (*chip, c), sibling) for j, chip in enumerate(chips)]
        for j, chip in enumerate(chips):
            copy(1 + j, (*chip, c), me).wait_recv()
            passed[j].start()
        copy(0, sibling, me).wait_recv()
        for j, chip in enumerate(chips):
            copy(4 + j, (*chip, 1 - c), me).wait_recv()
        for cp in first + passed:
            cp.wait_send()
        mine.wait()

    return pl.pallas_call(
        body, name=name,
        out_shape=jax.ShapeDtypeStruct((N_DEV * m_per, n), v.dtype),
        in_specs=[VMEM_SPEC], out_specs=VMEM_SPEC,
        scratch_shapes=[pltpu.SemaphoreType.DMA((7,)), pltpu.SemaphoreType.DMA((7,)), pltpu.SemaphoreType.DMA],
    )(v)


def _ag_big(name, arrs):
    n = len(arrs)

    def body(*refs):
        ins, outs = refs[:n], refs[n:2 * n]
        send_sems, recv_sems, local_sems = refs[2 * n:]
        x, y, c = _pos()
        me, sibling = (x, y, c), (x, y, 1 - c)
        chips = [(1 - x, y), (x, 1 - y), (1 - x, 1 - y)]

        def copy(a, k, block, to, src=None):
            dst = outs[a].at[4 * block[0] + 2 * block[1] + block[2]]
            return pltpu.make_async_remote_copy(
                src_ref=dst if src is None else src, dst_ref=dst,
                send_sem=send_sems.at[7 * a + k], recv_sem=recv_sems.at[7 * a + k],
                device_id=to, device_id_type=MESH)

        mine = [pltpu.make_async_copy(ins[a], outs[a].at[4 * x + 2 * y + c], local_sems.at[a]) for a in range(n)]
        for cp in mine:
            cp.start()
        first = []
        for a in range(n):
            first.append(copy(a, 0, me, sibling, src=ins[a]))
            first += [copy(a, 1 + j, me, (*chip, c), src=ins[a]) for j, chip in enumerate(chips)]
        for cp in first:
            cp.start()
        passed = []
        for j, chip in enumerate(chips):
            for a in range(n):
                copy(a, 1 + j, (*chip, c), me).wait_recv()
                cp = copy(a, 4 + j, (*chip, c), sibling)
                cp.start()
                passed.append(cp)
        for a in range(n):
            copy(a, 0, sibling, me).wait_recv()
            for j, chip in enumerate(chips):
                copy(a, 4 + j, (*chip, 1 - c), me).wait_recv()
        for cp in first + passed:
            cp.wait_send()
        for cp in mine:
            cp.wait()

    return pl.pallas_call(
        body, name=name,
        out_shape=[jax.ShapeDtypeStruct((N_DEV,) + a.shape, a.dtype) for a in arrs],
        in_specs=[HBM_SPEC] * n, out_specs=[HBM_SPEC] * n,
        scratch_shapes=[pltpu.SemaphoreType.DMA((7 * n,)), pltpu.SemaphoreType.DMA((7 * n,)),
                        pltpu.SemaphoreType.DMA((n,))],
    )(*arrs)


SEM_SPEC = pl.BlockSpec(memory_space=pltpu.SEMAPHORE)
ANY_SPEC = pl.BlockSpec(memory_space=pl.ANY)
EFFECT = pltpu.SideEffectType.DATAFLOW_SIDE_EFFECTING


def _hbm(a):
    return pltpu.with_memory_space_constraint(a, pltpu.HBM)


def _cast_own_block(name, w, layer, dev_idx):
    _, r, cdim = w.shape
    tr = _rows(r, cdim, 4)

    def body(d_ref, w_ref, o_ref):
        o_ref[...] = w_ref[...].astype(BF)

    return pl.pallas_call(
        body, name=name,
        out_shape=jax.ShapeDtypeStruct((N_DEV, r, cdim), BF),
        grid_spec=pltpu.PrefetchScalarGridSpec(
            num_scalar_prefetch=1, grid=(r // tr,),
            in_specs=[pl.BlockSpec((None, tr, cdim), lambda i, d_ref: (layer, i, 0))],
            out_specs=pl.BlockSpec((None, tr, cdim), lambda i, d_ref: (d_ref[0], i, 0))),
        compiler_params=_cparams(1),
    )(dev_idx, w)


def _ag_start(name, lands, groups, after):
    n = len(lands)
    ng = len(groups)
    n_after = len(after)

    def body(*refs):
        lnd = refs[:n]
        sems = refs[n + n_after:n + n_after + 2 * ng]
        token = refs[-1]
        x, y, c = _pos()
        targets = [(x, y, 1 - c), (1 - x, y, c), (x, 1 - y, c), (1 - x, 1 - y, c)]
        for gi, members in enumerate(groups):
            for pos, a in enumerate(members):
                own = lnd[a].at[4 * x + 2 * y + c]
                for k, to in enumerate(targets):
                    pltpu.make_async_remote_copy(
                        src_ref=own, dst_ref=own,
                        send_sem=sems[2 * gi].at[4 * pos + k], recv_sem=sems[2 * gi + 1].at[4 * pos + k],
                        device_id=to, device_id_type=MESH).start()
        token[...] = jnp.zeros_like(token)

    sem_shapes = []
    for members in groups:
        sem_shapes += [pltpu.SemaphoreType.DMA((4 * len(members),))] * 2
    res = pl.pallas_call(
        body, name=name,
        out_shape=(*sem_shapes, *[pltpu.HBM(a.shape, a.dtype) for a in lands], jax.ShapeDtypeStruct((8, LANES), F32)),
        in_specs=[HBM_SPEC] * n + [ANY_SPEC] * n_after,
        out_specs=(*[SEM_SPEC] * (2 * ng), *[HBM_SPEC] * n, VMEM_SPEC),
        input_output_aliases={i: 2 * ng + i for i in range(n)},
        compiler_params=pltpu.CompilerParams(has_side_effects=EFFECT),
    )(*[_hbm(a) for a in lands], *after)
    sem_pairs = [(res[2 * g], res[2 * g + 1]) for g in range(ng)]
    return sem_pairs, list(res[2 * ng:2 * ng + n]), res[-1]


def _ag_forward(name, lands, send_sems, recv_sems, after):
    n = len(lands)

    def body(*refs):
        lnd = refs[:n]
        s_sems, r_sems = refs[n], refs[n + 1]
        fs_sems, fr_sems = refs[n + 3], refs[n + 4]
        token = refs[-1]
        x, y, c = _pos()
        sources = [(x, y, 1 - c), (1 - x, y, c), (x, 1 - y, c), (1 - x, 1 - y, c)]
        for a in range(n):
            own = lnd[a].at[4 * x + 2 * y + c]
            for k, src in enumerate(sources):
                blk = lnd[a].at[4 * src[0] + 2 * src[1] + src[2]]
                arrived = pltpu.make_async_remote_copy(
                    src_ref=own, dst_ref=blk, send_sem=s_sems.at[4 * a + k], recv_sem=r_sems.at[4 * a + k],
                    device_id=src, device_id_type=MESH)
                arrived.wait_send()
                arrived.wait_recv()
                if k > 0:
                    pltpu.make_async_remote_copy(
                        src_ref=blk, dst_ref=blk, send_sem=fs_sems.at[3 * a + k - 1], recv_sem=fr_sems.at[3 * a + k - 1],
                        device_id=(x, y, 1 - c), device_id_type=MESH).start()
        token[...] = jnp.zeros_like(token)

    res = pl.pallas_call(
        body, name=name,
        out_shape=(pltpu.SemaphoreType.DMA((3 * n,)), pltpu.SemaphoreType.DMA((3 * n,)),
                   *[pltpu.HBM(a.shape, a.dtype) for a in lands], jax.ShapeDtypeStruct((8, LANES), F32)),
        in_specs=[HBM_SPEC] * n + [SEM_SPEC, SEM_SPEC, ANY_SPEC],
        out_specs=(SEM_SPEC, SEM_SPEC, *[HBM_SPEC] * n, VMEM_SPEC),
        input_output_aliases={i: 2 + i for i in range(n)},
        compiler_params=pltpu.CompilerParams(has_side_effects=EFFECT),
    )(*lands, send_sems, recv_sems, after)
    return res[0], res[1], list(res[2:2 + n]), res[-1]


def _ag_finish(name, lands, fs_sems, fr_sems, after):
    n = len(lands)

    def body(*refs):
        lnd = refs[:n]
        fs, fr = refs[n], refs[n + 1]
        x, y, c = _pos()
        chips = [(1 - x, y), (x, 1 - y), (1 - x, 1 - y)]
        for a in range(n):
            for j, chip in enumerate(chips):
                mine = lnd[a].at[4 * chip[0] + 2 * chip[1] + c]
                theirs = lnd[a].at[4 * chip[0] + 2 * chip[1] + 1 - c]
                cp = pltpu.make_async_remote_copy(
                    src_ref=mine, dst_ref=theirs, send_sem=fs.at[3 * a + j], recv_sem=fr.at[3 * a + j],
                    device_id=(x, y, 1 - c), device_id_type=MESH)
                cp.wait_send()
                cp.wait_recv()

    res = pl.pallas_call(
        body, name=name,
        out_shape=tuple(pltpu.HBM(a.shape, a.dtype) for a in lands),
        in_specs=[HBM_SPEC] * n + [SEM_SPEC, SEM_SPEC, ANY_SPEC],
        out_specs=tuple([HBM_SPEC] * n),
        input_output_aliases={i: i for i in range(n)},
        compiler_params=pltpu.CompilerParams(has_side_effects=EFFECT),
    )(*lands, fs_sems, fr_sems, after)
    return list(res)


def _sibling_routes(x, y, c):
    return [(2 * j + (1 - c), (x, y, 1 - c)) for j in range(4)]


def _chip_routes(x, y, c):
    return [(2 * cx + cy, (cx, cy, c)) for cx, cy in [(1 - x, y), (x, 1 - y), (1 - x, 1 - y)]]


def _push_start(name, arrs, routes):
    n = len(arrs)
    nr = len(routes(0, 0, 0))

    def body(*refs):
        ins, lnd = refs[:n], refs[n:2 * n]
        s_sems, r_sems = refs[2 * n], refs[2 * n + 1]
        token = refs[-1]
        for a in range(n):
            for k, (blk, to) in enumerate(routes(*_pos())):
                pltpu.make_async_remote_copy(
                    src_ref=ins[a].at[blk], dst_ref=lnd[a].at[k],
                    send_sem=s_sems.at[nr * a + k], recv_sem=r_sems.at[nr * a + k],
                    device_id=to, device_id_type=MESH).start()
        token[...] = jnp.zeros_like(token)

    lands = [lax.empty((nr,) + a.shape[1:], a.dtype) for a in arrs]
    res = pl.pallas_call(
        body, name=name,
        out_shape=(pltpu.SemaphoreType.DMA((nr * n,)), pltpu.SemaphoreType.DMA((nr * n,)),
                   *[pltpu.HBM(a.shape, a.dtype) for a in arrs], *[pltpu.HBM(a.shape, a.dtype) for a in lands],
                   jax.ShapeDtypeStruct((8, LANES), F32)),
        in_specs=[HBM_SPEC] * (2 * n),
        out_specs=(SEM_SPEC, SEM_SPEC, *[HBM_SPEC] * (2 * n), VMEM_SPEC),
        input_output_aliases={i: 2 + i for i in range(2 * n)},
        compiler_params=pltpu.CompilerParams(has_side_effects=EFFECT),
    )(*[_hbm(a) for a in arrs], *[_hbm(a) for a in lands])
    return res[0], res[1], list(res[2:2 + n]), list(res[2 + n:2 + 2 * n]), res[-1]


def _push_wait(name, arrs, lands, send_sems, recv_sems, routes, after):
    n = len(arrs)
    nr = len(routes(0, 0, 0))

    def body(*refs):
        ins, lnd = refs[:n], refs[n:2 * n]
        s_sems, r_sems = refs[2 * n], refs[2 * n + 1]
        for a in range(n):
            for k, (blk, to) in enumerate(routes(*_pos())):
                cp = pltpu.make_async_remote_copy(
                    src_ref=ins[a].at[blk], dst_ref=lnd[a].at[k],
                    send_sem=s_sems.at[nr * a + k], recv_sem=r_sems.at[nr * a + k],
                    device_id=to, device_id_type=MESH)
                cp.wait_send()
                cp.wait_recv()

    res = pl.pallas_call(
        body, name=name,
        out_shape=tuple(pltpu.HBM(a.shape, a.dtype) for a in list(arrs) + list(lands)),
        in_specs=[HBM_SPEC] * (2 * n) + [SEM_SPEC, SEM_SPEC, ANY_SPEC],
        out_specs=tuple([HBM_SPEC] * (2 * n)),
        input_output_aliases={i: i for i in range(2 * n)},
        compiler_params=pltpu.CompilerParams(has_side_effects=EFFECT),
    )(*arrs, *lands, send_sems, recv_sems, after)
    return list(res[:n]), list(res[n:])


def _pair_add(name, g, r1, c_idx):
    _, r, cdim = g.shape
    tr = _rows(r, cdim, 2)

    def body(c_ref, g_ref, r_ref, o_ref):
        o_ref[...] = (g_ref[...].astype(F32) + r_ref[...].astype(F32)).astype(o_ref.dtype)

    return pl.pallas_call(
        body, name=name,
        out_shape=jax.ShapeDtypeStruct((4, r, cdim), BF),
        grid_spec=pltpu.PrefetchScalarGridSpec(
            num_scalar_prefetch=1, grid=(4, r // tr),
            in_specs=[pl.BlockSpec((None, tr, cdim), lambda j, i, c_ref: (2 * j + c_ref[0], i, 0)),
                      pl.BlockSpec((None, tr, cdim), lambda j, i, c_ref: (j, i, 0))],
            out_specs=pl.BlockSpec((None, tr, cdim), lambda j, i, c_ref: (j, i, 0))),
        compiler_params=_cparams(2),
    )(c_idx, g, r1)


def _chip_sum(name, p, r2, chip_idx):
    _, r, cdim = p.shape
    tr = _rows(r, cdim, 4)

    def body(s_ref, p_ref, a_ref, b_ref, c_ref, o_ref):
        o_ref[...] = ((p_ref[...].astype(F32) + a_ref[...].astype(F32)) + b_ref[...].astype(F32)) + c_ref[...].astype(F32)

    def other(k):
        return pl.BlockSpec((None, tr, cdim), lambda i, s_ref: (k, i, 0))

    return pl.pallas_call(
        body, name=name,
        out_shape=jax.ShapeDtypeStruct((r, cdim), F32),
        grid_spec=pltpu.PrefetchScalarGridSpec(
            num_scalar_prefetch=1, grid=(r // tr,),
            in_specs=[pl.BlockSpec((None, tr, cdim), lambda i, s_ref: (s_ref[0], i, 0)), other(0), other(1), other(2)],
            out_specs=pl.BlockSpec((tr, cdim), lambda i, s_ref: (i, 0))),
        compiler_params=_cparams(1),
    )(chip_idx, p, r2, r2, r2)


def _adam_math(w, g, m, v):
    m = B1 * m + (1.0 - B1) * g
    v = B2 * v + (1.0 - B2) * (g * g)
    m_hat = m / (1.0 - B1 ** STEP)
    v_hat = v / (1.0 - B2 ** STEP)
    delta = -LR * (m_hat / (jnp.sqrt(v_hat) + ADAM_EPS) + WD * w)
    return delta, m, v


def _adam(name, w, g, m, v):
    shape = w.shape
    cdim = shape[-1]
    r = w.size // cdim
    flat = [a.reshape(r, cdim) for a in (w, g, m, v)]
    tr = _rows(r, cdim, 4)

    def body(w_ref, g_ref, m_ref, v_ref, d_ref, mo_ref, vo_ref):
        d, mn, vn = _adam_math(w_ref[...], g_ref[...], m_ref[...], v_ref[...])
        d_ref[...] = d
        mo_ref[...] = mn
        vo_ref[...] = vn

    spec = pl.BlockSpec((tr, cdim), lambda i: (i, 0))
    outs = pl.pallas_call(
        body, name=name,
        out_shape=[jax.ShapeDtypeStruct((r, cdim), F32)] * 3,
        grid=(r // tr,), in_specs=[spec] * 4, out_specs=[spec] * 3,
        compiler_params=_cparams(1),
    )(*flat)
    return [o.reshape(shape) for o in outs]


def _adam_sum(name, p, r2, chip_idx, w, m, v, layer, prev=None):
    nl, r, cdim = w.shape
    tr = _rows(r, cdim, 4)

    def body(s_ref, p_ref, a_ref, b_ref, c_ref, w_ref, m_ref, v_ref, *rest):
        g_ref, d_ref, mo_ref, vo_ref = rest[-4:]
        g = ((p_ref[...].astype(F32) + a_ref[...].astype(F32)) + b_ref[...].astype(F32)) + c_ref[...].astype(F32)
        dl, mn, vn = _adam_math(w_ref[...], g, m_ref[...], v_ref[...])
        g_ref[...] = g
        d_ref[...] = dl
        mo_ref[...] = mn
        vo_ref[...] = vn

    def slot(k):
        return pl.BlockSpec((None, tr, cdim), lambda i, s_ref: (k, i, 0))

    in_specs = [pl.BlockSpec((None, tr, cdim), lambda i, s_ref: (s_ref[0], i, 0)), slot(0), slot(1), slot(2)] + [slot(layer)] * 3
    operands = [chip_idx, p, r2, r2, r2, w, m, v]
    aliases = {}
    if prev is not None:
        in_specs += [ANY_SPEC] * 4
        aliases = {len(operands) + k: k for k in range(4)}
        operands += list(prev)
    return pl.pallas_call(
        body, name=name,
        out_shape=[jax.ShapeDtypeStruct((nl, r, cdim), F32)] * 4,
        grid_spec=pltpu.PrefetchScalarGridSpec(
            num_scalar_prefetch=1, grid=(r // tr,), in_specs=in_specs, out_specs=[slot(layer)] * 4),
        input_output_aliases=aliases,
        compiler_params=_cparams(1),
    )(*operands)


def _mod_matvec(name, cond_all, w_loc, layer):
    _, d, n_loc = w_loc.shape
    tn = _pick(n_loc, 512)

    def body(c_ref, w_ref, o_ref):
        o_ref[...] = jnp.dot(c_ref[...].astype(BF), w_ref[...].astype(BF), preferred_element_type=F32)

    return pl.pallas_call(
        body, name=name,
        out_shape=jax.ShapeDtypeStruct((N_DEV, n_loc), F32),
        grid=(n_loc // tn,),
        in_specs=[pl.BlockSpec((N_DEV, d), lambda j: (0, 0)), pl.BlockSpec((None, d, tn), lambda j: (layer, 0, j))],
        out_specs=pl.BlockSpec((N_DEV, tn), lambda j: (0, j)),
        compiler_params=_cparams(1),
    )(cond_all, w_loc)


def _mod_grad_adam(name, cond_t, dmod, w, m, v, layer, prev=None):
    nl, d, n_loc = w.shape
    tr = _rows(d, n_loc, 4)

    def body(ct_ref, dm_ref, w_ref, m_ref, v_ref, *rest):
        g_ref, d_ref, mo_ref, vo_ref = rest[-4:]
        g = ct_ref[:, 0:1] * dm_ref[0:1, :]
        for b in range(1, N_DEV):
            g = g + ct_ref[:, b:b + 1] * dm_ref[b:b + 1, :]
        dl, mn, vn = _adam_math(w_ref[...], g, m_ref[...], v_ref[...])
        g_ref[...] = g
        d_ref[...] = dl
        mo_ref[...] = mn
        vo_ref[...] = vn

    spec = pl.BlockSpec((None, tr, n_loc), lambda i: (layer, i, 0))
    in_specs = [pl.BlockSpec((tr, N_DEV), lambda i: (i, 0)), pl.BlockSpec((N_DEV, n_loc), lambda i: (0, 0)), spec, spec, spec]
    operands = [cond_t, dmod, w, m, v]
    aliases = {}
    if prev is not None:
        in_specs += [ANY_SPEC] * 4
        aliases = {len(operands) + k: k for k in range(4)}
        operands += list(prev)
    return pl.pallas_call(
        body, name=name,
        out_shape=[jax.ShapeDtypeStruct((nl, d, n_loc), F32)] * 4,
        grid=(d // tr,), in_specs=in_specs, out_specs=[spec] * 4,
        input_output_aliases=aliases,
        compiler_params=_cparams(1),
    )(*operands)


def _sum_devices(name, gathered):
    _, m, n = gathered.shape

    def body(g_ref, o_ref):
        acc = g_ref[0]
        for d in range(1, N_DEV):
            acc = acc + g_ref[d]
        o_ref[...] = acc

    return pl.pallas_call(
        body, name=name, out_shape=jax.ShapeDtypeStruct((m, n), F32),
        in_specs=[VMEM_SPEC], out_specs=VMEM_SPEC,
    )(gathered)


def _silu(name, c_pad):
    def body(c_ref, o_ref):
        c = c_ref[...]
        o_ref[...] = c * (1.0 / (1.0 + jnp.exp(-c)))

    return pl.pallas_call(body, name=name, out_shape=jax.ShapeDtypeStruct(c_pad.shape, F32),
                          in_specs=[VMEM_SPEC], out_specs=VMEM_SPEC)(c_pad)


def _mm_tiles(m, n_cap, k, bytes_per_out_elem):
    for pm, pn in MM_TILE_PREFS:
        tm, tn = _pick(m, pm), _pick(n_cap, pn)
        need = 2 * (2 * tm * k + 2 * k * tn + bytes_per_out_elem * tm * tn) + 4 * tm * tn
        if need <= MM_VMEM_BUDGET:
            return tm, tn
    return _pick(m, 8), _pick(n_cap, LANES)


def _mm(name, a, b, grid, a_spec, b_spec, outs, dot, extras=(), epilogue=None, token=None):
    n_ex, n_out = len(extras), len(outs)
    n_in = 2 + n_ex + (0 if token is None else 1)

    def body(*refs):
        ex_refs = refs[2:2 + n_ex]
        out_refs = refs[n_in:n_in + n_out]
        acc = dot(refs[0], refs[1])
        vals = (acc,) if epilogue is None else epilogue(acc, *[e[...] for e in ex_refs])
        for o_ref, val in zip(out_refs, vals):
            o_ref[...] = val.astype(o_ref.dtype)

    ins = [a, b] + [e for e, _ in extras]
    in_specs = [a_spec, b_spec] + [s for _, s in extras]
    if token is not None:
        ins.append(token)
        in_specs.append(pl.BlockSpec(token.shape, lambda i, j: (0, 0)))
    return pl.pallas_call(
        body, name=name,
        out_shape=[jax.ShapeDtypeStruct(s, d) for s, d, _ in outs],
        grid=grid, in_specs=in_specs, out_specs=[s for _, _, s in outs],
        compiler_params=_cparams(2),
    )(*ins)


def _dot(dims):
    return lambda a_ref, b_ref: lax.dot_general(a_ref[...], b_ref[...], dims, preferred_element_type=F32)


def _w_cols(w):
    return w.shape[1] if w.ndim == 2 else w.shape[0] * w.shape[2]


def _w_rows(w):
    return w.shape[0] if w.ndim == 2 else w.shape[1]


def _out_bytes(out_dtypes, extras):
    return sum(jnp.dtype(d).itemsize for d in out_dtypes) + sum(e.dtype.itemsize for e in extras)


def _linear(name, a, w, out_dtypes=(F32,), epilogue=None, extras=(), row_extras=(), tn_multiple=1, token=None):
    t, kdim = a.shape
    n = _w_cols(w)
    tm, tn = _mm_tiles(t, w.shape[-1], kdim, _out_bytes(out_dtypes, extras))
    assert tn % tn_multiple == 0 or tn == n, (tn, tn_multiple)
    if w.ndim == 2:
        w_spec = pl.BlockSpec((kdim, tn), lambda i, j: (0, j))
    else:
        per = w.shape[2] // tn
        w_spec = pl.BlockSpec((None, kdim, tn), lambda i, j: (j // per, 0, j % per))
    o_spec = pl.BlockSpec((tm, tn), lambda i, j: (i, j))
    return _mm(name, a, w, (t // tm, n // tn), pl.BlockSpec((tm, kdim), lambda i, j: (i, 0)), w_spec,
               [((t, n), dt, o_spec) for dt in out_dtypes], _dot(NN),
               extras=[(e, o_spec) for e in extras] + [(e, pl.BlockSpec((tm, e.shape[1]), lambda i, j: (i, 0))) for e in row_extras],
               epilogue=epilogue, token=token)


def _linear_t(name, dy, w, out_dtypes=(F32,), epilogue=None, extras=(), token=None):
    t, n = dy.shape
    kdim = _w_rows(w)
    tm, tn = _mm_tiles(t, kdim, n, _out_bytes(out_dtypes, extras))
    o_spec = pl.BlockSpec((tm, tn), lambda i, j: (i, j))
    if w.ndim == 2:
        w_spec = pl.BlockSpec((tn, n), lambda i, j: (j, 0))
        dot = _dot(NT)
    else:
        n_loc = w.shape[2]
        w_spec = pl.BlockSpec((N_DEV, tn, n_loc), lambda i, j: (0, j, 0))

        def dot(a_ref, w_ref):
            acc = lax.dot_general(a_ref[:, 0:n_loc], w_ref[0], NT, preferred_element_type=F32)
            for dev in range(1, N_DEV):
                acc = acc + lax.dot_general(a_ref[:, dev * n_loc:(dev + 1) * n_loc], w_ref[dev], NT, preferred_element_type=F32)
            return acc

    return _mm(name, dy, w, (t // tm, kdim // tn), pl.BlockSpec((tm, n), lambda i, j: (i, 0)), w_spec,
               [((t, kdim), dt, o_spec) for dt in out_dtypes], dot,
               extras=[(e, o_spec) for e in extras], epilogue=epilogue, token=token)


def _wgrad(name, a, dy, like):
    t, kdim = a.shape
    n = dy.shape[1]
    tm, tn = _mm_tiles(kdim, like.shape[-1], t, 2)
    if like.ndim == 2:
        o_spec = pl.BlockSpec((tm, tn), lambda i, j: (i, j))
    else:
        per = like.shape[2] // tn
        o_spec = pl.BlockSpec((None, tm, tn), lambda i, j: (j // per, i, j % per))
    return _mm(name, a, dy, (kdim // tm, n // tn), pl.BlockSpec((t, tm), lambda i, j: (0, i)),
               pl.BlockSpec((t, tn), lambda i, j: (0, j)), [(like.shape, BF, o_spec)], _dot(TNDIMS))[0]


def _rstd(x):
    return lax.rsqrt(jnp.mean(x * x, axis=-1, keepdims=True) + NORM_EPS)


def _colsum(x):
    return jnp.sum(x, axis=0, keepdims=True)


def _norm_fwd(name, x, y, vec, target, has_post, has_pre, has_loss):
    t, d = x.shape
    tm = _rows(t, d, 4)
    ins, in_specs = [x], [pl.BlockSpec((tm, d), lambda i: (i, 0))]
    row_spec = pl.BlockSpec((tm, d), lambda i: (i, 0))
    if has_post:
        ins.append(y)
        in_specs.append(row_spec)
    ins.append(vec)
    in_specs.append(pl.BlockSpec((8, d), lambda i: (0, 0)))
    if has_loss:
        ins.append(target)
        in_specs.append(row_spec)
    out_shape, out_specs = [], []
    if has_post and not has_loss:
        out_shape.append(jax.ShapeDtypeStruct((t, d), F32))
        out_specs.append(row_spec)
    if has_pre:
        out_shape.append(jax.ShapeDtypeStruct((t, d), BF))
        out_specs.append(row_spec)
    if has_loss:
        out_shape += [jax.ShapeDtypeStruct((t, d), F32), jax.ShapeDtypeStruct((8, d), F32)]
        out_specs += [row_spec, pl.BlockSpec((8, d), lambda i: (0, 0))]

    def body(*refs):
        it = iter(refs)
        x_ref = next(it)
        y_ref = next(it) if has_post else None
        vec_ref = next(it)
        tgt_ref = next(it) if has_loss else None
        xv = x_ref[...]
        if has_post:
            yv = y_ref[...]
            xv = xv + vec_ref[1:2, :] * ((yv * _rstd(yv)) * vec_ref[0:1, :])
            if not has_loss:
                next(it)[...] = xv
        if has_pre:
            hv = ((xv * _rstd(xv)) * vec_ref[2:3, :]) * (1.0 + vec_ref[3:4, :]) + vec_ref[4:5, :]
            next(it)[...] = hv.astype(BF)
        if has_loss:
            e = xv - tgt_ref[...]
            next(it)[...] = e * (1.0 / d)
            acc_ref = next(it)
            i = pl.program_id(0)

            @pl.when(i == 0)
            def _():
                acc_ref[...] = jnp.zeros_like(acc_ref)

            acc_ref[0:1, :] += _colsum(e * e)

    return pl.pallas_call(
        body, name=name, out_shape=out_shape, grid=(t // tm,), in_specs=in_specs, out_specs=out_specs,
        compiler_params=_cparams(1),
    )(*ins)


def _norm_bwd(name, dx_in, dh, x, y, vec, has_pre, has_post):
    t, d = dx_in.shape
    tm = _rows(t, d, 4)
    row_spec = pl.BlockSpec((tm, d), lambda i: (i, 0))
    vec_spec = pl.BlockSpec((8, d), lambda i: (0, 0))
    ins, in_specs = [dx_in], [row_spec]
    if has_pre:
        ins += [dh, x]
        in_specs += [row_spec, row_spec]
    if has_post:
        ins.append(y)
        in_specs.append(row_spec)
    ins.append(vec)
    in_specs.append(vec_spec)
    out_shape, out_specs = [], []
    if has_pre:
        out_shape.append(jax.ShapeDtypeStruct((t, d), F32))
        out_specs.append(row_spec)
    if has_post:
        out_shape.append(jax.ShapeDtypeStruct((t, d), BF))
        out_specs.append(row_spec)
    out_shape.append(jax.ShapeDtypeStruct((8, d), F32))
    out_specs.append(vec_spec)

    def body(*refs):
        it = iter(refs)
        dx = next(it)[...]
        dh_ref = next(it) if has_pre else None
        x_ref = next(it) if has_pre else None
        y_ref = next(it) if has_post else None
        vec_ref = next(it)
        dx_ref = next(it) if has_pre else None
        dy_ref = next(it) if has_post else None
        sums_ref = next(it)
        i = pl.program_id(0)

        @pl.when(i == 0)
        def _():
            sums_ref[...] = jnp.zeros_like(sums_ref)

        if has_pre:
            dhv, xv = dh_ref[...], x_ref[...]
            rs = _rstd(xv)
            xhat = xv * rs
            na = vec_ref[2:3, :]
            sums_ref[0:1, :] += _colsum(dhv)
            sums_ref[1:2, :] += _colsum(dhv * (xhat * na))
            tt = dhv * (1.0 + vec_ref[3:4, :])
            sums_ref[2:3, :] += _colsum(tt * xhat)
            u = tt * na
            dx = dx + rs * (u - xhat * jnp.mean(u * xhat, axis=-1, keepdims=True))
            dx_ref[...] = dx
        if has_post:
            yv = y_ref[...]
            rs = _rstd(yv)
            yhat = yv * rs
            nb = vec_ref[0:1, :]
            sums_ref[3:4, :] += _colsum(dx * (yhat * nb))
            tt = dx * vec_ref[1:2, :]
            sums_ref[4:5, :] += _colsum(tt * yhat)
            u = tt * nb
            dy_ref[...] = (rs * (u - yhat * jnp.mean(u * yhat, axis=-1, keepdims=True))).astype(BF)

    return pl.pallas_call(
        body, name=name, out_shape=out_shape, grid=(t // tm,), in_specs=in_specs, out_specs=out_specs,
        compiler_params=_cparams(1),
    )(*ins)


def _rope128(tv, cs, s1, s2):
    return tv * cs + pltpu.roll(tv, 96, 1) * s1 + pltpu.roll(tv, 32, 1) * s2


def _lat_post(name, lat, gq, gkv, cs, s1, s2, rq, rkv):
    t, w = lat.shape
    tm = _rows(t, w, 4)

    def body(lat_ref, gq_ref, gkv_ref, cs_ref, s1_ref, s2_ref, cq_ref, ckv_ref, kr_ref):
        lq = lat_ref[:, 0:rq]
        lkv = lat_ref[:, rq:rq + rkv]
        cq_ref[...] = ((lq * _rstd(lq)) * gq_ref[...]).astype(BF)
        ckv_ref[...] = ((lkv * _rstd(lkv)) * gkv_ref[...]).astype(BF)
        kr_ref[...] = _rope128(lat_ref[:, rq + rkv:rq + rkv + LANES], cs_ref[...], s1_ref[...], s2_ref[...]).astype(BF)

    def rows(c):
        return pl.BlockSpec((tm, c), lambda i: (i, 0))

    def vecs(c):
        return pl.BlockSpec((1, c), lambda i: (0, 0))

    return pl.pallas_call(
        body, name=name,
        out_shape=[jax.ShapeDtypeStruct((t, rq), BF), jax.ShapeDtypeStruct((t, rkv), BF), jax.ShapeDtypeStruct((t, LANES), BF)],
        grid=(t // tm,),
        in_specs=[rows(w), vecs(rq), vecs(rkv), rows(LANES), rows(LANES), rows(LANES)],
        out_specs=[rows(rq), rows(rkv), rows(LANES)],
        compiler_params=_cparams(1),
    )(lat, gq, gkv, cs, s1, s2)


def _lat_bwd(name, dcq, dckv, dkr, lat, gq, gkv, cs, s1, s2, rq, rkv, heads):
    t, w = lat.shape
    tm = _rows(t, max(w, heads * LANES), 4)
    assert rq == rkv

    def body(dcq_ref, dckv_ref, dkr_ref, lat_ref, gq_ref, gkv_ref, cs_ref, s1_ref, s2_ref, dlat_ref, sums_ref):
        i = pl.program_id(0)

        @pl.when(i == 0)
        def _():
            sums_ref[...] = jnp.zeros_like(sums_ref)

        def rms_bwd(dc, lv, g, row):
            rs = _rstd(lv)
            lhat = lv * rs
            sums_ref[row:row + 1, :] += _colsum(dc * lhat)
            u = dc * g
            return rs * (u - lhat * jnp.mean(u * lhat, axis=-1, keepdims=True))

        dlat_ref[:, 0:rq] = rms_bwd(dcq_ref[...], lat_ref[:, 0:rq], gq_ref[...], 0).astype(BF)
        dlat_ref[:, rq:rq + rkv] = rms_bwd(dckv_ref[...], lat_ref[:, rq:rq + rkv], gkv_ref[...], 1).astype(BF)
        dk = dkr_ref[:, 0:LANES]
        for h in range(1, heads):
            dk = dk + dkr_ref[:, h * LANES:(h + 1) * LANES]
        dlat_ref[:, rq + rkv:rq + rkv + LANES] = _rope128(dk, cs_ref[...], s1_ref[...], s2_ref[...]).astype(BF)

    def rows(c):
        return pl.BlockSpec((tm, c), lambda i: (i, 0))

    def vecs(c):
        return pl.BlockSpec((1, c), lambda i: (0, 0))

    return pl.pallas_call(
        body, name=name,
        out_shape=[jax.ShapeDtypeStruct((t, w), BF), jax.ShapeDtypeStruct((8, rq), F32)],
        grid=(t // tm,),
        in_specs=[rows(rq), rows(rkv), rows(heads * LANES), rows(w), vecs(rq), vecs(rkv), rows(LANES), rows(LANES), rows(LANES)],
        out_specs=[rows(w), pl.BlockSpec((8, rq), lambda i: (0, 0))],
        compiler_params=_cparams(1),
    )(dcq, dckv, dkr, lat, gq, gkv, cs, s1, s2)


def _causal_mask(s, row0):
    row = lax.broadcasted_iota(jnp.int32, s.shape, 0) + row0
    col = lax.broadcasted_iota(jnp.int32, s.shape, 1)
    return col <= row


def _flash_fwd(name, qb, kv, kr, heads, scale):
    t = qb.shape[0]
    tq = _pick(t, TQ)
    nq = t // tq
    sub = tq // Q_CHAINS
    c2 = scale * LOG2E

    def body(q_ref, kv_ref, kr_ref, o_ref, lse_ref):
        i = pl.program_id(1)
        qs = [q_ref[a * sub:(a + 1) * sub, :] for a in range(Q_CHAINS)]

        def block(j, carry, masked):
            rows = pl.ds(pl.multiple_of(j * tq, tq), tq)
            k = jnp.concatenate([kv_ref[rows, 0:LANES], kr_ref[rows, :]], axis=-1)
            v = kv_ref[rows, LANES:2 * LANES]
            out = []
            scores = [lax.dot_general(qs[a], k, NT, preferred_element_type=F32) for a in range(Q_CHAINS)]
            for a in range(Q_CHAINS):
                m_prev, l_prev, acc = carry[a]
                s = scores[a] * c2
                if masked:
                    s = jnp.where(_causal_mask(s, a * sub), s, -1e30)
                m_new = jnp.maximum(m_prev, jnp.max(s, axis=-1, keepdims=True))
                alpha = jnp.exp2(m_prev - m_new)
                p = jnp.exp2(s - m_new)
                l_new = alpha * l_prev + jnp.sum(p, axis=-1, keepdims=True)
                out.append((m_new, l_new, alpha * acc + jnp.dot(p.astype(BF), v, preferred_element_type=F32)))
            return tuple(out)

        init = tuple((jnp.full((sub, 1), -1e30, F32), jnp.zeros((sub, 1), F32), jnp.zeros((sub, V_DIM), F32))
                     for _ in range(Q_CHAINS))
        carry = lax.fori_loop(0, i, lambda j, cr: block(j, cr, False), init)
        fin = block(i, carry, True)
        for a in range(Q_CHAINS):
            m_fin, l_fin, acc = fin[a]
            o_ref[a * sub:(a + 1) * sub, :] = (acc / l_fin).astype(BF)
            lse_ref[a * sub:(a + 1) * sub, :] = jnp.broadcast_to(m_fin + jnp.log2(l_fin), (sub, LANES))

    return pl.pallas_call(
        body, name=name,
        out_shape=[jax.ShapeDtypeStruct((t, heads * V_DIM), BF), jax.ShapeDtypeStruct((t, heads * LANES), F32)],
        grid=(heads, nq),
        in_specs=[pl.BlockSpec((tq, 2 * LANES), lambda h, i: (i, h)),
                  pl.BlockSpec((t, 2 * LANES), lambda h, i: (0, h)),
                  pl.BlockSpec((t, LANES), lambda h, i: (0, 0))],
        out_specs=[pl.BlockSpec((tq, V_DIM), lambda h, i: (i, h)), pl.BlockSpec((tq, LANES), lambda h, i: (i, h))],
        compiler_params=_cparams(2),
    )(qb, kv, kr)


def _flash_bwd(name, qb, kv, kr, o, lse, do, cs, s1, s2, heads, scale):
    t = qb.shape[0]
    tq = _pick(t, TQ)
    nq = t // tq

    c2 = scale * LOG2E

    def body(q_ref, kv_ref, kr_ref, o_ref, lse_ref, do_ref, cs_ref, s1_ref, s2_ref, dkv_ref, dkr_ref, dq_ref, dq_acc):
        j = pl.program_id(1)

        @pl.when(j == 0)
        def _():
            dq_acc[...] = jnp.zeros_like(dq_acc)

        k = jnp.concatenate([kv_ref[:, 0:LANES], kr_ref[...]], axis=-1)
        v = kv_ref[:, LANES:2 * LANES]

        def block(i, carry, masked):
            dk, dv = carry
            rows = pl.ds(pl.multiple_of(i * tq, tq), tq)
            q = q_ref[rows, :]
            dov = do_ref[rows, :]
            delta = jnp.sum(dov.astype(F32) * o_ref[rows, :].astype(F32), axis=-1, keepdims=True)
            s = lax.dot_general(q, k, NT, preferred_element_type=F32) * c2
            p = jnp.exp2(s - lse_ref[rows, 0:1])
            if masked:
                p = jnp.where(_causal_mask(s, 0), p, 0.0)
            dv = dv + lax.dot_general(p.astype(BF), dov, TNDIMS, preferred_element_type=F32)
            dp = lax.dot_general(dov, v, NT, preferred_element_type=F32)
            ds = (p * (dp - delta) * scale).astype(BF)
            dk = dk + lax.dot_general(ds, q, TNDIMS, preferred_element_type=F32)
            dq_acc[rows, :] += jnp.dot(ds, k, preferred_element_type=F32)
            return dk, dv

        carry = block(j, (jnp.zeros((tq, 2 * LANES), F32), jnp.zeros((tq, V_DIM), F32)), True)
        dk, dv = lax.fori_loop(j + 1, nq, lambda i, cr: block(i, cr, False), carry)
        dkv_ref[:, 0:LANES] = dk[:, 0:LANES].astype(BF)
        dkv_ref[:, LANES:2 * LANES] = dv.astype(BF)
        dkr_ref[...] = dk[:, LANES:2 * LANES]

        @pl.when(j == nq - 1)
        def _():
            dq_ref[:, 0:LANES] = dq_acc[:, 0:LANES].astype(BF)
            dq_ref[:, LANES:2 * LANES] = _rope128(dq_acc[:, LANES:2 * LANES], cs_ref[...], s1_ref[...], s2_ref[...]).astype(BF)

    def per_head(width):
        return pl.BlockSpec((t, width), lambda h, j: (0, h))

    table = pl.BlockSpec((t, LANES), lambda h, j: (0, 0))
    return pl.pallas_call(
        body, name=name,
        out_shape=[jax.ShapeDtypeStruct((t, heads * 2 * LANES), BF), jax.ShapeDtypeStruct((t, heads * LANES), F32),
                   jax.ShapeDtypeStruct((t, heads * 2 * LANES), BF)],
        grid=(heads, nq),
        in_specs=[per_head(2 * LANES), pl.BlockSpec((tq, 2 * LANES), lambda h, j: (j, h)),
                  pl.BlockSpec((tq, LANES), lambda h, j: (j, 0)), per_head(LANES), per_head(LANES), per_head(LANES),
                  table, table, table],
        out_specs=[pl.BlockSpec((tq, 2 * LANES), lambda h, j: (j, h)), pl.BlockSpec((tq, LANES), lambda h, j: (j, h)),
                   per_head(2 * LANES)],
        scratch_shapes=[pltpu.VMEM((t, 2 * LANES), F32)],
        compiler_params=_cparams(2),
    )(qb, kv, kr, o, lse, do, cs, s1, s2)


def _shift_down(z, n):
    row = lax.broadcasted_iota(jnp.int32, z.shape, 0)
    return jnp.where(row >= n, pltpu.roll(z, n, 0), 0.0)


def _shift_up(z, n):
    t = z.shape[0]
    row = lax.broadcasted_iota(jnp.int32, z.shape, 0)
    return jnp.where(row < t - n, pltpu.roll(z, t - n, 0), 0.0)


def _conv_fwd(name, proj, cw):
    t, d3 = proj.shape
    d = d3 // 3
    tn = _pick(d, LANES)
    nb = d // tn

    def body(b_ref, c_ref, u_ref, w_ref, o_ref):
        z = c_ref[...] * u_ref[...]
        zc = w_ref[0:1, :] * _shift_down(z, 2) + w_ref[1:2, :] * _shift_down(z, 1) + w_ref[2:3, :] * z
        o_ref[...] = (b_ref[...] * zc).astype(BF)

    def part(p):
        return pl.BlockSpec((t, tn), lambda j: (0, p * nb + j))

    return pl.pallas_call(
        body, name=name, out_shape=jax.ShapeDtypeStruct((t, d), BF), grid=(nb,),
        in_specs=[part(0), part(1), part(2), pl.BlockSpec((8, tn), lambda j: (0, j))],
        out_specs=pl.BlockSpec((t, tn), lambda j: (0, j)),
        compiler_params=_cparams(1),
    )(proj, proj, proj, cw)


def _conv_bwd(name, dbz, proj, cw):
    t, d3 = proj.shape
    d = d3 // 3
    tn = _pick(d, LANES)
    nb = d // tn

    def body(g_ref, b_ref, c_ref, u_ref, w_ref, o_ref, sums_ref):
        p = pl.program_id(1)
        w0, w1, w2 = w_ref[0:1, :], w_ref[1:2, :], w_ref[2:3, :]

        @pl.when(p == 0)
        def _():
            z = c_ref[...] * u_ref[...]
            z1, z2 = _shift_down(z, 1), _shift_down(z, 2)
            gv = g_ref[...]
            o_ref[...] = (gv * (w0 * z2 + w1 * z1 + w2 * z)).astype(BF)
            dzc = gv * b_ref[...]
            sums_ref[...] = jnp.zeros_like(sums_ref)
            sums_ref[0:1, :] = _colsum(dzc * z2)
            sums_ref[1:2, :] = _colsum(dzc * z1)
            sums_ref[2:3, :] = _colsum(dzc * z)

        @pl.when(p > 0)
        def _():
            dzc = g_ref[...] * b_ref[...]
            dz = w2 * dzc + w1 * _shift_up(dzc, 1) + w0 * _shift_up(dzc, 2)
            other = jnp.where(p == 1, u_ref[...], c_ref[...])
            o_ref[...] = (dz * other).astype(BF)

    def part(q):
        return pl.BlockSpec((t, tn), lambda j, p: (0, q * nb + j))

    return pl.pallas_call(
        body, name=name,
        out_shape=[jax.ShapeDtypeStruct((t, d3), BF), jax.ShapeDtypeStruct((8, d), F32)],
        grid=(nb, 3),
        in_specs=[pl.BlockSpec((t, tn), lambda j, p: (0, j)), part(0), part(1), part(2),
                  pl.BlockSpec((8, tn), lambda j, p: (0, j))],
        out_specs=[pl.BlockSpec((t, tn), lambda j, p: (0, p * nb + j)), pl.BlockSpec((8, tn), lambda j, p: (0, j))],
        compiler_params=_cparams(2),
    )(dbz, proj, proj, proj, cw)


def _pad_rows8(v):
    return jnp.pad(v, ((0, 8 - v.shape[0]), (0, 0)))


def kernel(x, c, positions, w_mod, b_mod, norm_g, mla_w_in, mla_g_q, mla_g_kv, mla_w_uq, mla_w_ukv, mla_w_o, conv_w_in, conv_w, conv_w_out, mlp_w_up, mlp_w_down, loss_target, m_w_mod, m_b_mod, m_norm_g, m_mla_w_in, m_mla_g_q, m_mla_g_kv, m_mla_w_uq, m_mla_w_ukv, m_mla_w_o, m_conv_w_in, m_conv_w, m_conv_w_out, m_mlp_w_up, m_mlp_w_down, v_w_mod, v_b_mod, v_norm_g, v_mla_w_in, v_mla_g_q, v_mla_g_kv, v_mla_w_uq, v_mla_w_ukv, v_mla_w_o, v_conv_w_in, v_conv_w, v_conv_w_out, v_mlp_w_up, v_mlp_w_down):
    t, d = x.shape[1], x.shape[2]
    depth = w_mod.shape[0]
    n_mod_loc = w_mod.shape[2]
    d_loc = norm_g.shape[2]
    rq = mla_g_q.shape[1]
    rkv = mla_g_kv.shape[1]
    heads = mla_w_uq.shape[2]
    lat_w = mla_w_in.shape[2]
    lat_pad = rq + rkv + LANES
    scale = (NOPE_DIM + ROPE_DIM) ** -0.5
    xi, yi, ci = _pos()
    dev = 4 * xi + 2 * yi + ci
    c_idx = jnp.reshape(ci, (1,)).astype(jnp.int32)
    chip_idx = jnp.reshape(2 * xi + yi, (1,)).astype(jnp.int32)
    x0 = x[0]
    target = loss_target[0]

    w_in_loc = jnp.pad(mla_w_in[0], ((0, 0), (0, lat_pad - lat_w))).astype(BF)
    uq = mla_w_uq[0]
    uq_loc = jnp.pad(uq, ((0, 0), (0, 0), (0, LANES - ROPE_DIM))).reshape(uq.shape[0], heads * 2 * LANES).astype(BF)
    ukv_loc = mla_w_ukv[0].reshape(mla_w_ukv.shape[1], heads * (NOPE_DIM + V_DIM)).astype(BF)
    def merged(g):
        return g.reshape(g.shape[0] * g.shape[1], g.shape[2])

    g_win, g_uq, g_ukv = [merged(g) for g in _ag_big("ag_mla", [w_in_loc, uq_loc, ukv_loc])]

    n_ng = depth * 4 * d_loc
    small = jnp.concatenate([c.reshape(-1), norm_g.reshape(-1), conv_w.reshape(-1)])
    small_n = small.shape[0]
    small_rows = -(-small_n // (8 * LANES)) * 8
    small = jnp.pad(small, (0, small_rows * LANES - small_n)).reshape(small_rows, LANES)
    small_all = _ag_small("ag_small", small).reshape(N_DEV, small_rows * LANES)
    c_all = small_all[:, :d]
    ng_all = small_all[:, d:d + n_ng].reshape(N_DEV, depth, 4, d_loc).transpose(1, 2, 0, 3).reshape(depth, 4, d)
    cw_all = small_all[:, d + n_ng:d + n_ng + 3 * d_loc].reshape(N_DEV, 3, d_loc).transpose(1, 0, 2).reshape(3, d)
    cw8 = _pad_rows8(cw_all)
    cond_all = _silu("silu", c_all)

    mod_parts = jnp.stack([_mod_matvec(f"mod_matvec{i}", cond_all, w_mod, i) for i in range(depth)])
    mod_rows = depth * N_DEV * n_mod_loc // LANES
    mod_all = _ag_small("ag_mod", mod_parts.reshape(mod_rows, LANES)).reshape(N_DEV, depth, N_DEV, n_mod_loc)
    mod_mine = lax.dynamic_index_in_dim(mod_all, dev, axis=2, keepdims=False)
    mod = mod_mine.transpose(1, 0, 2).reshape(depth, N_DEV * n_mod_loc) + b_mod
    mod = mod.reshape(depth, 6, d)

    dev_idx = jnp.reshape(dev, (1,)).astype(jnp.int32)
    later = [(mla_w_o, 0), (mlp_w_up, 0), (mlp_w_down, 0), (conv_w_in, 0), (conv_w_out, 0), (mlp_w_up, 1), (mlp_w_down, 1)]
    lands = [_cast_own_block(f"cast_own_block{a}", w, layer, dev_idx) for a, (w, layer) in enumerate(later)]
    ag_groups = [[0], [1, 2], [3, 4, 5, 6]]
    ag_sems, lands_thru, ag_token = _ag_start("ag_start", lands, ag_groups, [mod, g_ukv])

    def fwd_vec(nb, gate, na, sc, sh, token=None):
        zero = jnp.zeros((d,), F32)
        rows = [zero if r is None else r for r in (nb, gate, na, sc, sh)]
        vec = _pad_rows8(jnp.stack(rows))
        return vec if token is None else vec + token[0, 0]

    inv_freq = ROPE_THETA ** (-jnp.arange(0, ROPE_DIM, 2, dtype=F32) / ROPE_DIM)
    ang = positions[0].astype(F32)[:, None] * inv_freq
    cos, sin = jnp.cos(ang), jnp.sin(ang)
    zh = jnp.zeros_like(cos)
    zpad = jnp.zeros((t, LANES - ROPE_DIM), F32)
    rope_c = jnp.concatenate([cos, cos, zpad], axis=1)
    rope_s1 = jnp.concatenate([-sin, zh, zpad], axis=1)
    rope_s2 = jnp.concatenate([zh, sin, zpad], axis=1)

    (h0,) = _norm_fwd("pre0", x0, None, fwd_vec(None, None, ng_all[0, 0], mod[0, 1], mod[0, 0], ag_token), None, False, True, False)
    (lat,) = _linear("mla_lat", h0, g_win)
    cq, ckv, kr = _lat_post("mla_lat_post", lat, mla_g_q, mla_g_kv, rope_c, rope_s1, rope_s2, rq, rkv)
    def rope_odd_groups(acc, csv, s1v, s2v):
        groups = [acc[:, g * LANES:(g + 1) * LANES] for g in range(acc.shape[1] // LANES)]
        return (jnp.concatenate([grp if g % 2 == 0 else _rope128(grp, csv, s1v, s2v) for g, grp in enumerate(groups)], axis=1),)

    (qb,) = _linear("mla_q", cq, g_uq, out_dtypes=(BF,), epilogue=rope_odd_groups,
                    row_extras=(rope_c, rope_s1, rope_s2), tn_multiple=2 * LANES)
    fs_o, fr_o, lands_o, _ = _ag_forward("ag_forward_o", lands_thru[0:1], *ag_sems[0], qb)
    (kvb,) = _linear("mla_kv", ckv, g_ukv, out_dtypes=(BF,))
    (g_wo,) = _ag_finish("ag_finish_o", lands_o, fs_o, fr_o, kvb)
    g_wo = merged(g_wo)
    o, lse = _flash_fwd("mla_attn", qb, kvb, kr, heads, scale)
    fs_a, fr_a, lands_a, _ = _ag_forward("ag_forward_a", lands_thru[1:3], *ag_sems[1], o)
    (y0,) = _linear("mla_out", o, g_wo)
    x1, h1 = _norm_fwd("postpre1", x0, y0, fwd_vec(ng_all[0, 1], mod[0, 2], ng_all[0, 2], mod[0, 4], mod[0, 3]), None, True, True, False)
    g_up0, g_dn0 = _ag_finish("ag_finish_a", lands_a, fs_a, fr_a, x1)
    g_dn0 = merged(g_dn0)

    def sq_relu(acc):
        a = jnp.maximum(acc, 0.0)
        return acc, a * a

    u1, act1 = _linear("mlp0_up", h1, g_up0, out_dtypes=(F32, BF), epilogue=sq_relu)
    (y1,) = _linear("mlp0_down", act1, g_dn0)
    fs_b, fr_b, lands_b, _ = _ag_forward("ag_forward_b", lands_thru[3:7], *ag_sems[2], y1)
    x2, h2 = _norm_fwd("postpre2", x1, y1, fwd_vec(ng_all[0, 3], mod[0, 5], ng_all[1, 0], mod[1, 1], mod[1, 0]), None, True, True, False)
    g_cin, g_cout, g_up1, g_dn1 = _ag_finish("ag_finish_b", lands_b, fs_b, fr_b, x2)
    g_cout, g_dn1 = merged(g_cout), merged(g_dn1)
    g_up, g_dn = [g_up0, g_up1], [g_dn0, g_dn1]

    (proj,) = _linear("conv_in", h2, g_cin)
    bz = _conv_fwd("conv_mix", proj, cw8)
    (y2,) = _linear("conv_out", bz, g_cout)
    x3, h3 = _norm_fwd("postpre3", x2, y2, fwd_vec(ng_all[1, 1], mod[1, 2], ng_all[1, 2], mod[1, 4], mod[1, 3]), None, True, True, False)

    u3, act3 = _linear("mlp1_up", h3, g_up[1], out_dtypes=(F32, BF), epilogue=sq_relu)
    (y3,) = _linear("mlp1_down", act3, g_dn[1])
    dx4, loss_cols = _norm_fwd("post_loss", x3, y3, fwd_vec(ng_all[1, 3], mod[1, 5], None, None, None), target, True, False, True)
    loss = lax.psum(0.5 * jnp.sum(loss_cols[0]) / d, ("x", "y", "c"))

    def bwd_vec(nb, gate, na, sc, token=None):
        zero = jnp.zeros((d,), F32)
        rows = [zero if r is None else r for r in (nb, gate, na, sc)]
        vec = _pad_rows8(jnp.stack(rows))
        return vec if token is None else vec + token[0, 0]

    def blocks(g, like):
        return g.reshape((N_DEV,) + like.shape)

    def rs_begin(tag, grads):
        return _push_start(f"rs_sibling_start_{tag}", grads, _sibling_routes)

    def rs_continue(tag, started, after):
        send_sems, recv_sems, grads, from_sibling, _ = started
        grads, from_sibling = _push_wait(f"rs_sibling_wait_{tag}", grads, from_sibling, send_sems, recv_sems, _sibling_routes, after)
        pair = [_pair_add(f"pair_add_{tag}{a}", grads[a], from_sibling[a], c_idx) for a in range(len(grads))]
        return _push_start(f"rs_start_{tag}", pair, _chip_routes)

    def mlp_bwd(tag, dy, h, u, act, w_up, w_dn, w_dn_shard):
        def relu_grad(acc, uv):
            return (acc * (2.0 * jnp.maximum(uv, 0.0)),)

        (du,) = _linear_t(f"{tag}_dact", dy, w_dn, out_dtypes=(BF,), epilogue=relu_grad, extras=(u,))
        dw_dn = _wgrad(f"{tag}_dw_down", act, dy, w_dn)
        dw_up = _wgrad(f"{tag}_dw_up", h, du, w_up)
        sib = rs_begin(tag, [dw_up, blocks(dw_dn, w_dn_shard)])
        (dh,) = _linear_t(f"{tag}_dh", du, w_up, token=sib[4])
        return dh, rs_continue(tag, sib, dh)

    dy3, sums4 = _norm_bwd("bwd_post3", dx4, None, None, y3, bwd_vec(ng_all[1, 3], mod[1, 5], None, None), False, True)
    dh3, rs_mlp1 = mlp_bwd("mlp1", dy3, h3, u3, act3, g_up[1], g_dn[1], mlp_w_down[1])
    dx3, dy2, sums3 = _norm_bwd("bwd_norm3", dx4, dh3, x3, y2, bwd_vec(ng_all[1, 1], mod[1, 2], ng_all[1, 2], mod[1, 4], rs_mlp1[4]), True, True)

    (dbz,) = _linear_t("conv_dbz", dy2, g_cout)
    dw_cout = _wgrad("conv_dw_out", bz, dy2, g_cout)
    dproj, conv_sums = _conv_bwd("conv_mix_bwd", dbz, proj, cw8)
    dw_cin = _wgrad("conv_dw_in", h2, dproj, g_cin)
    sib_conv = rs_begin("conv", [dw_cin, blocks(dw_cout, conv_w_out[0])])
    (dh2,) = _linear_t("conv_dh", dproj, g_cin, token=sib_conv[4])
    rs_conv = rs_continue("conv", sib_conv, dh2)
    dx2, dy1, sums2 = _norm_bwd("bwd_norm2", dx3, dh2, x2, y1, bwd_vec(ng_all[0, 3], mod[0, 5], ng_all[1, 0], mod[1, 1], rs_conv[4]), True, True)

    dh1, rs_mlp0 = mlp_bwd("mlp0", dy1, h1, u1, act1, g_up[0], g_dn[0], mlp_w_down[0])
    dx1, dy0, sums1 = _norm_bwd("bwd_norm1", dx2, dh1, x1, y0, bwd_vec(ng_all[0, 1], mod[0, 2], ng_all[0, 2], mod[0, 4], rs_mlp0[4]), True, True)

    (do,) = _linear_t("mla_do", dy0, g_wo, out_dtypes=(BF,))
    dw_o = _wgrad("mla_dw_o", o, dy0, g_wo)
    dkv, dkr, dq = _flash_bwd("mla_attn_bwd", qb, kvb, kr, o, lse, do, rope_c, -rope_s1, -rope_s2, heads, scale)
    dw_uq = _wgrad("mla_dw_uq", cq, dq, g_uq)
    (dcq,) = _linear_t("mla_dcq", dq, g_uq)
    dw_ukv = _wgrad("mla_dw_ukv", ckv, dkv, g_ukv)
    (dckv,) = _linear_t("mla_dckv", dkv, g_ukv)
    dlat, lat_sums = _lat_bwd("mla_lat_bwd", dcq, dckv, dkr, lat, mla_g_q, mla_g_kv, rope_c, -rope_s1, -rope_s2, rq, rkv, heads)
    dw_win = _wgrad("mla_dw_in", h0, dlat, g_win)
    sib_mla = rs_begin("mla", [blocks(dw_win, w_in_loc), blocks(dw_uq, uq_loc), blocks(dw_ukv, ukv_loc), blocks(dw_o, mla_w_o[0])])
    (dh0,) = _linear_t("mla_dh", dlat, g_win, token=sib_mla[4])
    rs_mla = rs_continue("mla", sib_mla, dh0)
    grad_x, sums0 = _norm_bwd("bwd_pre0", dx1, dh0, x0, None, bwd_vec(None, None, ng_all[0, 0], mod[0, 1], rs_mla[4]), True, False)

    def sums_rows(s, rows):
        return [s[r] for r in rows]

    dmod = jnp.stack([
        jnp.stack(sums_rows(sums0, (0, 1)) + sums_rows(sums1, (3,)) + sums_rows(sums1, (0, 1)) + sums_rows(sums2, (3,))),
        jnp.stack(sums_rows(sums2, (0, 1)) + sums_rows(sums3, (3,)) + sums_rows(sums3, (0, 1)) + sums_rows(sums4, (3,))),
    ])
    dng = jnp.stack([
        jnp.stack([sums0[2], sums1[4], sums1[2], sums2[4]]),
        jnp.stack([sums2[2], sums3[4], sums3[2], sums4[4]]),
    ])
    parts = [dmod.reshape(-1), dng.reshape(-1), lat_sums[0], lat_sums[1], conv_sums[0:3].reshape(-1)]
    sizes = [p.shape[0] for p in parts]
    packed = jnp.concatenate(parts)
    pk_rows = -(-packed.shape[0] // (8 * LANES)) * 8
    packed = jnp.pad(packed, (0, pk_rows * LANES - packed.shape[0])).reshape(pk_rows, LANES)
    packed_all = _ag_small("ag_small_grads", packed).reshape(N_DEV, pk_rows, LANES)
    total = _sum_devices("sum_small_grads", packed_all).reshape(-1)
    offs = [0]
    for s in sizes:
        offs.append(offs[-1] + s)
    g_b_mod = total[offs[0]:offs[1]].reshape(depth, 6 * d)
    ng_full = total[offs[1]:offs[2]].reshape(depth, 4, d)
    g_norm_g = lax.dynamic_slice_in_dim(ng_full, dev * d_loc, d_loc, axis=2)
    g_g_q = total[offs[2]:offs[3]].reshape(1, rq)
    g_g_kv = total[offs[3]:offs[4]].reshape(1, rkv)
    cw_full = total[offs[4]:offs[5]].reshape(1, 3, d)
    g_conv_w = lax.dynamic_slice_in_dim(cw_full, dev * d_loc, d_loc, axis=2)

    dmod_all = packed_all.reshape(N_DEV, -1)[:, :sizes[0]].reshape(N_DEV, depth, 6 * d)
    dmod_cols = lax.dynamic_slice_in_dim(dmod_all, dev * n_mod_loc, n_mod_loc, axis=2)
    cond_t = cond_all.T
    mod_out = None
    for i in range(depth):
        mod_out = _mod_grad_adam(f"w_mod_adam{i}", cond_t, dmod_cols[:, i, :], w_mod, m_w_mod, v_w_mod, i, mod_out)
    g_w_mod, d_w_mod, nm_w_mod, nv_w_mod = mod_out

    def rs_end(tag, started):
        send_sems, recv_sems, pair, lands, _ = started
        return _push_wait(f"rs_wait_{tag}", pair, lands, send_sems, recv_sems, _chip_routes, g_w_mod)

    pair_mlp1, lands_mlp1 = rs_end("mlp1", rs_mlp1)
    pair_conv, lands_conv = rs_end("conv", rs_conv)
    pair_mlp0, lands_mlp0 = rs_end("mlp0", rs_mlp0)
    pair_mla, lands_mla = rs_end("mla", rs_mla)

    results = {"w_mod": (g_w_mod, d_w_mod, nm_w_mod, nv_w_mod)}
    given = {"b_mod": (b_mod, m_b_mod, v_b_mod), "norm_g": (norm_g, m_norm_g, v_norm_g),
             "mla_w_in": (mla_w_in, m_mla_w_in, v_mla_w_in), "mla_g_q": (mla_g_q, m_mla_g_q, v_mla_g_q),
             "mla_g_kv": (mla_g_kv, m_mla_g_kv, v_mla_g_kv), "mla_w_uq": (mla_w_uq, m_mla_w_uq, v_mla_w_uq),
             "mla_w_ukv": (mla_w_ukv, m_mla_w_ukv, v_mla_w_ukv), "mla_w_o": (mla_w_o, m_mla_w_o, v_mla_w_o),
             "conv_w_in": (conv_w_in, m_conv_w_in, v_conv_w_in), "conv_w": (conv_w, m_conv_w, v_conv_w),
             "conv_w_out": (conv_w_out, m_conv_w_out, v_conv_w_out), "mlp_w_up": (mlp_w_up, m_mlp_w_up, v_mlp_w_up),
             "mlp_w_down": (mlp_w_down, m_mlp_w_down, v_mlp_w_down)}

    def fused(name, parts, prev=None):
        for layer, (pair, lands) in enumerate(parts):
            prev = _adam_sum(f"adam_{name}{layer}", pair, lands, chip_idx, *given[name], layer, prev)
        results[name] = tuple(prev)

    fused("mlp_w_up", [(pair_mlp0[0], lands_mlp0[0]), (pair_mlp1[0], lands_mlp1[0])])
    fused("mlp_w_down", [(pair_mlp0[1], lands_mlp0[1]), (pair_mlp1[1], lands_mlp1[1])])
    fused("conv_w_in", [(pair_conv[0], lands_conv[0])])
    fused("conv_w_out", [(pair_conv[1], lands_conv[1])])
    fused("mla_w_o", [(pair_mla[3], lands_mla[3])])

    red_win, red_uq, red_ukv = [_chip_sum(f"chip_sum_mla{a}", pair_mla[a], lands_mla[a], chip_idx) for a in range(3)]
    plain = {"b_mod": g_b_mod, "norm_g": g_norm_g, "mla_g_q": g_g_q, "mla_g_kv": g_g_kv, "conv_w": g_conv_w,
             "mla_w_in": red_win[:, :lat_w][None],
             "mla_w_uq": red_uq.reshape(-1, heads, 2 * LANES)[:, :, :NOPE_DIM + ROPE_DIM][None],
             "mla_w_ukv": red_ukv.reshape(mla_w_ukv.shape)}
    for name, g in plain.items():
        results[name] = (g, *_adam(f"adam_{name}", given[name][0], g, given[name][1], given[name][2]))

    order = ["w_mod", "b_mod", "norm_g", "mla_w_in", "mla_g_q", "mla_g_kv", "mla_w_uq", "mla_w_ukv", "mla_w_o",
             "conv_w_in", "conv_w", "conv_w_out", "mlp_w_up", "mlp_w_down"]
    return (loss, grad_x[None], *[results[n][0] for n in order], *[results[n][1] for n in order],
            *[results[n][2] for n in order], *[results[n][3] for n in order])
```

```python
import jax
import jax.numpy as jnp
from jax import lax
from jax.experimental import pallas as pl
from jax.experimental.pallas import tpu as pltpu

F32 = jnp.float32
BF = jnp.bfloat16
MESH = pl.DeviceIdType.MESH
N_DEV = 8
LANES = 128
NORM_EPS = 1e-6
ROPE_THETA = 10000.0
ROPE_DIM = 64
NOPE_DIM = 128
V_DIM = 128
LR, B1, B2, ADAM_EPS, WD, STEP = 0.001, 0.9, 0.999, 1e-08, 0.01, 10
VMEM_LIMIT = 56 * 1024 * 1024
TILE_BYTES = 2 * 1024 * 1024
MM_TILE_PREFS = ((1024, 1024), (1024, 512), (512, 1024), (512, 512), (256, 512), (256, 256), (128, 256), (128, 128))
MM_VMEM_BUDGET = 36 * 1024 * 1024
TQ = 512
LOG2E = 1.4426950408889634

NN = (((1,), (0,)), ((), ()))
NT = (((1,), (1,)), ((), ()))
TNDIMS = (((0,), (0,)), ((), ()))
HBM_SPEC = pl.BlockSpec(memory_space=pltpu.HBM)
VMEM_SPEC = pl.BlockSpec(memory_space=pltpu.VMEM)


def _cparams(n_axes):
    return pltpu.CompilerParams(dimension_semantics=("arbitrary",) * n_axes, vmem_limit_bytes=VMEM_LIMIT)


def _pick(dim, pref):
    if dim <= pref:
        return dim
    for t in range(pref - pref % LANES, 0, -LANES):
        if t > 0 and dim % t == 0:
            return t
    for t in range(pref, 0, -1):
        if dim % t == 0:
            return t
    return dim


def _rows(r, c, itemsize=4):
    want = max(8, TILE_BYTES // (itemsize * max(c, 1)))
    if r <= want:
        return r
    for t in range(want - want % 8, 0, -8):
        if t > 0 and r % t == 0:
            return t
    return r


def _pos():
    return lax.axis_index("x"), lax.axis_index("y"), lax.axis_index("c")


def _ag_small(name, v):
    m_per, n = v.shape

    def body(x_ref, out_ref, send_sems, recv_sems, local_sem):
        x, y, c = _pos()
        me, sibling = (x, y, c), (x, y, 1 - c)
        chips = [(1 - x, y), (x, 1 - y), (1 - x, 1 - y)]

        def rows(px, py, pc):
            return out_ref.at[pl.ds((4 * px + 2 * py + pc) * m_per, m_per), :]

        def copy(k, block, to, src=None):
            return pltpu.make_async_remote_copy(
                src_ref=rows(*block) if src is None else src, dst_ref=rows(*block),
                send_sem=send_sems.at[k], recv_sem=recv_sems.at[k], device_id=to, device_id_type=MESH)

        mine = pltpu.make_async_copy(x_ref, rows(*me), local_sem)
        mine.start()
        first = [copy(0, me, sibling, src=x_ref)]
        first += [copy(1 + j, me, (*chip, c), src=x_ref) for j, chip in enumerate(chips)]
        for cp in first:
            cp.start()
        passed = [copy(4 + j, (*chip, c), sibling) for j, chip in enumerate(chips)]
        for j, chip in enumerate(chips):
            copy(1 + j, (*chip, c), me).wait_recv()
            passed[j].start()
        copy(0, sibling, me).wait_recv()
        for j, chip in enumerate(chips):
            copy(4 + j, (*chip, 1 - c), me).wait_recv()
        for cp in first + passed:
            cp.wait_send()
        mine.wait()

    return pl.pallas_call(
        body, name=name,
        out_shape=jax.ShapeDtypeStruct((N_DEV * m_per, n), v.dtype),
        in_specs=[VMEM_SPEC], out_specs=VMEM_SPEC,
        scratch_shapes=[pltpu.SemaphoreType.DMA((7,)), pltpu.SemaphoreType.DMA((7,)), pltpu.SemaphoreType.DMA],
    )(v)


def _ag_big(name, arrs):
    n = len(arrs)

    def body(*refs):
        ins, outs = refs[:n], refs[n:2 * n]
        send_sems, recv_sems, local_sems = refs[2 * n:]
        x, y, c = _pos()
        me, sibling = (x, y, c), (x, y, 1 - c)
        chips = [(1 - x, y), (x, 1 - y), (1 - x, 1 - y)]

        def copy(a, k, block, to, src=None):
            dst = outs[a].at[4 * block[0] + 2 * block[1] + block[2]]
            return pltpu.make_async_remote_copy(
                src_ref=dst if src is None else src, dst_ref=dst,
                send_sem=send_sems.at[7 * a + k], recv_sem=recv_sems.at[7 * a + k],
                device_id=to, device_id_type=MESH)

        mine = [pltpu.make_async_copy(ins[a], outs[a].at[4 * x + 2 * y + c], local_sems.at[a]) for a in range(n)]
        for cp in mine:
            cp.start()
        first = []
        for a in range(n):
            first.append(copy(a, 0, me, sibling, src=ins[a]))
            first += [copy(a, 1 + j, me, (*chip, c), src=ins[a]) for j, chip in enumerate(chips)]
        for cp in first:
            cp.start()
        passed = []
        for j, chip in enumerate(chips):
            for a in range(n):
                copy(a, 1 + j, (*chip, c), me).wait_recv()
                cp = copy(a, 4 + j, (*chip, c), sibling)
                cp.start()
                passed.append(cp)
        for a in range(n):
            copy(a, 0, sibling, me).wait_recv()
            for j, chip in enumerate(chips):
                copy(a, 4 + j, (*chip, 1 - c), me).wait_recv()
        for cp in first + passed:
            cp.wait_send()
        for cp in mine:
            cp.wait()

    return pl.pallas_call(
        body, name=name,
        out_shape=[jax.ShapeDtypeStruct((N_DEV,) + a.shape, a.dtype) for a in arrs],
        in_specs=[HBM_SPEC] * n, out_specs=[HBM_SPEC] * n,
        scratch_shapes=[pltpu.SemaphoreType.DMA((7 * n,)), pltpu.SemaphoreType.DMA((7 * n,)),
                        pltpu.SemaphoreType.DMA((n,))],
    )(*arrs)


SEM_SPEC = pl.BlockSpec(memory_space=pltpu.SEMAPHORE)
ANY_SPEC = pl.BlockSpec(memory_space=pl.ANY)
EFFECT = pltpu.SideEffectType.DATAFLOW_SIDE_EFFECTING


def _hbm(a):
    return pltpu.with_memory_space_constraint(a, pltpu.HBM)


def _cast_own_block(name, w, layer, dev_idx):
    _, r, cdim = w.shape
    tr = _rows(r, cdim, 4)

    def body(d_ref, w_ref, o_ref):
        o_ref[...] = w_ref[...].astype(BF)

    return pl.pallas_call(
        body, name=name,
        out_shape=jax.ShapeDtypeStruct((N_DEV, r, cdim), BF),
        grid_spec=pltpu.PrefetchScalarGridSpec(
            num_scalar_prefetch=1, grid=(r // tr,),
            in_specs=[pl.BlockSpec((None, tr, cdim), lambda i, d_ref: (layer, i, 0))],
            out_specs=pl.BlockSpec((None, tr, cdim), lambda i, d_ref: (d_ref[0], i, 0))),
        compiler_params=_cparams(1),
    )(dev_idx, w)


def _ag_start(name, lands, groups, after):
    n = len(lands)
    ng = len(groups)
    n_after = len(after)

    def body(*refs):
        lnd = refs[:n]
        sems = refs[n + n_after:n + n_after + 2 * ng]
        token = refs[-1]
        x, y, c = _pos()
        targets = [(x, y, 1 - c), (1 - x, y, c), (x, 1 - y, c), (1 - x, 1 - y, c)]
        for gi, members in enumerate(groups):
            for pos, a in enumerate(members):
                own = lnd[a].at[4 * x + 2 * y + c]
                for k, to in enumerate(targets):
                    pltpu.make_async_remote_copy(
                        src_ref=own, dst_ref=own,
                        send_sem=sems[2 * gi].at[4 * pos + k], recv_sem=sems[2 * gi + 1].at[4 * pos + k],
                        device_id=to, device_id_type=MESH).start()
        token[...] = jnp.zeros_like(token)

    sem_shapes = []
    for members in groups:
        sem_shapes += [pltpu.SemaphoreType.DMA((4 * len(members),))] * 2
    res = pl.pallas_call(
        body, name=name,
        out_shape=(*sem_shapes, *[pltpu.HBM(a.shape, a.dtype) for a in lands], jax.ShapeDtypeStruct((8, LANES), F32)),
        in_specs=[HBM_SPEC] * n + [ANY_SPEC] * n_after,
        out_specs=(*[SEM_SPEC] * (2 * ng), *[HBM_SPEC] * n, VMEM_SPEC),
        input_output_aliases={i: 2 * ng + i for i in range(n)},
        compiler_params=pltpu.CompilerParams(has_side_effects=EFFECT),
    )(*[_hbm(a) for a in lands], *after)
    sem_pairs = [(res[2 * g], res[2 * g + 1]) for g in range(ng)]
    return sem_pairs, list(res[2 * ng:2 * ng + n]), res[-1]


def _ag_forward(name, lands, send_sems, recv_sems, after):
    n = len(lands)

    def body(*refs):
        lnd = refs[:n]
        s_sems, r_sems = refs[n], refs[n + 1]
        fs_sems, fr_sems = refs[n + 3], refs[n + 4]
        token = refs[-1]
        x, y, c = _pos()
        sources = [(x, y, 1 - c), (1 - x, y, c), (x, 1 - y, c), (1 - x, 1 - y, c)]
        for a in range(n):
            own = lnd[a].at[4 * x + 2 * y + c]
            for k, src in enumerate(sources):
                blk = lnd[a].at[4 * src[0] + 2 * src[1] + src[2]]
                arrived = pltpu.make_async_remote_copy(
                    src_ref=own, dst_ref=blk, send_sem=s_sems.at[4 * a + k], recv_sem=r_sems.at[4 * a + k],
                    device_id=src, device_id_type=MESH)
                arrived.wait_send()
                arrived.wait_recv()
                if k > 0:
                    pltpu.make_async_remote_copy(
                        src_ref=blk, dst_ref=blk, send_sem=fs_sems.at[3 * a + k - 1], recv_sem=fr_sems.at[3 * a + k - 1],
                        device_id=(x, y, 1 - c), device_id_type=MESH).start()
        token[...] = jnp.zeros_like(token)

    res = pl.pallas_call(
        body, name=name,
        out_shape=(pltpu.SemaphoreType.DMA((3 * n,)), pltpu.SemaphoreType.DMA((3 * n,)),
                   *[pltpu.HBM(a.shape, a.dtype) for a in lands], jax.ShapeDtypeStruct((8, LANES), F32)),
        in_specs=[HBM_SPEC] * n + [SEM_SPEC, SEM_SPEC, ANY_SPEC],
        out_specs=(SEM_SPEC, SEM_SPEC, *[HBM_SPEC] * n, VMEM_SPEC),
        input_output_aliases={i: 2 + i for i in range(n)},
        compiler_params=pltpu.CompilerParams(has_side_effects=EFFECT),
    )(*lands, send_sems, recv_sems, after)
    return res[0], res[1], list(res[2:2 + n]), res[-1]


def _ag_finish(name, lands, fs_sems, fr_sems, after):
    n = len(lands)

    def body(*refs):
        lnd = refs[:n]
        fs, fr = refs[n], refs[n + 1]
        x, y, c = _pos()
        chips = [(1 - x, y), (x, 1 - y), (1 - x, 1 - y)]
        for a in range(n):
            for j, chip in enumerate(chips):
                mine = lnd[a].at[4 * chip[0] + 2 * chip[1] + c]
                theirs = lnd[a].at[4 * chip[0] + 2 * chip[1] + 1 - c]
                cp = pltpu.make_async_remote_copy(
                    src_ref=mine, dst_ref=theirs, send_sem=fs.at[3 * a + j], recv_sem=fr.at[3 * a + j],
                    device_id=(x, y, 1 - c), device_id_type=MESH)
                cp.wait_send()
                cp.wait_recv()

    res = pl.pallas_call(
        body, name=name,
        out_shape=tuple(pltpu.HBM(a.shape, a.dtype) for a in lands),
        in_specs=[HBM_SPEC] * n + [SEM_SPEC, SEM_SPEC, ANY_SPEC],
        out_specs=tuple([HBM_SPEC] * n),
        input_output_aliases={i: i for i in range(n)},
        compiler_params=pltpu.CompilerParams(has_side_effects=EFFECT),
    )(*lands, fs_sems, fr_sems, after)
    return list(res)


def _sibling_routes(x, y, c):
    return [(2 * j + (1 - c), (x, y, 1 - c)) for j in range(4)]


def _chip_routes(x, y, c):
    return [(2 * cx + cy, (cx, cy, c)) for cx, cy in [(1 - x, y), (x, 1 - y), (1 - x, 1 - y)]]


def _push_start(name, arrs, routes):
    n = len(arrs)
    nr = len(routes(0, 0, 0))

    def body(*refs):
        ins, lnd = refs[:n], refs[n:2 * n]
        s_sems, r_sems = refs[2 * n], refs[2 * n + 1]
        token = refs[-1]
        for a in range(n):
            for k, (blk, to) in enumerate(routes(*_pos())):
                pltpu.make_async_remote_copy(
                    src_ref=ins[a].at[blk], dst_ref=lnd[a].at[k],
                    send_sem=s_sems.at[nr * a + k], recv_sem=r_sems.at[nr * a + k],
                    device_id=to, device_id_type=MESH).start()
        token[...] = jnp.zeros_like(token)

    lands = [lax.empty((nr,) + a.shape[1:], a.dtype) for a in arrs]
    res = pl.pallas_call(
        body, name=name,
        out_shape=(pltpu.SemaphoreType.DMA((nr * n,)), pltpu.SemaphoreType.DMA((nr * n,)),
                   *[pltpu.HBM(a.shape, a.dtype) for a in arrs], *[pltpu.HBM(a.shape, a.dtype) for a in lands],
                   jax.ShapeDtypeStruct((8, LANES), F32)),
        in_specs=[HBM_SPEC] * (2 * n),
        out_specs=(SEM_SPEC, SEM_SPEC, *[HBM_SPEC] * (2 * n), VMEM_SPEC),
        input_output_aliases={i: 2 + i for i in range(2 * n)},
        compiler_params=pltpu.CompilerParams(has_side_effects=EFFECT),
    )(*[_hbm(a) for a in arrs], *[_hbm(a) for a in lands])
    return res[0], res[1], list(res[2:2 + n]), list(res[2 + n:2 + 2 * n]), res[-1]


def _push_wait(name, arrs, lands, send_sems, recv_sems, routes, after):
    n = len(arrs)
    nr = len(routes(0, 0, 0))

    def body(*refs):
        ins, lnd = refs[:n], refs[n:2 * n]
        s_sems, r_sems = refs[2 * n], refs[2 * n + 1]
        for a in range(n):
            for k, (blk, to) in enumerate(routes(*_pos())):
                cp = pltpu.make_async_remote_copy(
                    src_ref=ins[a].at[blk], dst_ref=lnd[a].at[k],
                    send_sem=s_sems.at[nr * a + k], recv_sem=r_sems.at[nr * a + k],
                    device_id=to, device_id_type=MESH)
                cp.wait_send()
                cp.wait_recv()

    res = pl.pallas_call(
        body, name=name,
        out_shape=tuple(pltpu.HBM(a.shape, a.dtype) for a in list(arrs) + list(lands)),
        in_specs=[HBM_SPEC] * (2 * n) + [SEM_SPEC, SEM_SPEC, ANY_SPEC],
        out_specs=tuple([HBM_SPEC] * (2 * n)),
        input_output_aliases={i: i for i in range(2 * n)},
        compiler_params=pltpu.CompilerParams(has_side_effects=EFFECT),
    )(*arrs, *lands, send_sems, recv_sems, after)
    return list(res[:n]), list(res[n:])


def _pair_add(name, g, r1, c_idx):
    _, r, cdim = g.shape
    tr = _rows(r, cdim, 2)

    def body(c_ref, g_ref, r_ref, o_ref):
        o_ref[...] = (g_ref[...].astype(F32) + r_ref[...].astype(F32)).astype(o_ref.dtype)

    return pl.pallas_call(
        body, name=name,
        out_shape=jax.ShapeDtypeStruct((4, r, cdim), BF),
        grid_spec=pltpu.PrefetchScalarGridSpec(
            num_scalar_prefetch=1, grid=(4, r // tr),
            in_specs=[pl.BlockSpec((None, tr, cdim), lambda j, i, c_ref: (2 * j + c_ref[0], i, 0)),
                      pl.BlockSpec((None, tr, cdim), lambda j, i, c_ref: (j, i, 0))],
            out_specs=pl.BlockSpec((None, tr, cdim), lambda j, i, c_ref: (j, i, 0))),
        compiler_params=_cparams(2),
    )(c_idx, g, r1)


def _chip_sum(name, p, r2, chip_idx):
    _, r, cdim = p.shape
    tr = _rows(r, cdim, 4)

    def body(s_ref, p_ref, a_ref, b_ref, c_ref, o_ref):
        o_ref[...] = ((p_ref[...].astype(F32) + a_ref[...].astype(F32)) + b_ref[...].astype(F32)) + c_ref[...].astype(F32)

    def other(k):
        return pl.BlockSpec((None, tr, cdim), lambda i, s_ref: (k, i, 0))

    return pl.pallas_call(
        body, name=name,
        out_shape=jax.ShapeDtypeStruct((r, cdim), F32),
        grid_spec=pltpu.PrefetchScalarGridSpec(
            num_scalar_prefetch=1, grid=(r // tr,),
            in_specs=[pl.BlockSpec((None, tr, cdim), lambda i, s_ref: (s_ref[0], i, 0)), other(0), other(1), other(2)],
            out_specs=pl.BlockSpec((tr, cdim), lambda i, s_ref: (i, 0))),
        compiler_params=_cparams(1),
    )(chip_idx, p, r2, r2, r2)


def _adam_math(w, g, m, v):
    m = B1 * m + (1.0 - B1) * g
    v = B2 * v + (1.0 - B2) * (g * g)
    m_hat = m / (1.0 - B1 ** STEP)
    v_hat = v / (1.0 - B2 ** STEP)
    delta = -LR * (m_hat / (jnp.sqrt(v_hat) + ADAM_EPS) + WD * w)
    return delta, m, v


def _adam(name, w, g, m, v):
    shape = w.shape
    cdim = shape[-1]
    r = w.size // cdim
    flat = [a.reshape(r, cdim) for a in (w, g, m, v)]
    tr = _rows(r, cdim, 4)

    def body(w_ref, g_ref, m_ref, v_ref, d_ref, mo_ref, vo_ref):
        d, mn, vn = _adam_math(w_ref[...], g_ref[...], m_ref[...], v_ref[...])
        d_ref[...] = d
        mo_ref[...] = mn
        vo_ref[...] = vn

    spec = pl.BlockSpec((tr, cdim), lambda i: (i, 0))
    outs = pl.pallas_call(
        body, name=name,
        out_shape=[jax.ShapeDtypeStruct((r, cdim), F32)] * 3,
        grid=(r // tr,), in_specs=[spec] * 4, out_specs=[spec] * 3,
        compiler_params=_cparams(1),
    )(*flat)
    return [o.reshape(shape) for o in outs]


def _adam_sum(name, p, r2, chip_idx, w, m, v, layer, prev=None):
    nl, r, cdim = w.shape
    tr = _rows(r, cdim, 4)

    def body(s_ref, p_ref, a_ref, b_ref, c_ref, w_ref, m_ref, v_ref, *rest):
        g_ref, d_ref, mo_ref, vo_ref = rest[-4:]
        g = ((p_ref[...].astype(F32) + a_ref[...].astype(F32)) + b_ref[...].astype(F32)) + c_ref[...].astype(F32)
        dl, mn, vn = _adam_math(w_ref[...], g, m_ref[...], v_ref[...])
        g_ref[...] = g
        d_ref[...] = dl
        mo_ref[...] = mn
        vo_ref[...] = vn

    def slot(k):
        return pl.BlockSpec((None, tr, cdim), lambda i, s_ref: (k, i, 0))

    in_specs = [pl.BlockSpec((None, tr, cdim), lambda i, s_ref: (s_ref[0], i, 0)), slot(0), slot(1), slot(2)] + [slot(layer)] * 3
    operands = [chip_idx, p, r2, r2, r2, w, m, v]
    aliases = {}
    if prev is not None:
        in_specs += [ANY_SPEC] * 4
        aliases = {len(operands) + k: k for k in range(4)}
        operands += list(prev)
    return pl.pallas_call(
        body, name=name,
        out_shape=[jax.ShapeDtypeStruct((nl, r, cdim), F32)] * 4,
        grid_spec=pltpu.PrefetchScalarGridSpec(
            num_scalar_prefetch=1, grid=(r // tr,), in_specs=in_specs, out_specs=[slot(layer)] * 4),
        input_output_aliases=aliases,
        compiler_params=_cparams(1),
    )(*operands)


def _mod_matvec(name, cond_all, w_loc, layer):
    _, d, n_loc = w_loc.shape
    tn = _pick(n_loc, 512)

    def body(c_ref, w_ref, o_ref):
        o_ref[...] = jnp.dot(c_ref[...].astype(BF), w_ref[...].astype(BF), preferred_element_type=F32)

    return pl.pallas_call(
        body, name=name,
        out_shape=jax.ShapeDtypeStruct((N_DEV, n_loc), F32),
        grid=(n_loc // tn,),
        in_specs=[pl.BlockSpec((N_DEV, d), lambda j: (0, 0)), pl.BlockSpec((None, d, tn), lambda j: (layer, 0, j))],
        out_specs=pl.BlockSpec((N_DEV, tn), lambda j: (0, j)),
        compiler_params=_cparams(1),
    )(cond_all, w_loc)


def _mod_grad_adam(name, cond_t, dmod, w, m, v, layer, prev=None):
    nl, d, n_loc = w.shape
    tr = _rows(d, n_loc, 4)

    def body(ct_ref, dm_ref, w_ref, m_ref, v_ref, *rest):
        g_ref, d_ref, mo_ref, vo_ref = rest[-4:]
        g = ct_ref[:, 0:1] * dm_ref[0:1, :]
        for b in range(1, N_DEV):
            g = g + ct_ref[:, b:b + 1] * dm_ref[b:b + 1, :]
        dl, mn, vn = _adam_math(w_ref[...], g, m_ref[...], v_ref[...])
        g_ref[...] = g
        d_ref[...] = dl
        mo_ref[...] = mn
        vo_ref[...] = vn

    spec = pl.BlockSpec((None, tr, n_loc), lambda i: (layer, i, 0))
    in_specs = [pl.BlockSpec((tr, N_DEV), lambda i: (i, 0)), pl.BlockSpec((N_DEV, n_loc), lambda i: (0, 0)), spec, spec, spec]
    operands = [cond_t, dmod, w, m, v]
    aliases = {}
    if prev is not None:
        in_specs += [ANY_SPEC] * 4
        aliases = {len(operands) + k: k for k in range(4)}
        operands += list(prev)
    return pl.pallas_call(
        body, name=name,
        out_shape=[jax.ShapeDtypeStruct((nl, d, n_loc), F32)] * 4,
        grid=(d // tr,), in_specs=in_specs, out_specs=[spec] * 4,
        input_output_aliases=aliases,
        compiler_params=_cparams(1),
    )(*operands)


def _sum_devices(name, gathered):
    _, m, n = gathered.shape

    def body(g_ref, o_ref):
        acc = g_ref[0]
        for d in range(1, N_DEV):
            acc = acc + g_ref[d]
        o_ref[...] = acc

    return pl.pallas_call(
        body, name=name, out_shape=jax.ShapeDtypeStruct((m, n), F32),
        in_specs=[VMEM_SPEC], out_specs=VMEM_SPEC,
    )(gathered)


def _silu(name, c_pad):
    def body(c_ref, o_ref):
        c = c_ref[...]
        o_ref[...] = c * (1.0 / (1.0 + jnp.exp(-c)))

    return pl.pallas_call(body, name=name, out_shape=jax.ShapeDtypeStruct(c_pad.shape, F32),
                          in_specs=[VMEM_SPEC], out_specs=VMEM_SPEC)(c_pad)


def _mm_tiles(m, n_cap, k, bytes_per_out_elem):
    for pm, pn in MM_TILE_PREFS:
        tm, tn = _pick(m, pm), _pick(n_cap, pn)
        need = 2 * (2 * tm * k + 2 * k * tn + bytes_per_out_elem * tm * tn) + 4 * tm * tn
        if need <= MM_VMEM_BUDGET:
            return tm, tn
    return _pick(m, 8), _pick(n_cap, LANES)


def _mm(name, a, b, grid, a_spec, b_spec, outs, dot, extras=(), epilogue=None, token=None):
    n_ex, n_out = len(extras), len(outs)
    n_in = 2 + n_ex + (0 if token is None else 1)

    def body(*refs):
        ex_refs = refs[2:2 + n_ex]
        out_refs = refs[n_in:n_in + n_out]
        acc = dot(refs[0], refs[1])
        vals = (acc,) if epilogue is None else epilogue(acc, *[e[...] for e in ex_refs])
        for o_ref, val in zip(out_refs, vals):
            o_ref[...] = val.astype(o_ref.dtype)

    ins = [a, b] + [e for e, _ in extras]
    in_specs = [a_spec, b_spec] + [s for _, s in extras]
    if token is not None:
        ins.append(token)
        in_specs.append(pl.BlockSpec(token.shape, lambda i, j: (0, 0)))
    return pl.pallas_call(
        body, name=name,
        out_shape=[jax.ShapeDtypeStruct(s, d) for s, d, _ in outs],
        grid=grid, in_specs=in_specs, out_specs=[s for _, _, s in outs],
        compiler_params=_cparams(2),
    )(*ins)


def _dot(dims):
    return lambda a_ref, b_ref: lax.dot_general(a_ref[...], b_ref[...], dims, preferred_element_type=F32)


def _w_cols(w):
    return w.shape[1] if w.ndim == 2 else w.shape[0] * w.shape[2]


def _w_rows(w):
    return w.shape[0] if w.ndim == 2 else w.shape[1]


def _out_bytes(out_dtypes, extras):
    return sum(jnp.dtype(d).itemsize for d in out_dtypes) + sum(e.dtype.itemsize for e in extras)


def _linear(name, a, w, out_dtypes=(F32,), epilogue=None, extras=(), row_extras=(), tn_multiple=1, token=None):
    t, kdim = a.shape
    n = _w_cols(w)
    tm, tn = _mm_tiles(t, w.shape[-1], kdim, _out_bytes(out_dtypes, extras))
    assert tn % tn_multiple == 0 or tn == n, (tn, tn_multiple)
    if w.ndim == 2:
        w_spec = pl.BlockSpec((kdim, tn), lambda i, j: (0, j))
    else:
        per = w.shape[2] // tn
        w_spec = pl.BlockSpec((None, kdim, tn), lambda i, j: (j // per, 0, j % per))
    o_spec = pl.BlockSpec((tm, tn), lambda i, j: (i, j))
    return _mm(name, a, w, (t // tm, n // tn), pl.BlockSpec((tm, kdim), lambda i, j: (i, 0)), w_spec,
               [((t, n), dt, o_spec) for dt in out_dtypes], _dot(NN),
               extras=[(e, o_spec) for e in extras] + [(e, pl.BlockSpec((tm, e.shape[1]), lambda i, j: (i, 0))) for e in row_extras],
               epilogue=epilogue, token=token)


def _linear_t(name, dy, w, out_dtypes=(F32,), epilogue=None, extras=(), token=None):
    t, n = dy.shape
    kdim = _w_rows(w)
    tm, tn = _mm_tiles(t, kdim, n, _out_bytes(out_dtypes, extras))
    o_spec = pl.BlockSpec((tm, tn), lambda i, j: (i, j))
    if w.ndim == 2:
        w_spec = pl.BlockSpec((tn, n), lambda i, j: (j, 0))
        dot = _dot(NT)
    else:
        n_loc = w.shape[2]
        w_spec = pl.BlockSpec((N_DEV, tn, n_loc), lambda i, j: (0, j, 0))

        def dot(a_ref, w_ref):
            acc = lax.dot_general(a_ref[:, 0:n_loc], w_ref[0], NT, preferred_element_type=F32)
            for dev in range(1, N_DEV):
                acc = acc + lax.dot_general(a_ref[:, dev * n_loc:(dev + 1) * n_loc], w_ref[dev], NT, preferred_element_type=F32)
            return acc

    return _mm(name, dy, w, (t // tm, kdim // tn), pl.BlockSpec((tm, n), lambda i, j: (i, 0)), w_spec,
               [((t, kdim), dt, o_spec) for dt in out_dtypes], dot,
               extras=[(e, o_spec) for e in extras], epilogue=epilogue, token=token)


def _wgrad(name, a, dy, like):
    t, kdim = a.shape
    n = dy.shape[1]
    tm, tn = _mm_tiles(kdim, like.shape[-1], t, 2)
    if like.ndim == 2:
        o_spec = pl.BlockSpec((tm, tn), lambda i, j: (i, j))
    else:
        per = like.shape[2] // tn
        o_spec = pl.BlockSpec((None, tm, tn), lambda i, j: (j // per, i, j % per))
    return _mm(name, a, dy, (kdim // tm, n // tn), pl.BlockSpec((t, tm), lambda i, j: (0, i)),
               pl.BlockSpec((t, tn), lambda i, j: (0, j)), [(like.shape, BF, o_spec)], _dot(TNDIMS))[0]


def _rstd(x):
    return lax.rsqrt(jnp.mean(x * x, axis=-1, keepdims=True) + NORM_EPS)


def _colsum(x):
    return jnp.sum(x, axis=0, keepdims=True)


def _norm_fwd(name, x, y, vec, target, has_post, has_pre, has_loss):
    t, d = x.shape
    tm = _rows(t, d, 4)
    ins, in_specs = [x], [pl.BlockSpec((tm, d), lambda i: (i, 0))]
    row_spec = pl.BlockSpec((tm, d), lambda i: (i, 0))
    if has_post:
        ins.append(y)
        in_specs.append(row_spec)
    ins.append(vec)
    in_specs.append(pl.BlockSpec((8, d), lambda i: (0, 0)))
    if has_loss:
        ins.append(target)
        in_specs.append(row_spec)
    out_shape, out_specs = [], []
    if has_post and not has_loss:
        out_shape.append(jax.ShapeDtypeStruct((t, d), F32))
        out_specs.append(row_spec)
    if has_pre:
        out_shape.append(jax.ShapeDtypeStruct((t, d), BF))
        out_specs.append(row_spec)
    if has_loss:
        out_shape += [jax.ShapeDtypeStruct((t, d), F32), jax.ShapeDtypeStruct((8, d), F32)]
        out_specs += [row_spec, pl.BlockSpec((8, d), lambda i: (0, 0))]

    def body(*refs):
        it = iter(refs)
        x_ref = next(it)
        y_ref = next(it) if has_post else None
        vec_ref = next(it)
        tgt_ref = next(it) if has_loss else None
        xv = x_ref[...]
        if has_post:
            yv = y_ref[...]
            xv = xv + vec_ref[1:2, :] * ((yv * _rstd(yv)) * vec_ref[0:1, :])
            if not has_loss:
                next(it)[...] = xv
        if has_pre:
            hv = ((xv * _rstd(xv)) * vec_ref[2:3, :]) * (1.0 + vec_ref[3:4, :]) + vec_ref[4:5, :]
            next(it)[...] = hv.astype(BF)
        if has_loss:
            e = xv - tgt_ref[...]
            next(it)[...] = e * (1.0 / d)
            acc_ref = next(it)
            i = pl.program_id(0)

            @pl.when(i == 0)
            def _():
                acc_ref[...] = jnp.zeros_like(acc_ref)

            acc_ref[0:1, :] += _colsum(e * e)

    return pl.pallas_call(
        body, name=name, out_shape=out_shape, grid=(t // tm,), in_specs=in_specs, out_specs=out_specs,
        compiler_params=_cparams(1),
    )(*ins)


def _norm_bwd(name, dx_in, dh, x, y, vec, has_pre, has_post):
    t, d = dx_in.shape
    tm = _rows(t, d, 4)
    row_spec = pl.BlockSpec((tm, d), lambda i: (i, 0))
    vec_spec = pl.BlockSpec((8, d), lambda i: (0, 0))
    ins, in_specs = [dx_in], [row_spec]
    if has_pre:
        ins += [dh, x]
        in_specs += [row_spec, row_spec]
    if has_post:
        ins.append(y)
        in_specs.append(row_spec)
    ins.append(vec)
    in_specs.append(vec_spec)
    out_shape, out_specs = [], []
    if has_pre:
        out_shape.append(jax.ShapeDtypeStruct((t, d), F32))
        out_specs.append(row_spec)
    if has_post:
        out_shape.append(jax.ShapeDtypeStruct((t, d), BF))
        out_specs.append(row_spec)
    out_shape.append(jax.ShapeDtypeStruct((8, d), F32))
    out_specs.append(vec_spec)

    def body(*refs):
        it = iter(refs)
        dx = next(it)[...]
        dh_ref = next(it) if has_pre else None
        x_ref = next(it) if has_pre else None
        y_ref = next(it) if has_post else None
        vec_ref = next(it)
        dx_ref = next(it) if has_pre else None
        dy_ref = next(it) if has_post else None
        sums_ref = next(it)
        i = pl.program_id(0)

        @pl.when(i == 0)
        def _():
            sums_ref[...] = jnp.zeros_like(sums_ref)

        if has_pre:
            dhv, xv = dh_ref[...], x_ref[...]
            rs = _rstd(xv)
            xhat = xv * rs
            na = vec_ref[2:3, :]
            sums_ref[0:1, :] += _colsum(dhv)
            sums_ref[1:2, :] += _colsum(dhv * (xhat * na))
            tt = dhv * (1.0 + vec_ref[3:4, :])
            sums_ref[2:3, :] += _colsum(tt * xhat)
            u = tt * na
            dx = dx + rs * (u - xhat * jnp.mean(u * xhat, axis=-1, keepdims=True))
            dx_ref[...] = dx
        if has_post:
            yv = y_ref[...]
            rs = _rstd(yv)
            yhat = yv * rs
            nb = vec_ref[0:1, :]
            sums_ref[3:4, :] += _colsum(dx * (yhat * nb))
            tt = dx * vec_ref[1:2, :]
            sums_ref[4:5, :] += _colsum(tt * yhat)
            u = tt * nb
            dy_ref[...] = (rs * (u - yhat * jnp.mean(u * yhat, axis=-1, keepdims=True))).astype(BF)

    return pl.pallas_call(
        body, name=name, out_shape=out_shape, grid=(t // tm,), in_specs=in_specs, out_specs=out_specs,
        compiler_params=_cparams(1),
    )(*ins)


def _rope128(tv, cs, s1, s2):
    return tv * cs + pltpu.roll(tv, 96, 1) * s1 + pltpu.roll(tv, 32, 1) * s2


def _lat_post(name, lat, gq, gkv, cs, s1, s2, rq, rkv):
    t, w = lat.shape
    tm = _rows(t, w, 4)

    def body(lat_ref, gq_ref, gkv_ref, cs_ref, s1_ref, s2_ref, cq_ref, ckv_ref, kr_ref):
        lq = lat_ref[:, 0:rq]
        lkv = lat_ref[:, rq:rq + rkv]
        cq_ref[...] = ((lq * _rstd(lq)) * gq_ref[...]).astype(BF)
        ckv_ref[...] = ((lkv * _rstd(lkv)) * gkv_ref[...]).astype(BF)
        kr_ref[...] = _rope128(lat_ref[:, rq + rkv:rq + rkv + LANES], cs_ref[...], s1_ref[...], s2_ref[...]).astype(BF)

    def rows(c):
        return pl.BlockSpec((tm, c), lambda i: (i, 0))

    def vecs(c):
        return pl.BlockSpec((1, c), lambda i: (0, 0))

    return pl.pallas_call(
        body, name=name,
        out_shape=[jax.ShapeDtypeStruct((t, rq), BF), jax.ShapeDtypeStruct((t, rkv), BF), jax.ShapeDtypeStruct((t, LANES), BF)],
        grid=(t // tm,),
        in_specs=[rows(w), vecs(rq), vecs(rkv), rows(LANES), rows(LANES), rows(LANES)],
        out_specs=[rows(rq), rows(rkv), rows(LANES)],
        compiler_params=_cparams(1),
    )(lat, gq, gkv, cs, s1, s2)


def _lat_bwd(name, dcq, dckv, dkr, lat, gq, gkv, cs, s1, s2, rq, rkv, heads):
    t, w = lat.shape
    tm = _rows(t, max(w, heads * LANES), 4)
    assert rq == rkv

    def body(dcq_ref, dckv_ref, dkr_ref, lat_ref, gq_ref, gkv_ref, cs_ref, s1_ref, s2_ref, dlat_ref, sums_ref):
        i = pl.program_id(0)

        @pl.when(i == 0)
        def _():
            sums_ref[...] = jnp.zeros_like(sums_ref)

        def rms_bwd(dc, lv, g, row):
            rs = _rstd(lv)
            lhat = lv * rs
            sums_ref[row:row + 1, :] += _colsum(dc * lhat)
            u = dc * g
            return rs * (u - lhat * jnp.mean(u * lhat, axis=-1, keepdims=True))

        dlat_ref[:, 0:rq] = rms_bwd(dcq_ref[...], lat_ref[:, 0:rq], gq_ref[...], 0).astype(BF)
        dlat_ref[:, rq:rq + rkv] = rms_bwd(dckv_ref[...], lat_ref[:, rq:rq + rkv], gkv_ref[...], 1).astype(BF)
        dk = dkr_ref[:, 0:LANES]
        for h in range(1, heads):
            dk = dk + dkr_ref[:, h * LANES:(h + 1) * LANES]
        dlat_ref[:, rq + rkv:rq + rkv + LANES] = _rope128(dk, cs_ref[...], s1_ref[...], s2_ref[...]).astype(BF)

    def rows(c):
        return pl.BlockSpec((tm, c), lambda i: (i, 0))

    def vecs(c):
        return pl.BlockSpec((1, c), lambda i: (0, 0))

    return pl.pallas_call(
        body, name=name,
        out_shape=[jax.ShapeDtypeStruct((t, w), BF), jax.ShapeDtypeStruct((8, rq), F32)],
        grid=(t // tm,),
        in_specs=[rows(rq), rows(rkv), rows(heads * LANES), rows(w), vecs(rq), vecs(rkv), rows(LANES), rows(LANES), rows(LANES)],
        out_specs=[rows(w), pl.BlockSpec((8, rq), lambda i: (0, 0))],
        compiler_params=_cparams(1),
    )(dcq, dckv, dkr, lat, gq, gkv, cs, s1, s2)


def _causal_mask(s, row0):
    row = lax.broadcasted_iota(jnp.int32, s.shape, 0) + row0
    col = lax.broadcasted_iota(jnp.int32, s.shape, 1)
    return col <= row


def _flash_fwd(name, qb, kv, kr, heads, scale):
    t = qb.shape[0]
    tq = _pick(t, TQ)
    nq = t // tq
    c2 = scale * LOG2E

    def body(q_ref, kv_ref, kr_ref, o_ref, lse_ref):
        i = pl.program_id(1)
        q = q_ref[...]

        def block(j, carry, masked):
            m_prev, l_prev, acc = carry
            rows = pl.ds(pl.multiple_of(j * tq, tq), tq)
            k = jnp.concatenate([kv_ref[rows, 0:LANES], kr_ref[rows, :]], axis=-1)
            s = lax.dot_general(q, k, NT, preferred_element_type=F32) * c2
            if masked:
                s = jnp.where(_causal_mask(s, 0), s, -1e30)
            m_new = jnp.maximum(m_prev, jnp.max(s, axis=-1, keepdims=True))
            alpha = jnp.exp2(m_prev - m_new)
            p = jnp.exp2(s - m_new)
            l_new = alpha * l_prev + jnp.sum(p, axis=-1, keepdims=True)
            pv = jnp.dot(p.astype(BF), kv_ref[rows, LANES:2 * LANES], preferred_element_type=F32)
            return m_new, l_new, alpha * acc + pv

        init = (jnp.full((tq, 1), -1e30, F32), jnp.zeros((tq, 1), F32), jnp.zeros((tq, V_DIM), F32))
        carry = lax.fori_loop(0, i, lambda j, cr: block(j, cr, False), init)
        m_fin, l_fin, acc = block(i, carry, True)
        o_ref[...] = (acc / l_fin).astype(BF)
        lse_ref[...] = jnp.broadcast_to(m_fin + jnp.log2(l_fin), (tq, LANES))

    return pl.pallas_call(
        body, name=name,
        out_shape=[jax.ShapeDtypeStruct((t, heads * V_DIM), BF), jax.ShapeDtypeStruct((t, heads * LANES), F32)],
        grid=(heads, nq),
        in_specs=[pl.BlockSpec((tq, 2 * LANES), lambda h, i: (i, h)),
                  pl.BlockSpec((t, 2 * LANES), lambda h, i: (0, h)),
                  pl.BlockSpec((t, LANES), lambda h, i: (0, 0))],
        out_specs=[pl.BlockSpec((tq, V_DIM), lambda h, i: (i, h)), pl.BlockSpec((tq, LANES), lambda h, i: (i, h))],
        compiler_params=_cparams(2),
    )(qb, kv, kr)


def _flash_bwd(name, qb, kv, kr, o, lse, do, cs, s1, s2, heads, scale):
    t = qb.shape[0]
    tq = _pick(t, TQ)
    nq = t // tq

    c2 = scale * LOG2E

    def body(q_ref, kv_ref, kr_ref, o_ref, lse_ref, do_ref, cs_ref, s1_ref, s2_ref, dkv_ref, dkr_ref, dq_ref, dq_acc):
        j = pl.program_id(1)

        @pl.when(j == 0)
        def _():
            dq_acc[...] = jnp.zeros_like(dq_acc)

        k = jnp.concatenate([kv_ref[:, 0:LANES], kr_ref[...]], axis=-1)
        v = kv_ref[:, LANES:2 * LANES]

        def block(i, carry, masked):
            dk, dv = carry
            rows = pl.ds(pl.multiple_of(i * tq, tq), tq)
            q = q_ref[rows, :]
            dov = do_ref[rows, :]
            delta = jnp.sum(dov.astype(F32) * o_ref[rows, :].astype(F32), axis=-1, keepdims=True)
            s = lax.dot_general(q, k, NT, preferred_element_type=F32) * c2
            p = jnp.exp2(s - lse_ref[rows, 0:1])
            if masked:
                p = jnp.where(_causal_mask(s, 0), p, 0.0)
            dv = dv + lax.dot_general(p.astype(BF), dov, TNDIMS, preferred_element_type=F32)
            dp = lax.dot_general(dov, v, NT, preferred_element_type=F32)
            ds = (p * (dp - delta) * scale).astype(BF)
            dk = dk + lax.dot_general(ds, q, TNDIMS, preferred_element_type=F32)
            dq_acc[rows, :] += jnp.dot(ds, k, preferred_element_type=F32)
            return dk, dv

        carry = block(j, (jnp.zeros((tq, 2 * LANES), F32), jnp.zeros((tq, V_DIM), F32)), True)
        dk, dv = lax.fori_loop(j + 1, nq, lambda i, cr: block(i, cr, False), carry)
        dkv_ref[:, 0:LANES] = dk[:, 0:LANES].astype(BF)
        dkv_ref[:, LANES:2 * LANES] = dv.astype(BF)
        dkr_ref[...] = dk[:, LANES:2 * LANES]

        @pl.when(j == nq - 1)
        def _():
            dq_ref[:, 0:LANES] = dq_acc[:, 0:LANES].astype(BF)
            dq_ref[:, LANES:2 * LANES] = _rope128(dq_acc[:, LANES:2 * LANES], cs_ref[...], s1_ref[...], s2_ref[...]).astype(BF)

    def per_head(width):
        return pl.BlockSpec((t, width), lambda h, j: (0, h))

    table = pl.BlockSpec((t, LANES), lambda h, j: (0, 0))
    return pl.pallas_call(
        body, name=name,
        out_shape=[jax.ShapeDtypeStruct((t, heads * 2 * LANES), BF), jax.ShapeDtypeStruct((t, heads * LANES), F32),
                   jax.ShapeDtypeStruct((t, heads * 2 * LANES), BF)],
        grid=(heads, nq),
        in_specs=[per_head(2 * LANES), pl.BlockSpec((tq, 2 * LANES), lambda h, j: (j, h)),
                  pl.BlockSpec((tq, LANES), lambda h, j: (j, 0)), per_head(LANES), per_head(LANES), per_head(LANES),
                  table, table, table],
        out_specs=[pl.BlockSpec((tq, 2 * LANES), lambda h, j: (j, h)), pl.BlockSpec((tq, LANES), lambda h, j: (j, h)),
                   per_head(2 * LANES)],
        scratch_shapes=[pltpu.VMEM((t, 2 * LANES), F32)],
        compiler_params=_cparams(2),
    )(qb, kv, kr, o, lse, do, cs, s1, s2)


def _shift_down(z, n):
    row = lax.broadcasted_iota(jnp.int32, z.shape, 0)
    return jnp.where(row >= n, pltpu.roll(z, n, 0), 0.0)


def _shift_up(z, n):
    t = z.shape[0]
    row = lax.broadcasted_iota(jnp.int32, z.shape, 0)
    return jnp.where(row < t - n, pltpu.roll(z, t - n, 0), 0.0)


def _conv_fwd(name, proj, cw):
    t, d3 = proj.shape
    d = d3 // 3
    tn = _pick(d, LANES)
    nb = d // tn

    def body(b_ref, c_ref, u_ref, w_ref, o_ref):
        z = c_ref[...] * u_ref[...]
        zc = w_ref[0:1, :] * _shift_down(z, 2) + w_ref[1:2, :] * _shift_down(z, 1) + w_ref[2:3, :] * z
        o_ref[...] = (b_ref[...] * zc).astype(BF)

    def part(p):
        return pl.BlockSpec((t, tn), lambda j: (0, p * nb + j))

    return pl.pallas_call(
        body, name=name, out_shape=jax.ShapeDtypeStruct((t, d), BF), grid=(nb,),
        in_specs=[part(0), part(1), part(2), pl.BlockSpec((8, tn), lambda j: (0, j))],
        out_specs=pl.BlockSpec((t, tn), lambda j: (0, j)),
        compiler_params=_cparams(1),
    )(proj, proj, proj, cw)


def _conv_bwd(name, dbz, proj, cw):
    t, d3 = proj.shape
    d = d3 // 3
    tn = _pick(d, LANES)
    nb = d // tn

    def body(g_ref, b_ref, c_ref, u_ref, w_ref, o_ref, sums_ref):
        p = pl.program_id(1)
        w0, w1, w2 = w_ref[0:1, :], w_ref[1:2, :], w_ref[2:3, :]

        @pl.when(p == 0)
        def _():
            z = c_ref[...] * u_ref[...]
            z1, z2 = _shift_down(z, 1), _shift_down(z, 2)
            gv = g_ref[...]
            o_ref[...] = (gv * (w0 * z2 + w1 * z1 + w2 * z)).astype(BF)
            dzc = gv * b_ref[...]
            sums_ref[...] = jnp.zeros_like(sums_ref)
            sums_ref[0:1, :] = _colsum(dzc * z2)
            sums_ref[1:2, :] = _colsum(dzc * z1)
            sums_ref[2:3, :] = _colsum(dzc * z)

        @pl.when(p > 0)
        def _():
            dzc = g_ref[...] * b_ref[...]
            dz = w2 * dzc + w1 * _shift_up(dzc, 1) + w0 * _shift_up(dzc, 2)
            other = jnp.where(p == 1, u_ref[...], c_ref[...])
            o_ref[...] = (dz * other).astype(BF)

    def part(q):
        return pl.BlockSpec((t, tn), lambda j, p: (0, q * nb + j))

    return pl.pallas_call(
        body, name=name,
        out_shape=[jax.ShapeDtypeStruct((t, d3), BF), jax.ShapeDtypeStruct((8, d), F32)],
        grid=(nb, 3),
        in_specs=[pl.BlockSpec((t, tn), lambda j, p: (0, j)), part(0), part(1), part(2),
                  pl.BlockSpec((8, tn), lambda j, p: (0, j))],
        out_specs=[pl.BlockSpec((t, tn), lambda j, p: (0, p * nb + j)), pl.BlockSpec((8, tn), lambda j, p: (0, j))],
        compiler_params=_cparams(2),
    )(dbz, proj, proj, proj, cw)


def _pad_rows8(v):
    return jnp.pad(v, ((0, 8 - v.shape[0]), (0, 0)))


def kernel(x, c, positions, w_mod, b_mod, norm_g, mla_w_in, mla_g_q, mla_g_kv, mla_w_uq, mla_w_ukv, mla_w_o, conv_w_in, conv_w, conv_w_out, mlp_w_up, mlp_w_down, loss_target, m_w_mod, m_b_mod, m_norm_g, m_mla_w_in, m_mla_g_q, m_mla_g_kv, m_mla_w_uq, m_mla_w_ukv, m_mla_w_o, m_conv_w_in, m_conv_w, m_conv_w_out, m_mlp_w_up, m_mlp_w_down, v_w_mod, v_b_mod, v_norm_g, v_mla_w_in, v_mla_g_q, v_mla_g_kv, v_mla_w_uq, v_mla_w_ukv, v_mla_w_o, v_conv_w_in, v_conv_w, v_conv_w_out, v_mlp_w_up, v_mlp_w_down):
    t, d = x.shape[1], x.shape[2]
    depth = w_mod.shape[0]
    n_mod_loc = w_mod.shape[2]
    d_loc = norm_g.shape[2]
    rq = mla_g_q.shape[1]
    rkv = mla_g_kv.shape[1]
    heads = mla_w_uq.shape[2]
    lat_w = mla_w_in.shape[2]
    lat_pad = rq + rkv + LANES
    scale = (NOPE_DIM + ROPE_DIM) ** -0.5
    xi, yi, ci = _pos()
    dev = 4 * xi + 2 * yi + ci
    c_idx = jnp.reshape(ci, (1,)).astype(jnp.int32)
    chip_idx = jnp.reshape(2 * xi + yi, (1,)).astype(jnp.int32)
    x0 = x[0]
    target = loss_target[0]

    w_in_loc = jnp.pad(mla_w_in[0], ((0, 0), (0, lat_pad - lat_w))).astype(BF)
    uq = mla_w_uq[0]
    uq_loc = jnp.pad(uq, ((0, 0), (0, 0), (0, LANES - ROPE_DIM))).reshape(uq.shape[0], heads * 2 * LANES).astype(BF)
    ukv_loc = mla_w_ukv[0].reshape(mla_w_ukv.shape[1], heads * (NOPE_DIM + V_DIM)).astype(BF)
    def merged(g):
        return g.reshape(g.shape[0] * g.shape[1], g.shape[2])

    g_win, g_uq, g_ukv = [merged(g) for g in _ag_big("ag_mla", [w_in_loc, uq_loc, ukv_loc])]

    n_ng = depth * 4 * d_loc
    small = jnp.concatenate([c.reshape(-1), norm_g.reshape(-1), conv_w.reshape(-1)])
    small_n = small.shape[0]
    small_rows = -(-small_n // (8 * LANES)) * 8
    small = jnp.pad(small, (0, small_rows * LANES - small_n)).reshape(small_rows, LANES)
    small_all = _ag_small("ag_small", small).reshape(N_DEV, small_rows * LANES)
    c_all = small_all[:, :d]
    ng_all = small_all[:, d:d + n_ng].reshape(N_DEV, depth, 4, d_loc).transpose(1, 2, 0, 3).reshape(depth, 4, d)
    cw_all = small_all[:, d + n_ng:d + n_ng + 3 * d_loc].reshape(N_DEV, 3, d_loc).transpose(1, 0, 2).reshape(3, d)
    cw8 = _pad_rows8(cw_all)
    cond_all = _silu("silu", c_all)

    mod_parts = jnp.stack([_mod_matvec(f"mod_matvec{i}", cond_all, w_mod, i) for i in range(depth)])
    mod_rows = depth * N_DEV * n_mod_loc // LANES
    mod_all = _ag_small("ag_mod", mod_parts.reshape(mod_rows, LANES)).reshape(N_DEV, depth, N_DEV, n_mod_loc)
    mod_mine = lax.dynamic_index_in_dim(mod_all, dev, axis=2, keepdims=False)
    mod = mod_mine.transpose(1, 0, 2).reshape(depth, N_DEV * n_mod_loc) + b_mod
    mod = mod.reshape(depth, 6, d)

    dev_idx = jnp.reshape(dev, (1,)).astype(jnp.int32)
    later = [(mla_w_o, 0), (mlp_w_up, 0), (mlp_w_down, 0), (conv_w_in, 0), (conv_w_out, 0), (mlp_w_up, 1), (mlp_w_down, 1)]
    lands = [_cast_own_block(f"cast_own_block{a}", w, layer, dev_idx) for a, (w, layer) in enumerate(later)]
    ag_groups = [[0], [1, 2], [3, 4, 5, 6]]
    ag_sems, lands_thru, ag_token = _ag_start("ag_start", lands, ag_groups, [mod, g_ukv])

    def fwd_vec(nb, gate, na, sc, sh, token=None):
        zero = jnp.zeros((d,), F32)
        rows = [zero if r is None else r for r in (nb, gate, na, sc, sh)]
        vec = _pad_rows8(jnp.stack(rows))
        return vec if token is None else vec + token[0, 0]

    inv_freq = ROPE_THETA ** (-jnp.arange(0, ROPE_DIM, 2, dtype=F32) / ROPE_DIM)
    ang = positions[0].astype(F32)[:, None] * inv_freq
    cos, sin = jnp.cos(ang), jnp.sin(ang)
    zh = jnp.zeros_like(cos)
    zpad = jnp.zeros((t, LANES - ROPE_DIM), F32)
    rope_c = jnp.concatenate([cos, cos, zpad], axis=1)
    rope_s1 = jnp.concatenate([-sin, zh, zpad], axis=1)
    rope_s2 = jnp.concatenate([zh, sin, zpad], axis=1)

    (h0,) = _norm_fwd("pre0", x0, None, fwd_vec(None, None, ng_all[0, 0], mod[0, 1], mod[0, 0], ag_token), None, False, True, False)
    (lat,) = _linear("mla_lat", h0, g_win)
    cq, ckv, kr = _lat_post("mla_lat_post", lat, mla_g_q, mla_g_kv, rope_c, rope_s1, rope_s2, rq, rkv)
    def rope_odd_groups(acc, csv, s1v, s2v):
        groups = [acc[:, g * LANES:(g + 1) * LANES] for g in range(acc.shape[1] // LANES)]
        return (jnp.concatenate([grp if g % 2 == 0 else _rope128(grp, csv, s1v, s2v) for g, grp in enumerate(groups)], axis=1),)

    (qb,) = _linear("mla_q", cq, g_uq, out_dtypes=(BF,), epilogue=rope_odd_groups,
                    row_extras=(rope_c, rope_s1, rope_s2), tn_multiple=2 * LANES)
    (kvb,) = _linear("mla_kv", ckv, g_ukv, out_dtypes=(BF,))
    fs_o, fr_o, lands_o, _ = _ag_forward("ag_forward_o", lands_thru[0:1], *ag_sems[0], kvb)
    o, lse = _flash_fwd("mla_attn", qb, kvb, kr, heads, scale)
    (g_wo,) = _ag_finish("ag_finish_o", lands_o, fs_o, fr_o, o)
    g_wo = merged(g_wo)
    fs_a, fr_a, lands_a, _ = _ag_forward("ag_forward_a", lands_thru[1:3], *ag_sems[1], o)
    (y0,) = _linear("mla_out", o, g_wo)
    x1, h1 = _norm_fwd("postpre1", x0, y0, fwd_vec(ng_all[0, 1], mod[0, 2], ng_all[0, 2], mod[0, 4], mod[0, 3]), None, True, True, False)
    g_up0, g_dn0 = _ag_finish("ag_finish_a", lands_a, fs_a, fr_a, x1)
    g_dn0 = merged(g_dn0)

    def sq_relu(acc):
        a = jnp.maximum(acc, 0.0)
        return acc, a * a

    u1, act1 = _linear("mlp0_up", h1, g_up0, out_dtypes=(F32, BF), epilogue=sq_relu)
    (y1,) = _linear("mlp0_down", act1, g_dn0)
    fs_b, fr_b, lands_b, _ = _ag_forward("ag_forward_b", lands_thru[3:7], *ag_sems[2], y1)
    x2, h2 = _norm_fwd("postpre2", x1, y1, fwd_vec(ng_all[0, 3], mod[0, 5], ng_all[1, 0], mod[1, 1], mod[1, 0]), None, True, True, False)
    g_cin, g_cout, g_up1, g_dn1 = _ag_finish("ag_finish_b", lands_b, fs_b, fr_b, x2)
    g_cout, g_dn1 = merged(g_cout), merged(g_dn1)
    g_up, g_dn = [g_up0, g_up1], [g_dn0, g_dn1]

    (proj,) = _linear("conv_in", h2, g_cin)
    bz = _conv_fwd("conv_mix", proj, cw8)
    (y2,) = _linear("conv_out", bz, g_cout)
    x3, h3 = _norm_fwd("postpre3", x2, y2, fwd_vec(ng_all[1, 1], mod[1, 2], ng_all[1, 2], mod[1, 4], mod[1, 3]), None, True, True, False)

    u3, act3 = _linear("mlp1_up", h3, g_up[1], out_dtypes=(F32, BF), epilogue=sq_relu)
    (y3,) = _linear("mlp1_down", act3, g_dn[1])
    dx4, loss_cols = _norm_fwd("post_loss", x3, y3, fwd_vec(ng_all[1, 3], mod[1, 5], None, None, None), target, True, False, True)
    loss = lax.psum(0.5 * jnp.sum(loss_cols[0]) / d, ("x", "y", "c"))

    def bwd_vec(nb, gate, na, sc, token=None):
        zero = jnp.zeros((d,), F32)
        rows = [zero if r is None else r for r in (nb, gate, na, sc)]
        vec = _pad_rows8(jnp.stack(rows))
        return vec if token is None else vec + token[0, 0]

    def blocks(g, like):
        return g.reshape((N_DEV,) + like.shape)

    def rs_begin(tag, grads):
        return _push_start(f"rs_sibling_start_{tag}", grads, _sibling_routes)

    def rs_continue(tag, started, after):
        send_sems, recv_sems, grads, from_sibling, _ = started
        grads, from_sibling = _push_wait(f"rs_sibling_wait_{tag}", grads, from_sibling, send_sems, recv_sems, _sibling_routes, after)
        pair = [_pair_add(f"pair_add_{tag}{a}", grads[a], from_sibling[a], c_idx) for a in range(len(grads))]
        return _push_start(f"rs_start_{tag}", pair, _chip_routes)

    def mlp_bwd(tag, dy, h, u, act, w_up, w_dn, w_dn_shard):
        def relu_grad(acc, uv):
            return (acc * (2.0 * jnp.maximum(uv, 0.0)),)

        (du,) = _linear_t(f"{tag}_dact", dy, w_dn, out_dtypes=(BF,), epilogue=relu_grad, extras=(u,))
        dw_dn = _wgrad(f"{tag}_dw_down", act, dy, w_dn)
        dw_up = _wgrad(f"{tag}_dw_up", h, du, w_up)
        sib = rs_begin(tag, [dw_up, blocks(dw_dn, w_dn_shard)])
        (dh,) = _linear_t(f"{tag}_dh", du, w_up, token=sib[4])
        return dh, rs_continue(tag, sib, dh)

    dy3, sums4 = _norm_bwd("bwd_post3", dx4, None, None, y3, bwd_vec(ng_all[1, 3], mod[1, 5], None, None), False, True)
    dh3, rs_mlp1 = mlp_bwd("mlp1", dy3, h3, u3, act3, g_up[1], g_dn[1], mlp_w_down[1])
    dx3, dy2, sums3 = _norm_bwd("bwd_norm3", dx4, dh3, x3, y2, bwd_vec(ng_all[1, 1], mod[1, 2], ng_all[1, 2], mod[1, 4], rs_mlp1[4]), True, True)

    (dbz,) = _linear_t("conv_dbz", dy2, g_cout)
    dw_cout = _wgrad("conv_dw_out", bz, dy2, g_cout)
    dproj, conv_sums = _conv_bwd("conv_mix_bwd", dbz, proj, cw8)
    dw_cin = _wgrad("conv_dw_in", h2, dproj, g_cin)
    sib_conv = rs_begin("conv", [dw_cin, blocks(dw_cout, conv_w_out[0])])
    (dh2,) = _linear_t("conv_dh", dproj, g_cin, token=sib_conv[4])
    rs_conv = rs_continue("conv", sib_conv, dh2)
    dx2, dy1, sums2 = _norm_bwd("bwd_norm2", dx3, dh2, x2, y1, bwd_vec(ng_all[0, 3], mod[0, 5], ng_all[1, 0], mod[1, 1], rs_conv[4]), True, True)

    dh1, rs_mlp0 = mlp_bwd("mlp0", dy1, h1, u1, act1, g_up[0], g_dn[0], mlp_w_down[0])
    dx1, dy0, sums1 = _norm_bwd("bwd_norm1", dx2, dh1, x1, y0, bwd_vec(ng_all[0, 1], mod[0, 2], ng_all[0, 2], mod[0, 4], rs_mlp0[4]), True, True)

    (do,) = _linear_t("mla_do", dy0, g_wo, out_dtypes=(BF,))
    dw_o = _wgrad("mla_dw_o", o, dy0, g_wo)
    dkv, dkr, dq = _flash_bwd("mla_attn_bwd", qb, kvb, kr, o, lse, do, rope_c, -rope_s1, -rope_s2, heads, scale)
    dw_uq = _wgrad("mla_dw_uq", cq, dq, g_uq)
    (dcq,) = _linear_t("mla_dcq", dq, g_uq)
    dw_ukv = _wgrad("mla_dw_ukv", ckv, dkv, g_ukv)
    (dckv,) = _linear_t("mla_dckv", dkv, g_ukv)
    dlat, lat_sums = _lat_bwd("mla_lat_bwd", dcq, dckv, dkr, lat, mla_g_q, mla_g_kv, rope_c, -rope_s1, -rope_s2, rq, rkv, heads)
    dw_win = _wgrad("mla_dw_in", h0, dlat, g_win)
    sib_mla = rs_begin("mla", [blocks(dw_win, w_in_loc), blocks(dw_uq, uq_loc), blocks(dw_ukv, ukv_loc), blocks(dw_o, mla_w_o[0])])
    (dh0,) = _linear_t("mla_dh", dlat, g_win, token=sib_mla[4])
    grad_x, sums0 = _norm_bwd("bwd_pre0", dx1, dh0, x0, None, bwd_vec(None, None, ng_all[0, 0], mod[0, 1]), True, False)

    def sums_rows(s, rows):
        return [s[r] for r in rows]

    dmod = jnp.stack([
        jnp.stack(sums_rows(sums0, (0, 1)) + sums_rows(sums1, (3,)) + sums_rows(sums1, (0, 1)) + sums_rows(sums2, (3,))),
        jnp.stack(sums_rows(sums2, (0, 1)) + sums_rows(sums3, (3,)) + sums_rows(sums3, (0, 1)) + sums_rows(sums4, (3,))),
    ])
    dng = jnp.stack([
        jnp.stack([sums0[2], sums1[4], sums1[2], sums2[4]]),
        jnp.stack([sums2[2], sums3[4], sums3[2], sums4[4]]),
    ])
    parts = [dmod.reshape(-1), dng.reshape(-1), lat_sums[0], lat_sums[1], conv_sums[0:3].reshape(-1)]
    sizes = [p.shape[0] for p in parts]
    packed = jnp.concatenate(parts)
    pk_rows = -(-packed.shape[0] // (8 * LANES)) * 8
    packed = jnp.pad(packed, (0, pk_rows * LANES - packed.shape[0])).reshape(pk_rows, LANES)
    packed_all = _ag_small("ag_small_grads", packed).reshape(N_DEV, pk_rows, LANES)
    rs_mla = rs_continue("mla", sib_mla, packed_all)
    total = _sum_devices("sum_small_grads", packed_all).reshape(-1)
    offs = [0]
    for s in sizes:
        offs.append(offs[-1] + s)
    g_b_mod = total[offs[0]:offs[1]].reshape(depth, 6 * d)
    ng_full = total[offs[1]:offs[2]].reshape(depth, 4, d)
    g_norm_g = lax.dynamic_slice_in_dim(ng_full, dev * d_loc, d_loc, axis=2)
    g_g_q = total[offs[2]:offs[3]].reshape(1, rq)
    g_g_kv = total[offs[3]:offs[4]].reshape(1, rkv)
    cw_full = total[offs[4]:offs[5]].reshape(1, 3, d)
    g_conv_w = lax.dynamic_slice_in_dim(cw_full, dev * d_loc, d_loc, axis=2)

    dmod_all = packed_all.reshape(N_DEV, -1)[:, :sizes[0]].reshape(N_DEV, depth, 6 * d)
    dmod_cols = lax.dynamic_slice_in_dim(dmod_all, dev * n_mod_loc, n_mod_loc, axis=2)
    cond_t = cond_all.T
    mod_out = None
    for i in range(depth):
        mod_out = _mod_grad_adam(f"w_mod_adam{i}", cond_t, dmod_cols[:, i, :], w_mod, m_w_mod, v_w_mod, i, mod_out)
    g_w_mod, d_w_mod, nm_w_mod, nv_w_mod = mod_out

    def rs_end(tag, started):
        send_sems, recv_sems, pair, lands, _ = started
        return _push_wait(f"rs_wait_{tag}", pair, lands, send_sems, recv_sems, _chip_routes, g_w_mod)

    pair_mlp1, lands_mlp1 = rs_end("mlp1", rs_mlp1)
    pair_conv, lands_conv = rs_end("conv", rs_conv)
    pair_mlp0, lands_mlp0 = rs_end("mlp0", rs_mlp0)
    pair_mla, lands_mla = rs_end("mla", rs_mla)

    results = {"w_mod": (g_w_mod, d_w_mod, nm_w_mod, nv_w_mod)}
    given = {"b_mod": (b_mod, m_b_mod, v_b_mod), "norm_g": (norm_g, m_norm_g, v_norm_g),
             "mla_w_in": (mla_w_in, m_mla_w_in, v_mla_w_in), "mla_g_q": (mla_g_q, m_mla_g_q, v_mla_g_q),
             "mla_g_kv": (mla_g_kv, m_mla_g_kv, v_mla_g_kv), "mla_w_uq": (mla_w_uq, m_mla_w_uq, v_mla_w_uq),
             "mla_w_ukv": (mla_w_ukv, m_mla_w_ukv, v_mla_w_ukv), "mla_w_o": (mla_w_o, m_mla_w_o, v_mla_w_o),
             "conv_w_in": (conv_w_in, m_conv_w_in, v_conv_w_in), "conv_w": (conv_w, m_conv_w, v_conv_w),
             "conv_w_out": (conv_w_out, m_conv_w_out, v_conv_w_out), "mlp_w_up": (mlp_w_up, m_mlp_w_up, v_mlp_w_up),
             "mlp_w_down": (mlp_w_down, m_mlp_w_down, v_mlp_w_down)}

    def fused(name, parts, prev=None):
        for layer, (pair, lands) in enumerate(parts):
            prev = _adam_sum(f"adam_{name}{layer}", pair, lands, chip_idx, *given[name], layer, prev)
        results[name] = tuple(prev)

    fused("mlp_w_up", [(pair_mlp0[0], lands_mlp0[0]), (pair_mlp1[0], lands_mlp1[0])])
    fused("mlp_w_down", [(pair_mlp0[1], lands_mlp0[1]), (pair_mlp1[1], lands_mlp1[1])])
    fused("conv_w_in", [(pair_conv[0], lands_conv[0])])
    fused("conv_w_out", [(pair_conv[1], lands_conv[1])])
    fused("mla_w_o", [(pair_mla[3], lands_mla[3])])

    red_win, red_uq, red_ukv = [_chip_sum(f"chip_sum_mla{a}", pair_mla[a], lands_mla[a], chip_idx) for a in range(3)]
    plain = {"b_mod": g_b_mod, "norm_g": g_norm_g, "mla_g_q": g_g_q, "mla_g_kv": g_g_kv, "conv_w": g_conv_w,
             "mla_w_in": red_win[:, :lat_w][None],
             "mla_w_uq": red_uq.reshape(-1, heads, 2 * LANES)[:, :, :NOPE_DIM + ROPE_DIM][None],
             "mla_w_ukv": red_ukv.reshape(mla_w_ukv.shape)}
    for name, g in plain.items():
        results[name] = (g, *_adam(f"adam_{name}", given[name][0], g, given[name][1], given[name][2]))

    order = ["w_mod", "b_mod", "norm_g", "mla_w_in", "mla_g_q", "mla_g_kv", "mla_w_uq", "mla_w_ukv", "mla_w_o",
             "conv_w_in", "conv_w", "conv_w_out", "mlp_w_up", "mlp_w_down"]
    return (loss, grad_x[None], *[results[n][0] for n in order], *[results[n][1] for n in order],
            *[results[n][2] for n in order], *[results[n][3] for n in order])
```

```python
import jax
import jax.numpy as jnp
from jax import lax
from jax.experimental import pallas as pl
from jax.experimental.pallas import tpu as pltpu

F32 = jnp.float32
BF = jnp.bfloat16
MESH = pl.DeviceIdType.MESH
N_DEV = 8
LANES = 128
NORM_EPS = 1e-6
ROPE_THETA = 10000.0
ROPE_DIM = 64
NOPE_DIM = 128
V_DIM = 128
LR, B1, B2, ADAM_EPS, WD, STEP = 0.001, 0.9, 0.999, 1e-08, 0.01, 10
VMEM_LIMIT = 56 * 1024 * 1024
TILE_BYTES = 2 * 1024 * 1024
MM_TILE_PREFS = ((1024, 1024), (1024, 512), (512, 1024), (512, 512), (256, 512), (256, 256), (128, 256), (128, 128))
MM_VMEM_BUDGET = 36 * 1024 * 1024
TQ = 512
LOG2E = 1.4426950408889634

NN = (((1,), (0,)), ((), ()))
NT = (((1,), (1,)), ((), ()))
TNDIMS = (((0,), (0,)), ((), ()))
HBM_SPEC = pl.BlockSpec(memory_space=pltpu.HBM)
VMEM_SPEC = pl.BlockSpec(memory_space=pltpu.VMEM)


def _cparams(n_axes):
    return pltpu.CompilerParams(dimension_semantics=("arbitrary",) * n_axes, vmem_limit_bytes=VMEM_LIMIT)


def _pick(dim, pref):
    if dim <= pref:
        return dim
    for t in range(pref - pref % LANES, 0, -LANES):
        if t > 0 and dim % t == 0:
            return t
    for t in range(pref, 0, -1):
        if dim % t == 0:
            return t
    return dim


def _rows(r, c, itemsize=4):
    want = max(8, TILE_BYTES // (itemsize * max(c, 1)))
    if r <= want:
        return r
    for t in range(want - want % 8, 0, -8):
        if t > 0 and r % t == 0:
            return t
    return r


def _pos():
    return lax.axis_index("x"), lax.axis_index("y"), lax.axis_index("c")


def _ag_small(name, v):
    m_per, n = v.shape

    def body(x_ref, out_ref, send_sems, recv_sems, local_sem):
        x, y, c = _pos()
        me, sibling = (x, y, c), (x, y, 1 - c)
        chips = [(1 - x, y), (x, 1 - y), (1 - x, 1 - y)]

        def rows(px, py, pc):
            return out_ref.at[pl.ds((4 * px + 2 * py + pc) * m_per, m_per), :]

        def copy(k, block, to, src=None):
            return pltpu.make_async_remote_copy(
                src_ref=rows(*block) if src is None else src, dst_ref=rows(*block),
                send_sem=send_sems.at[k], recv_sem=recv_sems.at[k], device_id=to, device_id_type=MESH)

        mine = pltpu.make_async_copy(x_ref, rows(*me), local_sem)
        mine.start()
        first = [copy(0, me, sibling, src=x_ref)]
        first += [copy(1 + j, me, (*chip, c), src=x_ref) for j, chip in enumerate(chips)]
        for cp in first:
            cp.start()
        passed = [copy(4 + j, (*chip, c), sibling) for j, chip in enumerate(chips)]
        for j, chip in enumerate(chips):
            copy(1 + j, (*chip, c), me).wait_recv()
            passed[j].start()
        copy(0, sibling, me).wait_recv()
        for j, chip in enumerate(chips):
            copy(4 + j, (*chip, 1 - c), me).wait_recv()
        for cp in first + passed:
            cp.wait_send()
        mine.wait()

    return pl.pallas_call(
        body, name=name,
        out_shape=jax.ShapeDtypeStruct((N_DEV * m_per, n), v.dtype),
        in_specs=[VMEM_SPEC], out_specs=VMEM_SPEC,
        scratch_shapes=[pltpu.SemaphoreType.DMA((7,)), pltpu.SemaphoreType.DMA((7,)), pltpu.SemaphoreType.DMA],
    )(v)


def _ag_big(name, arrs):
    n = len(arrs)

    def body(*refs):
        ins, outs = refs[:n], refs[n:2 * n]
        send_sems, recv_sems, local_sems = refs[2 * n:]
        x, y, c = _pos()
        me, sibling = (x, y, c), (x, y, 1 - c)
        chips = [(1 - x, y), (x, 1 - y), (1 - x, 1 - y)]

        def copy(a, k, block, to, src=None):
            dst = outs[a].at[4 * block[0] + 2 * block[1] + block[2]]
            return pltpu.make_async_remote_copy(
                src_ref=dst if src is None else src, dst_ref=dst,
                send_sem=send_sems.at[7 * a + k], recv_sem=recv_sems.at[7 * a + k],
                device_id=to, device_id_type=MESH)

        mine = [pltpu.make_async_copy(ins[a], outs[a].at[4 * x + 2 * y + c], local_sems.at[a]) for a in range(n)]
        for cp in mine:
            cp.start()
        first = []
        for a in range(n):
            first.append(copy(a, 0, me, sibling, src=ins[a]))
            first += [copy(a, 1 + j, me, (*chip, c), src=ins[a]) for j, chip in enumerate(chips)]
        for cp in first:
            cp.start()
        passed = []
        for j, chip in enumerate(chips):
            for a in range(n):
                copy(a, 1 + j, (*chip, c), me).wait_recv()
                cp = copy(a, 4 + j, (*chip, c), sibling)
                cp.start()
                passed.append(cp)
        for a in range(n):
            copy(a, 0, sibling, me).wait_recv()
            for j, chip in enumerate(chips):
                copy(a, 4 + j, (*chip, 1 - c), me).wait_recv()
        for cp in first + passed:
            cp.wait_send()
        for cp in mine:
            cp.wait()

    return pl.pallas_call(
        body, name=name,
        out_shape=[jax.ShapeDtypeStruct((N_DEV,) + a.shape, a.dtype) for a in arrs],
        in_specs=[HBM_SPEC] * n, out_specs=[HBM_SPEC] * n,
        scratch_shapes=[pltpu.SemaphoreType.DMA((7 * n,)), pltpu.SemaphoreType.DMA((7 * n,)),
                        pltpu.SemaphoreType.DMA((n,))],
    )(*arrs)


SEM_SPEC = pl.BlockSpec(memory_space=pltpu.SEMAPHORE)
ANY_SPEC = pl.BlockSpec(memory_space=pl.ANY)
EFFECT = pltpu.SideEffectType.DATAFLOW_SIDE_EFFECTING


def _hbm(a):
    return pltpu.with_memory_space_constraint(a, pltpu.HBM)


def _cast_own_block(name, w, layer, dev_idx):
    _, r, cdim = w.shape
    tr = _rows(r, cdim, 4)

    def body(d_ref, w_ref, o_ref):
        o_ref[...] = w_ref[...].astype(BF)

    return pl.pallas_call(
        body, name=name,
        out_shape=jax.ShapeDtypeStruct((N_DEV, r, cdim), BF),
        grid_spec=pltpu.PrefetchScalarGridSpec(
            num_scalar_prefetch=1, grid=(r // tr,),
            in_specs=[pl.BlockSpec((None, tr, cdim), lambda i, d_ref: (layer, i, 0))],
            out_specs=pl.BlockSpec((None, tr, cdim), lambda i, d_ref: (d_ref[0], i, 0))),
        compiler_params=_cparams(1),
    )(dev_idx, w)


def _ag_start(name, lands, groups, after):
    n = len(lands)
    ng = len(groups)
    n_after = len(after)

    def body(*refs):
        lnd = refs[:n]
        sems = refs[n + n_after:n + n_after + 2 * ng]
        token = refs[-1]
        x, y, c = _pos()
        targets = [(x, y, 1 - c), (1 - x, y, c), (x, 1 - y, c), (1 - x, 1 - y, c)]
        for gi, members in enumerate(groups):
            for pos, a in enumerate(members):
                own = lnd[a].at[4 * x + 2 * y + c]
                for k, to in enumerate(targets):
                    pltpu.make_async_remote_copy(
                        src_ref=own, dst_ref=own,
                        send_sem=sems[2 * gi].at[4 * pos + k], recv_sem=sems[2 * gi + 1].at[4 * pos + k],
                        device_id=to, device_id_type=MESH).start()
        token[...] = jnp.zeros_like(token)

    sem_shapes = []
    for members in groups:
        sem_shapes += [pltpu.SemaphoreType.DMA((4 * len(members),))] * 2
    res = pl.pallas_call(
        body, name=name,
        out_shape=(*sem_shapes, *[pltpu.HBM(a.shape, a.dtype) for a in lands], jax.ShapeDtypeStruct((8, LANES), F32)),
        in_specs=[HBM_SPEC] * n + [ANY_SPEC] * n_after,
        out_specs=(*[SEM_SPEC] * (2 * ng), *[HBM_SPEC] * n, VMEM_SPEC),
        input_output_aliases={i: 2 * ng + i for i in range(n)},
        compiler_params=pltpu.CompilerParams(has_side_effects=EFFECT),
    )(*[_hbm(a) for a in lands], *after)
    sem_pairs = [(res[2 * g], res[2 * g + 1]) for g in range(ng)]
    return sem_pairs, list(res[2 * ng:2 * ng + n]), res[-1]


def _ag_forward(name, lands, send_sems, recv_sems, after):
    n = len(lands)

    def body(*refs):
        lnd = refs[:n]
        s_sems, r_sems = refs[n], refs[n + 1]
        fs_sems, fr_sems = refs[n + 3], refs[n + 4]
        token = refs[-1]
        x, y, c = _pos()
        sources = [(x, y, 1 - c), (1 - x, y, c), (x, 1 - y, c), (1 - x, 1 - y, c)]
        for a in range(n):
            own = lnd[a].at[4 * x + 2 * y + c]
            for k, src in enumerate(sources):
                blk = lnd[a].at[4 * src[0] + 2 * src[1] + src[2]]
                arrived = pltpu.make_async_remote_copy(
                    src_ref=own, dst_ref=blk, send_sem=s_sems.at[4 * a + k], recv_sem=r_sems.at[4 * a + k],
                    device_id=src, device_id_type=MESH)
                arrived.wait_send()
                arrived.wait_recv()
                if k > 0:
                    pltpu.make_async_remote_copy(
                        src_ref=blk, dst_ref=blk, send_sem=fs_sems.at[3 * a + k - 1], recv_sem=fr_sems.at[3 * a + k - 1],
                        device_id=(x, y, 1 - c), device_id_type=MESH).start()
        token[...] = jnp.zeros_like(token)

    res = pl.pallas_call(
        body, name=name,
        out_shape=(pltpu.SemaphoreType.DMA((3 * n,)), pltpu.SemaphoreType.DMA((3 * n,)),
                   *[pltpu.HBM(a.shape, a.dtype) for a in lands], jax.ShapeDtypeStruct((8, LANES), F32)),
        in_specs=[HBM_SPEC] * n + [SEM_SPEC, SEM_SPEC, ANY_SPEC],
        out_specs=(SEM_SPEC, SEM_SPEC, *[HBM_SPEC] * n, VMEM_SPEC),
        input_output_aliases={i: 2 + i for i in range(n)},
        compiler_params=pltpu.CompilerParams(has_side_effects=EFFECT),
    )(*lands, send_sems, recv_sems, after)
    return res[0], res[1], list(res[2:2 + n]), res[-1]


def _ag_finish(name, lands, fs_sems, fr_sems, after):
    n = len(lands)

    def body(*refs):
        lnd = refs[:n]
        fs, fr = refs[n], refs[n + 1]
        x, y, c = _pos()
        chips = [(1 - x, y), (x, 1 - y), (1 - x, 1 - y)]
        for a in range(n):
            for j, chip in enumerate(chips):
                mine = lnd[a].at[4 * chip[0] + 2 * chip[1] + c]
                theirs = lnd[a].at[4 * chip[0] + 2 * chip[1] + 1 - c]
                cp = pltpu.make_async_remote_copy(
                    src_ref=mine, dst_ref=theirs, send_sem=fs.at[3 * a + j], recv_sem=fr.at[3 * a + j],
                    device_id=(x, y, 1 - c), device_id_type=MESH)
                cp.wait_send()
                cp.wait_recv()

    res = pl.pallas_call(
        body, name=name,
        out_shape=tuple(pltpu.HBM(a.shape, a.dtype) for a in lands),
        in_specs=[HBM_SPEC] * n + [SEM_SPEC, SEM_SPEC, ANY_SPEC],
        out_specs=tuple([HBM_SPEC] * n),
        input_output_aliases={i: i for i in range(n)},
        compiler_params=pltpu.CompilerParams(has_side_effects=EFFECT),
    )(*lands, fs_sems, fr_sems, after)
    return list(res)


def _sibling_routes(x, y, c):
    return [(2 * j + (1 - c), (x, y, 1 - c)) for j in range(4)]


def _chip_routes(x, y, c):
    return [(2 * cx + cy, (cx, cy, c)) for cx, cy in [(1 - x, y), (x, 1 - y), (1 - x, 1 - y)]]


def _push_start(name, arrs, routes):
    n = len(arrs)
    nr = len(routes(0, 0, 0))

    def body(*refs):
        ins, lnd = refs[:n], refs[n:2 * n]
        s_sems, r_sems = refs[2 * n], refs[2 * n + 1]
        token = refs[-1]
        for a in range(n):
            for k, (blk, to) in enumerate(routes(*_pos())):
                pltpu.make_async_remote_copy(
                    src_ref=ins[a].at[blk], dst_ref=lnd[a].at[k],
                    send_sem=s_sems.at[nr * a + k], recv_sem=r_sems.at[nr * a + k],
                    device_id=to, device_id_type=MESH).start()
        token[...] = jnp.zeros_like(token)

    lands = [lax.empty((nr,) + a.shape[1:], a.dtype) for a in arrs]
    res = pl.pallas_call(
        body, name=name,
        out_shape=(pltpu.SemaphoreType.DMA((nr * n,)), pltpu.SemaphoreType.DMA((nr * n,)),
                   *[pltpu.HBM(a.shape, a.dtype) for a in arrs], *[pltpu.HBM(a.shape, a.dtype) for a in lands],
                   jax.ShapeDtypeStruct((8, LANES), F32)),
        in_specs=[HBM_SPEC] * (2 * n),
        out_specs=(SEM_SPEC, SEM_SPEC, *[HBM_SPEC] * (2 * n), VMEM_SPEC),
        input_output_aliases={i: 2 + i for i in range(2 * n)},
        compiler_params=pltpu.CompilerParams(has_side_effects=EFFECT),
    )(*[_hbm(a) for a in arrs], *[_hbm(a) for a in lands])
    return res[0], res[1], list(res[2:2 + n]), list(res[2 + n:2 + 2 * n]), res[-1]


def _push_wait(name, arrs, lands, send_sems, recv_sems, routes, after):
    n = len(arrs)
    nr = len(routes(0, 0, 0))

    def body(*refs):
        ins, lnd = refs[:n], refs[n:2 * n]
        s_sems, r_sems = refs[2 * n], refs[2 * n + 1]
        for a in range(n):
            for k, (blk, to) in enumerate(routes(*_pos())):
                cp = pltpu.make_async_remote_copy(
                    src_ref=ins[a].at[blk], dst_ref=lnd[a].at[k],
                    send_sem=s_sems.at[nr * a + k], recv_sem=r_sems.at[nr * a + k],
                    device_id=to, device_id_type=MESH)
                cp.wait_send()
                cp.wait_recv()

    res = pl.pallas_call(
        body, name=name,
        out_shape=tuple(pltpu.HBM(a.shape, a.dtype) for a in list(arrs) + list(lands)),
        in_specs=[HBM_SPEC] * (2 * n) + [SEM_SPEC, SEM_SPEC, ANY_SPEC],
        out_specs=tuple([HBM_SPEC] * (2 * n)),
        input_output_aliases={i: i for i in range(2 * n)},
        compiler_params=pltpu.CompilerParams(has_side_effects=EFFECT),
    )(*arrs, *lands, send_sems, recv_sems, after)
    return list(res[:n]), list(res[n:])


def _pair_add(name, g, r1, c_idx):
    _, r, cdim = g.shape
    tr = _rows(r, cdim, 2)

    def body(c_ref, g_ref, r_ref, o_ref):
        o_ref[...] = (g_ref[...].astype(F32) + r_ref[...].astype(F32)).astype(o_ref.dtype)

    return pl.pallas_call(
        body, name=name,
        out_shape=jax.ShapeDtypeStruct((4, r, cdim), BF),
        grid_spec=pltpu.PrefetchScalarGridSpec(
            num_scalar_prefetch=1, grid=(4, r // tr),
            in_specs=[pl.BlockSpec((None, tr, cdim), lambda j, i, c_ref: (2 * j + c_ref[0], i, 0)),
                      pl.BlockSpec((None, tr, cdim), lambda j, i, c_ref: (j, i, 0))],
            out_specs=pl.BlockSpec((None, tr, cdim), lambda j, i, c_ref: (j, i, 0))),
        compiler_params=_cparams(2),
    )(c_idx, g, r1)


def _chip_sum(name, p, r2, chip_idx):
    _, r, cdim = p.shape
    tr = _rows(r, cdim, 4)

    def body(s_ref, p_ref, a_ref, b_ref, c_ref, o_ref):
        o_ref[...] = ((p_ref[...].astype(F32) + a_ref[...].astype(F32)) + b_ref[...].astype(F32)) + c_ref[...].astype(F32)

    def other(k):
        return pl.BlockSpec((None, tr, cdim), lambda i, s_ref: (k, i, 0))

    return pl.pallas_call(
        body, name=name,
        out_shape=jax.ShapeDtypeStruct((r, cdim), F32),
        grid_spec=pltpu.PrefetchScalarGridSpec(
            num_scalar_prefetch=1, grid=(r // tr,),
            in_specs=[pl.BlockSpec((None, tr, cdim), lambda i, s_ref: (s_ref[0], i, 0)), other(0), other(1), other(2)],
            out_specs=pl.BlockSpec((tr, cdim), lambda i, s_ref: (i, 0))),
        compiler_params=_cparams(1),
    )(chip_idx, p, r2, r2, r2)


def _adam_math(w, g, m, v):
    m = B1 * m + (1.0 - B1) * g
    v = B2 * v + (1.0 - B2) * (g * g)
    m_hat = m / (1.0 - B1 ** STEP)
    v_hat = v / (1.0 - B2 ** STEP)
    delta = -LR * (m_hat / (jnp.sqrt(v_hat) + ADAM_EPS) + WD * w)
    return delta, m, v


def _adam(name, w, g, m, v):
    shape = w.shape
    cdim = shape[-1]
    r = w.size // cdim
    flat = [a.reshape(r, cdim) for a in (w, g, m, v)]
    tr = _rows(r, cdim, 4)

    def body(w_ref, g_ref, m_ref, v_ref, d_ref, mo_ref, vo_ref):
        d, mn, vn = _adam_math(w_ref[...], g_ref[...], m_ref[...], v_ref[...])
        d_ref[...] = d
        mo_ref[...] = mn
        vo_ref[...] = vn

    spec = pl.BlockSpec((tr, cdim), lambda i: (i, 0))
    outs = pl.pallas_call(
        body, name=name,
        out_shape=[jax.ShapeDtypeStruct((r, cdim), F32)] * 3,
        grid=(r // tr,), in_specs=[spec] * 4, out_specs=[spec] * 3,
        compiler_params=_cparams(1),
    )(*flat)
    return [o.reshape(shape) for o in outs]


def _adam_sum(name, p, r2, chip_idx, w, m, v, layer, prev=None):
    nl, r, cdim = w.shape
    tr = _rows(r, cdim, 4)

    def body(s_ref, p_ref, a_ref, b_ref, c_ref, w_ref, m_ref, v_ref, *rest):
        g_ref, d_ref, mo_ref, vo_ref = rest[-4:]
        g = ((p_ref[...].astype(F32) + a_ref[...].astype(F32)) + b_ref[...].astype(F32)) + c_ref[...].astype(F32)
        dl, mn, vn = _adam_math(w_ref[...], g, m_ref[...], v_ref[...])
        g_ref[...] = g
        d_ref[...] = dl
        mo_ref[...] = mn
        vo_ref[...] = vn

    def slot(k):
        return pl.BlockSpec((None, tr, cdim), lambda i, s_ref: (k, i, 0))

    in_specs = [pl.BlockSpec((None, tr, cdim), lambda i, s_ref: (s_ref[0], i, 0)), slot(0), slot(1), slot(2)] + [slot(layer)] * 3
    operands = [chip_idx, p, r2, r2, r2, w, m, v]
    aliases = {}
    if prev is not None:
        in_specs += [ANY_SPEC] * 4
        aliases = {len(operands) + k: k for k in range(4)}
        operands += list(prev)
    return pl.pallas_call(
        body, name=name,
        out_shape=[jax.ShapeDtypeStruct((nl, r, cdim), F32)] * 4,
        grid_spec=pltpu.PrefetchScalarGridSpec(
            num_scalar_prefetch=1, grid=(r // tr,), in_specs=in_specs, out_specs=[slot(layer)] * 4),
        input_output_aliases=aliases,
        compiler_params=_cparams(1),
    )(*operands)


def _mod_matvec(name, cond_all, w_loc, layer):
    _, d, n_loc = w_loc.shape
    tn = _pick(n_loc, 512)

    def body(c_ref, w_ref, o_ref):
        o_ref[...] = jnp.dot(c_ref[...].astype(BF), w_ref[...].astype(BF), preferred_element_type=F32)

    return pl.pallas_call(
        body, name=name,
        out_shape=jax.ShapeDtypeStruct((N_DEV, n_loc), F32),
        grid=(n_loc // tn,),
        in_specs=[pl.BlockSpec((N_DEV, d), lambda j: (0, 0)), pl.BlockSpec((None, d, tn), lambda j: (layer, 0, j))],
        out_specs=pl.BlockSpec((N_DEV, tn), lambda j: (0, j)),
        compiler_params=_cparams(1),
    )(cond_all, w_loc)


def _mod_grad_adam(name, cond_t, dmod, w, m, v, layer, prev=None):
    nl, d, n_loc = w.shape
    tr = _rows(d, n_loc, 4)

    def body(ct_ref, dm_ref, w_ref, m_ref, v_ref, *rest):
        g_ref, d_ref, mo_ref, vo_ref = rest[-4:]
        g = ct_ref[:, 0:1] * dm_ref[0:1, :]
        for b in range(1, N_DEV):
            g = g + ct_ref[:, b:b + 1] * dm_ref[b:b + 1, :]
        dl, mn, vn = _adam_math(w_ref[...], g, m_ref[...], v_ref[...])
        g_ref[...] = g
        d_ref[...] = dl
        mo_ref[...] = mn
        vo_ref[...] = vn

    spec = pl.BlockSpec((None, tr, n_loc), lambda i: (layer, i, 0))
    in_specs = [pl.BlockSpec((tr, N_DEV), lambda i: (i, 0)), pl.BlockSpec((N_DEV, n_loc), lambda i: (0, 0)), spec, spec, spec]
    operands = [cond_t, dmod, w, m, v]
    aliases = {}
    if prev is not None:
        in_specs += [ANY_SPEC] * 4
        aliases = {len(operands) + k: k for k in range(4)}
        operands += list(prev)
    return pl.pallas_call(
        body, name=name,
        out_shape=[jax.ShapeDtypeStruct((nl, d, n_loc), F32)] * 4,
        grid=(d // tr,), in_specs=in_specs, out_specs=[spec] * 4,
        input_output_aliases=aliases,
        compiler_params=_cparams(1),
    )(*operands)


def _sum_devices(name, gathered):
    _, m, n = gathered.shape

    def body(g_ref, o_ref):
        acc = g_ref[0]
        for d in range(1, N_DEV):
            acc = acc + g_ref[d]
        o_ref[...] = acc

    return pl.pallas_call(
        body, name=name, out_shape=jax.ShapeDtypeStruct((m, n), F32),
        in_specs=[VMEM_SPEC], out_specs=VMEM_SPEC,
    )(gathered)


def _silu(name, c_pad):
    def body(c_ref, o_ref):
        c = c_ref[...]
        o_ref[...] = c * (1.0 / (1.0 + jnp.exp(-c)))

    return pl.pallas_call(body, name=name, out_shape=jax.ShapeDtypeStruct(c_pad.shape, F32),
                          in_specs=[VMEM_SPEC], out_specs=VMEM_SPEC)(c_pad)


def _mm_tiles(m, n_cap, k, bytes_per_out_elem):
    for pm, pn in MM_TILE_PREFS:
        tm, tn = _pick(m, pm), _pick(n_cap, pn)
        need = 2 * (2 * tm * k + 2 * k * tn + bytes_per_out_elem * tm * tn) + 4 * tm * tn
        if need <= MM_VMEM_BUDGET:
            return tm, tn
    return _pick(m, 8), _pick(n_cap, LANES)


def _mm(name, a, b, grid, a_spec, b_spec, outs, dot, extras=(), epilogue=None, token=None):
    n_ex, n_out = len(extras), len(outs)
    n_in = 2 + n_ex + (0 if token is None else 1)

    def body(*refs):
        ex_refs = refs[2:2 + n_ex]
        out_refs = refs[n_in:n_in + n_out]
        acc = dot(refs[0], refs[1])
        vals = (acc,) if epilogue is None else epilogue(acc, *[e[...] for e in ex_refs])
        for o_ref, val in zip(out_refs, vals):
            o_ref[...] = val.astype(o_ref.dtype)

    ins = [a, b] + [e for e, _ in extras]
    in_specs = [a_spec, b_spec] + [s for _, s in extras]
    if token is not None:
        ins.append(token)
        in_specs.append(pl.BlockSpec(token.shape, lambda i, j: (0, 0)))
    return pl.pallas_call(
        body, name=name,
        out_shape=[jax.ShapeDtypeStruct(s, d) for s, d, _ in outs],
        grid=grid, in_specs=in_specs, out_specs=[s for _, _, s in outs],
        compiler_params=_cparams(2),
    )(*ins)


def _dot(dims):
    return lambda a_ref, b_ref: lax.dot_general(a_ref[...], b_ref[...], dims, preferred_element_type=F32)


def _w_cols(w):
    return w.shape[1] if w.ndim == 2 else w.shape[0] * w.shape[2]


def _w_rows(w):
    return w.shape[0] if w.ndim == 2 else w.shape[1]


def _out_bytes(out_dtypes, extras):
    return sum(jnp.dtype(d).itemsize for d in out_dtypes) + sum(e.dtype.itemsize for e in extras)


def _linear(name, a, w, out_dtypes=(F32,), epilogue=None, extras=(), row_extras=(), tn_multiple=1, token=None):
    t, kdim = a.shape
    n = _w_cols(w)
    tm, tn = _mm_tiles(t, w.shape[-1], kdim, _out_bytes(out_dtypes, extras))
    assert tn % tn_multiple == 0 or tn == n, (tn, tn_multiple)
    if w.ndim == 2:
        w_spec = pl.BlockSpec((kdim, tn), lambda i, j: (0, j))
    else:
        per = w.shape[2] // tn
        w_spec = pl.BlockSpec((None, kdim, tn), lambda i, j: (j // per, 0, j % per))
    o_spec = pl.BlockSpec((tm, tn), lambda i, j: (i, j))
    return _mm(name, a, w, (t // tm, n // tn), pl.BlockSpec((tm, kdim), lambda i, j: (i, 0)), w_spec,
               [((t, n), dt, o_spec) for dt in out_dtypes], _dot(NN),
               extras=[(e, o_spec) for e in extras] + [(e, pl.BlockSpec((tm, e.shape[1]), lambda i, j: (i, 0))) for e in row_extras],
               epilogue=epilogue, token=token)


def _linear_t(name, dy, w, out_dtypes=(F32,), epilogue=None, extras=(), token=None):
    t, n = dy.shape
    kdim = _w_rows(w)
    tm, tn = _mm_tiles(t, kdim, n, _out_bytes(out_dtypes, extras))
    o_spec = pl.BlockSpec((tm, tn), lambda i, j: (i, j))
    if w.ndim == 2:
        w_spec = pl.BlockSpec((tn, n), lambda i, j: (j, 0))
        dot = _dot(NT)
    else:
        n_loc = w.shape[2]
        w_spec = pl.BlockSpec((N_DEV, tn, n_loc), lambda i, j: (0, j, 0))

        def dot(a_ref, w_ref):
            acc = lax.dot_general(a_ref[:, 0:n_loc], w_ref[0], NT, preferred_element_type=F32)
            for dev in range(1, N_DEV):
                acc = acc + lax.dot_general(a_ref[:, dev * n_loc:(dev + 1) * n_loc], w_ref[dev], NT, preferred_element_type=F32)
            return acc

    return _mm(name, dy, w, (t // tm, kdim // tn), pl.BlockSpec((tm, n), lambda i, j: (i, 0)), w_spec,
               [((t, kdim), dt, o_spec) for dt in out_dtypes], dot,
               extras=[(e, o_spec) for e in extras], epilogue=epilogue, token=token)


def _wgrad(name, a, dy, like):
    t, kdim = a.shape
    n = dy.shape[1]
    tm, tn = _mm_tiles(kdim, like.shape[-1], t, 2)
    if like.ndim == 2:
        o_spec = pl.BlockSpec((tm, tn), lambda i, j: (i, j))
    else:
        per = like.shape[2] // tn
        o_spec = pl.BlockSpec((None, tm, tn), lambda i, j: (j // per, i, j % per))
    return _mm(name, a, dy, (kdim // tm, n // tn), pl.BlockSpec((t, tm), lambda i, j: (0, i)),
               pl.BlockSpec((t, tn), lambda i, j: (0, j)), [(like.shape, BF, o_spec)], _dot(TNDIMS))[0]


def _rstd(x):
    return lax.rsqrt(jnp.mean(x * x, axis=-1, keepdims=True) + NORM_EPS)


def _colsum(x):
    return jnp.sum(x, axis=0, keepdims=True)


def _norm_fwd(name, x, y, vec, target, has_post, has_pre, has_loss):
    t, d = x.shape
    tm = _rows(t, d, 4)
    ins, in_specs = [x], [pl.BlockSpec((tm, d), lambda i: (i, 0))]
    row_spec = pl.BlockSpec((tm, d), lambda i: (i, 0))
    if has_post:
        ins.append(y)
        in_specs.append(row_spec)
    ins.append(vec)
    in_specs.append(pl.BlockSpec((8, d), lambda i: (0, 0)))
    if has_loss:
        ins.append(target)
        in_specs.append(row_spec)
    out_shape, out_specs = [], []
    if has_post and not has_loss:
        out_shape.append(jax.ShapeDtypeStruct((t, d), F32))
        out_specs.append(row_spec)
    if has_pre:
        out_shape.append(jax.ShapeDtypeStruct((t, d), BF))
        out_specs.append(row_spec)
    if has_loss:
        out_shape += [jax.ShapeDtypeStruct((t, d), F32), jax.ShapeDtypeStruct((8, d), F32)]
        out_specs += [row_spec, pl.BlockSpec((8, d), lambda i: (0, 0))]

    def body(*refs):
        it = iter(refs)
        x_ref = next(it)
        y_ref = next(it) if has_post else None
        vec_ref = next(it)
        tgt_ref = next(it) if has_loss else None
        xv = x_ref[...]
        if has_post:
            yv = y_ref[...]
            xv = xv + vec_ref[1:2, :] * ((yv * _rstd(yv)) * vec_ref[0:1, :])
            if not has_loss:
                next(it)[...] = xv
        if has_pre:
            hv = ((xv * _rstd(xv)) * vec_ref[2:3, :]) * (1.0 + vec_ref[3:4, :]) + vec_ref[4:5, :]
            next(it)[...] = hv.astype(BF)
        if has_loss:
            e = xv - tgt_ref[...]
            next(it)[...] = e * (1.0 / d)
            acc_ref = next(it)
            i = pl.program_id(0)

            @pl.when(i == 0)
            def _():
                acc_ref[...] = jnp.zeros_like(acc_ref)

            acc_ref[0:1, :] += _colsum(e * e)

    return pl.pallas_call(
        body, name=name, out_shape=out_shape, grid=(t // tm,), in_specs=in_specs, out_specs=out_specs,
        compiler_params=_cparams(1),
    )(*ins)


def _norm_bwd(name, dx_in, dh, x, y, vec, has_pre, has_post):
    t, d = dx_in.shape
    tm = _rows(t, d, 4)
    row_spec = pl.BlockSpec((tm, d), lambda i: (i, 0))
    vec_spec = pl.BlockSpec((8, d), lambda i: (0, 0))
    ins, in_specs = [dx_in], [row_spec]
    if has_pre:
        ins += [dh, x]
        in_specs += [row_spec, row_spec]
    if has_post:
        ins.append(y)
        in_specs.append(row_spec)
    ins.append(vec)
    in_specs.append(vec_spec)
    out_shape, out_specs = [], []
    if has_pre:
        out_shape.append(jax.ShapeDtypeStruct((t, d), F32))
        out_specs.append(row_spec)
    if has_post:
        out_shape.append(jax.ShapeDtypeStruct((t, d), BF))
        out_specs.append(row_spec)
    out_shape.append(jax.ShapeDtypeStruct((8, d), F32))
    out_specs.append(vec_spec)

    def body(*refs):
        it = iter(refs)
        dx = next(it)[...]
        dh_ref = next(it) if has_pre else None
        x_ref = next(it) if has_pre else None
        y_ref = next(it) if has_post else None
        vec_ref = next(it)
        dx_ref = next(it) if has_pre else None
        dy_ref = next(it) if has_post else None
        sums_ref = next(it)
        i = pl.program_id(0)

        @pl.when(i == 0)
        def _():
            sums_ref[...] = jnp.zeros_like(sums_ref)

        if has_pre:
            dhv, xv = dh_ref[...], x_ref[...]
            rs = _rstd(xv)
            xhat = xv * rs
            na = vec_ref[2:3, :]
            sums_ref[0:1, :] += _colsum(dhv)
            sums_ref[1:2, :] += _colsum(dhv * (xhat * na))
            tt = dhv * (1.0 + vec_ref[3:4, :])
            sums_ref[2:3, :] += _colsum(tt * xhat)
            u = tt * na
            dx = dx + rs * (u - xhat * jnp.mean(u * xhat, axis=-1, keepdims=True))
            dx_ref[...] = dx
        if has_post:
            yv = y_ref[...]
            rs = _rstd(yv)
            yhat = yv * rs
            nb = vec_ref[0:1, :]
            sums_ref[3:4, :] += _colsum(dx * (yhat * nb))
            tt = dx * vec_ref[1:2, :]
            sums_ref[4:5, :] += _colsum(tt * yhat)
            u = tt * nb
            dy_ref[...] = (rs * (u - yhat * jnp.mean(u * yhat, axis=-1, keepdims=True))).astype(BF)

    return pl.pallas_call(
        body, name=name, out_shape=out_shape, grid=(t // tm,), in_specs=in_specs, out_specs=out_specs,
        compiler_params=_cparams(1),
    )(*ins)


def _rope128(tv, cs, s1, s2):
    return tv * cs + pltpu.roll(tv, 96, 1) * s1 + pltpu.roll(tv, 32, 1) * s2


def _lat_post(name, lat, gq, gkv, cs, s1, s2, rq, rkv):
    t, w = lat.shape
    tm = _rows(t, w, 4)

    def body(lat_ref, gq_ref, gkv_ref, cs_ref, s1_ref, s2_ref, cq_ref, ckv_ref, kr_ref):
        lq = lat_ref[:, 0:rq]
        lkv = lat_ref[:, rq:rq + rkv]
        cq_ref[...] = ((lq * _rstd(lq)) * gq_ref[...]).astype(BF)
        ckv_ref[...] = ((lkv * _rstd(lkv)) * gkv_ref[...]).astype(BF)
        kr_ref[...] = _rope128(lat_ref[:, rq + rkv:rq + rkv + LANES], cs_ref[...], s1_ref[...], s2_ref[...]).astype(BF)

    def rows(c):
        return pl.BlockSpec((tm, c), lambda i: (i, 0))

    def vecs(c):
        return pl.BlockSpec((1, c), lambda i: (0, 0))

    return pl.pallas_call(
        body, name=name,
        out_shape=[jax.ShapeDtypeStruct((t, rq), BF), jax.ShapeDtypeStruct((t, rkv), BF), jax.ShapeDtypeStruct((t, LANES), BF)],
        grid=(t // tm,),
        in_specs=[rows(w), vecs(rq), vecs(rkv), rows(LANES), rows(LANES), rows(LANES)],
        out_specs=[rows(rq), rows(rkv), rows(LANES)],
        compiler_params=_cparams(1),
    )(lat, gq, gkv, cs, s1, s2)


def _lat_bwd(name, dcq, dckv, dkr, lat, gq, gkv, cs, s1, s2, rq, rkv, heads):
    t, w = lat.shape
    tm = _rows(t, max(w, heads * LANES), 4)
    assert rq == rkv

    def body(dcq_ref, dckv_ref, dkr_ref, lat_ref, gq_ref, gkv_ref, cs_ref, s1_ref, s2_ref, dlat_ref, sums_ref):
        i = pl.program_id(0)

        @pl.when(i == 0)
        def _():
            sums_ref[...] = jnp.zeros_like(sums_ref)

        def rms_bwd(dc, lv, g, row):
            rs = _rstd(lv)
            lhat = lv * rs
            sums_ref[row:row + 1, :] += _colsum(dc * lhat)
            u = dc * g
            return rs * (u - lhat * jnp.mean(u * lhat, axis=-1, keepdims=True))

        dlat_ref[:, 0:rq] = rms_bwd(dcq_ref[...], lat_ref[:, 0:rq], gq_ref[...], 0).astype(BF)
        dlat_ref[:, rq:rq + rkv] = rms_bwd(dckv_ref[...], lat_ref[:, rq:rq + rkv], gkv_ref[...], 1).astype(BF)
        dk = dkr_ref[:, 0:LANES]
        for h in range(1, heads):
            dk = dk + dkr_ref[:, h * LANES:(h + 1) * LANES]
        dlat_ref[:, rq + rkv:rq + rkv + LANES] = _rope128(dk, cs_ref[...], s1_ref[...], s2_ref[...]).astype(BF)

    def rows(c):
        return pl.BlockSpec((tm, c), lambda i: (i, 0))

    def vecs(c):
        return pl.BlockSpec((1, c), lambda i: (0, 0))

    return pl.pallas_call(
        body, name=name,
        out_shape=[jax.ShapeDtypeStruct((t, w), BF), jax.ShapeDtypeStruct((8, rq), F32)],
        grid=(t // tm,),
        in_specs=[rows(rq), rows(rkv), rows(heads * LANES), rows(w), vecs(rq), vecs(rkv), rows(LANES), rows(LANES), rows(LANES)],
        out_specs=[rows(w), pl.BlockSpec((8, rq), lambda i: (0, 0))],
        compiler_params=_cparams(1),
    )(dcq, dckv, dkr, lat, gq, gkv, cs, s1, s2)


def _causal_mask(s, row0):
    row = lax.broadcasted_iota(jnp.int32, s.shape, 0) + row0
    col = lax.broadcasted_iota(jnp.int32, s.shape, 1)
    return col <= row


def _flash_fwd(name, qb, kv, kr, heads, scale):
    t = qb.shape[0]
    tq = _pick(t, TQ)
    nq = t // tq
    c2 = scale * LOG2E

    def body(q_ref, kv_ref, kr_ref, o_ref, lse_ref):
        i = pl.program_id(1)
        q = q_ref[...]

        def block(j, carry, masked):
            m_prev, l_prev, acc = carry
            rows = pl.ds(pl.multiple_of(j * tq, tq), tq)
            k = jnp.concatenate([kv_ref[rows, 0:LANES], kr_ref[rows, :]], axis=-1)
            s = lax.dot_general(q, k, NT, preferred_element_type=F32) * c2
            if masked:
                s = jnp.where(_causal_mask(s, 0), s, -1e30)
            m_new = jnp.maximum(m_prev, jnp.max(s, axis=-1, keepdims=True))
            alpha = jnp.exp2(m_prev - m_new)
            p = jnp.exp2(s - m_new)
            l_new = alpha * l_prev + jnp.sum(p, axis=-1, keepdims=True)
            pv = jnp.dot(p.astype(BF), kv_ref[rows, LANES:2 * LANES], preferred_element_type=F32)
            return m_new, l_new, alpha * acc + pv

        init = (jnp.full((tq, 1), -1e30, F32), jnp.zeros((tq, 1), F32), jnp.zeros((tq, V_DIM), F32))
        carry = lax.fori_loop(0, i, lambda j, cr: block(j, cr, False), init)
        m_fin, l_fin, acc = block(i, carry, True)
        o_ref[...] = (acc / l_fin).astype(BF)
        lse_ref[...] = jnp.broadcast_to(m_fin + jnp.log2(l_fin), (tq, LANES))

    return pl.pallas_call(
        body, name=name,
        out_shape=[jax.ShapeDtypeStruct((t, heads * V_DIM), BF), jax.ShapeDtypeStruct((t, heads * LANES), F32)],
        grid=(heads, nq),
        in_specs=[pl.BlockSpec((tq, 2 * LANES), lambda h, i: (i, h)),
                  pl.BlockSpec((t, 2 * LANES), lambda h, i: (0, h)),
                  pl.BlockSpec((t, LANES), lambda h, i: (0, 0))],
        out_specs=[pl.BlockSpec((tq, V_DIM), lambda h, i: (i, h)), pl.BlockSpec((tq, LANES), lambda h, i: (i, h))],
        compiler_params=_cparams(2),
    )(qb, kv, kr)


def _flash_bwd(name, qb, kv, kr, o, lse, do, cs, s1, s2, heads, scale):
    t = qb.shape[0]
    tq = _pick(t, TQ)
    nq = t // tq

    c2 = scale * LOG2E

    def body(q_ref, kv_ref, kr_ref, o_ref, lse_ref, do_ref, cs_ref, s1_ref, s2_ref, dkv_ref, dkr_ref, dq_ref, dq_acc):
        j = pl.program_id(1)

        @pl.when(j == 0)
        def _():
            dq_acc[...] = jnp.zeros_like(dq_acc)

        k = jnp.concatenate([kv_ref[:, 0:LANES], kr_ref[...]], axis=-1)
        v = kv_ref[:, LANES:2 * LANES]

        def block(i, carry, masked):
            dk, dv = carry
            rows = pl.ds(pl.multiple_of(i * tq, tq), tq)
            q = q_ref[rows, :]
            dov = do_ref[rows, :]
            delta = jnp.sum(dov.astype(F32) * o_ref[rows, :].astype(F32), axis=-1, keepdims=True)
            s = lax.dot_general(q, k, NT, preferred_element_type=F32) * c2
            p = jnp.exp2(s - lse_ref[rows, 0:1])
            if masked:
                p = jnp.where(_causal_mask(s, 0), p, 0.0)
            dv = dv + lax.dot_general(p.astype(BF), dov, TNDIMS, preferred_element_type=F32)
            dp = lax.dot_general(dov, v, NT, preferred_element_type=F32)
            ds = (p * (dp - delta) * scale).astype(BF)
            dk = dk + lax.dot_general(ds, q, TNDIMS, preferred_element_type=F32)
            dq_acc[rows, :] += jnp.dot(ds, k, preferred_element_type=F32)
            return dk, dv

        carry = block(j, (jnp.zeros((tq, 2 * LANES), F32), jnp.zeros((tq, V_DIM), F32)), True)
        dk, dv = lax.fori_loop(j + 1, nq, lambda i, cr: block(i, cr, False), carry)
        dkv_ref[:, 0:LANES] = dk[:, 0:LANES].astype(BF)
        dkv_ref[:, LANES:2 * LANES] = dv.astype(BF)
        dkr_ref[...] = dk[:, LANES:2 * LANES]

        @pl.when(j == nq - 1)
        def _():
            dq_ref[:, 0:LANES] = dq_acc[:, 0:LANES].astype(BF)
            dq_ref[:, LANES:2 * LANES] = _rope128(dq_acc[:, LANES:2 * LANES], cs_ref[...], s1_ref[...], s2_ref[...]).astype(BF)

    def per_head(width):
        return pl.BlockSpec((t, width), lambda h, j: (0, h))

    table = pl.BlockSpec((t, LANES), lambda h, j: (0, 0))
    return pl.pallas_call(
        body, name=name,
        out_shape=[jax.ShapeDtypeStruct((t, heads * 2 * LANES), BF), jax.ShapeDtypeStruct((t, heads * LANES), F32),
                   jax.ShapeDtypeStruct((t, heads * 2 * LANES), BF)],
        grid=(heads, nq),
        in_specs=[per_head(2 * LANES), pl.BlockSpec((tq, 2 * LANES), lambda h, j: (j, h)),
                  pl.BlockSpec((tq, LANES), lambda h, j: (j, 0)), per_head(LANES), per_head(LANES), per_head(LANES),
                  table, table, table],
        out_specs=[pl.BlockSpec((tq, 2 * LANES), lambda h, j: (j, h)), pl.BlockSpec((tq, LANES), lambda h, j: (j, h)),
                   per_head(2 * LANES)],
        scratch_shapes=[pltpu.VMEM((t, 2 * LANES), F32)],
        compiler_params=_cparams(2),
    )(qb, kv, kr, o, lse, do, cs, s1, s2)


def _shift_down(z, n):
    row = lax.broadcasted_iota(jnp.int32, z.shape, 0)
    return jnp.where(row >= n, pltpu.roll(z, n, 0), 0.0)


def _shift_up(z, n):
    t = z.shape[0]
    row = lax.broadcasted_iota(jnp.int32, z.shape, 0)
    return jnp.where(row < t - n, pltpu.roll(z, t - n, 0), 0.0)


def _conv_fwd(name, proj, cw):
    t, d3 = proj.shape
    d = d3 // 3
    tn = _pick(d, LANES)
    nb = d // tn

    def body(b_ref, c_ref, u_ref, w_ref, o_ref):
        z = c_ref[...] * u_ref[...]
        zc = w_ref[0:1, :] * _shift_down(z, 2) + w_ref[1:2, :] * _shift_down(z, 1) + w_ref[2:3, :] * z
        o_ref[...] = (b_ref[...] * zc).astype(BF)

    def part(p):
        return pl.BlockSpec((t, tn), lambda j: (0, p * nb + j))

    return pl.pallas_call(
        body, name=name, out_shape=jax.ShapeDtypeStruct((t, d), BF), grid=(nb,),
        in_specs=[part(0), part(1), part(2), pl.BlockSpec((8, tn), lambda j: (0, j))],
        out_specs=pl.BlockSpec((t, tn), lambda j: (0, j)),
        compiler_params=_cparams(1),
    )(proj, proj, proj, cw)


def _conv_bwd(name, dbz, proj, cw):
    t, d3 = proj.shape
    d = d3 // 3
    tn = _pick(d, LANES)
    nb = d // tn

    def body(g_ref, b_ref, c_ref, u_ref, w_ref, o_ref, sums_ref):
        p = pl.program_id(1)
        w0, w1, w2 = w_ref[0:1, :], w_ref[1:2, :], w_ref[2:3, :]

        @pl.when(p == 0)
        def _():
            z = c_ref[...] * u_ref[...]
            z1, z2 = _shift_down(z, 1), _shift_down(z, 2)
            gv = g_ref[...]
            o_ref[...] = (gv * (w0 * z2 + w1 * z1 + w2 * z)).astype(BF)
            dzc = gv * b_ref[...]
            sums_ref[...] = jnp.zeros_like(sums_ref)
            sums_ref[0:1, :] = _colsum(dzc * z2)
            sums_ref[1:2, :] = _colsum(dzc * z1)
            sums_ref[2:3, :] = _colsum(dzc * z)

        @pl.when(p > 0)
        def _():
            dzc = g_ref[...] * b_ref[...]
            dz = w2 * dzc + w1 * _shift_up(dzc, 1) + w0 * _shift_up(dzc, 2)
            other = jnp.where(p == 1, u_ref[...], c_ref[...])
            o_ref[...] = (dz * other).astype(BF)

    def part(q):
        return pl.BlockSpec((t, tn), lambda j, p: (0, q * nb + j))

    return pl.pallas_call(
        body, name=name,
        out_shape=[jax.ShapeDtypeStruct((t, d3), BF), jax.ShapeDtypeStruct((8, d), F32)],
        grid=(nb, 3),
        in_specs=[pl.BlockSpec((t, tn), lambda j, p: (0, j)), part(0), part(1), part(2),
                  pl.BlockSpec((8, tn), lambda j, p: (0, j))],
        out_specs=[pl.BlockSpec((t, tn), lambda j, p: (0, p * nb + j)), pl.BlockSpec((8, tn), lambda j, p: (0, j))],
        compiler_params=_cparams(2),
    )(dbz, proj, proj, proj, cw)


def _pad_rows8(v):
    return jnp.pad(v, ((0, 8 - v.shape[0]), (0, 0)))


def kernel(x, c, positions, w_mod, b_mod, norm_g, mla_w_in, mla_g_q, mla_g_kv, mla_w_uq, mla_w_ukv, mla_w_o, conv_w_in, conv_w, conv_w_out, mlp_w_up, mlp_w_down, loss_target, m_w_mod, m_b_mod, m_norm_g, m_mla_w_in, m_mla_g_q, m_mla_g_kv, m_mla_w_uq, m_mla_w_ukv, m_mla_w_o, m_conv_w_in, m_conv_w, m_conv_w_out, m_mlp_w_up, m_mlp_w_down, v_w_mod, v_b_mod, v_norm_g, v_mla_w_in, v_mla_g_q, v_mla_g_kv, v_mla_w_uq, v_mla_w_ukv, v_mla_w_o, v_conv_w_in, v_conv_w, v_conv_w_out, v_mlp_w_up, v_mlp_w_down):
    t, d = x.shape[1], x.shape[2]
    depth = w_mod.shape[0]
    n_mod_loc = w_mod.shape[2]
    d_loc = norm_g.shape[2]
    rq = mla_g_q.shape[1]
    rkv = mla_g_kv.shape[1]
    heads = mla_w_uq.shape[2]
    lat_w = mla_w_in.shape[2]
    lat_pad = rq + rkv + LANES
    scale = (NOPE_DIM + ROPE_DIM) ** -0.5
    xi, yi, ci = _pos()
    dev = 4 * xi + 2 * yi + ci
    c_idx = jnp.reshape(ci, (1,)).astype(jnp.int32)
    chip_idx = jnp.reshape(2 * xi + yi, (1,)).astype(jnp.int32)
    x0 = x[0]
    target = loss_target[0]

    w_in_loc = jnp.pad(mla_w_in[0], ((0, 0), (0, lat_pad - lat_w))).astype(BF)
    uq = mla_w_uq[0]
    uq_loc = jnp.pad(uq, ((0, 0), (0, 0), (0, LANES - ROPE_DIM))).reshape(uq.shape[0], heads * 2 * LANES).astype(BF)
    ukv_loc = mla_w_ukv[0].reshape(mla_w_ukv.shape[1], heads * (NOPE_DIM + V_DIM)).astype(BF)
    def merged(g):
        return g.reshape(g.shape[0] * g.shape[1], g.shape[2])

    g_win, g_uq, g_ukv = [merged(g) for g in _ag_big("ag_mla", [w_in_loc, uq_loc, ukv_loc])]

    n_ng = depth * 4 * d_loc
    small = jnp.concatenate([c.reshape(-1), norm_g.reshape(-1), conv_w.reshape(-1)])
    small_n = small.shape[0]
    small_rows = -(-small_n // (8 * LANES)) * 8
    small = jnp.pad(small, (0, small_rows * LANES - small_n)).reshape(small_rows, LANES)
    small_all = _ag_small("ag_small", small).reshape(N_DEV, small_rows * LANES)
    c_all = small_all[:, :d]
    ng_all = small_all[:, d:d + n_ng].reshape(N_DEV, depth, 4, d_loc).transpose(1, 2, 0, 3).reshape(depth, 4, d)
    cw_all = small_all[:, d + n_ng:d + n_ng + 3 * d_loc].reshape(N_DEV, 3, d_loc).transpose(1, 0, 2).reshape(3, d)
    cw8 = _pad_rows8(cw_all)
    cond_all = _silu("silu", c_all)

    mod_parts = jnp.stack([_mod_matvec(f"mod_matvec{i}", cond_all, w_mod, i) for i in range(depth)])
    mod_rows = depth * N_DEV * n_mod_loc // LANES
    mod_all = _ag_small("ag_mod", mod_parts.reshape(mod_rows, LANES)).reshape(N_DEV, depth, N_DEV, n_mod_loc)
    mod_mine = lax.dynamic_index_in_dim(mod_all, dev, axis=2, keepdims=False)
    mod = mod_mine.transpose(1, 0, 2).reshape(depth, N_DEV * n_mod_loc) + b_mod
    mod = mod.reshape(depth, 6, d)

    dev_idx = jnp.reshape(dev, (1,)).astype(jnp.int32)
    later = [(mla_w_o, 0), (mlp_w_up, 0), (mlp_w_down, 0), (conv_w_in, 0), (conv_w_out, 0), (mlp_w_up, 1), (mlp_w_down, 1)]
    lands = [_cast_own_block(f"cast_own_block{a}", w, layer, dev_idx) for a, (w, layer) in enumerate(later)]
    ag_groups = [[0], [1, 2], [3, 4, 5, 6]]
    ag_sems, lands_thru, ag_token = _ag_start("ag_start", lands, ag_groups, [mod, g_ukv])

    def fwd_vec(nb, gate, na, sc, sh, token=None):
        zero = jnp.zeros((d,), F32)
        rows = [zero if r is None else r for r in (nb, gate, na, sc, sh)]
        vec = _pad_rows8(jnp.stack(rows))
        return vec if token is None else vec + token[0, 0]

    inv_freq = ROPE_THETA ** (-jnp.arange(0, ROPE_DIM, 2, dtype=F32) / ROPE_DIM)
    ang = positions[0].astype(F32)[:, None] * inv_freq
    cos, sin = jnp.cos(ang), jnp.sin(ang)
    zh = jnp.zeros_like(cos)
    zpad = jnp.zeros((t, LANES - ROPE_DIM), F32)
    rope_c = jnp.concatenate([cos, cos, zpad], axis=1)
    rope_s1 = jnp.concatenate([-sin, zh, zpad], axis=1)
    rope_s2 = jnp.concatenate([zh, sin, zpad], axis=1)

    (h0,) = _norm_fwd("pre0", x0, None, fwd_vec(None, None, ng_all[0, 0], mod[0, 1], mod[0, 0], ag_token), None, False, True, False)
    (lat,) = _linear("mla_lat", h0, g_win)
    cq, ckv, kr = _lat_post("mla_lat_post", lat, mla_g_q, mla_g_kv, rope_c, rope_s1, rope_s2, rq, rkv)
    def rope_odd_groups(acc, csv, s1v, s2v):
        groups = [acc[:, g * LANES:(g + 1) * LANES] for g in range(acc.shape[1] // LANES)]
        return (jnp.concatenate([grp if g % 2 == 0 else _rope128(grp, csv, s1v, s2v) for g, grp in enumerate(groups)], axis=1),)

    (qb,) = _linear("mla_q", cq, g_uq, out_dtypes=(BF,), epilogue=rope_odd_groups,
                    row_extras=(rope_c, rope_s1, rope_s2), tn_multiple=2 * LANES)
    (kvb,) = _linear("mla_kv", ckv, g_ukv, out_dtypes=(BF,))
    fs_o, fr_o, lands_o, _ = _ag_forward("ag_forward_o", lands_thru[0:1], *ag_sems[0], kvb)
    o, lse = _flash_fwd("mla_attn", qb, kvb, kr, heads, scale)
    (g_wo,) = _ag_finish("ag_finish_o", lands_o, fs_o, fr_o, o)
    g_wo = merged(g_wo)
    fs_a, fr_a, lands_a, _ = _ag_forward("ag_forward_a", lands_thru[1:3], *ag_sems[1], o)
    (y0,) = _linear("mla_out", o, g_wo)
    x1, h1 = _norm_fwd("postpre1", x0, y0, fwd_vec(ng_all[0, 1], mod[0, 2], ng_all[0, 2], mod[0, 4], mod[0, 3]), None, True, True, False)
    g_up0, g_dn0 = _ag_finish("ag_finish_a", lands_a, fs_a, fr_a, x1)
    g_dn0 = merged(g_dn0)

    def sq_relu(acc):
        a = jnp.maximum(acc, 0.0)
        return acc, a * a

    u1, act1 = _linear("mlp0_up", h1, g_up0, out_dtypes=(F32, BF), epilogue=sq_relu)
    (y1,) = _linear("mlp0_down", act1, g_dn0)
    fs_b, fr_b, lands_b, _ = _ag_forward("ag_forward_b", lands_thru[3:7], *ag_sems[2], y1)
    x2, h2 = _norm_fwd("postpre2", x1, y1, fwd_vec(ng_all[0, 3], mod[0, 5], ng_all[1, 0], mod[1, 1], mod[1, 0]), None, True, True, False)
    g_cin, g_cout, g_up1, g_dn1 = _ag_finish("ag_finish_b", lands_b, fs_b, fr_b, x2)
    g_cout, g_dn1 = merged(g_cout), merged(g_dn1)
    g_up, g_dn = [g_up0, g_up1], [g_dn0, g_dn1]

    (proj,) = _linear("conv_in", h2, g_cin)
    bz = _conv_fwd("conv_mix", proj, cw8)
    (y2,) = _linear("conv_out", bz, g_cout)
    x3, h3 = _norm_fwd("postpre3", x2, y2, fwd_vec(ng_all[1, 1], mod[1, 2], ng_all[1, 2], mod[1, 4], mod[1, 3]), None, True, True, False)

    u3, act3 = _linear("mlp1_up", h3, g_up[1], out_dtypes=(F32, BF), epilogue=sq_relu)
    (y3,) = _linear("mlp1_down", act3, g_dn[1])
    dx4, loss_cols = _norm_fwd("post_loss", x3, y3, fwd_vec(ng_all[1, 3], mod[1, 5], None, None, None), target, True, False, True)
    loss = lax.psum(0.5 * jnp.sum(loss_cols[0]) / d, ("x", "y", "c"))

    def bwd_vec(nb, gate, na, sc, token=None):
        zero = jnp.zeros((d,), F32)
        rows = [zero if r is None else r for r in (nb, gate, na, sc)]
        vec = _pad_rows8(jnp.stack(rows))
        return vec if token is None else vec + token[0, 0]

    def blocks(g, like):
        return g.reshape((N_DEV,) + like.shape)

    def rs_begin(tag, grads):
        return _push_start(f"rs_sibling_start_{tag}", grads, _sibling_routes)

    def rs_continue(tag, started, after):
        send_sems, recv_sems, grads, from_sibling, _ = started
        grads, from_sibling = _push_wait(f"rs_sibling_wait_{tag}", grads, from_sibling, send_sems, recv_sems, _sibling_routes, after)
        pair = [_pair_add(f"pair_add_{tag}{a}", grads[a], from_sibling[a], c_idx) for a in range(len(grads))]
        return _push_start(f"rs_start_{tag}", pair, _chip_routes)

    def mlp_bwd(tag, dy, h, u, act, w_up, w_dn, w_dn_shard):
        def relu_grad(acc, uv):
            return (acc * (2.0 * jnp.maximum(uv, 0.0)),)

        (du,) = _linear_t(f"{tag}_dact", dy, w_dn, out_dtypes=(BF,), epilogue=relu_grad, extras=(u,))
        dw_dn = _wgrad(f"{tag}_dw_down", act, dy, w_dn)
        dw_up = _wgrad(f"{tag}_dw_up", h, du, w_up)
        sib = rs_begin(tag, [dw_up, blocks(dw_dn, w_dn_shard)])
        (dh,) = _linear_t(f"{tag}_dh", du, w_up, token=sib[4])
        return dh, rs_continue(tag, sib, dh)

    dy3, sums4 = _norm_bwd("bwd_post3", dx4, None, None, y3, bwd_vec(ng_all[1, 3], mod[1, 5], None, None), False, True)
    dh3, rs_mlp1 = mlp_bwd("mlp1", dy3, h3, u3, act3, g_up[1], g_dn[1], mlp_w_down[1])
    dx3, dy2, sums3 = _norm_bwd("bwd_norm3", dx4, dh3, x3, y2, bwd_vec(ng_all[1, 1], mod[1, 2], ng_all[1, 2], mod[1, 4], rs_mlp1[4]), True, True)

    (dbz,) = _linear_t("conv_dbz", dy2, g_cout)
    dw_cout = _wgrad("conv_dw_out", bz, dy2, g_cout)
    dproj, conv_sums = _conv_bwd("conv_mix_bwd", dbz, proj, cw8)
    dw_cin = _wgrad("conv_dw_in", h2, dproj, g_cin)
    sib_conv = rs_begin("conv", [dw_cin, blocks(dw_cout, conv_w_out[0])])
    (dh2,) = _linear_t("conv_dh", dproj, g_cin, token=sib_conv[4])
    rs_conv = rs_continue("conv", sib_conv, dh2)
    dx2, dy1, sums2 = _norm_bwd("bwd_norm2", dx3, dh2, x2, y1, bwd_vec(ng_all[0, 3], mod[0, 5], ng_all[1, 0], mod[1, 1], rs_conv[4]), True, True)

    dh1, rs_mlp0 = mlp_bwd("mlp0", dy1, h1, u1, act1, g_up[0], g_dn[0], mlp_w_down[0])
    dx1, dy0, sums1 = _norm_bwd("bwd_norm1", dx2, dh1, x1, y0, bwd_vec(ng_all[0, 1], mod[0, 2], ng_all[0, 2], mod[0, 4], rs_mlp0[4]), True, True)

    (do,) = _linear_t("mla_do", dy0, g_wo, out_dtypes=(BF,))
    dw_o = _wgrad("mla_dw_o", o, dy0, g_wo)
    dkv, dkr, dq = _flash_bwd("mla_attn_bwd", qb, kvb, kr, o, lse, do, rope_c, -rope_s1, -rope_s2, heads, scale)
    dw_uq = _wgrad("mla_dw_uq", cq, dq, g_uq)
    (dcq,) = _linear_t("mla_dcq", dq, g_uq)
    dw_ukv = _wgrad("mla_dw_ukv", ckv, dkv, g_ukv)
    (dckv,) = _linear_t("mla_dckv", dkv, g_ukv)
    dlat, lat_sums = _lat_bwd("mla_lat_bwd", dcq, dckv, dkr, lat, mla_g_q, mla_g_kv, rope_c, -rope_s1, -rope_s2, rq, rkv, heads)
    dw_win = _wgrad("mla_dw_in", h0, dlat, g_win)
    sib_mla = rs_begin("mla", [blocks(dw_win, w_in_loc), blocks(dw_uq, uq_loc), blocks(dw_ukv, ukv_loc), blocks(dw_o, mla_w_o[0])])
    (dh0,) = _linear_t("mla_dh", dlat, g_win, token=sib_mla[4])
    grad_x, sums0 = _norm_bwd("bwd_pre0", dx1, dh0, x0, None, bwd_vec(None, None, ng_all[0, 0], mod[0, 1]), True, False)

    def sums_rows(s, rows):
        return [s[r] for r in rows]

    dmod = jnp.stack([
        jnp.stack(sums_rows(sums0, (0, 1)) + sums_rows(sums1, (3,)) + sums_rows(sums1, (0, 1)) + sums_rows(sums2, (3,))),
        jnp.stack(sums_rows(sums2, (0, 1)) + sums_rows(sums3, (3,)) + sums_rows(sums3, (0, 1)) + sums_rows(sums4, (3,))),
    ])
    dng = jnp.stack([
        jnp.stack([sums0[2], sums1[4], sums1[2], sums2[4]]),
        jnp.stack([sums2[2], sums3[4], sums3[2], sums4[4]]),
    ])
    parts = [dmod.reshape(-1), dng.reshape(-1), lat_sums[0], lat_sums[1], conv_sums[0:3].reshape(-1)]
    sizes = [p.shape[0] for p in parts]
    packed = jnp.concatenate(parts)
    pk_rows = -(-packed.shape[0] // (8 * LANES)) * 8
    packed = jnp.pad(packed, (0, pk_rows * LANES - packed.shape[0])).reshape(pk_rows, LANES)
    packed_all = _ag_small("ag_small_grads", packed).reshape(N_DEV, pk_rows, LANES)
    rs_mla = rs_continue("mla", sib_mla, packed_all)
    total = _sum_devices("sum_small_grads", packed_all).reshape(-1)
    offs = [0]
    for s in sizes:
        offs.append(offs[-1] + s)
    g_b_mod = total[offs[0]:offs[1]].reshape(depth, 6 * d)
    ng_full = total[offs[1]:offs[2]].reshape(depth, 4, d)
    g_norm_g = lax.dynamic_slice_in_dim(ng_full, dev * d_loc, d_loc, axis=2)
    g_g_q = total[offs[2]:offs[3]].reshape(1, rq)
    g_g_kv = total[offs[3]:offs[4]].reshape(1, rkv)
    cw_full = total[offs[4]:offs[5]].reshape(1, 3, d)
    g_conv_w = lax.dynamic_slice_in_dim(cw_full, dev * d_loc, d_loc, axis=2)

    dmod_all = packed_all.reshape(N_DEV, -1)[:, :sizes[0]].reshape(N_DEV, depth, 6 * d)
    dmod_cols = lax.dynamic_slice_in_dim(dmod_all, dev * n_mod_loc, n_mod_loc, axis=2)
    cond_t = cond_all.T
    mod_out = None
    for i in range(depth):
        mod_out = _mod_grad_adam(f"w_mod_adam{i}", cond_t, dmod_cols[:, i, :], w_mod, m_w_mod, v_w_mod, i, mod_out)
    g_w_mod, d_w_mod, nm_w_mod, nv_w_mod = mod_out

    def rs_end(tag, started, after):
        send_sems, recv_sems, pair, lands, _ = started
        return _push_wait(f"rs_wait_{tag}", pair, lands, send_sems, recv_sems, _chip_routes, after)

    pair_mlp1, lands_mlp1 = rs_end("mlp1", rs_mlp1, g_w_mod)
    pair_conv, lands_conv = rs_end("conv", rs_conv, g_w_mod)
    pair_mlp0, lands_mlp0 = rs_end("mlp0", rs_mlp0, g_w_mod)

    results = {"w_mod": (g_w_mod, d_w_mod, nm_w_mod, nv_w_mod)}
    given = {"b_mod": (b_mod, m_b_mod, v_b_mod), "norm_g": (norm_g, m_norm_g, v_norm_g),
             "mla_w_in": (mla_w_in, m_mla_w_in, v_mla_w_in), "mla_g_q": (mla_g_q, m_mla_g_q, v_mla_g_q),
             "mla_g_kv": (mla_g_kv, m_mla_g_kv, v_mla_g_kv), "mla_w_uq": (mla_w_uq, m_mla_w_uq, v_mla_w_uq),
             "mla_w_ukv": (mla_w_ukv, m_mla_w_ukv, v_mla_w_ukv), "mla_w_o": (mla_w_o, m_mla_w_o, v_mla_w_o),
             "conv_w_in": (conv_w_in, m_conv_w_in, v_conv_w_in), "conv_w": (conv_w, m_conv_w, v_conv_w),
             "conv_w_out": (conv_w_out, m_conv_w_out, v_conv_w_out), "mlp_w_up": (mlp_w_up, m_mlp_w_up, v_mlp_w_up),
             "mlp_w_down": (mlp_w_down, m_mlp_w_down, v_mlp_w_down)}

    def fused(name, parts, prev=None):
        for layer, (pair, lands) in enumerate(parts):
            prev = _adam_sum(f"adam_{name}{layer}", pair, lands, chip_idx, *given[name], layer, prev)
        results[name] = tuple(prev)

    fused("mlp_w_up", [(pair_mlp0[0], lands_mlp0[0]), (pair_mlp1[0], lands_mlp1[0])])
    fused("mlp_w_down", [(pair_mlp0[1], lands_mlp0[1]), (pair_mlp1[1], lands_mlp1[1])])
    fused("conv_w_in", [(pair_conv[0], lands_conv[0])])
    fused("conv_w_out", [(pair_conv[1], lands_conv[1])])
    pair_mla, lands_mla = rs_end("mla", rs_mla, results["conv_w_out"][3])
    fused("mla_w_o", [(pair_mla[3], lands_mla[3])])

    red_win, red_uq, red_ukv = [_chip_sum(f"chip_sum_mla{a}", pair_mla[a], lands_mla[a], chip_idx) for a in range(3)]
    plain = {"b_mod": g_b_mod, "norm_g": g_norm_g, "mla_g_q": g_g_q, "mla_g_kv": g_g_kv, "conv_w": g_conv_w,
             "mla_w_in": red_win[:, :lat_w][None],
             "mla_w_uq": red_uq.reshape(-1, heads, 2 * LANES)[:, :, :NOPE_DIM + ROPE_DIM][None],
             "mla_w_ukv": red_ukv.reshape(mla_w_ukv.shape)}
    for name, g in plain.items():
        results[name] = (g, *_adam(f"adam_{name}", given[name][0], g, given[name][1], given[name][2]))

    order = ["w_mod", "b_mod", "norm_g", "mla_w_in", "mla_g_q", "mla_g_kv", "mla_w_uq", "mla_w_ukv", "mla_w_o",
             "conv_w_in", "conv_w", "conv_w_out", "mlp_w_up", "mlp_w_down"]
    return (loss, grad_x[None], *[results[n][0] for n in order], *[results[n][1] for n in order],
            *[results[n][2] for n in order], *[results[n][3] for n in order])
```

```python
import jax
import jax.numpy as jnp
from jax import lax
from jax.experimental import pallas as pl
from jax.experimental.pallas import tpu as pltpu

F32 = jnp.float32
BF = jnp.bfloat16
MESH = pl.DeviceIdType.MESH
N_DEV = 8
LANES = 128
NORM_EPS = 1e-6
ROPE_THETA = 10000.0
ROPE_DIM = 64
NOPE_DIM = 128
V_DIM = 128
LR, B1, B2, ADAM_EPS, WD, STEP = 0.001, 0.9, 0.999, 1e-08, 0.01, 10
VMEM_LIMIT = 56 * 1024 * 1024
TILE_BYTES = 2 * 1024 * 1024
MM_TILE_PREFS = ((1024, 1024), (1024, 512), (512, 1024), (512, 512), (256, 512), (256, 256), (128, 256), (128, 128))
MM_VMEM_BUDGET = 36 * 1024 * 1024
TQ = 512
LOG2E = 1.4426950408889634

NN = (((1,), (0,)), ((), ()))
NT = (((1,), (1,)), ((), ()))
TNDIMS = (((0,), (0,)), ((), ()))
HBM_SPEC = pl.BlockSpec(memory_space=pltpu.HBM)
VMEM_SPEC = pl.BlockSpec(memory_space=pltpu.VMEM)


def _cparams(n_axes):
    return pltpu.CompilerParams(dimension_semantics=("arbitrary",) * n_axes, vmem_limit_bytes=VMEM_LIMIT)


def _pick(dim, pref):
    if dim <= pref:
        return dim
    for t in range(pref - pref % LANES, 0, -LANES):
        if t > 0 and dim % t == 0:
            return t
    for t in range(pref, 0, -1):
        if dim % t == 0:
            return t
    return dim


def _rows(r, c, itemsize=4):
    want = max(8, TILE_BYTES // (itemsize * max(c, 1)))
    if r <= want:
        return r
    for t in range(want - want % 8, 0, -8):
        if t > 0 and r % t == 0:
            return t
    return r


def _pos():
    return lax.axis_index("x"), lax.axis_index("y"), lax.axis_index("c")


def _ag_small(name, v):
    m_per, n = v.shape

    def body(x_ref, out_ref, send_sems, recv_sems, local_sem):
        x, y, c = _pos()
        me, sibling = (x, y, c), (x, y, 1 - c)
        chips = [(1 - x, y), (x, 1 - y), (1 - x, 1 - y)]

        def rows(px, py, pc):
            return out_ref.at[pl.ds((4 * px + 2 * py + pc) * m_per, m_per), :]

        def copy(k, block, to, src=None):
            return pltpu.make_async_remote_copy(
                src_ref=rows(*block) if src is None else src, dst_ref=rows(*block),
                send_sem=send_sems.at[k], recv_sem=recv_sems.at[k], device_id=to, device_id_type=MESH)

        mine = pltpu.make_async_copy(x_ref, rows(*me), local_sem)
        mine.start()
        first = [copy(0, me, sibling, src=x_ref)]
        first += [copy(1 + j, me, (*chip, c), src=x_ref) for j, chip in enumerate(chips)]
        for cp in first:
            cp.start()
        passed = [copy(4 + j, (*chip, c), sibling) for j, chip in enumerate(chips)]
        for j, chip in enumerate(chips):
            copy(1 + j, (*chip, c), me).wait_recv()
            passed[j].start()
        copy(0, sibling, me).wait_recv()
        for j, chip in enumerate(chips):
            copy(4 + j, (*chip, 1 - c), me).wait_recv()
        for cp in first + passed:
            cp.wait_send()
        mine.wait()

    return pl.pallas_call(
        body, name=name,
        out_shape=jax.ShapeDtypeStruct((N_DEV * m_per, n), v.dtype),
        in_specs=[VMEM_SPEC], out_specs=VMEM_SPEC,
        scratch_shapes=[pltpu.SemaphoreType.DMA((7,)), pltpu.SemaphoreType.DMA((7,)), pltpu.SemaphoreType.DMA],
    )(v)


def _ag_big(name, arrs):
    n = len(arrs)

    def body(*refs):
        ins, outs = refs[:n], refs[n:2 * n]
        send_sems, recv_sems, local_sems = refs[2 * n:]
        x, y, c = _pos()
        me, sibling = (x, y, c), (x, y, 1 - c)
        chips = [(1 - x, y), (x, 1 - y), (1 - x, 1 - y)]

        def copy(a, k, block, to, src=None):
            dst = outs[a].at[4 * block[0] + 2 * block[1] + block[2]]
            return pltpu.make_async_remote_copy(
                src_ref=dst if src is None else src, dst_ref=dst,
                send_sem=send_sems.at[7 * a + k], recv_sem=recv_sems.at[7 * a + k],
                device_id=to, device_id_type=MESH)

        mine = [pltpu.make_async_copy(ins[a], outs[a].at[4 * x + 2 * y + c], local_sems.at[a]) for a in range(n)]
        for cp in mine:
            cp.start()
        first = []
        for a in range(n):
            first.append(copy(a, 0, me, sibling, src=ins[a]))
            first += [copy(a, 1 + j, me, (*chip, c), src=ins[a]) for j, chip in enumerate(chips)]
        for cp in first:
            cp.start()
        passed = []
        for j, chip in enumerate(chips):
            for a in range(n):
                copy(a, 1 + j, (*chip, c), me).wait_recv()
                cp = copy(a, 4 + j, (*chip, c), sibling)
                cp.start()
                passed.append(cp)
        for a in range(n):
            copy(a, 0, sibling, me).wait_recv()
            for j, chip in enumerate(chips):
                copy(a, 4 + j, (*chip, 1 - c), me).wait_recv()
        for cp in first + passed:
            cp.wait_send()
        for cp in mine:
            cp.wait()

    return pl.pallas_call(
        body, name=name,
        out_shape=[jax.ShapeDtypeStruct((N_DEV,) + a.shape, a.dtype) for a in arrs],
        in_specs=[HBM_SPEC] * n, out_specs=[HBM_SPEC] * n,
        scratch_shapes=[pltpu.SemaphoreType.DMA((7 * n,)), pltpu.SemaphoreType.DMA((7 * n,)),
                        pltpu.SemaphoreType.DMA((n,))],
    )(*arrs)


SEM_SPEC = pl.BlockSpec(memory_space=pltpu.SEMAPHORE)
ANY_SPEC = pl.BlockSpec(memory_space=pl.ANY)
EFFECT = pltpu.SideEffectType.DATAFLOW_SIDE_EFFECTING


def _hbm(a):
    return pltpu.with_memory_space_constraint(a, pltpu.HBM)


def _cast_own_block(name, w, layer, dev_idx):
    _, r, cdim = w.shape
    tr = _rows(r, cdim, 4)

    def body(d_ref, w_ref, o_ref):
        o_ref[...] = w_ref[...].astype(BF)

    return pl.pallas_call(
        body, name=name,
        out_shape=jax.ShapeDtypeStruct((N_DEV, r, cdim), BF),
        grid_spec=pltpu.PrefetchScalarGridSpec(
            num_scalar_prefetch=1, grid=(r // tr,),
            in_specs=[pl.BlockSpec((None, tr, cdim), lambda i, d_ref: (layer, i, 0))],
            out_specs=pl.BlockSpec((None, tr, cdim), lambda i, d_ref: (d_ref[0], i, 0))),
        compiler_params=_cparams(1),
    )(dev_idx, w)


def _ag_start(name, lands, groups, after):
    n = len(lands)
    ng = len(groups)
    n_after = len(after)

    def body(*refs):
        lnd = refs[:n]
        sems = refs[n + n_after:n + n_after + 2 * ng]
        token = refs[-1]
        x, y, c = _pos()
        targets = [(x, y, 1 - c), (1 - x, y, c), (x, 1 - y, c), (1 - x, 1 - y, c)]
        for gi, members in enumerate(groups):
            for pos, a in enumerate(members):
                own = lnd[a].at[4 * x + 2 * y + c]
                for k, to in enumerate(targets):
                    pltpu.make_async_remote_copy(
                        src_ref=own, dst_ref=own,
                        send_sem=sems[2 * gi].at[4 * pos + k], recv_sem=sems[2 * gi + 1].at[4 * pos + k],
                        device_id=to, device_id_type=MESH).start()
        token[...] = jnp.zeros_like(token)

    sem_shapes = []
    for members in groups:
        sem_shapes += [pltpu.SemaphoreType.DMA((4 * len(members),))] * 2
    res = pl.pallas_call(
        body, name=name,
        out_shape=(*sem_shapes, *[pltpu.HBM(a.shape, a.dtype) for a in lands], jax.ShapeDtypeStruct((8, LANES), F32)),
        in_specs=[HBM_SPEC] * n + [ANY_SPEC] * n_after,
        out_specs=(*[SEM_SPEC] * (2 * ng), *[HBM_SPEC] * n, VMEM_SPEC),
        input_output_aliases={i: 2 * ng + i for i in range(n)},
        compiler_params=pltpu.CompilerParams(has_side_effects=EFFECT),
    )(*[_hbm(a) for a in lands], *after)
    sem_pairs = [(res[2 * g], res[2 * g + 1]) for g in range(ng)]
    return sem_pairs, list(res[2 * ng:2 * ng + n]), res[-1]


def _ag_forward(name, lands, send_sems, recv_sems, after):
    n = len(lands)

    def body(*refs):
        lnd = refs[:n]
        s_sems, r_sems = refs[n], refs[n + 1]
        fs_sems, fr_sems = refs[n + 3], refs[n + 4]
        token = refs[-1]
        x, y, c = _pos()
        sources = [(x, y, 1 - c), (1 - x, y, c), (x, 1 - y, c), (1 - x, 1 - y, c)]
        for a in range(n):
            own = lnd[a].at[4 * x + 2 * y + c]
            for k, src in enumerate(sources):
                blk = lnd[a].at[4 * src[0] + 2 * src[1] + src[2]]
                arrived = pltpu.make_async_remote_copy(
                    src_ref=own, dst_ref=blk, send_sem=s_sems.at[4 * a + k], recv_sem=r_sems.at[4 * a + k],
                    device_id=src, device_id_type=MESH)
                arrived.wait_send()
                arrived.wait_recv()
                if k > 0:
                    pltpu.make_async_remote_copy(
                        src_ref=blk, dst_ref=blk, send_sem=fs_sems.at[3 * a + k - 1], recv_sem=fr_sems.at[3 * a + k - 1],
                        device_id=(x, y, 1 - c), device_id_type=MESH).start()
        token[...] = jnp.zeros_like(token)

    res = pl.pallas_call(
        body, name=name,
        out_shape=(pltpu.SemaphoreType.DMA((3 * n,)), pltpu.SemaphoreType.DMA((3 * n,)),
                   *[pltpu.HBM(a.shape, a.dtype) for a in lands], jax.ShapeDtypeStruct((8, LANES), F32)),
        in_specs=[HBM_SPEC] * n + [SEM_SPEC, SEM_SPEC, ANY_SPEC],
        out_specs=(SEM_SPEC, SEM_SPEC, *[HBM_SPEC] * n, VMEM_SPEC),
        input_output_aliases={i: 2 + i for i in range(n)},
        compiler_params=pltpu.CompilerParams(has_side_effects=EFFECT),
    )(*lands, send_sems, recv_sems, after)
    return res[0], res[1], list(res[2:2 + n]), res[-1]


def _ag_finish(name, lands, fs_sems, fr_sems, after):
    n = len(lands)

    def body(*refs):
        lnd = refs[:n]
        fs, fr = refs[n], refs[n + 1]
        x, y, c = _pos()
        chips = [(1 - x, y), (x, 1 - y), (1 - x, 1 - y)]
        for a in range(n):
            for j, chip in enumerate(chips):
                mine = lnd[a].at[4 * chip[0] + 2 * chip[1] + c]
                theirs = lnd[a].at[4 * chip[0] + 2 * chip[1] + 1 - c]
                cp = pltpu.make_async_remote_copy(
                    src_ref=mine, dst_ref=theirs, send_sem=fs.at[3 * a + j], recv_sem=fr.at[3 * a + j],
                    device_id=(x, y, 1 - c), device_id_type=MESH)
                cp.wait_send()
                cp.wait_recv()

    res = pl.pallas_call(
        body, name=name,
        out_shape=tuple(pltpu.HBM(a.shape, a.dtype) for a in lands),
        in_specs=[HBM_SPEC] * n + [SEM_SPEC, SEM_SPEC, ANY_SPEC],
        out_specs=tuple([HBM_SPEC] * n),
        input_output_aliases={i: i for i in range(n)},
        compiler_params=pltpu.CompilerParams(has_side_effects=EFFECT),
    )(*lands, fs_sems, fr_sems, after)
    return list(res)


def _sibling_routes(x, y, c):
    return [(2 * j + (1 - c), (x, y, 1 - c)) for j in range(4)]


def _chip_routes(x, y, c):
    return [(2 * cx + cy, (cx, cy, c)) for cx, cy in [(1 - x, y), (x, 1 - y), (1 - x, 1 - y)]]


def _push_start(name, arrs, routes):
    n = len(arrs)
    nr = len(routes(0, 0, 0))

    def body(*refs):
        ins, lnd = refs[:n], refs[n:2 * n]
        s_sems, r_sems = refs[2 * n], refs[2 * n + 1]
        token = refs[-1]
        for a in range(n):
            for k, (blk, to) in enumerate(routes(*_pos())):
                pltpu.make_async_remote_copy(
                    src_ref=ins[a].at[blk], dst_ref=lnd[a].at[k],
                    send_sem=s_sems.at[nr * a + k], recv_sem=r_sems.at[nr * a + k],
                    device_id=to, device_id_type=MESH).start()
        token[...] = jnp.zeros_like(token)

    lands = [lax.empty((nr,) + a.shape[1:], a.dtype) for a in arrs]
    res = pl.pallas_call(
        body, name=name,
        out_shape=(pltpu.SemaphoreType.DMA((nr * n,)), pltpu.SemaphoreType.DMA((nr * n,)),
                   *[pltpu.HBM(a.shape, a.dtype) for a in arrs], *[pltpu.HBM(a.shape, a.dtype) for a in lands],
                   jax.ShapeDtypeStruct((8, LANES), F32)),
        in_specs=[HBM_SPEC] * (2 * n),
        out_specs=(SEM_SPEC, SEM_SPEC, *[HBM_SPEC] * (2 * n), VMEM_SPEC),
        input_output_aliases={i: 2 + i for i in range(2 * n)},
        compiler_params=pltpu.CompilerParams(has_side_effects=EFFECT),
    )(*[_hbm(a) for a in arrs], *[_hbm(a) for a in lands])
    return res[0], res[1], list(res[2:2 + n]), list(res[2 + n:2 + 2 * n]), res[-1]


def _push_wait(name, arrs, lands, send_sems, recv_sems, routes, after):
    n = len(arrs)
    nr = len(routes(0, 0, 0))

    def body(*refs):
        ins, lnd = refs[:n], refs[n:2 * n]
        s_sems, r_sems = refs[2 * n], refs[2 * n + 1]
        for a in range(n):
            for k, (blk, to) in enumerate(routes(*_pos())):
                cp = pltpu.make_async_remote_copy(
                    src_ref=ins[a].at[blk], dst_ref=lnd[a].at[k],
                    send_sem=s_sems.at[nr * a + k], recv_sem=r_sems.at[nr * a + k],
                    device_id=to, device_id_type=MESH)
                cp.wait_send()
                cp.wait_recv()

    res = pl.pallas_call(
        body, name=name,
        out_shape=tuple(pltpu.HBM(a.shape, a.dtype) for a in list(arrs) + list(lands)),
        in_specs=[HBM_SPEC] * (2 * n) + [SEM_SPEC, SEM_SPEC, ANY_SPEC],
        out_specs=tuple([HBM_SPEC] * (2 * n)),
        input_output_aliases={i: i for i in range(2 * n)},
        compiler_params=pltpu.CompilerParams(has_side_effects=EFFECT),
    )(*arrs, *lands, send_sems, recv_sems, after)
    return list(res[:n]), list(res[n:])


def _pair_add(name, g, r1, c_idx):
    _, r, cdim = g.shape
    tr = _rows(r, cdim, 2)

    def body(c_ref, g_ref, r_ref, o_ref):
        o_ref[...] = (g_ref[...].astype(F32) + r_ref[...].astype(F32)).astype(o_ref.dtype)

    return pl.pallas_call(
        body, name=name,
        out_shape=jax.ShapeDtypeStruct((4, r, cdim), BF),
        grid_spec=pltpu.PrefetchScalarGridSpec(
            num_scalar_prefetch=1, grid=(4, r // tr),
            in_specs=[pl.BlockSpec((None, tr, cdim), lambda j, i, c_ref: (2 * j + c_ref[0], i, 0)),
                      pl.BlockSpec((None, tr, cdim), lambda j, i, c_ref: (j, i, 0))],
            out_specs=pl.BlockSpec((None, tr, cdim), lambda j, i, c_ref: (j, i, 0))),
        compiler_params=_cparams(2),
    )(c_idx, g, r1)


def _chip_sum(name, p, r2, chip_idx):
    _, r, cdim = p.shape
    tr = _rows(r, cdim, 4)

    def body(s_ref, p_ref, a_ref, b_ref, c_ref, o_ref):
        o_ref[...] = ((p_ref[...].astype(F32) + a_ref[...].astype(F32)) + b_ref[...].astype(F32)) + c_ref[...].astype(F32)

    def other(k):
        return pl.BlockSpec((None, tr, cdim), lambda i, s_ref: (k, i, 0))

    return pl.pallas_call(
        body, name=name,
        out_shape=jax.ShapeDtypeStruct((r, cdim), F32),
        grid_spec=pltpu.PrefetchScalarGridSpec(
            num_scalar_prefetch=1, grid=(r // tr,),
            in_specs=[pl.BlockSpec((None, tr, cdim), lambda i, s_ref: (s_ref[0], i, 0)), other(0), other(1), other(2)],
            out_specs=pl.BlockSpec((tr, cdim), lambda i, s_ref: (i, 0))),
        compiler_params=_cparams(1),
    )(chip_idx, p, r2, r2, r2)


def _adam_math(w, g, m, v):
    m = B1 * m + (1.0 - B1) * g
    v = B2 * v + (1.0 - B2) * (g * g)
    m_hat = m / (1.0 - B1 ** STEP)
    v_hat = v / (1.0 - B2 ** STEP)
    delta = -LR * (m_hat / (jnp.sqrt(v_hat) + ADAM_EPS) + WD * w)
    return delta, m, v


def _adam(name, w, g, m, v):
    shape = w.shape
    cdim = shape[-1]
    r = w.size // cdim
    flat = [a.reshape(r, cdim) for a in (w, g, m, v)]
    tr = _rows(r, cdim, 4)

    def body(w_ref, g_ref, m_ref, v_ref, d_ref, mo_ref, vo_ref):
        d, mn, vn = _adam_math(w_ref[...], g_ref[...], m_ref[...], v_ref[...])
        d_ref[...] = d
        mo_ref[...] = mn
        vo_ref[...] = vn

    spec = pl.BlockSpec((tr, cdim), lambda i: (i, 0))
    outs = pl.pallas_call(
        body, name=name,
        out_shape=[jax.ShapeDtypeStruct((r, cdim), F32)] * 3,
        grid=(r // tr,), in_specs=[spec] * 4, out_specs=[spec] * 3,
        compiler_params=_cparams(1),
    )(*flat)
    return [o.reshape(shape) for o in outs]


def _adam_sum(name, p, r2, chip_idx, w, m, v, layer, prev=None):
    nl, r, cdim = w.shape
    tr = _rows(r, cdim, 4)

    def body(s_ref, p_ref, a_ref, b_ref, c_ref, w_ref, m_ref, v_ref, *rest):
        g_ref, d_ref, mo_ref, vo_ref = rest[-4:]
        g = ((p_ref[...].astype(F32) + a_ref[...].astype(F32)) + b_ref[...].astype(F32)) + c_ref[...].astype(F32)
        dl, mn, vn = _adam_math(w_ref[...], g, m_ref[...], v_ref[...])
        g_ref[...] = g
        d_ref[...] = dl
        mo_ref[...] = mn
        vo_ref[...] = vn

    def slot(k):
        return pl.BlockSpec((None, tr, cdim), lambda i, s_ref: (k, i, 0))

    in_specs = [pl.BlockSpec((None, tr, cdim), lambda i, s_ref: (s_ref[0], i, 0)), slot(0), slot(1), slot(2)] + [slot(layer)] * 3
    operands = [chip_idx, p, r2, r2, r2, w, m, v]
    aliases = {}
    if prev is not None:
        in_specs += [ANY_SPEC] * 4
        aliases = {len(operands) + k: k for k in range(4)}
        operands += list(prev)
    return pl.pallas_call(
        body, name=name,
        out_shape=[jax.ShapeDtypeStruct((nl, r, cdim), F32)] * 4,
        grid_spec=pltpu.PrefetchScalarGridSpec(
            num_scalar_prefetch=1, grid=(r // tr,), in_specs=in_specs, out_specs=[slot(layer)] * 4),
        input_output_aliases=aliases,
        compiler_params=_cparams(1),
    )(*operands)


def _mod_matvec(name, cond_all, w_loc, layer):
    _, d, n_loc = w_loc.shape
    tn = _pick(n_loc, 512)

    def body(c_ref, w_ref, o_ref):
        o_ref[...] = jnp.dot(c_ref[...].astype(BF), w_ref[...].astype(BF), preferred_element_type=F32)

    return pl.pallas_call(
        body, name=name,
        out_shape=jax.ShapeDtypeStruct((N_DEV, n_loc), F32),
        grid=(n_loc // tn,),
        in_specs=[pl.BlockSpec((N_DEV, d), lambda j: (0, 0)), pl.BlockSpec((None, d, tn), lambda j: (layer, 0, j))],
        out_specs=pl.BlockSpec((N_DEV, tn), lambda j: (0, j)),
        compiler_params=_cparams(1),
    )(cond_all, w_loc)


def _mod_grad_adam(name, cond_t, dmod, w, m, v, layer, prev=None):
    nl, d, n_loc = w.shape
    tr = _rows(d, n_loc, 4)

    def body(ct_ref, dm_ref, w_ref, m_ref, v_ref, *rest):
        g_ref, d_ref, mo_ref, vo_ref = rest[-4:]
        g = ct_ref[:, 0:1] * dm_ref[0:1, :]
        for b in range(1, N_DEV):
            g = g + ct_ref[:, b:b + 1] * dm_ref[b:b + 1, :]
        dl, mn, vn = _adam_math(w_ref[...], g, m_ref[...], v_ref[...])
        g_ref[...] = g
        d_ref[...] = dl
        mo_ref[...] = mn
        vo_ref[...] = vn

    spec = pl.BlockSpec((None, tr, n_loc), lambda i: (layer, i, 0))
    in_specs = [pl.BlockSpec((tr, N_DEV), lambda i: (i, 0)), pl.BlockSpec((N_DEV, n_loc), lambda i: (0, 0)), spec, spec, spec]
    operands = [cond_t, dmod, w, m, v]
    aliases = {}
    if prev is not None:
        in_specs += [ANY_SPEC] * 4
        aliases = {len(operands) + k: k for k in range(4)}
        operands += list(prev)
    return pl.pallas_call(
        body, name=name,
        out_shape=[jax.ShapeDtypeStruct((nl, d, n_loc), F32)] * 4,
        grid=(d // tr,), in_specs=in_specs, out_specs=[spec] * 4,
        input_output_aliases=aliases,
        compiler_params=_cparams(1),
    )(*operands)


def _sum_devices(name, gathered):
    _, m, n = gathered.shape

    def body(g_ref, o_ref):
        acc = g_ref[0]
        for d in range(1, N_DEV):
            acc = acc + g_ref[d]
        o_ref[...] = acc

    return pl.pallas_call(
        body, name=name, out_shape=jax.ShapeDtypeStruct((m, n), F32),
        in_specs=[VMEM_SPEC], out_specs=VMEM_SPEC,
    )(gathered)


def _silu(name, c_pad):
    def body(c_ref, o_ref):
        c = c_ref[...]
        o_ref[...] = c * (1.0 / (1.0 + jnp.exp(-c)))

    return pl.pallas_call(body, name=name, out_shape=jax.ShapeDtypeStruct(c_pad.shape, F32),
                          in_specs=[VMEM_SPEC], out_specs=VMEM_SPEC)(c_pad)


def _mm_tiles(m, n_cap, k, bytes_per_out_elem):
    for pm, pn in MM_TILE_PREFS:
        tm, tn = _pick(m, pm), _pick(n_cap, pn)
        need = 2 * (2 * tm * k + 2 * k * tn + bytes_per_out_elem * tm * tn) + 4 * tm * tn
        if need <= MM_VMEM_BUDGET:
            return tm, tn
    return _pick(m, 8), _pick(n_cap, LANES)


def _mm(name, a, b, grid, a_spec, b_spec, outs, dot, extras=(), epilogue=None, token=None):
    n_ex, n_out = len(extras), len(outs)
    n_in = 2 + n_ex + (0 if token is None else 1)

    def body(*refs):
        ex_refs = refs[2:2 + n_ex]
        out_refs = refs[n_in:n_in + n_out]
        acc = dot(refs[0], refs[1])
        vals = (acc,) if epilogue is None else epilogue(acc, *[e[...] for e in ex_refs])
        for o_ref, val in zip(out_refs, vals):
            o_ref[...] = val.astype(o_ref.dtype)

    ins = [a, b] + [e for e, _ in extras]
    in_specs = [a_spec, b_spec] + [s for _, s in extras]
    if token is not None:
        ins.append(token)
        in_specs.append(pl.BlockSpec(token.shape, lambda i, j: (0, 0)))
    return pl.pallas_call(
        body, name=name,
        out_shape=[jax.ShapeDtypeStruct(s, d) for s, d, _ in outs],
        grid=grid, in_specs=in_specs, out_specs=[s for _, _, s in outs],
        compiler_params=_cparams(2),
    )(*ins)


def _dot(dims):
    return lambda a_ref, b_ref: lax.dot_general(a_ref[...], b_ref[...], dims, preferred_element_type=F32)


def _w_cols(w):
    return w.shape[1] if w.ndim == 2 else w.shape[0] * w.shape[2]


def _w_rows(w):
    return w.shape[0] if w.ndim == 2 else w.shape[1]


def _out_bytes(out_dtypes, extras):
    return sum(jnp.dtype(d).itemsize for d in out_dtypes) + sum(e.dtype.itemsize for e in extras)


def _linear(name, a, w, out_dtypes=(F32,), epilogue=None, extras=(), row_extras=(), tn_multiple=1, token=None):
    t, kdim = a.shape
    n = _w_cols(w)
    tm, tn = _mm_tiles(t, w.shape[-1], kdim, _out_bytes(out_dtypes, extras))
    assert tn % tn_multiple == 0 or tn == n, (tn, tn_multiple)
    if w.ndim == 2:
        w_spec = pl.BlockSpec((kdim, tn), lambda i, j: (0, j))
    else:
        per = w.shape[2] // tn
        w_spec = pl.BlockSpec((None, kdim, tn), lambda i, j: (j // per, 0, j % per))
    o_spec = pl.BlockSpec((tm, tn), lambda i, j: (i, j))
    return _mm(name, a, w, (t // tm, n // tn), pl.BlockSpec((tm, kdim), lambda i, j: (i, 0)), w_spec,
               [((t, n), dt, o_spec) for dt in out_dtypes], _dot(NN),
               extras=[(e, o_spec) for e in extras] + [(e, pl.BlockSpec((tm, e.shape[1]), lambda i, j: (i, 0))) for e in row_extras],
               epilogue=epilogue, token=token)


def _linear_t(name, dy, w, out_dtypes=(F32,), epilogue=None, extras=(), token=None):
    t, n = dy.shape
    kdim = _w_rows(w)
    tm, tn = _mm_tiles(t, kdim, n, _out_bytes(out_dtypes, extras))
    o_spec = pl.BlockSpec((tm, tn), lambda i, j: (i, j))
    if w.ndim == 2:
        w_spec = pl.BlockSpec((tn, n), lambda i, j: (j, 0))
        dot = _dot(NT)
    else:
        n_loc = w.shape[2]
        w_spec = pl.BlockSpec((N_DEV, tn, n_loc), lambda i, j: (0, j, 0))

        def dot(a_ref, w_ref):
            acc = lax.dot_general(a_ref[:, 0:n_loc], w_ref[0], NT, preferred_element_type=F32)
            for dev in range(1, N_DEV):
                acc = acc + lax.dot_general(a_ref[:, dev * n_loc:(dev + 1) * n_loc], w_ref[dev], NT, preferred_element_type=F32)
            return acc

    return _mm(name, dy, w, (t // tm, kdim // tn), pl.BlockSpec((tm, n), lambda i, j: (i, 0)), w_spec,
               [((t, kdim), dt, o_spec) for dt in out_dtypes], dot,
               extras=[(e, o_spec) for e in extras], epilogue=epilogue, token=token)


def _wgrad(name, a, dy, like):
    t, kdim = a.shape
    n = dy.shape[1]
    tm, tn = _mm_tiles(kdim, like.shape[-1], t, 2)
    if like.ndim == 2:
        o_spec = pl.BlockSpec((tm, tn), lambda i, j: (i, j))
    else:
        per = like.shape[2] // tn
        o_spec = pl.BlockSpec((None, tm, tn), lambda i, j: (j // per, i, j % per))
    return _mm(name, a, dy, (kdim // tm, n // tn), pl.BlockSpec((t, tm), lambda i, j: (0, i)),
               pl.BlockSpec((t, tn), lambda i, j: (0, j)), [(like.shape, BF, o_spec)], _dot(TNDIMS))[0]


def _rstd(x):
    return lax.rsqrt(jnp.mean(x * x, axis=-1, keepdims=True) + NORM_EPS)


def _colsum(x):
    return jnp.sum(x, axis=0, keepdims=True)


def _norm_fwd(name, x, y, vec, target, has_post, has_pre, has_loss):
    t, d = x.shape
    tm = _rows(t, d, 4)
    ins, in_specs = [x], [pl.BlockSpec((tm, d), lambda i: (i, 0))]
    row_spec = pl.BlockSpec((tm, d), lambda i: (i, 0))
    if has_post:
        ins.append(y)
        in_specs.append(row_spec)
    ins.append(vec)
    in_specs.append(pl.BlockSpec((8, d), lambda i: (0, 0)))
    if has_loss:
        ins.append(target)
        in_specs.append(row_spec)
    out_shape, out_specs = [], []
    if has_post and not has_loss:
        out_shape.append(jax.ShapeDtypeStruct((t, d), F32))
        out_specs.append(row_spec)
    if has_pre:
        out_shape.append(jax.ShapeDtypeStruct((t, d), BF))
        out_specs.append(row_spec)
    if has_loss:
        out_shape += [jax.ShapeDtypeStruct((t, d), F32), jax.ShapeDtypeStruct((8, d), F32)]
        out_specs += [row_spec, pl.BlockSpec((8, d), lambda i: (0, 0))]

    def body(*refs):
        it = iter(refs)
        x_ref = next(it)
        y_ref = next(it) if has_post else None
        vec_ref = next(it)
        tgt_ref = next(it) if has_loss else None
        xv = x_ref[...]
        if has_post:
            yv = y_ref[...]
            xv = xv + vec_ref[1:2, :] * ((yv * _rstd(yv)) * vec_ref[0:1, :])
            if not has_loss:
                next(it)[...] = xv
        if has_pre:
            hv = ((xv * _rstd(xv)) * vec_ref[2:3, :]) * (1.0 + vec_ref[3:4, :]) + vec_ref[4:5, :]
            next(it)[...] = hv.astype(BF)
        if has_loss:
            e = xv - tgt_ref[...]
            next(it)[...] = e * (1.0 / d)
            acc_ref = next(it)
            i = pl.program_id(0)

            @pl.when(i == 0)
            def _():
                acc_ref[...] = jnp.zeros_like(acc_ref)

            acc_ref[0:1, :] += _colsum(e * e)

    return pl.pallas_call(
        body, name=name, out_shape=out_shape, grid=(t // tm,), in_specs=in_specs, out_specs=out_specs,
        compiler_params=_cparams(1),
    )(*ins)


def _norm_bwd(name, dx_in, dh, x, y, vec, has_pre, has_post):
    t, d = dx_in.shape
    tm = _rows(t, d, 4)
    row_spec = pl.BlockSpec((tm, d), lambda i: (i, 0))
    vec_spec = pl.BlockSpec((8, d), lambda i: (0, 0))
    ins, in_specs = [dx_in], [row_spec]
    if has_pre:
        ins += [dh, x]
        in_specs += [row_spec, row_spec]
    if has_post:
        ins.append(y)
        in_specs.append(row_spec)
    ins.append(vec)
    in_specs.append(vec_spec)
    out_shape, out_specs = [], []
    if has_pre:
        out_shape.append(jax.ShapeDtypeStruct((t, d), F32))
        out_specs.append(row_spec)
    if has_post:
        out_shape.append(jax.ShapeDtypeStruct((t, d), BF))
        out_specs.append(row_spec)
    out_shape.append(jax.ShapeDtypeStruct((8, d), F32))
    out_specs.append(vec_spec)

    def body(*refs):
        it = iter(refs)
        dx = next(it)[...]
        dh_ref = next(it) if has_pre else None
        x_ref = next(it) if has_pre else None
        y_ref = next(it) if has_post else None
        vec_ref = next(it)
        dx_ref = next(it) if has_pre else None
        dy_ref = next(it) if has_post else None
        sums_ref = next(it)
        i = pl.program_id(0)

        @pl.when(i == 0)
        def _():
            sums_ref[...] = jnp.zeros_like(sums_ref)

        if has_pre:
            dhv, xv = dh_ref[...], x_ref[...]
            rs = _rstd(xv)
            xhat = xv * rs
            na = vec_ref[2:3, :]
            sums_ref[0:1, :] += _colsum(dhv)
            sums_ref[1:2, :] += _colsum(dhv * (xhat * na))
            tt = dhv * (1.0 + vec_ref[3:4, :])
            sums_ref[2:3, :] += _colsum(tt * xhat)
            u = tt * na
            dx = dx + rs * (u - xhat * jnp.mean(u * xhat, axis=-1, keepdims=True))
            dx_ref[...] = dx
        if has_post:
            yv = y_ref[...]
            rs = _rstd(yv)
            yhat = yv * rs
            nb = vec_ref[0:1, :]
            sums_ref[3:4, :] += _colsum(dx * (yhat * nb))
            tt = dx * vec_ref[1:2, :]
            sums_ref[4:5, :] += _colsum(tt * yhat)
            u = tt * nb
            dy_ref[...] = (rs * (u - yhat * jnp.mean(u * yhat, axis=-1, keepdims=True))).astype(BF)

    return pl.pallas_call(
        body, name=name, out_shape=out_shape, grid=(t // tm,), in_specs=in_specs, out_specs=out_specs,
        compiler_params=_cparams(1),
    )(*ins)


def _rope128(tv, cs, s1, s2):
    return tv * cs + pltpu.roll(tv, 96, 1) * s1 + pltpu.roll(tv, 32, 1) * s2


def _lat_post(name, lat, gq, gkv, cs, s1, s2, rq, rkv):
    t, w = lat.shape
    tm = _rows(t, w, 4)

    def body(lat_ref, gq_ref, gkv_ref, cs_ref, s1_ref, s2_ref, cq_ref, ckv_ref, kr_ref):
        lq = lat_ref[:, 0:rq]
        lkv = lat_ref[:, rq:rq + rkv]
        cq_ref[...] = ((lq * _rstd(lq)) * gq_ref[...]).astype(BF)
        ckv_ref[...] = ((lkv * _rstd(lkv)) * gkv_ref[...]).astype(BF)
        kr_ref[...] = _rope128(lat_ref[:, rq + rkv:rq + rkv + LANES], cs_ref[...], s1_ref[...], s2_ref[...]).astype(BF)

    def rows(c):
        return pl.BlockSpec((tm, c), lambda i: (i, 0))

    def vecs(c):
        return pl.BlockSpec((1, c), lambda i: (0, 0))

    return pl.pallas_call(
        body, name=name,
        out_shape=[jax.ShapeDtypeStruct((t, rq), BF), jax.ShapeDtypeStruct((t, rkv), BF), jax.ShapeDtypeStruct((t, LANES), BF)],
        grid=(t // tm,),
        in_specs=[rows(w), vecs(rq), vecs(rkv), rows(LANES), rows(LANES), rows(LANES)],
        out_specs=[rows(rq), rows(rkv), rows(LANES)],
        compiler_params=_cparams(1),
    )(lat, gq, gkv, cs, s1, s2)


def _lat_bwd(name, dcq, dckv, dkr, lat, gq, gkv, cs, s1, s2, rq, rkv, heads):
    t, w = lat.shape
    tm = _rows(t, max(w, heads * LANES), 4)
    assert rq == rkv

    def body(dcq_ref, dckv_ref, dkr_ref, lat_ref, gq_ref, gkv_ref, cs_ref, s1_ref, s2_ref, dlat_ref, sums_ref):
        i = pl.program_id(0)

        @pl.when(i == 0)
        def _():
            sums_ref[...] = jnp.zeros_like(sums_ref)

        def rms_bwd(dc, lv, g, row):
            rs = _rstd(lv)
            lhat = lv * rs
            sums_ref[row:row + 1, :] += _colsum(dc * lhat)
            u = dc * g
            return rs * (u - lhat * jnp.mean(u * lhat, axis=-1, keepdims=True))

        dlat_ref[:, 0:rq] = rms_bwd(dcq_ref[...], lat_ref[:, 0:rq], gq_ref[...], 0).astype(BF)
        dlat_ref[:, rq:rq + rkv] = rms_bwd(dckv_ref[...], lat_ref[:, rq:rq + rkv], gkv_ref[...], 1).astype(BF)
        dk = dkr_ref[:, 0:LANES]
        for h in range(1, heads):
            dk = dk + dkr_ref[:, h * LANES:(h + 1) * LANES]
        dlat_ref[:, rq + rkv:rq + rkv + LANES] = _rope128(dk, cs_ref[...], s1_ref[...], s2_ref[...]).astype(BF)

    def rows(c):
        return pl.BlockSpec((tm, c), lambda i: (i, 0))

    def vecs(c):
        return pl.BlockSpec((1, c), lambda i: (0, 0))

    return pl.pallas_call(
        body, name=name,
        out_shape=[jax.ShapeDtypeStruct((t, w), BF), jax.ShapeDtypeStruct((8, rq), F32)],
        grid=(t // tm,),
        in_specs=[rows(rq), rows(rkv), rows(heads * LANES), rows(w), vecs(rq), vecs(rkv), rows(LANES), rows(LANES), rows(LANES)],
        out_specs=[rows(w), pl.BlockSpec((8, rq), lambda i: (0, 0))],
        compiler_params=_cparams(1),
    )(dcq, dckv, dkr, lat, gq, gkv, cs, s1, s2)


def _causal_mask(s, row0):
    row = lax.broadcasted_iota(jnp.int32, s.shape, 0) + row0
    col = lax.broadcasted_iota(jnp.int32, s.shape, 1)
    return col <= row


def _flash_fwd(name, qb, kv, kr, heads, scale):
    t = qb.shape[0]
    tq = _pick(t, TQ)
    nq = t // tq
    c2 = scale * LOG2E

    def body(q_ref, kv_ref, kr_ref, o_ref, lse_ref):
        i = pl.program_id(1)
        q = q_ref[...]

        def block(j, carry, masked):
            m_prev, l_prev, acc = carry
            rows = pl.ds(pl.multiple_of(j * tq, tq), tq)
            k = jnp.concatenate([kv_ref[rows, 0:LANES], kr_ref[rows, :]], axis=-1)
            s = lax.dot_general(q, k, NT, preferred_element_type=F32) * c2
            if masked:
                s = jnp.where(_causal_mask(s, 0), s, -1e30)
            m_new = jnp.maximum(m_prev, jnp.max(s, axis=-1, keepdims=True))
            alpha = jnp.exp2(m_prev - m_new)
            p = jnp.exp2(s - m_new)
            l_new = alpha * l_prev + jnp.sum(p, axis=-1, keepdims=True)
            pv = jnp.dot(p.astype(BF), kv_ref[rows, LANES:2 * LANES], preferred_element_type=F32)
            return m_new, l_new, alpha * acc + pv

        init = (jnp.full((tq, 1), -1e30, F32), jnp.zeros((tq, 1), F32), jnp.zeros((tq, V_DIM), F32))
        carry = lax.fori_loop(0, i, lambda j, cr: block(j, cr, False), init)
        m_fin, l_fin, acc = block(i, carry, True)
        o_ref[...] = (acc / l_fin).astype(BF)
        lse_ref[...] = jnp.broadcast_to(m_fin + jnp.log2(l_fin), (tq, LANES))

    return pl.pallas_call(
        body, name=name,
        out_shape=[jax.ShapeDtypeStruct((t, heads * V_DIM), BF), jax.ShapeDtypeStruct((t, heads * LANES), F32)],
        grid=(heads, nq),
        in_specs=[pl.BlockSpec((tq, 2 * LANES), lambda h, i: (i, h)),
                  pl.BlockSpec((t, 2 * LANES), lambda h, i: (0, h)),
                  pl.BlockSpec((t, LANES), lambda h, i: (0, 0))],
        out_specs=[pl.BlockSpec((tq, V_DIM), lambda h, i: (i, h)), pl.BlockSpec((tq, LANES), lambda h, i: (i, h))],
        compiler_params=_cparams(2),
    )(qb, kv, kr)


def _flash_bwd(name, qb, kv, kr, o, lse, do, cs, s1, s2, heads, scale):
    t = qb.shape[0]
    tq = _pick(t, TQ)
    nq = t // tq

    c2 = scale * LOG2E

    def body(q_ref, kv_ref, kr_ref, o_ref, lse_ref, do_ref, cs_ref, s1_ref, s2_ref, dkv_ref, dkr_ref, dq_ref, dq_acc):
        j = pl.program_id(1)

        @pl.when(j == 0)
        def _():
            dq_acc[...] = jnp.zeros_like(dq_acc)

        k = jnp.concatenate([kv_ref[:, 0:LANES], kr_ref[...]], axis=-1)
        v = kv_ref[:, LANES:2 * LANES]

        def block(i, carry, masked):
            dk, dv = carry
            rows = pl.ds(pl.multiple_of(i * tq, tq), tq)
            q = q_ref[rows, :]
            dov = do_ref[rows, :]
            delta = jnp.sum(dov.astype(F32) * o_ref[rows, :].astype(F32), axis=-1, keepdims=True)
            s = lax.dot_general(q, k, NT, preferred_element_type=F32) * c2
            p = jnp.exp2(s - lse_ref[rows, 0:1])
            if masked:
                p = jnp.where(_causal_mask(s, 0), p, 0.0)
            dv = dv + lax.dot_general(p.astype(BF), dov, TNDIMS, preferred_element_type=F32)
            dp = lax.dot_general(dov, v, NT, preferred_element_type=F32)
            ds = (p * (dp - delta) * scale).astype(BF)
            dk = dk + lax.dot_general(ds, q, TNDIMS, preferred_element_type=F32)
            dq_acc[rows, :] += jnp.dot(ds, k, preferred_element_type=F32)
            return dk, dv

        carry = block(j, (jnp.zeros((tq, 2 * LANES), F32), jnp.zeros((tq, V_DIM), F32)), True)
        dk, dv = lax.fori_loop(j + 1, nq, lambda i, cr: block(i, cr, False), carry)
        dkv_ref[:, 0:LANES] = dk[:, 0:LANES].astype(BF)
        dkv_ref[:, LANES:2 * LANES] = dv.astype(BF)
        dkr_ref[...] = dk[:, LANES:2 * LANES]

        @pl.when(j == nq - 1)
        def _():
            dq_ref[:, 0:LANES] = dq_acc[:, 0:LANES].astype(BF)
            dq_ref[:, LANES:2 * LANES] = _rope128(dq_acc[:, LANES:2 * LANES], cs_ref[...], s1_ref[...], s2_ref[...]).astype(BF)

    def per_head(width):
        return pl.BlockSpec((t, width), lambda h, j: (0, h))

    table = pl.BlockSpec((t, LANES), lambda h, j: (0, 0))
    return pl.pallas_call(
        body, name=name,
        out_shape=[jax.ShapeDtypeStruct((t, heads * 2 * LANES), BF), jax.ShapeDtypeStruct((t, heads * LANES), F32),
                   jax.ShapeDtypeStruct((t, heads * 2 * LANES), BF)],
        grid=(heads, nq),
        in_specs=[per_head(2 * LANES), pl.BlockSpec((tq, 2 * LANES), lambda h, j: (j, h)),
                  pl.BlockSpec((tq, LANES), lambda h, j: (j, 0)), per_head(LANES), per_head(LANES), per_head(LANES),
                  table, table, table],
        out_specs=[pl.BlockSpec((tq, 2 * LANES), lambda h, j: (j, h)), pl.BlockSpec((tq, LANES), lambda h, j: (j, h)),
                   per_head(2 * LANES)],
        scratch_shapes=[pltpu.VMEM((t, 2 * LANES), F32)],
        compiler_params=_cparams(2),
    )(qb, kv, kr, o, lse, do, cs, s1, s2)


def _shift_down(z, n):
    row = lax.broadcasted_iota(jnp.int32, z.shape, 0)
    return jnp.where(row >= n, pltpu.roll(z, n, 0), 0.0)


def _shift_up(z, n):
    t = z.shape[0]
    row = lax.broadcasted_iota(jnp.int32, z.shape, 0)
    return jnp.where(row < t - n, pltpu.roll(z, t - n, 0), 0.0)


def _conv_fwd(name, proj, cw):
    t, d3 = proj.shape
    d = d3 // 3
    tn = _pick(d, LANES)
    nb = d // tn

    def body(b_ref, c_ref, u_ref, w_ref, o_ref):
        z = c_ref[...] * u_ref[...]
        zc = w_ref[0:1, :] * _shift_down(z, 2) + w_ref[1:2, :] * _shift_down(z, 1) + w_ref[2:3, :] * z
        o_ref[...] = (b_ref[...] * zc).astype(BF)

    def part(p):
        return pl.BlockSpec((t, tn), lambda j: (0, p * nb + j))

    return pl.pallas_call(
        body, name=name, out_shape=jax.ShapeDtypeStruct((t, d), BF), grid=(nb,),
        in_specs=[part(0), part(1), part(2), pl.BlockSpec((8, tn), lambda j: (0, j))],
        out_specs=pl.BlockSpec((t, tn), lambda j: (0, j)),
        compiler_params=_cparams(1),
    )(proj, proj, proj, cw)


def _conv_bwd(name, dbz, proj, cw):
    t, d3 = proj.shape
    d = d3 // 3
    tn = _pick(d, LANES)
    nb = d // tn

    def body(g_ref, b_ref, c_ref, u_ref, w_ref, o_ref, sums_ref):
        p = pl.program_id(1)
        w0, w1, w2 = w_ref[0:1, :], w_ref[1:2, :], w_ref[2:3, :]

        @pl.when(p == 0)
        def _():
            z = c_ref[...] * u_ref[...]
            z1, z2 = _shift_down(z, 1), _shift_down(z, 2)
            gv = g_ref[...]
            o_ref[...] = (gv * (w0 * z2 + w1 * z1 + w2 * z)).astype(BF)
            dzc = gv * b_ref[...]
            sums_ref[...] = jnp.zeros_like(sums_ref)
            sums_ref[0:1, :] = _colsum(dzc * z2)
            sums_ref[1:2, :] = _colsum(dzc * z1)
            sums_ref[2:3, :] = _colsum(dzc * z)

        @pl.when(p > 0)
        def _():
            dzc = g_ref[...] * b_ref[...]
            dz = w2 * dzc + w1 * _shift_up(dzc, 1) + w0 * _shift_up(dzc, 2)
            other = jnp.where(p == 1, u_ref[...], c_ref[...])
            o_ref[...] = (dz * other).astype(BF)

    def part(q):
        return pl.BlockSpec((t, tn), lambda j, p: (0, q * nb + j))

    return pl.pallas_call(
        body, name=name,
        out_shape=[jax.ShapeDtypeStruct((t, d3), BF), jax.ShapeDtypeStruct((8, d), F32)],
        grid=(nb, 3),
        in_specs=[pl.BlockSpec((t, tn), lambda j, p: (0, j)), part(0), part(1), part(2),
                  pl.BlockSpec((8, tn), lambda j, p: (0, j))],
        out_specs=[pl.BlockSpec((t, tn), lambda j, p: (0, p * nb + j)), pl.BlockSpec((8, tn), lambda j, p: (0, j))],
        compiler_params=_cparams(2),
    )(dbz, proj, proj, proj, cw)


def _pad_rows8(v):
    return jnp.pad(v, ((0, 8 - v.shape[0]), (0, 0)))


def kernel(x, c, positions, w_mod, b_mod, norm_g, mla_w_in, mla_g_q, mla_g_kv, mla_w_uq, mla_w_ukv, mla_w_o, conv_w_in, conv_w, conv_w_out, mlp_w_up, mlp_w_down, loss_target, m_w_mod, m_b_mod, m_norm_g, m_mla_w_in, m_mla_g_q, m_mla_g_kv, m_mla_w_uq, m_mla_w_ukv, m_mla_w_o, m_conv_w_in, m_conv_w, m_conv_w_out, m_mlp_w_up, m_mlp_w_down, v_w_mod, v_b_mod, v_norm_g, v_mla_w_in, v_mla_g_q, v_mla_g_kv, v_mla_w_uq, v_mla_w_ukv, v_mla_w_o, v_conv_w_in, v_conv_w, v_conv_w_out, v_mlp_w_up, v_mlp_w_down):
    t, d = x.shape[1], x.shape[2]
    depth = w_mod.shape[0]
    n_mod_loc = w_mod.shape[2]
    d_loc = norm_g.shape[2]
    rq = mla_g_q.shape[1]
    rkv = mla_g_kv.shape[1]
    heads = mla_w_uq.shape[2]
    lat_w = mla_w_in.shape[2]
    lat_pad = rq + rkv + LANES
    scale = (NOPE_DIM + ROPE_DIM) ** -0.5
    xi, yi, ci = _pos()
    dev = 4 * xi + 2 * yi + ci
    c_idx = jnp.reshape(ci, (1,)).astype(jnp.int32)
    chip_idx = jnp.reshape(2 * xi + yi, (1,)).astype(jnp.int32)
    x0 = x[0]
    target = loss_target[0]

    w_in_loc = jnp.pad(mla_w_in[0], ((0, 0), (0, lat_pad - lat_w))).astype(BF)
    uq = mla_w_uq[0]
    uq_loc = jnp.pad(uq, ((0, 0), (0, 0), (0, LANES - ROPE_DIM))).reshape(uq.shape[0], heads * 2 * LANES).astype(BF)
    ukv_loc = mla_w_ukv[0].reshape(mla_w_ukv.shape[1], heads * (NOPE_DIM + V_DIM)).astype(BF)
    def merged(g):
        return g.reshape(g.shape[0] * g.shape[1], g.shape[2])

    g_win, g_uq, g_ukv = [merged(g) for g in _ag_big("ag_mla", [w_in_loc, uq_loc, ukv_loc])]

    n_ng = depth * 4 * d_loc
    small = jnp.concatenate([c.reshape(-1), norm_g.reshape(-1), conv_w.reshape(-1)])
    small_n = small.shape[0]
    small_rows = -(-small_n // (8 * LANES)) * 8
    small = jnp.pad(small, (0, small_rows * LANES - small_n)).reshape(small_rows, LANES)
    small_all = _ag_small("ag_small", small).reshape(N_DEV, small_rows * LANES)
    c_all = small_all[:, :d]
    ng_all = small_all[:, d:d + n_ng].reshape(N_DEV, depth, 4, d_loc).transpose(1, 2, 0, 3).reshape(depth, 4, d)
    cw_all = small_all[:, d + n_ng:d + n_ng + 3 * d_loc].reshape(N_DEV, 3, d_loc).transpose(1, 0, 2).reshape(3, d)
    cw8 = _pad_rows8(cw_all)
    cond_all = _silu("silu", c_all)

    mod_parts = jnp.stack([_mod_matvec(f"mod_matvec{i}", cond_all, w_mod, i) for i in range(depth)])
    mod_rows = depth * N_DEV * n_mod_loc // LANES
    mod_all = _ag_small("ag_mod", mod_parts.reshape(mod_rows, LANES)).reshape(N_DEV, depth, N_DEV, n_mod_loc)
    mod_mine = lax.dynamic_index_in_dim(mod_all, dev, axis=2, keepdims=False)
    mod = mod_mine.transpose(1, 0, 2).reshape(depth, N_DEV * n_mod_loc) + b_mod
    mod = mod.reshape(depth, 6, d)

    dev_idx = jnp.reshape(dev, (1,)).astype(jnp.int32)
    later = [(mla_w_o, 0), (mlp_w_up, 0), (mlp_w_down, 0), (conv_w_in, 0), (conv_w_out, 0), (mlp_w_up, 1), (mlp_w_down, 1)]
    lands = [_cast_own_block(f"cast_own_block{a}", w, layer, dev_idx) for a, (w, layer) in enumerate(later)]
    ag_groups = [[0], [1, 2], [3, 4, 5, 6]]
    ag_sems, lands_thru, ag_token = _ag_start("ag_start", lands, ag_groups, [mod, g_ukv])

    def fwd_vec(nb, gate, na, sc, sh, token=None):
        zero = jnp.zeros((d,), F32)
        rows = [zero if r is None else r for r in (nb, gate, na, sc, sh)]
        vec = _pad_rows8(jnp.stack(rows))
        return vec if token is None else vec + token[0, 0]

    inv_freq = ROPE_THETA ** (-jnp.arange(0, ROPE_DIM, 2, dtype=F32) / ROPE_DIM)
    ang = positions[0].astype(F32)[:, None] * inv_freq
    cos, sin = jnp.cos(ang), jnp.sin(ang)
    zh = jnp.zeros_like(cos)
    zpad = jnp.zeros((t, LANES - ROPE_DIM), F32)
    rope_c = jnp.concatenate([cos, cos, zpad], axis=1)
    rope_s1 = jnp.concatenate([-sin, zh, zpad], axis=1)
    rope_s2 = jnp.concatenate([zh, sin, zpad], axis=1)

    (h0,) = _norm_fwd("pre0", x0, None, fwd_vec(None, None, ng_all[0, 0], mod[0, 1], mod[0, 0], ag_token), None, False, True, False)
    (lat,) = _linear("mla_lat", h0, g_win)
    cq, ckv, kr = _lat_post("mla_lat_post", lat, mla_g_q, mla_g_kv, rope_c, rope_s1, rope_s2, rq, rkv)
    def rope_odd_groups(acc, csv, s1v, s2v):
        groups = [acc[:, g * LANES:(g + 1) * LANES] for g in range(acc.shape[1] // LANES)]
        return (jnp.concatenate([grp if g % 2 == 0 else _rope128(grp, csv, s1v, s2v) for g, grp in enumerate(groups)], axis=1),)

    (qb,) = _linear("mla_q", cq, g_uq, out_dtypes=(BF,), epilogue=rope_odd_groups,
                    row_extras=(rope_c, rope_s1, rope_s2), tn_multiple=2 * LANES)
    (kvb,) = _linear("mla_kv", ckv, g_ukv, out_dtypes=(BF,))
    fs_o, fr_o, lands_o, _ = _ag_forward("ag_forward_o", lands_thru[0:1], *ag_sems[0], kvb)
    o, lse = _flash_fwd("mla_attn", qb, kvb, kr, heads, scale)
    (g_wo,) = _ag_finish("ag_finish_o", lands_o, fs_o, fr_o, o)
    g_wo = merged(g_wo)
    fs_a, fr_a, lands_a, _ = _ag_forward("ag_forward_a", lands_thru[1:3], *ag_sems[1], o)
    (y0,) = _linear("mla_out", o, g_wo)
    x1, h1 = _norm_fwd("postpre1", x0, y0, fwd_vec(ng_all[0, 1], mod[0, 2], ng_all[0, 2], mod[0, 4], mod[0, 3]), None, True, True, False)
    g_up0, g_dn0 = _ag_finish("ag_finish_a", lands_a, fs_a, fr_a, x1)
    g_dn0 = merged(g_dn0)

    def sq_relu(acc):
        a = jnp.maximum(acc, 0.0)
        return acc, a * a

    u1, act1 = _linear("mlp0_up", h1, g_up0, out_dtypes=(F32, BF), epilogue=sq_relu)
    (y1,) = _linear("mlp0_down", act1, g_dn0)
    fs_b, fr_b, lands_b, _ = _ag_forward("ag_forward_b", lands_thru[3:7], *ag_sems[2], y1)
    x2, h2 = _norm_fwd("postpre2", x1, y1, fwd_vec(ng_all[0, 3], mod[0, 5], ng_all[1, 0], mod[1, 1], mod[1, 0]), None, True, True, False)
    g_cin, g_cout, g_up1, g_dn1 = _ag_finish("ag_finish_b", lands_b, fs_b, fr_b, x2)
    g_cout, g_dn1 = merged(g_cout), merged(g_dn1)
    g_up, g_dn = [g_up0, g_up1], [g_dn0, g_dn1]

    (proj,) = _linear("conv_in", h2, g_cin)
    bz = _conv_fwd("conv_mix", proj, cw8)
    (y2,) = _linear("conv_out", bz, g_cout)
    x3, h3 = _norm_fwd("postpre3", x2, y2, fwd_vec(ng_all[1, 1], mod[1, 2], ng_all[1, 2], mod[1, 4], mod[1, 3]), None, True, True, False)

    u3, act3 = _linear("mlp1_up", h3, g_up[1], out_dtypes=(F32, BF), epilogue=sq_relu)
    (y3,) = _linear("mlp1_down", act3, g_dn[1])
    dx4, loss_cols = _norm_fwd("post_loss", x3, y3, fwd_vec(ng_all[1, 3], mod[1, 5], None, None, None), target, True, False, True)
    loss = lax.psum(0.5 * jnp.sum(loss_cols[0]) / d, ("x", "y", "c"))

    def bwd_vec(nb, gate, na, sc, token=None):
        zero = jnp.zeros((d,), F32)
        rows = [zero if r is None else r for r in (nb, gate, na, sc)]
        vec = _pad_rows8(jnp.stack(rows))
        return vec if token is None else vec + token[0, 0]

    def blocks(g, like):
        return g.reshape((N_DEV,) + like.shape)

    def rs_begin(tag, grads):
        return _push_start(f"rs_sibling_start_{tag}", grads, _sibling_routes)

    def rs_continue(tag, started, after):
        send_sems, recv_sems, grads, from_sibling, _ = started
        grads, from_sibling = _push_wait(f"rs_sibling_wait_{tag}", grads, from_sibling, send_sems, recv_sems, _sibling_routes, after)
        pair = [_pair_add(f"pair_add_{tag}{a}", grads[a], from_sibling[a], c_idx) for a in range(len(grads))]
        return _push_start(f"rs_start_{tag}", pair, _chip_routes)

    def mlp_bwd(tag, dy, h, u, act, w_up, w_dn, w_dn_shard):
        def relu_grad(acc, uv):
            return (acc * (2.0 * jnp.maximum(uv, 0.0)),)

        (du,) = _linear_t(f"{tag}_dact", dy, w_dn, out_dtypes=(BF,), epilogue=relu_grad, extras=(u,))
        dw_dn = _wgrad(f"{tag}_dw_down", act, dy, w_dn)
        dw_up = _wgrad(f"{tag}_dw_up", h, du, w_up)
        sib = rs_begin(tag, [dw_up, blocks(dw_dn, w_dn_shard)])
        (dh,) = _linear_t(f"{tag}_dh", du, w_up, token=sib[4])
        return dh, rs_continue(tag, sib, dh)

    dy3, sums4 = _norm_bwd("bwd_post3", dx4, None, None, y3, bwd_vec(ng_all[1, 3], mod[1, 5], None, None), False, True)
    dh3, rs_mlp1 = mlp_bwd("mlp1", dy3, h3, u3, act3, g_up[1], g_dn[1], mlp_w_down[1])
    dx3, dy2, sums3 = _norm_bwd("bwd_norm3", dx4, dh3, x3, y2, bwd_vec(ng_all[1, 1], mod[1, 2], ng_all[1, 2], mod[1, 4], rs_mlp1[4]), True, True)

    (dbz,) = _linear_t("conv_dbz", dy2, g_cout)
    dw_cout = _wgrad("conv_dw_out", bz, dy2, g_cout)
    dproj, conv_sums = _conv_bwd("conv_mix_bwd", dbz, proj, cw8)
    dw_cin = _wgrad("conv_dw_in", h2, dproj, g_cin)
    sib_conv = rs_begin("conv", [dw_cin, blocks(dw_cout, conv_w_out[0])])
    (dh2,) = _linear_t("conv_dh", dproj, g_cin, token=sib_conv[4])
    rs_conv = rs_continue("conv", sib_conv, dh2)
    dx2, dy1, sums2 = _norm_bwd("bwd_norm2", dx3, dh2, x2, y1, bwd_vec(ng_all[0, 3], mod[0, 5], ng_all[1, 0], mod[1, 1], rs_conv[4]), True, True)

    dh1, rs_mlp0 = mlp_bwd("mlp0", dy1, h1, u1, act1, g_up[0], g_dn[0], mlp_w_down[0])
    dx1, dy0, sums1 = _norm_bwd("bwd_norm1", dx2, dh1, x1, y0, bwd_vec(ng_all[0, 1], mod[0, 2], ng_all[0, 2], mod[0, 4], rs_mlp0[4]), True, True)

    (do,) = _linear_t("mla_do", dy0, g_wo, out_dtypes=(BF,))
    dw_o = _wgrad("mla_dw_o", o, dy0, g_wo)
    dkv, dkr, dq = _flash_bwd("mla_attn_bwd", qb, kvb, kr, o, lse, do, rope_c, -rope_s1, -rope_s2, heads, scale)
    dw_uq = _wgrad("mla_dw_uq", cq, dq, g_uq)
    (dcq,) = _linear_t("mla_dcq", dq, g_uq)
    dw_ukv = _wgrad("mla_dw_ukv", ckv, dkv, g_ukv)
    (dckv,) = _linear_t("mla_dckv", dkv, g_ukv)
    dlat, lat_sums = _lat_bwd("mla_lat_bwd", dcq, dckv, dkr, lat, mla_g_q, mla_g_kv, rope_c, -rope_s1, -rope_s2, rq, rkv, heads)
    dw_win = _wgrad("mla_dw_in", h0, dlat, g_win)
    sib_mla = rs_begin("mla", [blocks(dw_win, w_in_loc), blocks(dw_uq, uq_loc), blocks(dw_ukv, ukv_loc), blocks(dw_o, mla_w_o[0])])
    (dh0,) = _linear_t("mla_dh", dlat, g_win, token=sib_mla[4])
    grad_x, sums0 = _norm_bwd("bwd_pre0", dx1, dh0, x0, None, bwd_vec(None, None, ng_all[0, 0], mod[0, 1]), True, False)

    def sums_rows(s, rows):
        return [s[r] for r in rows]

    dmod = jnp.stack([
        jnp.stack(sums_rows(sums0, (0, 1)) + sums_rows(sums1, (3,)) + sums_rows(sums1, (0, 1)) + sums_rows(sums2, (3,))),
        jnp.stack(sums_rows(sums2, (0, 1)) + sums_rows(sums3, (3,)) + sums_rows(sums3, (0, 1)) + sums_rows(sums4, (3,))),
    ])
    dng = jnp.stack([
        jnp.stack([sums0[2], sums1[4], sums1[2], sums2[4]]),
        jnp.stack([sums2[2], sums3[4], sums3[2], sums4[4]]),
    ])
    parts = [dmod.reshape(-1), dng.reshape(-1), lat_sums[0], lat_sums[1], conv_sums[0:3].reshape(-1)]
    sizes = [p.shape[0] for p in parts]
    packed = jnp.concatenate(parts)
    pk_rows = -(-packed.shape[0] // (8 * LANES)) * 8
    packed = jnp.pad(packed, (0, pk_rows * LANES - packed.shape[0])).reshape(pk_rows, LANES)
    packed_all = _ag_small("ag_small_grads", packed).reshape(N_DEV, pk_rows, LANES)
    rs_mla = rs_continue("mla", sib_mla, packed_all)
    total = _sum_devices("sum_small_grads", packed_all).reshape(-1)
    offs = [0]
    for s in sizes:
        offs.append(offs[-1] + s)
    g_b_mod = total[offs[0]:offs[1]].reshape(depth, 6 * d)
    ng_full = total[offs[1]:offs[2]].reshape(depth, 4, d)
    g_norm_g = lax.dynamic_slice_in_dim(ng_full, dev * d_loc, d_loc, axis=2)
    g_g_q = total[offs[2]:offs[3]].reshape(1, rq)
    g_g_kv = total[offs[3]:offs[4]].reshape(1, rkv)
    cw_full = total[offs[4]:offs[5]].reshape(1, 3, d)
    g_conv_w = lax.dynamic_slice_in_dim(cw_full, dev * d_loc, d_loc, axis=2)

    dmod_all = packed_all.reshape(N_DEV, -1)[:, :sizes[0]].reshape(N_DEV, depth, 6 * d)
    dmod_cols = lax.dynamic_slice_in_dim(dmod_all, dev * n_mod_loc, n_mod_loc, axis=2)
    dmod_cols = dmod_cols + rs_mla[4][0, 0]
    cond_t = cond_all.T
    mod_out = None
    for i in range(depth):
        mod_out = _mod_grad_adam(f"w_mod_adam{i}", cond_t, dmod_cols[:, i, :], w_mod, m_w_mod, v_w_mod, i, mod_out)
    g_w_mod, d_w_mod, nm_w_mod, nv_w_mod = mod_out

    def rs_end(tag, started, after):
        send_sems, recv_sems, pair, lands, _ = started
        return _push_wait(f"rs_wait_{tag}", pair, lands, send_sems, recv_sems, _chip_routes, after)

    pair_mlp1, lands_mlp1 = rs_end("mlp1", rs_mlp1, g_w_mod)
    pair_conv, lands_conv = rs_end("conv", rs_conv, g_w_mod)
    pair_mlp0, lands_mlp0 = rs_end("mlp0", rs_mlp0, g_w_mod)

    results = {"w_mod": (g_w_mod, d_w_mod, nm_w_mod, nv_w_mod)}
    given = {"b_mod": (b_mod, m_b_mod, v_b_mod), "norm_g": (norm_g, m_norm_g, v_norm_g),
             "mla_w_in": (mla_w_in, m_mla_w_in, v_mla_w_in), "mla_g_q": (mla_g_q, m_mla_g_q, v_mla_g_q),
             "mla_g_kv": (mla_g_kv, m_mla_g_kv, v_mla_g_kv), "mla_w_uq": (mla_w_uq, m_mla_w_uq, v_mla_w_uq),
             "mla_w_ukv": (mla_w_ukv, m_mla_w_ukv, v_mla_w_ukv), "mla_w_o": (mla_w_o, m_mla_w_o, v_mla_w_o),
             "conv_w_in": (conv_w_in, m_conv_w_in, v_conv_w_in), "conv_w": (conv_w, m_conv_w, v_conv_w),
             "conv_w_out": (conv_w_out, m_conv_w_out, v_conv_w_out), "mlp_w_up": (mlp_w_up, m_mlp_w_up, v_mlp_w_up),
             "mlp_w_down": (mlp_w_down, m_mlp_w_down, v_mlp_w_down)}

    def fused(name, parts, prev=None):
        for layer, (pair, lands) in enumerate(parts):
            prev = _adam_sum(f"adam_{name}{layer}", pair, lands, chip_idx, *given[name], layer, prev)
        results[name] = tuple(prev)

    fused("mlp_w_up", [(pair_mlp0[0], lands_mlp0[0]), (pair_mlp1[0], lands_mlp1[0])])
    fused("mlp_w_down", [(pair_mlp0[1], lands_mlp0[1]), (pair_mlp1[1], lands_mlp1[1])])
    fused("conv_w_in", [(pair_conv[0], lands_conv[0])])
    fused("conv_w_out", [(pair_conv[1], lands_conv[1])])
    pair_mla, lands_mla = rs_end("mla", rs_mla, results["conv_w_out"][3])
    fused("mla_w_o", [(pair_mla[3], lands_mla[3])])

    red_win, red_uq, red_ukv = [_chip_sum(f"chip_sum_mla{a}", pair_mla[a], lands_mla[a], chip_idx) for a in range(3)]
    plain = {"b_mod": g_b_mod, "norm_g": g_norm_g, "mla_g_q": g_g_q, "mla_g_kv": g_g_kv, "conv_w": g_conv_w,
             "mla_w_in": red_win[:, :lat_w][None],
             "mla_w_uq": red_uq.reshape(-1, heads, 2 * LANES)[:, :, :NOPE_DIM + ROPE_DIM][None],
             "mla_w_ukv": red_ukv.reshape(mla_w_ukv.shape)}
    for name, g in plain.items():
        results[name] = (g, *_adam(f"adam_{name}", given[name][0], g, given[name][1], given[name][2]))

    order = ["w_mod", "b_mod", "norm_g", "mla_w_in", "mla_g_q", "mla_g_kv", "mla_w_uq", "mla_w_ukv", "mla_w_o",
             "conv_w_in", "conv_w", "conv_w_out", "mlp_w_up", "mlp_w_down"]
    return (loss, grad_x[None], *[results[n][0] for n in order], *[results[n][1] for n in order],
            *[results[n][2] for n in order], *[results[n][3] for n in order])
```
